```python
import jax
import jax.numpy as jnp
from jax import lax
import numpy as np


D_MODEL = 1024
BATCH = 8
SEQ = 2048
DEPTH = 2

GRID_W = 64
CTX_LEN = 256
HEAD_DIM = 64
A_Q_HEADS = 6
A_KV_HEADS = 2
A_WINDOW = 128
A_BLOCK = 128
ROPE_BASE = 10000.0
B_GROUPS = 4
B_CHUNK = 128
C_HEADS = 6
C_WIN_R = 8
C_WIN_C = 16
A_WIDTH = A_Q_HEADS * HEAD_DIM
B_WIDTH = B_GROUPS * HEAD_DIM
C_WIDTH = C_HEADS * HEAD_DIM
MIX_WIDTH = A_WIDTH + B_WIDTH + C_WIDTH
IN_SIZES = (A_WIDTH, A_KV_HEADS * HEAD_DIM, A_KV_HEADS * HEAD_DIM, B_WIDTH, B_WIDTH, C_WIDTH, C_WIDTH, C_WIDTH)
IN_WIDTH = sum(IN_SIZES)
IN_SPLITS = tuple(int(s) for s in np.cumsum(IN_SIZES)[:-1])
N_EXPERTS = 64
TOP_K = 8
N_GROUPS = 8
TOPK_GROUPS = 4
EXPERT_FF = 256
SHARED_FF = 256
ROUTED_SCALE = 2.5
MOE_BLOCK = 256
EPS = 1e-6
NEG = -1e30
SCALE = HEAD_DIM ** -0.5

kernel_name = 'hybrid_dit_block'


def rmsnorm(x, g):
    x32 = x.astype(jnp.float32)
    y = x32 * lax.rsqrt(jnp.mean(x32 * x32, axis=-1, keepdims=True) + EPS)
    return (y * g.astype(jnp.float32)).astype(x.dtype)


def modulate(x, shift, scale):
    return x * (1.0 + scale) + shift


def axial_rope_tables(n_tokens, dtype):
    t = jnp.arange(n_tokens)
    row = (t // GRID_W).astype(jnp.float32)
    col = (t % GRID_W).astype(jnp.float32)
    half = HEAD_DIM // 2
    inv = ROPE_BASE ** (-jnp.arange(0, half, 2, dtype=jnp.float32) / half)
    ang_r = row[:, None] * inv
    ang_c = col[:, None] * inv
    ang = jnp.concatenate([ang_r, ang_r, ang_c, ang_c], axis=-1)
    return jnp.cos(ang).astype(dtype), jnp.sin(ang).astype(dtype)


def apply_axial_rope(x, cos, sin):
    x1, x2, x3, x4 = jnp.split(x, 4, axis=-1)
    rot = jnp.concatenate([-x2, x1, -x4, x3], axis=-1)
    return x * cos[:, None, :] + rot * sin[:, None, :]


def split_heads(z):
    b, n, _ = z.shape
    qa, ka, va, ub, vb, qn, kn, vn = jnp.split(z, IN_SPLITS, axis=-1)
    hd = lambda t: t.reshape(b, n, -1, HEAD_DIM)
    return (hd(qa), hd(ka), hd(va), jax.nn.gelu(ub), jax.nn.gelu(vb), hd(qn), hd(kn), hd(vn))


def window_gqa_latent(q, k, v, k_ctx, v_ctx, sink):
    b, s_len, _, dh = q.shape
    nb = s_len // A_BLOCK
    g = A_Q_HEADS // A_KV_HEADS
    qb = q.reshape(b, nb, A_BLOCK, A_KV_HEADS, g, dh)
    pad = ((0, 0), (1, 1), (0, 0), (0, 0), (0, 0))
    kb = jnp.pad(k.reshape(b, nb, A_BLOCK, A_KV_HEADS, dh), pad)
    vb = jnp.pad(v.reshape(b, nb, A_BLOCK, A_KV_HEADS, dh), pad)
    k_band = jnp.concatenate([kb[:, :-2], kb[:, 1:-1], kb[:, 2:]], axis=2)
    v_band = jnp.concatenate([vb[:, :-2], vb[:, 1:-1], vb[:, 2:]], axis=2)
    qpos = np.arange(nb)[:, None] * A_BLOCK + np.arange(A_BLOCK)
    kpos = (np.arange(nb)[:, None] - 1) * A_BLOCK + np.arange(3 * A_BLOCK)
    kp = kpos[:, None, :]
    mask = (kp >= 0) & (kp < s_len) & (np.abs(kp - qpos[:, :, None]) <= A_WINDOW)
    s_band = jnp.einsum('bnqkgd,bnskd->bnkgqs', qb, k_band, preferred_element_type=jnp.float32) * SCALE
    s_band = jnp.where(mask[None, :, None, None], s_band, NEG)
    s_ctx = jnp.einsum('bnqkgd,bckd->bnkgqc', qb, k_ctx, preferred_element_type=jnp.float32) * SCALE
    s_sink = jnp.broadcast_to(sink.astype(jnp.float32).reshape(1, 1, A_KV_HEADS, g, 1, 1), s_ctx.shape[:-1] + (1,))
    p = jax.nn.softmax(jnp.concatenate([s_sink, s_ctx, s_band], axis=-1), axis=-1).astype(v.dtype)
    n_ctx = k_ctx.shape[1]
    out = (jnp.einsum('bnkgqc,bckd->bnqkgd', p[..., 1:1 + n_ctx], v_ctx)
           + jnp.einsum('bnkgqs,bnskd->bnqkgd', p[..., 1 + n_ctx:], v_band))
    return out.reshape(b, s_len, A_WIDTH)


def dense_gqa_context(q, k, v, sink):
    b, n, _, dh = q.shape
    g = A_Q_HEADS // A_KV_HEADS
    qg = q.reshape(b, n, A_KV_HEADS, g, dh)
    s = jnp.einsum('bqkgd,bskd->bkgqs', qg, k, preferred_element_type=jnp.float32) * SCALE
    s_sink = jnp.broadcast_to(sink.astype(jnp.float32).reshape(1, A_KV_HEADS, g, 1, 1), s.shape[:-1] + (1,))
    p = jax.nn.softmax(jnp.concatenate([s_sink, s], axis=-1), axis=-1)[..., 1:].astype(v.dtype)
    out = jnp.einsum('bkgqs,bskd->bqkgd', p, v)
    return out.reshape(b, n, A_WIDTH)


def chunk_sgu(u, v, norm_g, w_s, b_s):
    b, n, _ = u.shape
    nc = n // B_CHUNK
    vg = rmsnorm(v.reshape(b, n, B_GROUPS, HEAD_DIM), norm_g.reshape(B_GROUPS, HEAD_DIM))
    vg = vg.reshape(b, nc, B_CHUNK, B_GROUPS, HEAD_DIM)
    gate = jnp.einsum('gpq,bnqgd->bnpgd', w_s, vg) + b_s.T[:, :, None]
    return u * gate.reshape(b, n, B_WIDTH)


def neighbourhood_attention_latent(q, k, v, k_ctx, v_ctx, rpb):
    b, s_len, h, dh = q.shape
    rows = s_len // GRID_W
    kr = min(C_WIN_R, rows)
    r = np.arange(rows)
    row_idx = np.clip(r - kr // 2, 0, rows - kr)[:, None] + np.arange(kr)
    cq = np.arange(GRID_W)
    col_start = np.clip(cq - C_WIN_C // 2, 0, GRID_W - C_WIN_C)
    col_ok = (cq[None, :] >= col_start[:, None]) & (cq[None, :] < col_start[:, None] + C_WIN_C)
    mask = np.broadcast_to(col_ok[:, None, :], (GRID_W, kr, GRID_W)).reshape(GRID_W, kr * GRID_W)
    qg = q.reshape(b, rows, GRID_W, h, dh)
    kg = k.reshape(b, rows, GRID_W, h, dh)[:, row_idx].reshape(b, rows, kr * GRID_W, h, dh)
    vg = v.reshape(b, rows, GRID_W, h, dh)[:, row_idx].reshape(b, rows, kr * GRID_W, h, dh)
    dr = row_idx - r[:, None] + (C_WIN_R - 1)
    dc = np.clip(cq[None, :] - cq[:, None], -(C_WIN_C - 1), C_WIN_C - 1) + (C_WIN_C - 1)
    bias = rpb[:, dr[:, None, :, None], dc[None, :, None, :]]
    bias = bias.reshape(h, rows, GRID_W, kr * GRID_W).transpose(1, 0, 2, 3).astype(jnp.float32)
    s_nb = jnp.einsum('brqhd,brkhd->brhqk', qg, kg, preferred_element_type=jnp.float32) * SCALE + bias[None]
    s_nb = jnp.where(mask, s_nb, NEG)
    s_ctx = jnp.einsum('brqhd,bchd->brhqc', qg, k_ctx, preferred_element_type=jnp.float32) * SCALE
    p = jax.nn.softmax(jnp.concatenate([s_ctx, s_nb], axis=-1), axis=-1).astype(v.dtype)
    n_ctx = k_ctx.shape[1]
    out = (jnp.einsum('brhqc,bchd->brqhd', p[..., :n_ctx], v_ctx)
           + jnp.einsum('brhqk,brkhd->brqhd', p[..., n_ctx:], vg))
    return out.reshape(b, s_len, C_WIDTH)


def dense_mha_context(q, k, v):
    s = jnp.einsum('bqhd,bkhd->bhqk', q, k, preferred_element_type=jnp.float32) * SCALE
    p = jax.nn.softmax(s, axis=-1).astype(v.dtype)
    out = jnp.einsum('bhqk,bkhd->bqhd', p, v)
    return out.reshape(q.shape[0], q.shape[1], C_WIDTH)


def merge_heads(ya, yb, yc, g, w_out):
    ga, gb, gc = jnp.split(g, [A_WIDTH, A_WIDTH + B_WIDTH])
    y = jnp.concatenate([rmsnorm(ya, ga), rmsnorm(yb, gb), rmsnorm(yc, gc)], axis=-1)
    return y @ w_out


def moe_ffn(xf, router_w, router_bias, w_gate, w_up, w_down, s_gate, s_up, s_down):
    n, d = xf.shape
    scores = jax.nn.sigmoid((xf @ router_w).astype(jnp.float32))
    sel = scores + router_bias.astype(jnp.float32)
    grp = sel.reshape(n, N_GROUPS, N_EXPERTS // N_GROUPS)
    grp_score = lax.top_k(grp, 2)[0].sum(-1)
    gidx = lax.top_k(grp_score, TOPK_GROUPS)[1]
    gmask = jax.nn.one_hot(gidx, N_GROUPS).sum(1) > 0
    emask = jnp.repeat(gmask, N_EXPERTS // N_GROUPS, axis=1)
    eidx = lax.top_k(jnp.where(emask, sel, NEG), TOP_K)[1]
    w = jnp.take_along_axis(scores, eidx, axis=1)
    w = (w / jnp.sum(w, axis=-1, keepdims=True) * ROUTED_SCALE).astype(xf.dtype)
    n_assign = n * TOP_K
    e_flat = eidx.reshape(-1)
    t_flat = jnp.repeat(jnp.arange(n, dtype=jnp.int32), TOP_K)
    order = jnp.argsort(e_flat)
    e_s = e_flat[order]
    t_s = t_flat[order]
    w_s = w.reshape(-1)[order]
    counts = jnp.bincount(e_flat, length=N_EXPERTS)
    starts = jnp.cumsum(counts) - counts
    nblk = (counts + MOE_BLOCK - 1) // MOE_BLOCK
    blk_end = jnp.cumsum(nblk)
    pstart = (blk_end - nblk) * MOE_BLOCK
    dest = pstart[e_s] + (jnp.arange(n_assign) - starts[e_s])
    nb = (n_assign + MOE_BLOCK - 1) // MOE_BLOCK + N_EXPERTS
    tok_buf = jnp.full((nb * MOE_BLOCK,), n, jnp.int32).at[dest].set(t_s)
    w_buf = jnp.zeros((nb * MOE_BLOCK,), xf.dtype).at[dest].set(w_s)
    blk_expert = jnp.minimum(jnp.searchsorted(blk_end, jnp.arange(nb), side='right'), N_EXPERTS - 1)
    xpad = jnp.concatenate([xf, jnp.zeros((1, d), xf.dtype)], axis=0)

    def expert_block(args):
        e, toks, ws = args
        xb = xpad[toks]
        hb = jax.nn.silu(xb @ w_gate[e]) * (xb @ w_up[e])
        return (hb @ w_down[e]) * ws[:, None]

    y_blocks = lax.map(expert_block, (blk_expert, tok_buf.reshape(nb, MOE_BLOCK), w_buf.reshape(nb, MOE_BLOCK)))
    routed = jax.ops.segment_sum(y_blocks.reshape(-1, d), tok_buf, num_segments=n + 1)[:n]
    shared = (jax.nn.silu(xf @ s_gate) * (xf @ s_up)) @ s_down
    return shared + routed


def setup_inputs(seed: int = 0) -> dict:
    key = jax.random.key(seed)
    ks = jax.random.split(key, 32)
    f32 = jnp.float32
    nrm = lambda k, shape, s: jax.random.normal(k, shape, f32) * s
    D = D_MODEL
    L = DEPTH
    E = N_EXPERTS
    return {
        'x': nrm(ks[0], (BATCH, SEQ, D), 1.0),
        'c': nrm(ks[1], (BATCH, D), 1.0),
        'ctx': nrm(ks[2], (BATCH, CTX_LEN, D), 1.0),
        'c_ctx': nrm(ks[3], (D,), 1.0),
        'mod_w': nrm(ks[4], (L, D, 6 * D), 0.5 * D ** -0.5),
        'mod_b': nrm(ks[5], (L, 6 * D), 0.01),
        'norm1_g': 1.0 + nrm(ks[6], (L, D), 0.02),
        'w_in': nrm(ks[7], (L, D, IN_WIDTH), D ** -0.5),
        'attn_sink': nrm(ks[8], (L, A_Q_HEADS), 0.5),
        'sgu_norm_g': 1.0 + nrm(ks[9], (L, B_WIDTH), 0.02),
        'sgu_w': nrm(ks[10], (L, B_GROUPS, B_CHUNK, B_CHUNK), 0.5 * B_CHUNK ** -0.5),
        'sgu_b': 1.0 + nrm(ks[11], (L, B_GROUPS, B_CHUNK), 0.1),
        'na_rpb': nrm(ks[12], (L, C_HEADS, 2 * C_WIN_R - 1, 2 * C_WIN_C - 1), 0.1),
        'group_norm_g': 1.0 + nrm(ks[13], (L, MIX_WIDTH), 0.02),
        'w_out': nrm(ks[14], (L, MIX_WIDTH, D), MIX_WIDTH ** -0.5),
        'norm2_g': 1.0 + nrm(ks[15], (L, D), 0.02),
        'router_w': nrm(ks[16], (L, D, E), D ** -0.5),
        'router_bias': nrm(ks[17], (L, E), 0.01),
        'exp_w_gate': nrm(ks[18], (L, E, D, EXPERT_FF), D ** -0.5),
        'exp_w_up': nrm(ks[19], (L, E, D, EXPERT_FF), D ** -0.5),
        'exp_w_down': nrm(ks[20], (L, E, EXPERT_FF, D), EXPERT_FF ** -0.5),
        'shared_w_gate': nrm(ks[21], (L, D, SHARED_FF), D ** -0.5),
        'shared_w_up': nrm(ks[22], (L, D, SHARED_FF), D ** -0.5),
        'shared_w_down': nrm(ks[23], (L, SHARED_FF, D), SHARED_FF ** -0.5),
        'final_g': 1.0 + nrm(ks[24], (D,), 0.02),
    }


def reference(x, c, ctx, c_ctx, mod_w, mod_b, norm1_g, w_in, attn_sink, sgu_norm_g, sgu_w, sgu_b, na_rpb,
              group_norm_g, w_out, norm2_g, router_w, router_bias, exp_w_gate, exp_w_up, exp_w_down,
              shared_w_gate, shared_w_up, shared_w_down, final_g):
    b, s_len, d = x.shape
    n_ctx = ctx.shape[1]
    cos, sin = axial_rope_tables(s_len, x.dtype)
    for l in range(DEPTH):
        last = l == DEPTH - 1
        m_x = (jax.nn.silu(c) @ mod_w[l] + mod_b[l])[:, None, :]
        m_c = (jax.nn.silu(c_ctx) @ mod_w[l] + mod_b[l])[None, None, :]
        sh1, sc1, g1, sh2, sc2, g2 = jnp.split(m_x, 6, axis=-1)
        csh1, csc1, cg1, csh2, csc2, cg2 = jnp.split(m_c, 6, axis=-1)
        qa, ka, va, ub, vb, qn, kn, vn = split_heads(modulate(rmsnorm(x, norm1_g[l]), sh1, sc1) @ w_in[l])
        qa_c, ka_c, va_c, ub_c, vb_c, qn_c, kn_c, vn_c = split_heads(
            modulate(rmsnorm(ctx, norm1_g[l]), csh1, csc1) @ w_in[l])
        ya = window_gqa_latent(apply_axial_rope(qa, cos, sin), apply_axial_rope(ka, cos, sin), va,
                               ka_c, va_c, attn_sink[l])
        yb = chunk_sgu(ub, vb, sgu_norm_g[l], sgu_w[l], sgu_b[l])
        yc = neighbourhood_attention_latent(qn, kn, vn, kn_c, vn_c, na_rpb[l])
        x = x + g1 * merge_heads(ya, yb, yc, group_norm_g[l], w_out[l])
        if not last:
            ya_c = dense_gqa_context(qa_c, ka_c, va_c, attn_sink[l])
            yb_c = chunk_sgu(ub_c, vb_c, sgu_norm_g[l], sgu_w[l], sgu_b[l])
            yc_c = dense_mha_context(qn_c, kn_c, vn_c)
            ctx = ctx + cg1 * merge_heads(ya_c, yb_c, yc_c, group_norm_g[l], w_out[l])
        hx = modulate(rmsnorm(x, norm2_g[l]), sh2, sc2).reshape(b * s_len, d)
        if last:
            y = moe_ffn(hx, router_w[l], router_bias[l], exp_w_gate[l], exp_w_up[l], exp_w_down[l],
                        shared_w_gate[l], shared_w_up[l], shared_w_down[l])
            x = x + g2 * y.reshape(b, s_len, d)
        else:
            hc = modulate(rmsnorm(ctx, norm2_g[l]), csh2, csc2).reshape(b * n_ctx, d)
            y = moe_ffn(jnp.concatenate([hx, hc], axis=0), router_w[l], router_bias[l], exp_w_gate[l],
                        exp_w_up[l], exp_w_down[l], shared_w_gate[l], shared_w_up[l], shared_w_down[l])
            x = x + g2 * y[:b * s_len].reshape(b, s_len, d)
            ctx = ctx + cg2 * y[b * s_len:].reshape(b, n_ctx, d)
    return rmsnorm(x, final_g)
```

```python
import functools

import jax
import jax.numpy as jnp
import numpy as np
from jax import lax
from jax.experimental import pallas as pl
from jax.experimental.pallas import tpu as pltpu

F32 = jnp.float32
BF16 = jnp.bfloat16
I32 = jnp.int32

D = 1024
B = 8
S = 2048
C = 256
DEPTH = 2
GRID_W = 64
HD = 64
A_HEADS = 6
A_KV = 2
A_WIN = 128
A_BLK = 128
ROPE_BASE = 10000.0
SGU_GROUPS = 4
SGU_CHUNK = 128
C_HEADS = 6
C_WIN_R = 8
C_WIN_C = 16
A_W = A_HEADS * HD
B_W = SGU_GROUPS * HD
C_W = C_HEADS * HD
KV_W = A_KV * HD
IN_W = A_W + 2 * KV_W + 2 * B_W + 3 * C_W
N_EXP = 64
TOP_K = 8
N_GRP = 8
TOPK_GRP = 4
EXP_FF = 256
ROUTED_SCALE = 2.5
MOE_BLK = 256
EPS = 1e-6
NEG = -1e30
SCALE = HD ** -0.5

N_LAT = B * S
N_CTX = B * C
N_ALL = N_LAT + N_CTX
ROWS = S // GRID_W

HIGHEST = lax.Precision.HIGHEST
ARB = pltpu.ARBITRARY

NT_DIMS = (((1,), (1,)), ((), ()))


def _cparams(n_axes, vmem_mb=48):
    return pltpu.CompilerParams(
        dimension_semantics=(ARB,) * n_axes, vmem_limit_bytes=vmem_mb * 1024 * 1024)


def _mod_row(i, tm):
    return jnp.where(i < N_LAT // tm, i // (S // tm), B)


def _rms(x):
    return x * lax.rsqrt(jnp.mean(x * x, axis=-1, keepdims=True) + EPS)


MOD_TN = 1024


def _mod_body(cc_ref, w_ref, b_ref, o_ref):
    a = cc_ref[...]
    a = a * jax.nn.sigmoid(a)
    o_ref[...] = jnp.dot(a.astype(BF16), w_ref[...].astype(BF16),
                         preferred_element_type=F32) + b_ref[...]


def _modulation(cc, mod_w, mod_b):
    n_col = 6 * D // MOD_TN
    return pl.pallas_call(
        _mod_body,
        grid=(DEPTH, n_col),
        in_specs=[
            pl.BlockSpec((16, D), lambda l, j: (0, 0)),
            pl.BlockSpec((None, D, MOD_TN), lambda l, j: (l, 0, j)),
            pl.BlockSpec((None, 1, MOD_TN), lambda l, j: (l, 0, j)),
        ],
        out_specs=pl.BlockSpec((None, 16, MOD_TN), lambda l, j: (l, 0, j)),
        out_shape=jax.ShapeDtypeStruct((DEPTH, 16, 6 * D), F32),
        compiler_params=_cparams(2),
        name="modulation",
    )(cc, mod_w, mod_b.reshape(DEPTH, 1, 6 * D))


IN_TM = 256
ROT_W = A_W + KV_W


def _in_body(x_ref, m_ref, g_ref, w_ref, cos_ref, sin_ref, za_ref, zc_ref, uv_ref):
    x = x_ref[...]
    h = _rms(x) * g_ref[...]
    h = h * (1.0 + m_ref[:, D:2 * D]) + m_ref[:, 0:D]
    z = jnp.dot(h.astype(BF16), w_ref[...], preferred_element_type=F32)
    qk = z[:, :ROT_W]
    lane = lax.broadcasted_iota(I32, qk.shape, 1)
    rot = jnp.where((lane & 16) == 0,
                    -pltpu.roll(qk, ROT_W - 16, 1), pltpu.roll(qk, 16, 1))
    qk = qk * cos_ref[...] + rot * sin_ref[...]
    za_ref[...] = jnp.concatenate(
        [qk[:, :A_W] * SCALE, qk[:, A_W:], z[:, ROT_W:ROT_W + KV_W]], axis=1).astype(BF16)
    u0 = ROT_W + KV_W
    uv_ref[...] = jax.nn.gelu(z[:, u0:u0 + 2 * B_W])
    c0 = u0 + 2 * B_W
    zc_ref[...] = jnp.concatenate(
        [z[:, c0:c0 + C_W] * SCALE, z[:, c0 + C_W:]], axis=1).astype(BF16)


def _in_proj(xc, mods, g, w_bf, cos_t, sin_t):
    tm = IN_TM
    n_t = N_ALL // tm

    def tab_idx(i):
        return (jnp.where(i < N_LAT // tm, i % (S // tm), S // tm), 0)

    return pl.pallas_call(
        _in_body,
        grid=(n_t,),
        in_specs=[
            pl.BlockSpec((tm, D), lambda i: (i, 0)),
            pl.BlockSpec((None, 1, 6 * D), lambda i: (_mod_row(i, tm), 0, 0)),
            pl.BlockSpec((1, D), lambda i: (0, 0)),
            pl.BlockSpec((D, IN_W), lambda i: (0, 0)),
            pl.BlockSpec((tm, ROT_W), tab_idx),
            pl.BlockSpec((tm, ROT_W), tab_idx),
        ],
        out_specs=[
            pl.BlockSpec((tm, A_W + 2 * KV_W), lambda i: (i, 0)),
            pl.BlockSpec((tm, 3 * C_W), lambda i: (i, 0)),
            pl.BlockSpec((tm, 2 * B_W), lambda i: (i, 0)),
        ],
        out_shape=[
            jax.ShapeDtypeStruct((N_ALL, A_W + 2 * KV_W), BF16),
            jax.ShapeDtypeStruct((N_ALL, 3 * C_W), BF16),
            jax.ShapeDtypeStruct((N_ALL, 2 * B_W), F32),
        ],
        compiler_params=_cparams(1),
        name="in_proj",
    )(xc, mods, g, w_bf, cos_t, sin_t)


A_BAND = 3 * A_BLK
N_QB_LAT = N_LAT // A_BLK
QB_PER_SEQ = S // A_BLK
QB_PER_CTX = C // A_BLK


def _softmax_pv(s_list, v_list, extra_logit=None):
    m = s_list[0].max(axis=-1, keepdims=True)
    for s in s_list[1:]:
        m = jnp.maximum(m, s.max(axis=-1, keepdims=True))
    if extra_logit is not None:
        m = jnp.maximum(m, extra_logit)
    den = None
    out = None
    for s, v in zip(s_list, v_list):
        p = jnp.exp(s - m)
        d = p.sum(axis=-1, keepdims=True)
        o = jnp.dot(p.astype(BF16), v, preferred_element_type=F32)
        den = d if den is None else den + d
        out = o if out is None else out + o
    if extra_logit is not None:
        den = den + jnp.exp(extra_logit - m)
    return out / den


def _attn_a_body(sink_ref, q_ref, kb_ref, vb_ref, kc_ref, vc_ref, o_ref):
    i = pl.program_id(0)
    is_lat = i < N_QB_LAT
    n = i % QB_PER_SEQ
    start = pl.multiple_of(jnp.clip((n - 1) * A_BLK, 0, S - A_BAND), A_BLK)
    qpos = n * A_BLK + lax.broadcasted_iota(I32, (A_BLK, A_BAND), 0)
    kpos = start + lax.broadcasted_iota(I32, (A_BLK, A_BAND), 1)
    mask = jnp.abs(kpos - qpos) <= jnp.where(is_lat, A_WIN, -1)
    kb = kb_ref[pl.ds(start, A_BAND), :]
    vb = vb_ref[pl.ds(start, A_BAND), :]
    kc = kc_ref[...]
    vc = vc_ref[...]
    outs = []
    for h in range(A_HEADS):
        kv = h // (A_HEADS // A_KV)
        sl = slice(kv * HD, (kv + 1) * HD)
        q = q_ref[:, h * HD:(h + 1) * HD]
        s_b = lax.dot_general(q, kb[:, sl], NT_DIMS, preferred_element_type=F32)
        s_b = jnp.where(mask, s_b, NEG)
        s_c = lax.dot_general(q, kc[:, sl], NT_DIMS, preferred_element_type=F32)
        outs.append(_softmax_pv([s_c, s_b], [vc[:, sl], vb[:, sl]], sink_ref[h]))
    o_ref[...] = jnp.concatenate(outs, axis=1)


def _attn_a(za, sink, with_ctx):
    n_qb = N_QB_LAT + (N_CTX // A_BLK if with_ctx else 0)

    def bidx(i):
        return jnp.where(i < N_QB_LAT, i // QB_PER_SEQ, (i - N_QB_LAT) // QB_PER_CTX)

    k_col = A_W // KV_W
    v_col = k_col + 1
    return pl.pallas_call(
        _attn_a_body,
        grid=(n_qb,),
        in_specs=[
            pl.BlockSpec(memory_space=pltpu.SMEM),
            pl.BlockSpec((A_BLK, A_W), lambda i: (i, 0)),
            pl.BlockSpec((S, KV_W), lambda i: (bidx(i), k_col)),
            pl.BlockSpec((S, KV_W), lambda i: (bidx(i), v_col)),
            pl.BlockSpec((C, KV_W), lambda i: (N_LAT // C + bidx(i), k_col)),
            pl.BlockSpec((C, KV_W), lambda i: (N_LAT // C + bidx(i), v_col)),
        ],
        out_specs=pl.BlockSpec((A_BLK, A_W), lambda i: (i, 0)),
        out_shape=jax.ShapeDtypeStruct((n_qb * A_BLK, A_W), F32),
        compiler_params=_cparams(1),
        name="attn_window",
    )(sink, za, za, za, za, za)


NA_R = 4
NA_TQ = NA_R * GRID_W
NA_WIN = C_WIN_R * GRID_W
NA_STEPS = ROWS // NA_R


def _na_body(q_ref, k_ref, v_ref, kc_ref, vc_ref, tab_ref, o_ref):
    j = pl.program_id(1)
    ctx_penalty = jnp.where(j < NA_STEPS, 0.0, NEG)
    kc = kc_ref[...]
    vc = vc_ref[...]

    def row(rr, carry):
        r = jnp.minimum(j * NA_R + rr, ROWS - 1)
        start = jnp.clip(r - C_WIN_R // 2, 0, ROWS - C_WIN_R)
        base = start - r + (C_WIN_R - 1)
        k0 = pl.multiple_of(start * GRID_W, GRID_W)
        kw = k_ref[pl.ds(k0, NA_WIN), :]
        vw = v_ref[pl.ds(k0, NA_WIN), :]
        q0 = pl.multiple_of(rr * GRID_W, GRID_W)
        q = q_ref[pl.ds(q0, GRID_W), :]
        outs = []
        for h in range(C_HEADS):
            sl = slice(h * HD, (h + 1) * HD)
            qh = q[:, sl]
            s_w = lax.dot_general(qh, kw[:, sl], NT_DIMS, preferred_element_type=F32)
            bias = jnp.concatenate(
                [tab_ref[h, pl.ds(base + 2 * jj, 1)][0] for jj in range(C_WIN_R // 2)],
                axis=1)
            s_w = s_w + bias + ctx_penalty
            s_c = lax.dot_general(qh, kc[:, sl], NT_DIMS, preferred_element_type=F32)
            outs.append(_softmax_pv([s_c, s_w], [vc[:, sl], vw[:, sl]]))
        o_ref[pl.ds(q0, GRID_W), :] = jnp.concatenate(outs, axis=1)
        return carry

    lax.fori_loop(0, NA_R, row, 0)


def _na(zc, tab, with_ctx):
    n_j = NA_STEPS + (1 if with_ctx else 0)

    def qidx(b, j):
        return jnp.where(j < NA_STEPS, b * NA_STEPS + j, N_LAT // NA_TQ + b)

    n_out = N_LAT + (N_CTX if with_ctx else 0)
    return pl.pallas_call(
        _na_body,
        grid=(B, n_j),
        in_specs=[
            pl.BlockSpec((NA_TQ, C_W), lambda b, j: (qidx(b, j), 0)),
            pl.BlockSpec((S, C_W), lambda b, j: (b, 1)),
            pl.BlockSpec((S, C_W), lambda b, j: (b, 2)),
            pl.BlockSpec((C, C_W), lambda b, j: (N_LAT // C + b, 1)),
            pl.BlockSpec((C, C_W), lambda b, j: (N_LAT // C + b, 2)),
            pl.BlockSpec((C_HEADS, 2 * C_WIN_R, GRID_W, 2 * GRID_W), lambda b, j: (0, 0, 0, 0)),
        ],
        out_specs=pl.BlockSpec((NA_TQ, C_W), lambda b, j: (qidx(b, j), 0)),
        out_shape=jax.ShapeDtypeStruct((n_out, C_W), F32),
        compiler_params=_cparams(2),
        name="attn_neighbourhood",
    )(zc, zc, zc, zc, zc, tab)


def _na_bias_table(rpb):
    cq = np.arange(GRID_W)
    col_start = np.clip(cq - C_WIN_C // 2, 0, GRID_W - C_WIN_C)
    col_ok = (cq[None, :] >= col_start[:, None]) & (cq[None, :] < col_start[:, None] + C_WIN_C)
    dc = np.clip(cq[None, :] - cq[:, None], -(C_WIN_C - 1), C_WIN_C - 1) + (C_WIN_C - 1)
    t = rpb.astype(F32)[:, :, dc]
    t = jnp.where(col_ok[None, None], t, NEG)
    n_dr = 2 * C_WIN_R - 1
    lo = t[:, np.minimum(np.arange(2 * C_WIN_R), n_dr - 1)]
    hi = t[:, np.minimum(np.arange(2 * C_WIN_R) + 1, n_dr - 1)]
    return jnp.concatenate([lo, hi], axis=-1)


MG_TM = 256


def _merge_body(x_ref, ya_ref, yc_ref, uv_ref, m_ref, gg_ref, sg_ref, sw_ref, sb_ref, bd_ref,
                wo_ref, g2_ref, rw_ref, xo_ref, hx_ref, lg_ref):
    u = uv_ref[:, :B_W]
    v = uv_ref[:, B_W:]
    ms = jnp.dot(v * v, bd_ref[...], precision=HIGHEST, preferred_element_type=F32)
    vn = (v * lax.rsqrt(ms + EPS) * sg_ref[...]).astype(BF16)
    lane_grp = lax.broadcasted_iota(I32, (SGU_CHUNK, B_W), 1) // HD
    gates = []
    for c in range(MG_TM // SGU_CHUNK):
        vc = vn[c * SGU_CHUNK:(c + 1) * SGU_CHUNK, :]
        gate = sb_ref[...]
        acc = jnp.zeros((SGU_CHUNK, B_W), F32)
        for g in range(SGU_GROUPS):
            r = jnp.dot(sw_ref[g], vc, preferred_element_type=F32)
            acc = jnp.where(lane_grp == g, r, acc)
        gates.append(acc + gate)
    yb = u * jnp.concatenate(gates, axis=0)
    gg = gg_ref[...]
    ycat = jnp.concatenate([
        _rms(ya_ref[...]) * gg[:, :A_W],
        _rms(yb) * gg[:, A_W:A_W + B_W],
        _rms(yc_ref[...]) * gg[:, A_W + B_W:],
    ], axis=1)
    proj = jnp.dot(ycat.astype(BF16), wo_ref[...], preferred_element_type=F32)
    xn = x_ref[...] + m_ref[:, 2 * D:3 * D] * proj
    xo_ref[...] = xn
    hx = _rms(xn) * g2_ref[...]
    hx = hx * (1.0 + m_ref[:, 4 * D:5 * D]) + m_ref[:, 3 * D:4 * D]
    hx_ref[...] = hx
    lg_ref[...] = lax.dot_general(rw_ref[...], hx, NT_DIMS, precision=HIGHEST,
                                  preferred_element_type=F32)


def _merge(xc, ya, yc, uv, mods, gg, sg, sw_bf, sb_tab, bd, wo_bf, g2, rw_t, n_rows):
    tm = MG_TM
    n_t = n_rows // tm
    const2 = lambda i: (0, 0)
    row = lambda i: (i, 0)
    return pl.pallas_call(
        _merge_body,
        grid=(n_t,),
        in_specs=[
            pl.BlockSpec((tm, D), row),
            pl.BlockSpec((tm, A_W), row),
            pl.BlockSpec((tm, C_W), row),
            pl.BlockSpec((tm, 2 * B_W), row),
            pl.BlockSpec((None, 1, 6 * D), lambda i: (_mod_row(i, tm), 0, 0)),
            pl.BlockSpec((1, D), const2),
            pl.BlockSpec((1, B_W), const2),
            pl.BlockSpec((SGU_GROUPS, SGU_CHUNK, SGU_CHUNK), lambda i: (0, 0, 0)),
            pl.BlockSpec((SGU_CHUNK, B_W), const2),
            pl.BlockSpec((B_W, B_W), const2),
            pl.BlockSpec((D, D), const2),
            pl.BlockSpec((1, D), const2),
            pl.BlockSpec((N_EXP, D), const2),
        ],
        out_specs=[
            pl.BlockSpec((tm, D), row),
            pl.BlockSpec((tm, D), row),
            pl.BlockSpec((N_EXP, tm), lambda i: (0, i)),
        ],
        out_shape=[
            jax.ShapeDtypeStruct((n_rows, D), F32),
            jax.ShapeDtypeStruct((n_rows, D), F32),
            jax.ShapeDtypeStruct((N_EXP, n_rows), F32),
        ],
        compiler_params=_cparams(1),
        name="merge",
    )(xc, ya, yc, uv, mods, gg, sg, sw_bf, sb_tab, bd, wo_bf, g2, rw_t)


RT_TM = 256
GRP_SZ = N_EXP // N_GRP


def _first_argmax(v, iota):
    m = v.max(axis=0, keepdims=True)
    idx = jnp.where(v == m, iota, float(v.shape[0])).min(axis=0, keepdims=True)
    return m, idx


def _stack_rows(rows, iota):
    out = jnp.zeros(iota.shape, F32)
    for r, v in enumerate(rows):
        out = jnp.where(iota == float(r), v, out)
    return out


def _route_body(lg_ref, rb_ref, e_ref, w_ref, rk_ref, cnt_ref, base_ref):
    i = pl.program_id(0)

    @pl.when(i == 0)
    def _():
        base_ref[...] = jnp.zeros_like(base_ref)

    tm = RT_TM
    scores = jax.nn.sigmoid(lg_ref[...])
    sel = scores + rb_ref[...]
    iota_g = lax.broadcasted_iota(I32, (GRP_SZ, tm), 0).astype(F32)
    gs = []
    for g in range(N_GRP):
        v = sel[g * GRP_SZ:(g + 1) * GRP_SZ, :]
        m1, i1 = _first_argmax(v, iota_g)
        m2 = jnp.where(iota_g == i1, -jnp.inf, v).max(axis=0, keepdims=True)
        gs.append(m1 + m2)
    iota_n = lax.broadcasted_iota(I32, (N_GRP, tm), 0).astype(F32)
    gscore = _stack_rows(gs, iota_n)
    gsel = jnp.zeros((N_GRP, tm), F32)
    for _ in range(TOPK_GRP):
        _, gi = _first_argmax(gscore, iota_n)
        hit = iota_n == gi
        gsel = jnp.where(hit, 1.0, gsel)
        gscore = jnp.where(hit, -jnp.inf, gscore)
    emask = jnp.concatenate(
        [jnp.broadcast_to(gsel[g:g + 1, :], (GRP_SZ, tm)) for g in range(N_GRP)], axis=0)
    cand = jnp.where(emask > 0.5, sel, NEG)
    iota_e = lax.broadcasted_iota(I32, (N_EXP, tm), 0).astype(F32)
    hits = []
    idxs = []
    ws = []
    member = jnp.zeros((N_EXP, tm), F32)
    for _ in range(TOP_K):
        _, ei = _first_argmax(cand, iota_e)
        hit = iota_e == ei
        hits.append(hit)
        idxs.append(ei)
        ws.append(jnp.where(hit, scores, 0.0).sum(axis=0, keepdims=True))
        member = jnp.where(hit, 1.0, member)
        cand = jnp.where(hit, -jnp.inf, cand)
    wsum = ws[0]
    for w in ws[1:]:
        wsum = wsum + w
    iota_k = lax.broadcasted_iota(I32, (TOP_K, tm), 0).astype(F32)
    e_ref[...] = _stack_rows(idxs, iota_k).astype(I32)
    w_ref[...] = _stack_rows(ws, iota_k) / wsum * ROUTED_SCALE
    r_i = lax.broadcasted_iota(I32, (tm, tm), 0)
    c_i = lax.broadcasted_iota(I32, (tm, tm), 1)
    tri = jnp.where(r_i < c_i, 1.0, 0.0).astype(BF16)
    before = jnp.dot(member.astype(BF16), tri, preferred_element_type=F32) + base_ref[...]
    rk_ref[...] = _stack_rows(
        [jnp.where(hit, before, 0.0).sum(axis=0, keepdims=True) for hit in hits],
        iota_k).astype(I32)
    base_ref[...] = base_ref[...] + member.sum(axis=1, keepdims=True)
    cnt_ref[...] = base_ref[...]


def _route(logits_t, rbias, n_tok):
    tm = RT_TM
    tok = lambda i: (0, i)
    return pl.pallas_call(
        _route_body,
        grid=(n_tok // tm,),
        in_specs=[
            pl.BlockSpec((N_EXP, tm), tok),
            pl.BlockSpec((N_EXP, 1), lambda i: (0, 0)),
        ],
        out_specs=[
            pl.BlockSpec((TOP_K, tm), tok),
            pl.BlockSpec((TOP_K, tm), tok),
            pl.BlockSpec((TOP_K, tm), tok),
            pl.BlockSpec((N_EXP, 1), lambda i: (0, 0)),
        ],
        out_shape=[
            jax.ShapeDtypeStruct((TOP_K, n_tok), I32),
            jax.ShapeDtypeStruct((TOP_K, n_tok), F32),
            jax.ShapeDtypeStruct((TOP_K, n_tok), I32),
            jax.ShapeDtypeStruct((N_EXP, 1), F32),
        ],
        scratch_shapes=[pltpu.VMEM((N_EXP, 1), F32)],
        compiler_params=_cparams(1),
        name="route",
    )(logits_t, rbias)


DP_TM = 512


def _dispatch_body(tails_ref, dest_ref, hx_ref, xs_ref, zero_ref, sem):
    i = pl.program_id(0)

    @pl.when(i == 0)
    def _():
        zero_ref[...] = jnp.zeros_like(zero_ref)

        def fill(e, carry):
            t0 = tails_ref[e]

            @pl.when(t0 >= 0)
            def _():
                cp = pltpu.make_async_copy(
                    zero_ref, xs_ref.at[pl.ds(pl.multiple_of(t0, MOE_BLK), MOE_BLK)], sem)
                cp.start()
                cp.wait()
            return carry

        lax.fori_loop(0, N_EXP, fill, 0)

    def row_copy(t, k):
        return pltpu.make_async_copy(
            hx_ref.at[pl.ds(i * DP_TM + t, 1)], xs_ref.at[pl.ds(dest_ref[k, t], 1)], sem)

    def issue(t, carry):
        for k in range(TOP_K):
            row_copy(t, k).start()
        return carry

    lax.fori_loop(0, DP_TM, issue, 0)

    def drain(t, carry):
        for k in range(TOP_K):
            row_copy(t, k).wait()
        return carry

    lax.fori_loop(0, DP_TM, drain, 0)


def _dispatch(tails, dest, hx, n_tok, n_slots):
    return pl.pallas_call(
        _dispatch_body,
        grid_spec=pltpu.PrefetchScalarGridSpec(
            num_scalar_prefetch=1,
            grid=(n_tok // DP_TM,),
            in_specs=[
                pl.BlockSpec((TOP_K, DP_TM), lambda i, tails: (0, i), memory_space=pltpu.SMEM),
                pl.BlockSpec(memory_space=pl.ANY),
            ],
            out_specs=pl.BlockSpec(memory_space=pl.ANY),
            scratch_shapes=[pltpu.VMEM((MOE_BLK, D), F32), pltpu.SemaphoreType.DMA],
        ),
        out_shape=jax.ShapeDtypeStruct((n_slots, D), F32),
        compiler_params=_cparams(1),
        name="moe_dispatch",
    )(tails, dest, hx)


def _experts_body(bexp_ref, xblk_ref, nused_ref, xs_ref, wg_ref, wu_ref, wd_ref, ys_ref):
    j = pl.program_id(0)

    @pl.when(j < nused_ref[0])
    def _():
        x = xs_ref[...].astype(BF16)
        g = jnp.dot(x, wg_ref[...].astype(BF16), preferred_element_type=F32)
        u = jnp.dot(x, wu_ref[...].astype(BF16), preferred_element_type=F32)
        h = (g * jax.nn.sigmoid(g) * u).astype(BF16)
        ys_ref[...] = jnp.dot(h, wd_ref[...].astype(BF16), preferred_element_type=F32)


def _experts(bexp, xblk, nused, xs, wg, wu, wd, n_blk):
    return pl.pallas_call(
        _experts_body,
        grid_spec=pltpu.PrefetchScalarGridSpec(
            num_scalar_prefetch=3,
            grid=(n_blk,),
            in_specs=[
                pl.BlockSpec((MOE_BLK, D), lambda j, be, xb, nu: (xb[j], 0)),
                pl.BlockSpec((None, D, EXP_FF), lambda j, be, xb, nu: (be[j], 0, 0)),
                pl.BlockSpec((None, D, EXP_FF), lambda j, be, xb, nu: (be[j], 0, 0)),
                pl.BlockSpec((None, EXP_FF, D), lambda j, be, xb, nu: (be[j], 0, 0)),
            ],
            out_specs=pl.BlockSpec((MOE_BLK, D), lambda j, be, xb, nu: (xb[j], 0)),
        ),
        out_shape=jax.ShapeDtypeStruct((n_blk * MOE_BLK, D), F32),
        compiler_params=_cparams(1),
        name="moe_experts",
    )(bexp, xblk, nused, xs, wg, wu, wd)


CB_TM = 128


def _combine_body(dest_ref, ys_ref, x_ref, hx_ref, w_ref, m_ref, sg_ref, su_ref, sd_ref, fg_ref,
                  o_ref, buf_ref, sem, *, final_norm):
    def row_copy(t, k):
        return pltpu.make_async_copy(
            ys_ref.at[pl.ds(dest_ref[k, t], 1)], buf_ref.at[k, pl.ds(t, 1)], sem)

    def issue(t, carry):
        for k in range(TOP_K):
            row_copy(t, k).start()
        return carry

    lax.fori_loop(0, CB_TM, issue, 0)

    hx = hx_ref[...].astype(BF16)
    g = jnp.dot(hx, sg_ref[...], preferred_element_type=F32)
    u = jnp.dot(hx, su_ref[...], preferred_element_type=F32)
    h = (g * jax.nn.sigmoid(g) * u).astype(BF16)
    y = jnp.dot(h, sd_ref[...], preferred_element_type=F32)

    def drain(t, carry):
        for k in range(TOP_K):
            row_copy(t, k).wait()
        return carry

    lax.fori_loop(0, CB_TM, drain, 0)

    w = w_ref[...]
    for k in range(TOP_K):
        y = y + buf_ref[k] * w[:, k:k + 1]
    out = x_ref[...] + m_ref[:, 5 * D:6 * D] * y
    if final_norm:
        out = _rms(out) * fg_ref[...]
    o_ref[...] = out


def _combine(dest, ys, x, hx, w_rows, mods, sg_bf, su_bf, sd_bf, fg, n_tok, final_norm):
    tm = CB_TM
    row = lambda i: (i, 0)
    const2 = lambda i: (0, 0)
    return pl.pallas_call(
        functools.partial(_combine_body, final_norm=final_norm),
        grid=(n_tok // tm,),
        in_specs=[
            pl.BlockSpec((TOP_K, tm), lambda i: (0, i), memory_space=pltpu.SMEM),
            pl.BlockSpec(memory_space=pl.ANY),
            pl.BlockSpec((tm, D), row),
            pl.BlockSpec((tm, D), row),
            pl.BlockSpec((tm, TOP_K), row),
            pl.BlockSpec((None, 1, 6 * D), lambda i: (_mod_row(i, tm), 0, 0)),
            pl.BlockSpec((D, EXP_FF), const2),
            pl.BlockSpec((D, EXP_FF), const2),
            pl.BlockSpec((EXP_FF, D), const2),
            pl.BlockSpec((1, D), const2),
        ],
        out_specs=pl.BlockSpec((tm, D), row),
        out_shape=jax.ShapeDtypeStruct((n_tok, D), F32),
        scratch_shapes=[pltpu.VMEM((TOP_K, tm, D), F32), pltpu.SemaphoreType.DMA],
        compiler_params=_cparams(1),
        name="moe_combine",
    )(dest, ys, x, hx, w_rows, mods, sg_bf, su_bf, sd_bf, fg)


def _moe(x_new, hx, logits_t, mods, rbias, wg, wu, wd, sg_bf, su_bf, sd_bf, fg, n_tok, final_norm):
    eidx, w, rank, counts = _route(logits_t, rbias, n_tok)
    n_blk = n_tok * TOP_K // MOE_BLK + N_EXP
    counts = counts[:, 0].astype(I32)
    nblk_e = (counts + MOE_BLK - 1) // MOE_BLK
    blk_end = jnp.cumsum(nblk_e)
    pstart = (blk_end - nblk_e) * MOE_BLK
    expert_ids = jnp.arange(N_EXP, dtype=I32)
    dest = rank + jnp.sum(
        jnp.where(eidx[..., None] == expert_ids, pstart, 0), axis=-1).astype(I32)
    nused = blk_end[-1:].astype(I32)
    xblk = jnp.minimum(jnp.arange(n_blk, dtype=I32), nused[0] - 1)
    bexp = jnp.minimum(
        jnp.searchsorted(blk_end, xblk, side='right'), N_EXP - 1).astype(I32)
    tails = jnp.where(nblk_e > 0, pstart + (nblk_e - 1) * MOE_BLK, -1).astype(I32)
    xs = _dispatch(tails, dest, hx, n_tok, n_blk * MOE_BLK)
    ys = _experts(bexp, xblk, nused, xs, wg, wu, wd, n_blk)
    return _combine(dest, ys, x_new, hx, w.T, mods, sg_bf, su_bf, sd_bf, fg, n_tok, final_norm)


def _rope_tables():
    t = np.arange(S)
    row = (t // GRID_W).astype(np.float32)
    col = (t % GRID_W).astype(np.float32)
    half = HD // 2
    inv = jnp.asarray(ROPE_BASE, F32) ** (-jnp.arange(0, half, 2, dtype=F32) / half)
    ang_r = jnp.asarray(row)[:, None] * inv
    ang_c = jnp.asarray(col)[:, None] * inv
    ang = jnp.concatenate([ang_r, ang_r, ang_c, ang_c], axis=-1)
    n_rep = ROT_W // HD
    cos = jnp.tile(jnp.cos(ang), (1, n_rep))
    sin = jnp.tile(jnp.sin(ang), (1, n_rep))
    cos = jnp.concatenate([cos, jnp.ones((IN_TM, ROT_W), F32)], axis=0)
    sin = jnp.concatenate([sin, jnp.zeros((IN_TM, ROT_W), F32)], axis=0)
    return cos, sin


def kernel(x, c, ctx, c_ctx, mod_w, mod_b, norm1_g, w_in, attn_sink, sgu_norm_g, sgu_w, sgu_b, na_rpb,
           group_norm_g, w_out, norm2_g, router_w, router_bias, exp_w_gate, exp_w_up, exp_w_down,
           shared_w_gate, shared_w_up, shared_w_down, final_g):
    xc = jnp.concatenate([x.reshape(N_LAT, D), ctx.reshape(N_CTX, D)], axis=0)
    cc = jnp.concatenate([c, c_ctx[None, :], jnp.zeros((16 - B - 1, D), F32)], axis=0)
    mods_all = _modulation(cc, mod_w, mod_b)
    cos_t, sin_t = _rope_tables()
    seg = np.arange(B_W) // HD
    bd = jnp.asarray((seg[:, None] == seg[None, :]).astype(np.float32) / HD)
    fg = final_g.reshape(1, D)

    out = None
    for l in range(DEPTH):
        last = l == DEPTH - 1
        mods = mods_all[l].reshape(16, 1, 6 * D)
        za, zc, uv = _in_proj(xc, mods, norm1_g[l].reshape(1, D), w_in[l].astype(BF16), cos_t, sin_t)
        ya = _attn_a(za, attn_sink[l], with_ctx=not last)
        yc = _na(zc, _na_bias_table(na_rpb[l]), with_ctx=not last)
        n_tok = N_LAT if last else N_ALL
        sb_tab = jnp.repeat(sgu_b[l].T, HD, axis=1)
        x_new, hx, logits_t = _merge(
            xc, ya, yc, uv, mods, group_norm_g[l].reshape(1, D), sgu_norm_g[l].reshape(1, B_W),
            sgu_w[l].astype(BF16), sb_tab, bd, w_out[l].astype(BF16), norm2_g[l].reshape(1, D),
            router_w[l].T, n_tok)
        res = _moe(x_new, hx, logits_t, mods, router_bias[l].reshape(N_EXP, 1),
                   exp_w_gate[l], exp_w_up[l], exp_w_down[l],
                   shared_w_gate[l].astype(BF16), shared_w_up[l].astype(BF16),
                   shared_w_down[l].astype(BF16), fg, n_tok, final_norm=last)
        if last:
            out = res.reshape(B, S, D)
        else:
            xc = res
    return out
```

```python
import functools

import jax
import jax.numpy as jnp
import numpy as np
from jax import lax
from jax.experimental import pallas as pl
from jax.experimental.pallas import tpu as pltpu

F32 = jnp.float32
BF16 = jnp.bfloat16
I32 = jnp.int32

D = 1024
B = 8
S = 2048
C = 256
DEPTH = 2
GRID_W = 64
HD = 64
A_HEADS = 6
A_KV = 2
A_WIN = 128
A_BLK = 128
ROPE_BASE = 10000.0
SGU_GROUPS = 4
SGU_CHUNK = 128
C_HEADS = 6
C_WIN_R = 8
C_WIN_C = 16
A_W = A_HEADS * HD
B_W = SGU_GROUPS * HD
C_W = C_HEADS * HD
KV_W = A_KV * HD
IN_W = A_W + 2 * KV_W + 2 * B_W + 3 * C_W
N_EXP = 64
TOP_K = 8
N_GRP = 8
TOPK_GRP = 4
EXP_FF = 256
ROUTED_SCALE = 2.5
MOE_BLK = 256
EPS = 1e-6
NEG = -1e30
SCALE = HD ** -0.5

N_LAT = B * S
N_CTX = B * C
N_ALL = N_LAT + N_CTX
ROWS = S // GRID_W

HIGHEST = lax.Precision.HIGHEST
ARB = pltpu.ARBITRARY

NT_DIMS = (((1,), (1,)), ((), ()))


def _cparams(n_axes, vmem_mb=48):
    return pltpu.CompilerParams(
        dimension_semantics=(ARB,) * n_axes, vmem_limit_bytes=vmem_mb * 1024 * 1024)


def _mod_row(i, tm):
    return jnp.where(i < N_LAT // tm, i // (S // tm), B)


def _rms(x):
    return x * lax.rsqrt(jnp.mean(x * x, axis=-1, keepdims=True) + EPS)


MOD_TN = 1024


def _mod_body(cc_ref, w_ref, b_ref, o_ref):
    a = cc_ref[...]
    a = a * jax.nn.sigmoid(a)
    o_ref[...] = jnp.dot(a.astype(BF16), w_ref[...].astype(BF16),
                         preferred_element_type=F32) + b_ref[...]


def _modulation(cc, mod_w, mod_b):
    n_col = 6 * D // MOD_TN
    return pl.pallas_call(
        _mod_body,
        grid=(DEPTH, n_col),
        in_specs=[
            pl.BlockSpec((16, D), lambda l, j: (0, 0)),
            pl.BlockSpec((None, D, MOD_TN), lambda l, j: (l, 0, j)),
            pl.BlockSpec((None, 1, MOD_TN), lambda l, j: (l, 0, j)),
        ],
        out_specs=pl.BlockSpec((None, 16, MOD_TN), lambda l, j: (l, 0, j)),
        out_shape=jax.ShapeDtypeStruct((DEPTH, 16, 6 * D), F32),
        compiler_params=_cparams(2),
        name="modulation",
    )(cc, mod_w, mod_b.reshape(DEPTH, 1, 6 * D))


IN_TM = 256
ROT_W = A_W + KV_W


def _in_body(x_ref, m_ref, g_ref, w_ref, cos_ref, sin_ref, za_ref, zc_ref, uv_ref):
    x = x_ref[...]
    h = _rms(x) * g_ref[...]
    h = h * (1.0 + m_ref[:, D:2 * D]) + m_ref[:, 0:D]
    z = jnp.dot(h.astype(BF16), w_ref[...], preferred_element_type=F32)
    qk = z[:, :ROT_W]
    lane = lax.broadcasted_iota(I32, qk.shape, 1)
    rot = jnp.where((lane & 16) == 0,
                    -pltpu.roll(qk, ROT_W - 16, 1), pltpu.roll(qk, 16, 1))
    qk = qk * cos_ref[...] + rot * sin_ref[...]
    za_ref[...] = jnp.concatenate(
        [qk[:, :A_W] * SCALE, qk[:, A_W:], z[:, ROT_W:ROT_W + KV_W]], axis=1).astype(BF16)
    u0 = ROT_W + KV_W
    uv_ref[...] = jax.nn.gelu(z[:, u0:u0 + 2 * B_W])
    c0 = u0 + 2 * B_W
    zc_ref[...] = jnp.concatenate(
        [z[:, c0:c0 + C_W] * SCALE, z[:, c0 + C_W:]], axis=1).astype(BF16)


def _in_proj(xc, mods, g, w_bf, cos_t, sin_t):
    tm = IN_TM
    n_t = N_ALL // tm

    def tab_idx(i):
        return (jnp.where(i < N_LAT // tm, i % (S // tm), S // tm), 0)

    return pl.pallas_call(
        _in_body,
        grid=(n_t,),
        in_specs=[
            pl.BlockSpec((tm, D), lambda i: (i, 0)),
            pl.BlockSpec((None, 1, 6 * D), lambda i: (_mod_row(i, tm), 0, 0)),
            pl.BlockSpec((1, D), lambda i: (0, 0)),
            pl.BlockSpec((D, IN_W), lambda i: (0, 0)),
            pl.BlockSpec((tm, ROT_W), tab_idx),
            pl.BlockSpec((tm, ROT_W), tab_idx),
        ],
        out_specs=[
            pl.BlockSpec((tm, A_W + 2 * KV_W), lambda i: (i, 0)),
            pl.BlockSpec((tm, 3 * C_W), lambda i: (i, 0)),
            pl.BlockSpec((tm, 2 * B_W), lambda i: (i, 0)),
        ],
        out_shape=[
            jax.ShapeDtypeStruct((N_ALL, A_W + 2 * KV_W), BF16),
            jax.ShapeDtypeStruct((N_ALL, 3 * C_W), BF16),
            jax.ShapeDtypeStruct((N_ALL, 2 * B_W), F32),
        ],
        compiler_params=_cparams(1),
        name="in_proj",
    )(xc, mods, g, w_bf, cos_t, sin_t)


A_BAND = 3 * A_BLK
N_QB_LAT = N_LAT // A_BLK
QB_PER_SEQ = S // A_BLK
QB_PER_CTX = C // A_BLK


def _softmax_pv(s_list, v_list, extra_logit=None):
    m = s_list[0].max(axis=-1, keepdims=True)
    for s in s_list[1:]:
        m = jnp.maximum(m, s.max(axis=-1, keepdims=True))
    if extra_logit is not None:
        m = jnp.maximum(m, extra_logit)
    den = None
    out = None
    for s, v in zip(s_list, v_list):
        p = jnp.exp(s - m)
        d = p.sum(axis=-1, keepdims=True)
        o = jnp.dot(p.astype(BF16), v, preferred_element_type=F32)
        den = d if den is None else den + d
        out = o if out is None else out + o
    if extra_logit is not None:
        den = den + jnp.exp(extra_logit - m)
    return out / den


def _attn_a_body(sink_ref, q_ref, kb_ref, vb_ref, kc_ref, vc_ref, o_ref):
    i = pl.program_id(0)
    is_lat = i < N_QB_LAT
    n = i % QB_PER_SEQ
    start = pl.multiple_of(jnp.clip((n - 1) * A_BLK, 0, S - A_BAND), A_BLK)
    qpos = n * A_BLK + lax.broadcasted_iota(I32, (A_BLK, A_BAND), 0)
    kpos = start + lax.broadcasted_iota(I32, (A_BLK, A_BAND), 1)
    mask = jnp.abs(kpos - qpos) <= jnp.where(is_lat, A_WIN, -1)
    kb = kb_ref[pl.ds(start, A_BAND), :]
    vb = vb_ref[pl.ds(start, A_BAND), :]
    kc = kc_ref[...]
    vc = vc_ref[...]
    outs = []
    for h in range(A_HEADS):
        kv = h // (A_HEADS // A_KV)
        sl = slice(kv * HD, (kv + 1) * HD)
        q = q_ref[:, h * HD:(h + 1) * HD]
        s_b = lax.dot_general(q, kb[:, sl], NT_DIMS, preferred_element_type=F32)
        s_b = jnp.where(mask, s_b, NEG)
        s_c = lax.dot_general(q, kc[:, sl], NT_DIMS, preferred_element_type=F32)
        outs.append(_softmax_pv([s_c, s_b], [vc[:, sl], vb[:, sl]], sink_ref[h]))
    o_ref[...] = jnp.concatenate(outs, axis=1)


def _attn_a(za, sink, with_ctx):
    n_qb = N_QB_LAT + (N_CTX // A_BLK if with_ctx else 0)

    def bidx(i):
        return jnp.where(i < N_QB_LAT, i // QB_PER_SEQ, (i - N_QB_LAT) // QB_PER_CTX)

    k_col = A_W // KV_W
    v_col = k_col + 1
    return pl.pallas_call(
        _attn_a_body,
        grid=(n_qb,),
        in_specs=[
            pl.BlockSpec(memory_space=pltpu.SMEM),
            pl.BlockSpec((A_BLK, A_W), lambda i: (i, 0)),
            pl.BlockSpec((S, KV_W), lambda i: (bidx(i), k_col)),
            pl.BlockSpec((S, KV_W), lambda i: (bidx(i), v_col)),
            pl.BlockSpec((C, KV_W), lambda i: (N_LAT // C + bidx(i), k_col)),
            pl.BlockSpec((C, KV_W), lambda i: (N_LAT // C + bidx(i), v_col)),
        ],
        out_specs=pl.BlockSpec((A_BLK, A_W), lambda i: (i, 0)),
        out_shape=jax.ShapeDtypeStruct((n_qb * A_BLK, A_W), F32),
        compiler_params=_cparams(1),
        name="attn_window",
    )(sink, za, za, za, za, za)


NA_R = 4
NA_TQ = NA_R * GRID_W
NA_WIN = C_WIN_R * GRID_W
NA_STEPS = ROWS // NA_R


def _na_body(q_ref, k_ref, v_ref, kc_ref, vc_ref, tab_ref, o_ref):
    j = pl.program_id(1)
    ctx_penalty = jnp.where(j < NA_STEPS, 0.0, NEG)
    kc = kc_ref[...]
    vc = vc_ref[...]

    def row(rr, carry):
        r = jnp.minimum(j * NA_R + rr, ROWS - 1)
        start = jnp.clip(r - C_WIN_R // 2, 0, ROWS - C_WIN_R)
        base = start - r + (C_WIN_R - 1)
        k0 = pl.multiple_of(start * GRID_W, GRID_W)
        kw = k_ref[pl.ds(k0, NA_WIN), :]
        vw = v_ref[pl.ds(k0, NA_WIN), :]
        q0 = pl.multiple_of(rr * GRID_W, GRID_W)
        q = q_ref[pl.ds(q0, GRID_W), :]
        outs = []
        for h in range(C_HEADS):
            sl = slice(h * HD, (h + 1) * HD)
            qh = q[:, sl]
            s_w = lax.dot_general(qh, kw[:, sl], NT_DIMS, preferred_element_type=F32)
            bias = jnp.concatenate(
                [tab_ref[h, pl.ds(base + 2 * jj, 1)][0] for jj in range(C_WIN_R // 2)],
                axis=1)
            s_w = s_w + bias + ctx_penalty
            s_c = lax.dot_general(qh, kc[:, sl], NT_DIMS, preferred_element_type=F32)
            outs.append(_softmax_pv([s_c, s_w], [vc[:, sl], vw[:, sl]]))
        o_ref[pl.ds(q0, GRID_W), :] = jnp.concatenate(outs, axis=1)
        return carry

    lax.fori_loop(0, NA_R, row, 0)


def _na(zc, tab, with_ctx):
    n_j = NA_STEPS + (1 if with_ctx else 0)

    def qidx(b, j):
        return jnp.where(j < NA_STEPS, b * NA_STEPS + j, N_LAT // NA_TQ + b)

    n_out = N_LAT + (N_CTX if with_ctx else 0)
    return pl.pallas_call(
        _na_body,
        grid=(B, n_j),
        in_specs=[
            pl.BlockSpec((NA_TQ, C_W), lambda b, j: (qidx(b, j), 0)),
            pl.BlockSpec((S, C_W), lambda b, j: (b, 1)),
            pl.BlockSpec((S, C_W), lambda b, j: (b, 2)),
            pl.BlockSpec((C, C_W), lambda b, j: (N_LAT // C + b, 1)),
            pl.BlockSpec((C, C_W), lambda b, j: (N_LAT // C + b, 2)),
            pl.BlockSpec((C_HEADS, 2 * C_WIN_R, GRID_W, 2 * GRID_W), lambda b, j: (0, 0, 0, 0)),
        ],
        out_specs=pl.BlockSpec((NA_TQ, C_W), lambda b, j: (qidx(b, j), 0)),
        out_shape=jax.ShapeDtypeStruct((n_out, C_W), F32),
        compiler_params=_cparams(2),
        name="attn_neighbourhood",
    )(zc, zc, zc, zc, zc, tab)


def _na_bias_table(rpb):
    cq = np.arange(GRID_W)
    col_start = np.clip(cq - C_WIN_C // 2, 0, GRID_W - C_WIN_C)
    col_ok = (cq[None, :] >= col_start[:, None]) & (cq[None, :] < col_start[:, None] + C_WIN_C)
    dc = np.clip(cq[None, :] - cq[:, None], -(C_WIN_C - 1), C_WIN_C - 1) + (C_WIN_C - 1)
    t = rpb.astype(F32)[:, :, dc]
    t = jnp.where(col_ok[None, None], t, NEG)
    n_dr = 2 * C_WIN_R - 1
    lo = t[:, np.minimum(np.arange(2 * C_WIN_R), n_dr - 1)]
    hi = t[:, np.minimum(np.arange(2 * C_WIN_R) + 1, n_dr - 1)]
    return jnp.concatenate([lo, hi], axis=-1)


MG_TM = 256


def _merge_body(x_ref, ya_ref, yc_ref, uv_ref, m_ref, gg_ref, sg_ref, sw_ref, sb_ref, bd_ref,
                wo_ref, g2_ref, rw_ref, xo_ref, hx_ref, lg_ref):
    u = uv_ref[:, :B_W]
    v = uv_ref[:, B_W:]
    ms = jnp.dot(v * v, bd_ref[...], precision=HIGHEST, preferred_element_type=F32)
    vn = (v * lax.rsqrt(ms + EPS) * sg_ref[...]).astype(BF16)
    lane_grp = lax.broadcasted_iota(I32, (SGU_CHUNK, B_W), 1) // HD
    gates = []
    for c in range(MG_TM // SGU_CHUNK):
        vc = vn[c * SGU_CHUNK:(c + 1) * SGU_CHUNK, :]
        gate = sb_ref[...]
        acc = jnp.zeros((SGU_CHUNK, B_W), F32)
        for g in range(SGU_GROUPS):
            r = jnp.dot(sw_ref[g], vc, preferred_element_type=F32)
            acc = jnp.where(lane_grp == g, r, acc)
        gates.append(acc + gate)
    yb = u * jnp.concatenate(gates, axis=0)
    gg = gg_ref[...]
    ycat = jnp.concatenate([
        _rms(ya_ref[...]) * gg[:, :A_W],
        _rms(yb) * gg[:, A_W:A_W + B_W],
        _rms(yc_ref[...]) * gg[:, A_W + B_W:],
    ], axis=1)
    proj = jnp.dot(ycat.astype(BF16), wo_ref[...], preferred_element_type=F32)
    xn = x_ref[...] + m_ref[:, 2 * D:3 * D] * proj
    xo_ref[...] = xn
    hx = _rms(xn) * g2_ref[...]
    hx = hx * (1.0 + m_ref[:, 4 * D:5 * D]) + m_ref[:, 3 * D:4 * D]
    hx_ref[...] = hx
    lg_ref[...] = lax.dot_general(rw_ref[...], hx, NT_DIMS, precision=HIGHEST,
                                  preferred_element_type=F32)


def _merge(xc, ya, yc, uv, mods, gg, sg, sw_bf, sb_tab, bd, wo_bf, g2, rw_t, n_rows):
    tm = MG_TM
    n_t = n_rows // tm
    const2 = lambda i: (0, 0)
    row = lambda i: (i, 0)
    return pl.pallas_call(
        _merge_body,
        grid=(n_t,),
        in_specs=[
            pl.BlockSpec((tm, D), row),
            pl.BlockSpec((tm, A_W), row),
            pl.BlockSpec((tm, C_W), row),
            pl.BlockSpec((tm, 2 * B_W), row),
            pl.BlockSpec((None, 1, 6 * D), lambda i: (_mod_row(i, tm), 0, 0)),
            pl.BlockSpec((1, D), const2),
            pl.BlockSpec((1, B_W), const2),
            pl.BlockSpec((SGU_GROUPS, SGU_CHUNK, SGU_CHUNK), lambda i: (0, 0, 0)),
            pl.BlockSpec((SGU_CHUNK, B_W), const2),
            pl.BlockSpec((B_W, B_W), const2),
            pl.BlockSpec((D, D), const2),
            pl.BlockSpec((1, D), const2),
            pl.BlockSpec((N_EXP, D), const2),
        ],
        out_specs=[
            pl.BlockSpec((tm, D), row),
            pl.BlockSpec((tm, D), row),
            pl.BlockSpec((N_EXP, tm), lambda i: (0, i)),
        ],
        out_shape=[
            jax.ShapeDtypeStruct((n_rows, D), F32),
            jax.ShapeDtypeStruct((n_rows, D), F32),
            jax.ShapeDtypeStruct((N_EXP, n_rows), F32),
        ],
        compiler_params=_cparams(1),
        name="merge",
    )(xc, ya, yc, uv, mods, gg, sg, sw_bf, sb_tab, bd, wo_bf, g2, rw_t)


RT_TM = 256
GRP_SZ = N_EXP // N_GRP


def _first_argmax(v, iota):
    m = v.max(axis=0, keepdims=True)
    idx = jnp.where(v == m, iota, float(v.shape[0])).min(axis=0, keepdims=True)
    return m, idx


def _stack_rows(rows, iota):
    out = jnp.zeros(iota.shape, F32)
    for r, v in enumerate(rows):
        out = jnp.where(iota == float(r), v, out)
    return out


def _route_body(lg_ref, rb_ref, e_ref, w_ref, rk_ref, cnt_ref, base_ref):
    i = pl.program_id(0)

    @pl.when(i == 0)
    def _():
        base_ref[...] = jnp.zeros_like(base_ref)

    tm = RT_TM
    scores = jax.nn.sigmoid(lg_ref[...])
    sel = scores + rb_ref[...]
    iota_g = lax.broadcasted_iota(I32, (GRP_SZ, tm), 0).astype(F32)
    gs = []
    for g in range(N_GRP):
        v = sel[g * GRP_SZ:(g + 1) * GRP_SZ, :]
        m1, i1 = _first_argmax(v, iota_g)
        m2 = jnp.where(iota_g == i1, -jnp.inf, v).max(axis=0, keepdims=True)
        gs.append(m1 + m2)
    iota_n = lax.broadcasted_iota(I32, (N_GRP, tm), 0).astype(F32)
    gscore = _stack_rows(gs, iota_n)
    gsel = jnp.zeros((N_GRP, tm), F32)
    for _ in range(TOPK_GRP):
        _, gi = _first_argmax(gscore, iota_n)
        hit = iota_n == gi
        gsel = jnp.where(hit, 1.0, gsel)
        gscore = jnp.where(hit, -jnp.inf, gscore)
    emask = jnp.concatenate(
        [jnp.broadcast_to(gsel[g:g + 1, :], (GRP_SZ, tm)) for g in range(N_GRP)], axis=0)
    cand = jnp.where(emask > 0.5, sel, NEG)
    iota_e = lax.broadcasted_iota(I32, (N_EXP, tm), 0).astype(F32)
    hits = []
    idxs = []
    ws = []
    member = jnp.zeros((N_EXP, tm), F32)
    for _ in range(TOP_K):
        _, ei = _first_argmax(cand, iota_e)
        hit = iota_e == ei
        hits.append(hit)
        idxs.append(ei)
        ws.append(jnp.where(hit, scores, 0.0).sum(axis=0, keepdims=True))
        member = jnp.where(hit, 1.0, member)
        cand = jnp.where(hit, -jnp.inf, cand)
    wsum = ws[0]
    for w in ws[1:]:
        wsum = wsum + w
    iota_k = lax.broadcasted_iota(I32, (TOP_K, tm), 0).astype(F32)
    e_ref[...] = _stack_rows(idxs, iota_k).astype(I32)
    w_ref[...] = _stack_rows(ws, iota_k) / wsum * ROUTED_SCALE
    r_i = lax.broadcasted_iota(I32, (tm, tm), 0)
    c_i = lax.broadcasted_iota(I32, (tm, tm), 1)
    tri = jnp.where(r_i < c_i, 1.0, 0.0).astype(BF16)
    before = jnp.dot(member.astype(BF16), tri, preferred_element_type=F32) + base_ref[...]
    rk_ref[...] = _stack_rows(
        [jnp.where(hit, before, 0.0).sum(axis=0, keepdims=True) for hit in hits],
        iota_k).astype(I32)
    base_ref[...] = base_ref[...] + member.sum(axis=1, keepdims=True)
    cnt_ref[...] = base_ref[...]


def _route(logits_t, rbias, n_tok):
    tm = RT_TM
    tok = lambda i: (0, i)
    return pl.pallas_call(
        _route_body,
        grid=(n_tok // tm,),
        in_specs=[
            pl.BlockSpec((N_EXP, tm), tok),
            pl.BlockSpec((N_EXP, 1), lambda i: (0, 0)),
        ],
        out_specs=[
            pl.BlockSpec((TOP_K, tm), tok),
            pl.BlockSpec((TOP_K, tm), tok),
            pl.BlockSpec((TOP_K, tm), tok),
            pl.BlockSpec((N_EXP, 1), lambda i: (0, 0)),
        ],
        out_shape=[
            jax.ShapeDtypeStruct((TOP_K, n_tok), I32),
            jax.ShapeDtypeStruct((TOP_K, n_tok), F32),
            jax.ShapeDtypeStruct((TOP_K, n_tok), I32),
            jax.ShapeDtypeStruct((N_EXP, 1), F32),
        ],
        scratch_shapes=[pltpu.VMEM((N_EXP, 1), F32)],
        compiler_params=_cparams(1),
        name="route",
    )(logits_t, rbias)


DP_TM = 256


def _dispatch_body(tails_ref, dest_ref, hx_ref, xs_ref, zero_ref, sem):
    i = pl.program_id(0)

    @pl.when(i == 0)
    def _():
        zero_ref[...] = jnp.zeros_like(zero_ref)

        def fill(e, carry):
            t0 = tails_ref[e]

            @pl.when(t0 >= 0)
            def _():
                cp = pltpu.make_async_copy(
                    zero_ref, xs_ref.at[pl.ds(pl.multiple_of(t0, MOE_BLK), MOE_BLK)], sem)
                cp.start()
                cp.wait()
            return carry

        lax.fori_loop(0, N_EXP, fill, 0)

    def row_copy(t, k):
        return pltpu.make_async_copy(
            hx_ref.at[pl.ds(t, 1)], xs_ref.at[pl.ds(dest_ref[k, t], 1)], sem)

    def issue(t, carry):
        for k in range(TOP_K):
            row_copy(t, k).start()
        return carry

    lax.fori_loop(0, DP_TM, issue, 0)

    def drain(t, carry):
        for k in range(TOP_K):
            row_copy(t, k).wait()
        return carry

    lax.fori_loop(0, DP_TM, drain, 0)


def _dispatch(tails, dest, hx, n_tok, n_slots):
    return pl.pallas_call(
        _dispatch_body,
        grid_spec=pltpu.PrefetchScalarGridSpec(
            num_scalar_prefetch=1,
            grid=(n_tok // DP_TM,),
            in_specs=[
                pl.BlockSpec((TOP_K, DP_TM), lambda i, tails: (0, i), memory_space=pltpu.SMEM),
                pl.BlockSpec((DP_TM, D), lambda i, tails: (i, 0)),
            ],
            out_specs=pl.BlockSpec(memory_space=pl.ANY),
            scratch_shapes=[pltpu.VMEM((MOE_BLK, D), F32), pltpu.SemaphoreType.DMA],
        ),
        out_shape=jax.ShapeDtypeStruct((n_slots, D), F32),
        compiler_params=_cparams(1),
        name="moe_dispatch",
    )(tails, dest, hx)


def _experts_body(bexp_ref, xblk_ref, nused_ref, xs_ref, wg_ref, wu_ref, wd_ref, ys_ref):
    j = pl.program_id(0)

    @pl.when(j < nused_ref[0])
    def _():
        x = xs_ref[...].astype(BF16)
        g = jnp.dot(x, wg_ref[...].astype(BF16), preferred_element_type=F32)
        u = jnp.dot(x, wu_ref[...].astype(BF16), preferred_element_type=F32)
        h = (g * jax.nn.sigmoid(g) * u).astype(BF16)
        ys_ref[...] = jnp.dot(h, wd_ref[...].astype(BF16), preferred_element_type=F32)


def _experts(bexp, xblk, nused, xs, wg, wu, wd, n_blk, layer):
    w_idx = lambda j, be, xb, nu: (layer, be[j], 0, 0)
    return pl.pallas_call(
        _experts_body,
        grid_spec=pltpu.PrefetchScalarGridSpec(
            num_scalar_prefetch=3,
            grid=(n_blk,),
            in_specs=[
                pl.BlockSpec((MOE_BLK, D), lambda j, be, xb, nu: (xb[j], 0)),
                pl.BlockSpec((None, None, D, EXP_FF), w_idx),
                pl.BlockSpec((None, None, D, EXP_FF), w_idx),
                pl.BlockSpec((None, None, EXP_FF, D), w_idx),
            ],
            out_specs=pl.BlockSpec((MOE_BLK, D), lambda j, be, xb, nu: (xb[j], 0)),
        ),
        out_shape=jax.ShapeDtypeStruct((n_blk * MOE_BLK, D), F32),
        compiler_params=_cparams(1),
        name="moe_experts",
    )(bexp, xblk, nused, xs, wg, wu, wd)


CB_TM = 128


def _combine_body(dest_ref, ys_ref, x_ref, hx_ref, w_ref, m_ref, sg_ref, su_ref, sd_ref, fg_ref,
                  o_ref, buf_ref, sem, *, final_norm):
    def row_copy(t, k):
        return pltpu.make_async_copy(
            ys_ref.at[pl.ds(dest_ref[k, t], 1)], buf_ref.at[k, pl.ds(t, 1)], sem)

    def issue(t, carry):
        for k in range(TOP_K):
            row_copy(t, k).start()
        return carry

    lax.fori_loop(0, CB_TM, issue, 0)

    hx = hx_ref[...].astype(BF16)
    g = jnp.dot(hx, sg_ref[...], preferred_element_type=F32)
    u = jnp.dot(hx, su_ref[...], preferred_element_type=F32)
    h = (g * jax.nn.sigmoid(g) * u).astype(BF16)
    y = jnp.dot(h, sd_ref[...], preferred_element_type=F32)

    def drain(t, carry):
        for k in range(TOP_K):
            row_copy(t, k).wait()
        return carry

    lax.fori_loop(0, CB_TM, drain, 0)

    w = w_ref[...]
    for k in range(TOP_K):
        y = y + buf_ref[k] * w[:, k:k + 1]
    out = x_ref[...] + m_ref[:, 5 * D:6 * D] * y
    if final_norm:
        out = _rms(out) * fg_ref[...]
    o_ref[...] = out


def _combine(dest, ys, x, hx, w_rows, mods, sg_bf, su_bf, sd_bf, fg, n_tok, final_norm):
    tm = CB_TM
    row = lambda i: (i, 0)
    const2 = lambda i: (0, 0)
    return pl.pallas_call(
        functools.partial(_combine_body, final_norm=final_norm),
        grid=(n_tok // tm,),
        in_specs=[
            pl.BlockSpec((TOP_K, tm), lambda i: (0, i), memory_space=pltpu.SMEM),
            pl.BlockSpec(memory_space=pl.ANY),
            pl.BlockSpec((tm, D), row),
            pl.BlockSpec((tm, D), row),
            pl.BlockSpec((tm, TOP_K), row),
            pl.BlockSpec((None, 1, 6 * D), lambda i: (_mod_row(i, tm), 0, 0)),
            pl.BlockSpec((D, EXP_FF), const2),
            pl.BlockSpec((D, EXP_FF), const2),
            pl.BlockSpec((EXP_FF, D), const2),
            pl.BlockSpec((1, D), const2),
        ],
        out_specs=pl.BlockSpec((tm, D), row),
        out_shape=jax.ShapeDtypeStruct((n_tok, D), F32),
        scratch_shapes=[pltpu.VMEM((TOP_K, tm, D), F32), pltpu.SemaphoreType.DMA],
        compiler_params=_cparams(1),
        name="moe_combine",
    )(dest, ys, x, hx, w_rows, mods, sg_bf, su_bf, sd_bf, fg)


def _moe(x_new, hx, logits_t, mods, rbias, wg, wu, wd, sg_bf, su_bf, sd_bf, fg, n_tok, layer,
         final_norm):
    eidx, w, rank, counts = _route(logits_t, rbias, n_tok)
    n_blk = n_tok * TOP_K // MOE_BLK + N_EXP
    counts = counts[:, 0].astype(I32)
    nblk_e = (counts + MOE_BLK - 1) // MOE_BLK
    blk_end = jnp.cumsum(nblk_e)
    pstart = (blk_end - nblk_e) * MOE_BLK
    expert_ids = jnp.arange(N_EXP, dtype=I32)
    dest = rank + jnp.sum(
        jnp.where(eidx[..., None] == expert_ids, pstart, 0), axis=-1).astype(I32)
    nused = blk_end[-1:].astype(I32)
    xblk = jnp.minimum(jnp.arange(n_blk, dtype=I32), nused[0] - 1)
    bexp = jnp.minimum(
        jnp.sum(blk_end[None, :] <= xblk[:, None], axis=1), N_EXP - 1).astype(I32)
    tails = jnp.where(nblk_e > 0, pstart + (nblk_e - 1) * MOE_BLK, -1).astype(I32)
    xs = _dispatch(tails, dest, hx, n_tok, n_blk * MOE_BLK)
    ys = _experts(bexp, xblk, nused, xs, wg, wu, wd, n_blk, layer)
    return _combine(dest, ys, x_new, hx, w.T, mods, sg_bf, su_bf, sd_bf, fg, n_tok, final_norm)


def _rope_tables():
    t = np.arange(S)
    row = (t // GRID_W).astype(np.float32)
    col = (t % GRID_W).astype(np.float32)
    half = HD // 2
    inv = jnp.asarray(ROPE_BASE, F32) ** (-jnp.arange(0, half, 2, dtype=F32) / half)
    ang_r = jnp.asarray(row)[:, None] * inv
    ang_c = jnp.asarray(col)[:, None] * inv
    ang = jnp.concatenate([ang_r, ang_r, ang_c, ang_c], axis=-1)
    n_rep = ROT_W // HD
    cos = jnp.tile(jnp.cos(ang), (1, n_rep))
    sin = jnp.tile(jnp.sin(ang), (1, n_rep))
    cos = jnp.concatenate([cos, jnp.ones((IN_TM, ROT_W), F32)], axis=0)
    sin = jnp.concatenate([sin, jnp.zeros((IN_TM, ROT_W), F32)], axis=0)
    return cos, sin


def kernel(x, c, ctx, c_ctx, mod_w, mod_b, norm1_g, w_in, attn_sink, sgu_norm_g, sgu_w, sgu_b, na_rpb,
           group_norm_g, w_out, norm2_g, router_w, router_bias, exp_w_gate, exp_w_up, exp_w_down,
           shared_w_gate, shared_w_up, shared_w_down, final_g):
    xc = jnp.concatenate([x.reshape(N_LAT, D), ctx.reshape(N_CTX, D)], axis=0)
    cc = jnp.concatenate([c, c_ctx[None, :], jnp.zeros((16 - B - 1, D), F32)], axis=0)
    mods_all = _modulation(cc, mod_w, mod_b)
    cos_t, sin_t = _rope_tables()
    seg = np.arange(B_W) // HD
    bd = jnp.asarray((seg[:, None] == seg[None, :]).astype(np.float32) / HD)
    fg = final_g.reshape(1, D)

    out = None
    for l in range(DEPTH):
        last = l == DEPTH - 1
        mods = mods_all[l].reshape(16, 1, 6 * D)
        za, zc, uv = _in_proj(xc, mods, norm1_g[l].reshape(1, D), w_in[l].astype(BF16), cos_t, sin_t)
        ya = _attn_a(za, attn_sink[l], with_ctx=not last)
        yc = _na(zc, _na_bias_table(na_rpb[l]), with_ctx=not last)
        n_tok = N_LAT if last else N_ALL
        sb_tab = jnp.repeat(sgu_b[l].T, HD, axis=1)
        x_new, hx, logits_t = _merge(
            xc, ya, yc, uv, mods, group_norm_g[l].reshape(1, D), sgu_norm_g[l].reshape(1, B_W),
            sgu_w[l].astype(BF16), sb_tab, bd, w_out[l].astype(BF16), norm2_g[l].reshape(1, D),
            router_w[l].T, n_tok)
        res = _moe(x_new, hx, logits_t, mods, router_bias[l].reshape(N_EXP, 1),
                   exp_w_gate, exp_w_up, exp_w_down,
                   shared_w_gate[l].astype(BF16), shared_w_up[l].astype(BF16),
                   shared_w_down[l].astype(BF16), fg, n_tok, layer=l, final_norm=last)
        if last:
            out = res.reshape(B, S, D)
        else:
            xc = res
    return out
```

```python
import functools

import jax
import jax.numpy as jnp
import numpy as np
from jax import lax
from jax.experimental import pallas as pl
from jax.experimental.pallas import tpu as pltpu

F32 = jnp.float32
BF16 = jnp.bfloat16
I32 = jnp.int32
U32 = jnp.uint32

D = 1024
B = 8
S = 2048
C = 256
DEPTH = 2
GRID_W = 64
HD = 64
A_HEADS = 6
A_KV = 2
A_WIN = 128
A_BLK = 128
ROPE_BASE = 10000.0
SGU_GROUPS = 4
SGU_CHUNK = 128
C_HEADS = 6
C_WIN_R = 8
C_WIN_C = 16
A_W = A_HEADS * HD
B_W = SGU_GROUPS * HD
C_W = C_HEADS * HD
KV_W = A_KV * HD
IN_W = A_W + 2 * KV_W + 2 * B_W + 3 * C_W
N_EXP = 64
TOP_K = 8
N_GRP = 8
TOPK_GRP = 4
EXP_FF = 256
ROUTED_SCALE = 2.5
MOE_BLK = 256
EPS = 1e-6
NEG = -1e30
SCALE = HD ** -0.5

N_LAT = B * S
N_CTX = B * C
N_ALL = N_LAT + N_CTX
ROWS = S // GRID_W

HIGHEST = lax.Precision.HIGHEST
ARB = pltpu.ARBITRARY

NT_DIMS = (((1,), (1,)), ((), ()))


def _cparams(n_axes, vmem_mb=48):
    return pltpu.CompilerParams(
        dimension_semantics=(ARB,) * n_axes, vmem_limit_bytes=vmem_mb * 1024 * 1024)


def _mod_row(i, tm):
    return jnp.where(i < N_LAT // tm, i // (S // tm), B)


def _rms(x):
    return x * lax.rsqrt(jnp.mean(x * x, axis=-1, keepdims=True) + EPS)


MOD_TN = 1024


def _mod_body(cc_ref, w_ref, b_ref, o_ref):
    a = cc_ref[...]
    a = a * jax.nn.sigmoid(a)
    o_ref[...] = jnp.dot(a.astype(BF16), w_ref[...].astype(BF16),
                         preferred_element_type=F32) + b_ref[...]


def _modulation(cc, mod_w, mod_b):
    n_col = 6 * D // MOD_TN
    return pl.pallas_call(
        _mod_body,
        grid=(DEPTH, n_col),
        in_specs=[
            pl.BlockSpec((16, D), lambda l, j: (0, 0)),
            pl.BlockSpec((None, D, MOD_TN), lambda l, j: (l, 0, j)),
            pl.BlockSpec((None, 1, MOD_TN), lambda l, j: (l, 0, j)),
        ],
        out_specs=pl.BlockSpec((None, 16, MOD_TN), lambda l, j: (l, 0, j)),
        out_shape=jax.ShapeDtypeStruct((DEPTH, 16, 6 * D), F32),
        compiler_params=_cparams(2),
        name="modulation",
    )(cc, mod_w, mod_b.reshape(DEPTH, 1, 6 * D))


IN_TM = 256
ROT_W = A_W + KV_W


def _in_body(x_ref, m_ref, g_ref, w_ref, cos_ref, sin_ref, za_ref, zc_ref, uv_ref):
    x = x_ref[...]
    h = _rms(x) * g_ref[...]
    h = h * (1.0 + m_ref[:, D:2 * D]) + m_ref[:, 0:D]
    z = jnp.dot(h.astype(BF16), w_ref[...], preferred_element_type=F32)
    qk = z[:, :ROT_W]
    lane = lax.broadcasted_iota(I32, qk.shape, 1)
    rot = jnp.where((lane & 16) == 0,
                    -pltpu.roll(qk, ROT_W - 16, 1), pltpu.roll(qk, 16, 1))
    qk = qk * cos_ref[...] + rot * sin_ref[...]
    za_ref[...] = jnp.concatenate(
        [qk[:, :A_W] * SCALE, qk[:, A_W:], z[:, ROT_W:ROT_W + KV_W]], axis=1).astype(BF16)
    u0 = ROT_W + KV_W
    uv_ref[...] = jax.nn.gelu(z[:, u0:u0 + 2 * B_W])
    c0 = u0 + 2 * B_W
    zc_ref[...] = jnp.concatenate(
        [z[:, c0:c0 + C_W] * SCALE, z[:, c0 + C_W:]], axis=1).astype(BF16)


def _in_proj(xc, mods, g, w_bf, cos_t, sin_t):
    tm = IN_TM
    n_t = N_ALL // tm

    def tab_idx(i):
        return (jnp.where(i < N_LAT // tm, i % (S // tm), S // tm), 0)

    return pl.pallas_call(
        _in_body,
        grid=(n_t,),
        in_specs=[
            pl.BlockSpec((tm, D), lambda i: (i, 0)),
            pl.BlockSpec((None, 1, 6 * D), lambda i: (_mod_row(i, tm), 0, 0)),
            pl.BlockSpec((1, D), lambda i: (0, 0)),
            pl.BlockSpec((D, IN_W), lambda i: (0, 0)),
            pl.BlockSpec((tm, ROT_W), tab_idx),
            pl.BlockSpec((tm, ROT_W), tab_idx),
        ],
        out_specs=[
            pl.BlockSpec((tm, A_W + 2 * KV_W), lambda i: (i, 0)),
            pl.BlockSpec((tm, 3 * C_W), lambda i: (i, 0)),
            pl.BlockSpec((tm, 2 * B_W), lambda i: (i, 0)),
        ],
        out_shape=[
            jax.ShapeDtypeStruct((N_ALL, A_W + 2 * KV_W), BF16),
            jax.ShapeDtypeStruct((N_ALL, 3 * C_W), BF16),
            jax.ShapeDtypeStruct((N_ALL, 2 * B_W), F32),
        ],
        compiler_params=_cparams(1),
        name="in_proj",
    )(xc, mods, g, w_bf, cos_t, sin_t)


A_BAND = 3 * A_BLK
N_QB_LAT = N_LAT // A_BLK
QB_PER_SEQ = S // A_BLK
QB_PER_CTX = C // A_BLK


def _softmax_pv(s_list, v_list, extra_logit=None):
    m = s_list[0].max(axis=-1, keepdims=True)
    for s in s_list[1:]:
        m = jnp.maximum(m, s.max(axis=-1, keepdims=True))
    if extra_logit is not None:
        m = jnp.maximum(m, extra_logit)
    den = None
    out = None
    for s, v in zip(s_list, v_list):
        p = jnp.exp(s - m)
        d = p.sum(axis=-1, keepdims=True)
        o = jnp.dot(p.astype(BF16), v, preferred_element_type=F32)
        den = d if den is None else den + d
        out = o if out is None else out + o
    if extra_logit is not None:
        den = den + jnp.exp(extra_logit - m)
    return out / den


def _attn_a_body(sink_ref, q_ref, kb_ref, vb_ref, kc_ref, vc_ref, o_ref):
    i = pl.program_id(0)
    is_lat = i < N_QB_LAT
    n = i % QB_PER_SEQ
    start = pl.multiple_of(jnp.clip((n - 1) * A_BLK, 0, S - A_BAND), A_BLK)
    qpos = n * A_BLK + lax.broadcasted_iota(I32, (A_BLK, A_BAND), 0)
    kpos = start + lax.broadcasted_iota(I32, (A_BLK, A_BAND), 1)
    mask = jnp.abs(kpos - qpos) <= jnp.where(is_lat, A_WIN, -1)
    kb = kb_ref[pl.ds(start, A_BAND), :]
    vb = vb_ref[pl.ds(start, A_BAND), :]
    kc = kc_ref[...]
    vc = vc_ref[...]
    outs = []
    for h in range(A_HEADS):
        kv = h // (A_HEADS // A_KV)
        sl = slice(kv * HD, (kv + 1) * HD)
        q = q_ref[:, h * HD:(h + 1) * HD]
        s_b = lax.dot_general(q, kb[:, sl], NT_DIMS, preferred_element_type=F32)
        s_b = jnp.where(mask, s_b, NEG)
        s_c = lax.dot_general(q, kc[:, sl], NT_DIMS, preferred_element_type=F32)
        outs.append(_softmax_pv([s_c, s_b], [vc[:, sl], vb[:, sl]], sink_ref[h]))
    o_ref[...] = jnp.concatenate(outs, axis=1)


def _attn_a(za, sink, with_ctx):
    n_qb = N_QB_LAT + (N_CTX // A_BLK if with_ctx else 0)

    def bidx(i):
        return jnp.where(i < N_QB_LAT, i // QB_PER_SEQ, (i - N_QB_LAT) // QB_PER_CTX)

    k_col = A_W // KV_W
    v_col = k_col + 1
    return pl.pallas_call(
        _attn_a_body,
        grid=(n_qb,),
        in_specs=[
            pl.BlockSpec(memory_space=pltpu.SMEM),
            pl.BlockSpec((A_BLK, A_W), lambda i: (i, 0)),
            pl.BlockSpec((S, KV_W), lambda i: (bidx(i), k_col)),
            pl.BlockSpec((S, KV_W), lambda i: (bidx(i), v_col)),
            pl.BlockSpec((C, KV_W), lambda i: (N_LAT // C + bidx(i), k_col)),
            pl.BlockSpec((C, KV_W), lambda i: (N_LAT // C + bidx(i), v_col)),
        ],
        out_specs=pl.BlockSpec((A_BLK, A_W), lambda i: (i, 0)),
        out_shape=jax.ShapeDtypeStruct((n_qb * A_BLK, A_W), F32),
        compiler_params=_cparams(1),
        name="attn_window",
    )(sink, za, za, za, za, za)


NA_R = 4
NA_TQ = NA_R * GRID_W
NA_WIN = C_WIN_R * GRID_W
NA_STEPS = ROWS // NA_R


def _na_body(q_ref, k_ref, v_ref, kc_ref, vc_ref, tab_ref, o_ref):
    j = pl.program_id(1)
    ctx_penalty = jnp.where(j < NA_STEPS, 0.0, NEG)
    kc = kc_ref[...]
    vc = vc_ref[...]

    def row(rr, carry):
        r = jnp.minimum(j * NA_R + rr, ROWS - 1)
        start = jnp.clip(r - C_WIN_R // 2, 0, ROWS - C_WIN_R)
        base = start - r + (C_WIN_R - 1)
        k0 = pl.multiple_of(start * GRID_W, GRID_W)
        kw = k_ref[pl.ds(k0, NA_WIN), :]
        vw = v_ref[pl.ds(k0, NA_WIN), :]
        q0 = pl.multiple_of(rr * GRID_W, GRID_W)
        q = q_ref[pl.ds(q0, GRID_W), :]
        outs = []
        for h in range(C_HEADS):
            sl = slice(h * HD, (h + 1) * HD)
            qh = q[:, sl]
            s_w = lax.dot_general(qh, kw[:, sl], NT_DIMS, preferred_element_type=F32)
            bias = jnp.concatenate(
                [tab_ref[h, pl.ds(base + 2 * jj, 1)][0] for jj in range(C_WIN_R // 2)],
                axis=1)
            s_w = s_w + bias + ctx_penalty
            s_c = lax.dot_general(qh, kc[:, sl], NT_DIMS, preferred_element_type=F32)
            outs.append(_softmax_pv([s_c, s_w], [vc[:, sl], vw[:, sl]]))
        o_ref[pl.ds(q0, GRID_W), :] = jnp.concatenate(outs, axis=1)
        return carry

    lax.fori_loop(0, NA_R, row, 0)


def _na(zc, tab, with_ctx):
    n_j = NA_STEPS + (1 if with_ctx else 0)

    def qidx(b, j):
        return jnp.where(j < NA_STEPS, b * NA_STEPS + j, N_LAT // NA_TQ + b)

    n_out = N_LAT + (N_CTX if with_ctx else 0)
    return pl.pallas_call(
        _na_body,
        grid=(B, n_j),
        in_specs=[
            pl.BlockSpec((NA_TQ, C_W), lambda b, j: (qidx(b, j), 0)),
            pl.BlockSpec((S, C_W), lambda b, j: (b, 1)),
            pl.BlockSpec((S, C_W), lambda b, j: (b, 2)),
            pl.BlockSpec((C, C_W), lambda b, j: (N_LAT // C + b, 1)),
            pl.BlockSpec((C, C_W), lambda b, j: (N_LAT // C + b, 2)),
            pl.BlockSpec((C_HEADS, 2 * C_WIN_R, GRID_W, 2 * GRID_W), lambda b, j: (0, 0, 0, 0)),
        ],
        out_specs=pl.BlockSpec((NA_TQ, C_W), lambda b, j: (qidx(b, j), 0)),
        out_shape=jax.ShapeDtypeStruct((n_out, C_W), F32),
        compiler_params=_cparams(2),
        name="attn_neighbourhood",
    )(zc, zc, zc, zc, zc, tab)


def _na_bias_table(rpb):
    cq = np.arange(GRID_W)
    col_start = np.clip(cq - C_WIN_C // 2, 0, GRID_W - C_WIN_C)
    col_ok = (cq[None, :] >= col_start[:, None]) & (cq[None, :] < col_start[:, None] + C_WIN_C)
    dc = np.clip(cq[None, :] - cq[:, None], -(C_WIN_C - 1), C_WIN_C - 1) + (C_WIN_C - 1)
    t = rpb.astype(F32)[:, :, dc]
    t = jnp.where(col_ok[None, None], t, NEG)
    n_dr = 2 * C_WIN_R - 1
    lo = t[:, np.minimum(np.arange(2 * C_WIN_R), n_dr - 1)]
    hi = t[:, np.minimum(np.arange(2 * C_WIN_R) + 1, n_dr - 1)]
    return jnp.concatenate([lo, hi], axis=-1)


MG_TM = 256


def _merge_body(x_ref, ya_ref, yc_ref, uv_ref, m_ref, gg_ref, sg_ref, sw_ref, sb_ref, bd_ref,
                wo_ref, g2_ref, rw_ref, xo_ref, hx_ref, hp_ref, lg_ref):
    u = uv_ref[:, :B_W]
    v = uv_ref[:, B_W:]
    ms = jnp.dot(v * v, bd_ref[...], precision=HIGHEST, preferred_element_type=F32)
    vn = (v * lax.rsqrt(ms + EPS) * sg_ref[...]).astype(BF16)
    lane_grp = lax.broadcasted_iota(I32, (SGU_CHUNK, B_W), 1) // HD
    gates = []
    for c in range(MG_TM // SGU_CHUNK):
        vc = vn[c * SGU_CHUNK:(c + 1) * SGU_CHUNK, :]
        gate = sb_ref[...]
        acc = jnp.zeros((SGU_CHUNK, B_W), F32)
        for g in range(SGU_GROUPS):
            r = jnp.dot(sw_ref[g], vc, preferred_element_type=F32)
            acc = jnp.where(lane_grp == g, r, acc)
        gates.append(acc + gate)
    yb = u * jnp.concatenate(gates, axis=0)
    gg = gg_ref[...]
    ycat = jnp.concatenate([
        _rms(ya_ref[...]) * gg[:, :A_W],
        _rms(yb) * gg[:, A_W:A_W + B_W],
        _rms(yc_ref[...]) * gg[:, A_W + B_W:],
    ], axis=1)
    proj = jnp.dot(ycat.astype(BF16), wo_ref[...], preferred_element_type=F32)
    xn = x_ref[...] + m_ref[:, 2 * D:3 * D] * proj
    xo_ref[...] = xn
    hx = _rms(xn) * g2_ref[...]
    hx = hx * (1.0 + m_ref[:, 4 * D:5 * D]) + m_ref[:, 3 * D:4 * D]
    hx_ref[...] = hx
    bits = lax.bitcast_convert_type(hx.astype(BF16).astype(F32), U32)
    hp_ref[...] = (bits[:, :D // 2] >> 16) | (bits[:, D // 2:] & jnp.uint32(0xFFFF0000))
    lg_ref[...] = lax.dot_general(rw_ref[...], hx, NT_DIMS, precision=HIGHEST,
                                  preferred_element_type=F32)


def _merge(xc, ya, yc, uv, mods, gg, sg, sw_bf, sb_tab, bd, wo_bf, g2, rw_t, n_rows):
    tm = MG_TM
    n_t = n_rows // tm
    const2 = lambda i: (0, 0)
    row = lambda i: (i, 0)
    return pl.pallas_call(
        _merge_body,
        grid=(n_t,),
        in_specs=[
            pl.BlockSpec((tm, D), row),
            pl.BlockSpec((tm, A_W), row),
            pl.BlockSpec((tm, C_W), row),
            pl.BlockSpec((tm, 2 * B_W), row),
            pl.BlockSpec((None, 1, 6 * D), lambda i: (_mod_row(i, tm), 0, 0)),
            pl.BlockSpec((1, D), const2),
            pl.BlockSpec((1, B_W), const2),
            pl.BlockSpec((SGU_GROUPS, SGU_CHUNK, SGU_CHUNK), lambda i: (0, 0, 0)),
            pl.BlockSpec((SGU_CHUNK, B_W), const2),
            pl.BlockSpec((B_W, B_W), const2),
            pl.BlockSpec((D, D), const2),
            pl.BlockSpec((1, D), const2),
            pl.BlockSpec((N_EXP, D), const2),
        ],
        out_specs=[
            pl.BlockSpec((tm, D), row),
            pl.BlockSpec((tm, D), row),
            pl.BlockSpec((tm, D // 2), row),
            pl.BlockSpec((N_EXP, tm), lambda i: (0, i)),
        ],
        out_shape=[
            jax.ShapeDtypeStruct((n_rows, D), F32),
            jax.ShapeDtypeStruct((n_rows, D), F32),
            jax.ShapeDtypeStruct((n_rows, D // 2), U32),
            jax.ShapeDtypeStruct((N_EXP, n_rows), F32),
        ],
        compiler_params=_cparams(1),
        name="merge",
    )(xc, ya, yc, uv, mods, gg, sg, sw_bf, sb_tab, bd, wo_bf, g2, rw_t)


RT_TM = 256
GRP_SZ = N_EXP // N_GRP


def _first_argmax(v, iota):
    m = v.max(axis=0, keepdims=True)
    idx = jnp.where(v == m, iota, float(v.shape[0])).min(axis=0, keepdims=True)
    return m, idx


def _stack_rows(rows, iota):
    out = jnp.zeros(iota.shape, F32)
    for r, v in enumerate(rows):
        out = jnp.where(iota == float(r), v, out)
    return out


def _route_body(lg_ref, rb_ref, e_ref, w_ref, rk_ref, cnt_ref, base_ref):
    i = pl.program_id(0)

    @pl.when(i == 0)
    def _():
        base_ref[...] = jnp.zeros_like(base_ref)

    tm = RT_TM
    scores = jax.nn.sigmoid(lg_ref[...])
    sel = scores + rb_ref[...]
    iota_g = lax.broadcasted_iota(I32, (GRP_SZ, tm), 0).astype(F32)
    gs = []
    for g in range(N_GRP):
        v = sel[g * GRP_SZ:(g + 1) * GRP_SZ, :]
        m1, i1 = _first_argmax(v, iota_g)
        m2 = jnp.where(iota_g == i1, -jnp.inf, v).max(axis=0, keepdims=True)
        gs.append(m1 + m2)
    iota_n = lax.broadcasted_iota(I32, (N_GRP, tm), 0).astype(F32)
    gscore = _stack_rows(gs, iota_n)
    gsel = jnp.zeros((N_GRP, tm), F32)
    for _ in range(TOPK_GRP):
        _, gi = _first_argmax(gscore, iota_n)
        hit = iota_n == gi
        gsel = jnp.where(hit, 1.0, gsel)
        gscore = jnp.where(hit, -jnp.inf, gscore)
    emask = jnp.concatenate(
        [jnp.broadcast_to(gsel[g:g + 1, :], (GRP_SZ, tm)) for g in range(N_GRP)], axis=0)
    cand = jnp.where(emask > 0.5, sel, NEG)
    iota_e = lax.broadcasted_iota(I32, (N_EXP, tm), 0).astype(F32)
    hits = []
    idxs = []
    ws = []
    member = jnp.zeros((N_EXP, tm), F32)
    for _ in range(TOP_K):
        _, ei = _first_argmax(cand, iota_e)
        hit = iota_e == ei
        hits.append(hit)
        idxs.append(ei)
        ws.append(jnp.where(hit, scores, 0.0).sum(axis=0, keepdims=True))
        member = jnp.where(hit, 1.0, member)
        cand = jnp.where(hit, -jnp.inf, cand)
    wsum = ws[0]
    for w in ws[1:]:
        wsum = wsum + w
    iota_k = lax.broadcasted_iota(I32, (TOP_K, tm), 0).astype(F32)
    e_ref[...] = _stack_rows(idxs, iota_k).astype(I32)
    w_ref[...] = _stack_rows(ws, iota_k) / wsum * ROUTED_SCALE
    r_i = lax.broadcasted_iota(I32, (tm, tm), 0)
    c_i = lax.broadcasted_iota(I32, (tm, tm), 1)
    tri = jnp.where(r_i < c_i, 1.0, 0.0).astype(BF16)
    before = jnp.dot(member.astype(BF16), tri, preferred_element_type=F32) + base_ref[...]
    rk_ref[...] = _stack_rows(
        [jnp.where(hit, before, 0.0).sum(axis=0, keepdims=True) for hit in hits],
        iota_k).astype(I32)
    base_ref[...] = base_ref[...] + member.sum(axis=1, keepdims=True)
    cnt_ref[...] = base_ref[...]


def _route(logits_t, rbias, n_tok):
    tm = RT_TM
    tok = lambda i: (0, i)
    return pl.pallas_call(
        _route_body,
        grid=(n_tok // tm,),
        in_specs=[
            pl.BlockSpec((N_EXP, tm), tok),
            pl.BlockSpec((N_EXP, 1), lambda i: (0, 0)),
        ],
        out_specs=[
            pl.BlockSpec((TOP_K, tm), tok),
            pl.BlockSpec((TOP_K, tm), tok),
            pl.BlockSpec((TOP_K, tm), tok),
            pl.BlockSpec((N_EXP, 1), lambda i: (0, 0)),
        ],
        out_shape=[
            jax.ShapeDtypeStruct((TOP_K, n_tok), I32),
            jax.ShapeDtypeStruct((TOP_K, n_tok), F32),
            jax.ShapeDtypeStruct((TOP_K, n_tok), I32),
            jax.ShapeDtypeStruct((N_EXP, 1), F32),
        ],
        scratch_shapes=[pltpu.VMEM((N_EXP, 1), F32)],
        compiler_params=_cparams(1),
        name="route",
    )(logits_t, rbias)


GATHER_UNROLL = 8


def _experts_body(bexp_ref, xblk_ref, nused_ref, tok_ref, hp_hbm, wg_ref, wu_ref, wd_ref, ys_ref,
                  hp_ref, xbuf_ref, sem):
    j = pl.program_id(0)

    @pl.when(j == 0)
    def _():
        cp = pltpu.make_async_copy(hp_hbm, hp_ref, sem)
        cp.start()
        cp.wait()

    @pl.when(j < nused_ref[0])
    def _():
        def gather(g, carry):
            for q in range(GATHER_UNROLL):
                r = g * GATHER_UNROLL + q
                xbuf_ref[pl.ds(r, 1), :] = hp_ref[pl.ds(tok_ref[0, r], 1), :]
            return carry

        lax.fori_loop(0, MOE_BLK // GATHER_UNROLL, gather, 0)
        bits = xbuf_ref[...]
        lo = lax.bitcast_convert_type(bits << 16, F32)
        hi = lax.bitcast_convert_type(bits & jnp.uint32(0xFFFF0000), F32)
        x = jnp.concatenate([lo, hi], axis=1).astype(BF16)
        g = jnp.dot(x, wg_ref[...].astype(BF16), preferred_element_type=F32)
        u = jnp.dot(x, wu_ref[...].astype(BF16), preferred_element_type=F32)
        h = (g * jax.nn.sigmoid(g) * u).astype(BF16)
        ys_ref[...] = jnp.dot(h, wd_ref[...].astype(BF16), preferred_element_type=F32)

    @pl.when(j >= nused_ref[0])
    def _():
        ys_ref[...] = jnp.zeros_like(ys_ref)


def _experts(bexp, xblk, nused, slot_tok, hx_packed, wg, wu, wd, n_blk, n_tok, layer):
    w_idx = lambda j, be, xb, nu: (layer, be[j], 0, 0)
    return pl.pallas_call(
        _experts_body,
        grid_spec=pltpu.PrefetchScalarGridSpec(
            num_scalar_prefetch=3,
            grid=(n_blk,),
            in_specs=[
                pl.BlockSpec((None, 1, MOE_BLK), lambda j, be, xb, nu: (xb[j], 0, 0),
                             memory_space=pltpu.SMEM),
                pl.BlockSpec(memory_space=pl.ANY),
                pl.BlockSpec((None, None, D, EXP_FF), w_idx),
                pl.BlockSpec((None, None, D, EXP_FF), w_idx),
                pl.BlockSpec((None, None, EXP_FF, D), w_idx),
            ],
            out_specs=pl.BlockSpec((MOE_BLK, D), lambda j, be, xb, nu: (j, 0)),
            scratch_shapes=[
                pltpu.VMEM((n_tok, D // 2), U32),
                pltpu.VMEM((MOE_BLK, D // 2), U32),
                pltpu.SemaphoreType.DMA,
            ],
        ),
        out_shape=jax.ShapeDtypeStruct((n_blk * MOE_BLK, D), F32),
        compiler_params=_cparams(1, vmem_mb=60),
        name="moe_experts",
    )(bexp, xblk, nused, slot_tok.reshape(n_blk, 1, MOE_BLK), hx_packed, wg, wu, wd)


CB_TM = 128


def _combine_body(dest_ref, ys_ref, x_ref, hx_ref, w_ref, m_ref, sg_ref, su_ref, sd_ref, fg_ref,
                  o_ref, buf_ref, sem, *, final_norm):
    def row_copy(t, k):
        return pltpu.make_async_copy(
            ys_ref.at[pl.ds(dest_ref[k, t], 1)], buf_ref.at[k, pl.ds(t, 1)], sem)

    def issue(t, carry):
        for k in range(TOP_K):
            row_copy(t, k).start()
        return carry

    lax.fori_loop(0, CB_TM, issue, 0)

    hx = hx_ref[...].astype(BF16)
    g = jnp.dot(hx, sg_ref[...], preferred_element_type=F32)
    u = jnp.dot(hx, su_ref[...], preferred_element_type=F32)
    h = (g * jax.nn.sigmoid(g) * u).astype(BF16)
    y = jnp.dot(h, sd_ref[...], preferred_element_type=F32)

    def drain(t, carry):
        for k in range(TOP_K):
            row_copy(t, k).wait()
        return carry

    lax.fori_loop(0, CB_TM, drain, 0)

    w = w_ref[...]
    for k in range(TOP_K):
        y = y + buf_ref[k] * w[:, k:k + 1]
    out = x_ref[...] + m_ref[:, 5 * D:6 * D] * y
    if final_norm:
        out = _rms(out) * fg_ref[...]
    o_ref[...] = out


def _combine(dest, ys, x, hx, w_rows, mods, sg_bf, su_bf, sd_bf, fg, n_tok, final_norm):
    tm = CB_TM
    row = lambda i: (i, 0)
    const2 = lambda i: (0, 0)
    return pl.pallas_call(
        functools.partial(_combine_body, final_norm=final_norm),
        grid=(n_tok // tm,),
        in_specs=[
            pl.BlockSpec((TOP_K, tm), lambda i: (0, i), memory_space=pltpu.SMEM),
            pl.BlockSpec(memory_space=pl.ANY),
            pl.BlockSpec((tm, D), row),
            pl.BlockSpec((tm, D), row),
            pl.BlockSpec((tm, TOP_K), row),
            pl.BlockSpec((None, 1, 6 * D), lambda i: (_mod_row(i, tm), 0, 0)),
            pl.BlockSpec((D, EXP_FF), const2),
            pl.BlockSpec((D, EXP_FF), const2),
            pl.BlockSpec((EXP_FF, D), const2),
            pl.BlockSpec((1, D), const2),
        ],
        out_specs=pl.BlockSpec((tm, D), row),
        out_shape=jax.ShapeDtypeStruct((n_tok, D), F32),
        scratch_shapes=[pltpu.VMEM((TOP_K, tm, D), F32), pltpu.SemaphoreType.DMA],
        compiler_params=_cparams(1),
        name="moe_combine",
    )(dest, ys, x, hx, w_rows, mods, sg_bf, su_bf, sd_bf, fg)


def _moe(x_new, hx, hx_packed, logits_t, mods, rbias, wg, wu, wd, sg_bf, su_bf, sd_bf, fg, n_tok,
         layer, final_norm):
    eidx, w, rank, counts = _route(logits_t, rbias, n_tok)
    n_blk = n_tok * TOP_K // MOE_BLK + N_EXP
    counts = counts[:, 0].astype(I32)
    nblk_e = (counts + MOE_BLK - 1) // MOE_BLK
    blk_end = jnp.cumsum(nblk_e)
    pstart = (blk_end - nblk_e) * MOE_BLK
    expert_ids = jnp.arange(N_EXP, dtype=I32)
    dest = rank + jnp.sum(
        jnp.where(eidx[..., None] == expert_ids, pstart, 0), axis=-1).astype(I32)
    nused = blk_end[-1:].astype(I32)
    xblk = jnp.minimum(jnp.arange(n_blk, dtype=I32), nused[0] - 1)
    bexp = jnp.minimum(
        jnp.sum(blk_end[None, :] <= xblk[:, None], axis=1), N_EXP - 1).astype(I32)
    slot_tok = jnp.zeros((n_blk * MOE_BLK,), I32).at[dest.reshape(-1)].set(
        jnp.tile(jnp.arange(n_tok, dtype=I32), TOP_K), unique_indices=True)
    ys = _experts(bexp, xblk, nused, slot_tok, hx_packed, wg, wu, wd, n_blk, n_tok, layer)
    return _combine(dest, ys, x_new, hx, w.T, mods, sg_bf, su_bf, sd_bf, fg, n_tok, final_norm)


def _rope_tables():
    t = np.arange(S)
    row = (t // GRID_W).astype(np.float32)
    col = (t % GRID_W).astype(np.float32)
    half = HD // 2
    inv = jnp.asarray(ROPE_BASE, F32) ** (-jnp.arange(0, half, 2, dtype=F32) / half)
    ang_r = jnp.asarray(row)[:, None] * inv
    ang_c = jnp.asarray(col)[:, None] * inv
    ang = jnp.concatenate([ang_r, ang_r, ang_c, ang_c], axis=-1)
    n_rep = ROT_W // HD
    cos = jnp.tile(jnp.cos(ang), (1, n_rep))
    sin = jnp.tile(jnp.sin(ang), (1, n_rep))
    cos = jnp.concatenate([cos, jnp.ones((IN_TM, ROT_W), F32)], axis=0)
    sin = jnp.concatenate([sin, jnp.zeros((IN_TM, ROT_W), F32)], axis=0)
    return cos, sin


def kernel(x, c, ctx, c_ctx, mod_w, mod_b, norm1_g, w_in, attn_sink, sgu_norm_g, sgu_w, sgu_b, na_rpb,
           group_norm_g, w_out, norm2_g, router_w, router_bias, exp_w_gate, exp_w_up, exp_w_down,
           shared_w_gate, shared_w_up, shared_w_down, final_g):
    xc = jnp.concatenate([x.reshape(N_LAT, D), ctx.reshape(N_CTX, D)], axis=0)
    cc = jnp.concatenate([c, c_ctx[None, :], jnp.zeros((16 - B - 1, D), F32)], axis=0)
    mods_all = _modulation(cc, mod_w, mod_b)
    cos_t, sin_t = _rope_tables()
    seg = np.arange(B_W) // HD
    bd = jnp.asarray((seg[:, None] == seg[None, :]).astype(np.float32) / HD)
    fg = final_g.reshape(1, D)

    out = None
    for l in range(DEPTH):
        last = l == DEPTH - 1
        mods = mods_all[l].reshape(16, 1, 6 * D)
        za, zc, uv = _in_proj(xc, mods, norm1_g[l].reshape(1, D), w_in[l].astype(BF16), cos_t, sin_t)
        ya = _attn_a(za, attn_sink[l], with_ctx=not last)
        yc = _na(zc, _na_bias_table(na_rpb[l]), with_ctx=not last)
        n_tok = N_LAT if last else N_ALL
        sb_tab = jnp.repeat(sgu_b[l].T, HD, axis=1)
        x_new, hx, hx_packed, logits_t = _merge(
            xc, ya, yc, uv, mods, group_norm_g[l].reshape(1, D), sgu_norm_g[l].reshape(1, B_W),
            sgu_w[l].astype(BF16), sb_tab, bd, w_out[l].astype(BF16), norm2_g[l].reshape(1, D),
            router_w[l].T, n_tok)
        res = _moe(x_new, hx, hx_packed, logits_t, mods, router_bias[l].reshape(N_EXP, 1),
                   exp_w_gate, exp_w_up, exp_w_down,
                   shared_w_gate[l].astype(BF16), shared_w_up[l].astype(BF16),
                   shared_w_down[l].astype(BF16), fg, n_tok, layer=l, final_norm=last)
        if last:
            out = res.reshape(B, S, D)
        else:
            xc = res
    return out
```

```python
import functools

import jax
import jax.numpy as jnp
import numpy as np
from jax import lax
from jax.experimental import pallas as pl
from jax.experimental.pallas import tpu as pltpu

F32 = jnp.float32
BF16 = jnp.bfloat16
I32 = jnp.int32
U32 = jnp.uint32

D = 1024
B = 8
S = 2048
C = 256
DEPTH = 2
GRID_W = 64
HD = 64
A_HEADS = 6
A_KV = 2
A_WIN = 128
A_BLK = 128
ROPE_BASE = 10000.0
SGU_GROUPS = 4
SGU_CHUNK = 128
C_HEADS = 6
C_WIN_R = 8
C_WIN_C = 16
A_W = A_HEADS * HD
B_W = SGU_GROUPS * HD
C_W = C_HEADS * HD
KV_W = A_KV * HD
IN_W = A_W + 2 * KV_W + 2 * B_W + 3 * C_W
N_EXP = 64
TOP_K = 8
N_GRP = 8
TOPK_GRP = 4
EXP_FF = 256
ROUTED_SCALE = 2.5
MOE_BLK = 256
EPS = 1e-6
NEG = -1e30
SCALE = HD ** -0.5

N_LAT = B * S
N_CTX = B * C
N_ALL = N_LAT + N_CTX
ROWS = S // GRID_W

HIGHEST = lax.Precision.HIGHEST
ARB = pltpu.ARBITRARY

NT_DIMS = (((1,), (1,)), ((), ()))


def _cparams(n_axes, vmem_mb=48):
    return pltpu.CompilerParams(
        dimension_semantics=(ARB,) * n_axes, vmem_limit_bytes=vmem_mb * 1024 * 1024)


def _mod_row(i, tm):
    return jnp.where(i < N_LAT // tm, i // (S // tm), B)


def _rms(x):
    return x * lax.rsqrt(jnp.mean(x * x, axis=-1, keepdims=True) + EPS)


MOD_TN = 1024


def _mod_body(cc_ref, w_ref, b_ref, o_ref):
    a = cc_ref[...]
    a = a * jax.nn.sigmoid(a)
    o_ref[...] = jnp.dot(a.astype(BF16), w_ref[...].astype(BF16),
                         preferred_element_type=F32) + b_ref[...]


def _modulation(cc, mod_w, mod_b):
    n_col = 6 * D // MOD_TN
    return pl.pallas_call(
        _mod_body,
        grid=(DEPTH, n_col),
        in_specs=[
            pl.BlockSpec((16, D), lambda l, j: (0, 0)),
            pl.BlockSpec((None, D, MOD_TN), lambda l, j: (l, 0, j)),
            pl.BlockSpec((None, 1, MOD_TN), lambda l, j: (l, 0, j)),
        ],
        out_specs=pl.BlockSpec((None, 16, MOD_TN), lambda l, j: (l, 0, j)),
        out_shape=jax.ShapeDtypeStruct((DEPTH, 16, 6 * D), F32),
        compiler_params=_cparams(2),
        name="modulation",
    )(cc, mod_w, mod_b.reshape(DEPTH, 1, 6 * D))


IN_TM = 256
ROT_W = A_W + KV_W


def _in_body(x_ref, m_ref, g_ref, w_ref, cos_ref, sin_ref, za_ref, zc_ref, uv_ref):
    x = x_ref[...]
    h = _rms(x) * g_ref[...]
    h = h * (1.0 + m_ref[:, D:2 * D]) + m_ref[:, 0:D]
    z = jnp.dot(h.astype(BF16), w_ref[...], preferred_element_type=F32)
    qk = z[:, :ROT_W]
    lane = lax.broadcasted_iota(I32, qk.shape, 1)
    rot = jnp.where((lane & 16) == 0,
                    -pltpu.roll(qk, ROT_W - 16, 1), pltpu.roll(qk, 16, 1))
    qk = qk * cos_ref[...] + rot * sin_ref[...]
    za_ref[...] = jnp.concatenate(
        [qk[:, :A_W] * SCALE, qk[:, A_W:], z[:, ROT_W:ROT_W + KV_W]], axis=1).astype(BF16)
    u0 = ROT_W + KV_W
    uv_ref[...] = jax.nn.gelu(z[:, u0:u0 + 2 * B_W])
    c0 = u0 + 2 * B_W
    zc_ref[...] = jnp.concatenate(
        [z[:, c0:c0 + C_W] * SCALE, z[:, c0 + C_W:]], axis=1).astype(BF16)


def _in_proj(xc, mods, g, w_bf, cos_t, sin_t):
    tm = IN_TM
    n_t = N_ALL // tm

    def tab_idx(i):
        return (jnp.where(i < N_LAT // tm, i % (S // tm), S // tm), 0)

    return pl.pallas_call(
        _in_body,
        grid=(n_t,),
        in_specs=[
            pl.BlockSpec((tm, D), lambda i: (i, 0)),
            pl.BlockSpec((None, 1, 6 * D), lambda i: (_mod_row(i, tm), 0, 0)),
            pl.BlockSpec((1, D), lambda i: (0, 0)),
            pl.BlockSpec((D, IN_W), lambda i: (0, 0)),
            pl.BlockSpec((tm, ROT_W), tab_idx),
            pl.BlockSpec((tm, ROT_W), tab_idx),
        ],
        out_specs=[
            pl.BlockSpec((tm, A_W + 2 * KV_W), lambda i: (i, 0)),
            pl.BlockSpec((tm, 3 * C_W), lambda i: (i, 0)),
            pl.BlockSpec((tm, 2 * B_W), lambda i: (i, 0)),
        ],
        out_shape=[
            jax.ShapeDtypeStruct((N_ALL, A_W + 2 * KV_W), BF16),
            jax.ShapeDtypeStruct((N_ALL, 3 * C_W), BF16),
            jax.ShapeDtypeStruct((N_ALL, 2 * B_W), F32),
        ],
        compiler_params=_cparams(1),
        name="in_proj",
    )(xc, mods, g, w_bf, cos_t, sin_t)


A_BAND = 3 * A_BLK
N_QB_LAT = N_LAT // A_BLK
QB_PER_SEQ = S // A_BLK
QB_PER_CTX = C // A_BLK


def _softmax_pv(s_list, v_list, extra_logit=None):
    m = s_list[0].max(axis=-1, keepdims=True)
    for s in s_list[1:]:
        m = jnp.maximum(m, s.max(axis=-1, keepdims=True))
    if extra_logit is not None:
        m = jnp.maximum(m, extra_logit)
    den = None
    out = None
    for s, v in zip(s_list, v_list):
        p = jnp.exp(s - m)
        d = p.sum(axis=-1, keepdims=True)
        o = jnp.dot(p.astype(BF16), v, preferred_element_type=F32)
        den = d if den is None else den + d
        out = o if out is None else out + o
    if extra_logit is not None:
        den = den + jnp.exp(extra_logit - m)
    return out / den


def _attn_a_body(sink_ref, q_ref, kb_ref, vb_ref, kc_ref, vc_ref, o_ref):
    i = pl.program_id(0)
    is_lat = i < N_QB_LAT
    n = i % QB_PER_SEQ
    start = pl.multiple_of(jnp.clip((n - 1) * A_BLK, 0, S - A_BAND), A_BLK)
    qpos = n * A_BLK + lax.broadcasted_iota(I32, (A_BLK, A_BAND), 0)
    kpos = start + lax.broadcasted_iota(I32, (A_BLK, A_BAND), 1)
    mask = jnp.abs(kpos - qpos) <= jnp.where(is_lat, A_WIN, -1)
    kb = kb_ref[pl.ds(start, A_BAND), :]
    vb = vb_ref[pl.ds(start, A_BAND), :]
    kc = kc_ref[...]
    vc = vc_ref[...]
    outs = []
    for h in range(A_HEADS):
        kv = h // (A_HEADS // A_KV)
        sl = slice(kv * HD, (kv + 1) * HD)
        q = q_ref[:, h * HD:(h + 1) * HD]
        s_b = lax.dot_general(q, kb[:, sl], NT_DIMS, preferred_element_type=F32)
        s_b = jnp.where(mask, s_b, NEG)
        s_c = lax.dot_general(q, kc[:, sl], NT_DIMS, preferred_element_type=F32)
        outs.append(_softmax_pv([s_c, s_b], [vc[:, sl], vb[:, sl]], sink_ref[h]))
    o_ref[...] = jnp.concatenate(outs, axis=1)


def _attn_a(za, sink, with_ctx):
    n_qb = N_QB_LAT + (N_CTX // A_BLK if with_ctx else 0)

    def bidx(i):
        return jnp.where(i < N_QB_LAT, i // QB_PER_SEQ, (i - N_QB_LAT) // QB_PER_CTX)

    k_col = A_W // KV_W
    v_col = k_col + 1
    return pl.pallas_call(
        _attn_a_body,
        grid=(n_qb,),
        in_specs=[
            pl.BlockSpec(memory_space=pltpu.SMEM),
            pl.BlockSpec((A_BLK, A_W), lambda i: (i, 0)),
            pl.BlockSpec((S, KV_W), lambda i: (bidx(i), k_col)),
            pl.BlockSpec((S, KV_W), lambda i: (bidx(i), v_col)),
            pl.BlockSpec((C, KV_W), lambda i: (N_LAT // C + bidx(i), k_col)),
            pl.BlockSpec((C, KV_W), lambda i: (N_LAT // C + bidx(i), v_col)),
        ],
        out_specs=pl.BlockSpec((A_BLK, A_W), lambda i: (i, 0)),
        out_shape=jax.ShapeDtypeStruct((n_qb * A_BLK, A_W), F32),
        compiler_params=_cparams(1),
        name="attn_window",
    )(sink, za, za, za, za, za)


NA_R = 4
NA_TQ = NA_R * GRID_W
NA_WIN = C_WIN_R * GRID_W
NA_STEPS = ROWS // NA_R


def _na_body(q_ref, k_ref, v_ref, kc_ref, vc_ref, tab_ref, o_ref):
    j = pl.program_id(1)
    ctx_penalty = jnp.where(j < NA_STEPS, 0.0, NEG)
    kc = kc_ref[...]
    vc = vc_ref[...]

    def row(rr, carry):
        r = jnp.minimum(j * NA_R + rr, ROWS - 1)
        start = jnp.clip(r - C_WIN_R // 2, 0, ROWS - C_WIN_R)
        base = start - r + (C_WIN_R - 1)
        k0 = pl.multiple_of(start * GRID_W, GRID_W)
        kw = k_ref[pl.ds(k0, NA_WIN), :]
        vw = v_ref[pl.ds(k0, NA_WIN), :]
        q0 = pl.multiple_of(rr * GRID_W, GRID_W)
        q = q_ref[pl.ds(q0, GRID_W), :]
        outs = []
        for h in range(C_HEADS):
            sl = slice(h * HD, (h + 1) * HD)
            qh = q[:, sl]
            s_w = lax.dot_general(qh, kw[:, sl], NT_DIMS, preferred_element_type=F32)
            bias = jnp.concatenate(
                [tab_ref[h, pl.ds(base + 2 * jj, 1)][0] for jj in range(C_WIN_R // 2)],
                axis=1)
            s_w = s_w + bias + ctx_penalty
            s_c = lax.dot_general(qh, kc[:, sl], NT_DIMS, preferred_element_type=F32)
            outs.append(_softmax_pv([s_c, s_w], [vc[:, sl], vw[:, sl]]))
        o_ref[pl.ds(q0, GRID_W), :] = jnp.concatenate(outs, axis=1)
        return carry

    lax.fori_loop(0, NA_R, row, 0)


def _na(zc, tab, with_ctx):
    n_j = NA_STEPS + (1 if with_ctx else 0)

    def qidx(b, j):
        return jnp.where(j < NA_STEPS, b * NA_STEPS + j, N_LAT // NA_TQ + b)

    n_out = N_LAT + (N_CTX if with_ctx else 0)
    return pl.pallas_call(
        _na_body,
        grid=(B, n_j),
        in_specs=[
            pl.BlockSpec((NA_TQ, C_W), lambda b, j: (qidx(b, j), 0)),
            pl.BlockSpec((S, C_W), lambda b, j: (b, 1)),
            pl.BlockSpec((S, C_W), lambda b, j: (b, 2)),
            pl.BlockSpec((C, C_W), lambda b, j: (N_LAT // C + b, 1)),
            pl.BlockSpec((C, C_W), lambda b, j: (N_LAT // C + b, 2)),
            pl.BlockSpec((C_HEADS, 2 * C_WIN_R, GRID_W, 2 * GRID_W), lambda b, j: (0, 0, 0, 0)),
        ],
        out_specs=pl.BlockSpec((NA_TQ, C_W), lambda b, j: (qidx(b, j), 0)),
        out_shape=jax.ShapeDtypeStruct((n_out, C_W), F32),
        compiler_params=_cparams(2),
        name="attn_neighbourhood",
    )(zc, zc, zc, zc, zc, tab)


def _na_bias_table(rpb):
    cq = np.arange(GRID_W)
    col_start = np.clip(cq - C_WIN_C // 2, 0, GRID_W - C_WIN_C)
    col_ok = (cq[None, :] >= col_start[:, None]) & (cq[None, :] < col_start[:, None] + C_WIN_C)
    dc = np.clip(cq[None, :] - cq[:, None], -(C_WIN_C - 1), C_WIN_C - 1) + (C_WIN_C - 1)
    t = rpb.astype(F32)[:, :, dc]
    t = jnp.where(col_ok[None, None], t, NEG)
    n_dr = 2 * C_WIN_R - 1
    lo = t[:, np.minimum(np.arange(2 * C_WIN_R), n_dr - 1)]
    hi = t[:, np.minimum(np.arange(2 * C_WIN_R) + 1, n_dr - 1)]
    return jnp.concatenate([lo, hi], axis=-1)


MG_TM = 256


def _merge_body(x_ref, ya_ref, yc_ref, uv_ref, m_ref, gg_ref, sg_ref, sw_ref, sb_ref, bd_ref,
                wo_ref, g2_ref, rw_ref, xo_ref, hx_ref, lg_ref):
    u = uv_ref[:, :B_W]
    v = uv_ref[:, B_W:]
    ms = jnp.dot(v * v, bd_ref[...], precision=HIGHEST, preferred_element_type=F32)
    vn = (v * lax.rsqrt(ms + EPS) * sg_ref[...]).astype(BF16)
    lane_grp = lax.broadcasted_iota(I32, (SGU_CHUNK, B_W), 1) // HD
    gates = []
    for c in range(MG_TM // SGU_CHUNK):
        vc = vn[c * SGU_CHUNK:(c + 1) * SGU_CHUNK, :]
        gate = sb_ref[...]
        acc = jnp.zeros((SGU_CHUNK, B_W), F32)
        for g in range(SGU_GROUPS):
            r = jnp.dot(sw_ref[g], vc, preferred_element_type=F32)
            acc = jnp.where(lane_grp == g, r, acc)
        gates.append(acc + gate)
    yb = u * jnp.concatenate(gates, axis=0)
    gg = gg_ref[...]
    ycat = jnp.concatenate([
        _rms(ya_ref[...]) * gg[:, :A_W],
        _rms(yb) * gg[:, A_W:A_W + B_W],
        _rms(yc_ref[...]) * gg[:, A_W + B_W:],
    ], axis=1)
    proj = jnp.dot(ycat.astype(BF16), wo_ref[...], preferred_element_type=F32)
    xn = x_ref[...] + m_ref[:, 2 * D:3 * D] * proj
    xo_ref[...] = xn
    hx = _rms(xn) * g2_ref[...]
    hx = hx * (1.0 + m_ref[:, 4 * D:5 * D]) + m_ref[:, 3 * D:4 * D]
    hx_ref[...] = hx
    lg_ref[...] = lax.dot_general(rw_ref[...], hx, NT_DIMS, precision=HIGHEST,
                                  preferred_element_type=F32)


def _merge(xc, ya, yc, uv, mods, gg, sg, sw_bf, sb_tab, bd, wo_bf, g2, rw_t, n_rows):
    tm = MG_TM
    n_t = n_rows // tm
    const2 = lambda i: (0, 0)
    row = lambda i: (i, 0)
    return pl.pallas_call(
        _merge_body,
        grid=(n_t,),
        in_specs=[
            pl.BlockSpec((tm, D), row),
            pl.BlockSpec((tm, A_W), row),
            pl.BlockSpec((tm, C_W), row),
            pl.BlockSpec((tm, 2 * B_W), row),
            pl.BlockSpec((None, 1, 6 * D), lambda i: (_mod_row(i, tm), 0, 0)),
            pl.BlockSpec((1, D), const2),
            pl.BlockSpec((1, B_W), const2),
            pl.BlockSpec((SGU_GROUPS, SGU_CHUNK, SGU_CHUNK), lambda i: (0, 0, 0)),
            pl.BlockSpec((SGU_CHUNK, B_W), const2),
            pl.BlockSpec((B_W, B_W), const2),
            pl.BlockSpec((D, D), const2),
            pl.BlockSpec((1, D), const2),
            pl.BlockSpec((N_EXP, D), const2),
        ],
        out_specs=[
            pl.BlockSpec((tm, D), row),
            pl.BlockSpec((tm, D), row),
            pl.BlockSpec((N_EXP, tm), lambda i: (0, i)),
        ],
        out_shape=[
            jax.ShapeDtypeStruct((n_rows, D), F32),
            jax.ShapeDtypeStruct((n_rows, D), F32),
            jax.ShapeDtypeStruct((N_EXP, n_rows), F32),
        ],
        compiler_params=_cparams(1),
        name="merge",
    )(xc, ya, yc, uv, mods, gg, sg, sw_bf, sb_tab, bd, wo_bf, g2, rw_t)


RT_TM = 256
GRP_SZ = N_EXP // N_GRP
SLOT_ALIGN = 8


def _first_argmax(v, iota):
    m = v.max(axis=0, keepdims=True)
    idx = jnp.where(v == m, iota, float(v.shape[0])).min(axis=0, keepdims=True)
    return m, idx


def _stack_rows(rows, iota):
    out = jnp.zeros(iota.shape, F32)
    for r, v in enumerate(rows):
        out = jnp.where(iota == float(r), v, out)
    return out


def _route_body(lg_ref, rb_ref, e_ref, w_ref, rk_ref, lr_ref, tc_ref, cnt_ref, base_ref):
    i = pl.program_id(0)

    @pl.when(i == 0)
    def _():
        base_ref[...] = jnp.zeros_like(base_ref)

    tm = RT_TM
    scores = jax.nn.sigmoid(lg_ref[...])
    sel = scores + rb_ref[...]
    iota_g = lax.broadcasted_iota(I32, (GRP_SZ, tm), 0).astype(F32)
    gs = []
    for g in range(N_GRP):
        v = sel[g * GRP_SZ:(g + 1) * GRP_SZ, :]
        m1, i1 = _first_argmax(v, iota_g)
        m2 = jnp.where(iota_g == i1, -jnp.inf, v).max(axis=0, keepdims=True)
        gs.append(m1 + m2)
    iota_n = lax.broadcasted_iota(I32, (N_GRP, tm), 0).astype(F32)
    gscore = _stack_rows(gs, iota_n)
    gsel = jnp.zeros((N_GRP, tm), F32)
    for _ in range(TOPK_GRP):
        _, gi = _first_argmax(gscore, iota_n)
        hit = iota_n == gi
        gsel = jnp.where(hit, 1.0, gsel)
        gscore = jnp.where(hit, -jnp.inf, gscore)
    emask = jnp.concatenate(
        [jnp.broadcast_to(gsel[g:g + 1, :], (GRP_SZ, tm)) for g in range(N_GRP)], axis=0)
    cand = jnp.where(emask > 0.5, sel, NEG)
    iota_e = lax.broadcasted_iota(I32, (N_EXP, tm), 0).astype(F32)
    hits = []
    idxs = []
    ws = []
    member = jnp.zeros((N_EXP, tm), F32)
    for _ in range(TOP_K):
        _, ei = _first_argmax(cand, iota_e)
        hit = iota_e == ei
        hits.append(hit)
        idxs.append(ei)
        ws.append(jnp.where(hit, scores, 0.0).sum(axis=0, keepdims=True))
        member = jnp.where(hit, 1.0, member)
        cand = jnp.where(hit, -jnp.inf, cand)
    wsum = ws[0]
    for w in ws[1:]:
        wsum = wsum + w
    iota_k = lax.broadcasted_iota(I32, (TOP_K, tm), 0).astype(F32)
    e_ref[...] = _stack_rows(idxs, iota_k).astype(I32)
    w_ref[...] = _stack_rows(ws, iota_k) / wsum * ROUTED_SCALE
    r_i = lax.broadcasted_iota(I32, (tm, tm), 0)
    c_i = lax.broadcasted_iota(I32, (tm, tm), 1)
    tri = jnp.where(r_i < c_i, 1.0, 0.0).astype(BF16)
    local = jnp.dot(member.astype(BF16), tri, preferred_element_type=F32)
    lranks = [jnp.where(hit, local, 0.0).sum(axis=0, keepdims=True) for hit in hits]
    bases = [jnp.where(hit, base_ref[...], 0.0).sum(axis=0, keepdims=True) for hit in hits]
    lr_ref[...] = _stack_rows(lranks, iota_k).astype(I32)
    rk_ref[...] = _stack_rows([a + b for a, b in zip(lranks, bases)], iota_k).astype(I32)
    tile_cnt = member.sum(axis=1, keepdims=True)
    tc_ref[...] = tile_cnt
    base_ref[...] = base_ref[...] + jnp.ceil(tile_cnt / SLOT_ALIGN) * SLOT_ALIGN
    cnt_ref[...] = base_ref[...]


def _route(logits_t, rbias, n_tok):
    tm = RT_TM
    tok = lambda i: (0, i)
    return pl.pallas_call(
        _route_body,
        grid=(n_tok // tm,),
        in_specs=[
            pl.BlockSpec((N_EXP, tm), tok),
            pl.BlockSpec((N_EXP, 1), lambda i: (0, 0)),
        ],
        out_specs=[
            pl.BlockSpec((TOP_K, tm), tok),
            pl.BlockSpec((TOP_K, tm), tok),
            pl.BlockSpec((TOP_K, tm), tok),
            pl.BlockSpec((TOP_K, tm), tok),
            pl.BlockSpec((None, N_EXP, 1), lambda i: (i, 0, 0)),
            pl.BlockSpec((N_EXP, 1), lambda i: (0, 0)),
        ],
        out_shape=[
            jax.ShapeDtypeStruct((TOP_K, n_tok), I32),
            jax.ShapeDtypeStruct((TOP_K, n_tok), F32),
            jax.ShapeDtypeStruct((TOP_K, n_tok), I32),
            jax.ShapeDtypeStruct((TOP_K, n_tok), I32),
            jax.ShapeDtypeStruct((n_tok // tm, N_EXP, 1), F32),
            jax.ShapeDtypeStruct((N_EXP, 1), F32),
        ],
        scratch_shapes=[pltpu.VMEM((N_EXP, 1), F32)],
        compiler_params=_cparams(1),
        name="route",
    )(logits_t, rbias)


DP_TM = RT_TM
DP_CAP = 48
DP_ROWS = N_EXP * DP_CAP


def _dispatch_body(off_ref, cnt_ref, npass_ref, tails_ref, hx_ref, e_ref, lr_ref, xs_ref,
                   zbuf_ref, zero_ref, sem, state_ref):
    i = pl.program_id(0)
    trash_row0 = xs_ref.shape[0] - DP_ROWS

    def chunk_copy(slot, e, dst_row):
        src = zbuf_ref.at[slot, pl.ds(pl.multiple_of(e * DP_CAP, 8), DP_CAP)]
        dst = xs_ref.at[pl.ds(pl.multiple_of(dst_row, SLOT_ALIGN), DP_CAP)]
        return pltpu.make_async_copy(src, dst, sem)

    def drain():
        def wait_one(e, carry):
            chunk_copy(0, 0, 0).wait()
            return carry

        lax.fori_loop(0, N_EXP, wait_one, 0)

    @pl.when(i == 0)
    def _():
        zero_ref[...] = jnp.zeros_like(zero_ref)
        state_ref[0] = 0
        state_ref[1] = 0

        def fill(q, carry):
            t0 = tails_ref[q]

            @pl.when(t0 >= 0)
            def _():
                cp = pltpu.make_async_copy(
                    zero_ref, xs_ref.at[pl.ds(pl.multiple_of(t0, MOE_BLK), MOE_BLK)], sem)
                cp.start()
                cp.wait()
            return carry

        lax.fori_loop(0, 2 * N_EXP, fill, 0)

        def zero_copy(b):
            return pltpu.make_async_copy(
                zero_ref, xs_ref.at[pl.ds(pl.multiple_of(b * MOE_BLK, MOE_BLK), MOE_BLK)], sem)

        def zero_start(b, carry):
            zero_copy(b).start()
            return carry

        def zero_wait(b, carry):
            zero_copy(b).wait()
            return carry

        first_unused = tails_ref[2 * N_EXP]
        lax.fori_loop(first_unused, xs_ref.shape[0] // MOE_BLK, zero_start, 0)
        lax.fori_loop(first_unused, xs_ref.shape[0] // MOE_BLK, zero_wait, 0)

    x = hx_ref[...].astype(BF16)
    eid = e_ref[...]
    lrank = lr_ref[...]
    row_iota = lax.broadcasted_iota(I32, (DP_ROWS, DP_TM), 0)

    def one_pass(p, carry):
        slot = state_ref[0]
        lo = p * DP_CAP
        in_pass = jnp.where(lrank >= lo, jnp.where(lrank < lo + DP_CAP, 1, 0), 0)
        rho = jnp.where(in_pass == 1, eid * DP_CAP + lrank - lo, -1)
        onehot = jnp.zeros((DP_ROWS, DP_TM), F32)
        for k in range(TOP_K):
            onehot = jnp.where(row_iota == rho[k:k + 1, :], 1.0, onehot)
        z = jnp.dot(onehot.astype(BF16), x, preferred_element_type=F32)
        bits = lax.bitcast_convert_type(z, U32)
        zbuf_ref[slot] = (bits[:, :D // 2] >> 16) | (bits[:, D // 2:] & jnp.uint32(0xFFFF0000))

        @pl.when(state_ref[1] == 1)
        def _():
            drain()

        def issue(e, carry2):
            dst = jnp.where(cnt_ref[i * N_EXP + e] > lo, off_ref[i * N_EXP + e] + lo,
                            trash_row0 + e * DP_CAP)
            chunk_copy(slot, e, dst).start()
            return carry2

        lax.fori_loop(0, N_EXP, issue, 0)
        state_ref[0] = 1 - slot
        state_ref[1] = 1
        return carry

    lax.fori_loop(0, npass_ref[i], one_pass, 0)

    @pl.when(jnp.logical_and(i == pl.num_programs(0) - 1, state_ref[1] == 1))
    def _():
        drain()


def _dispatch(off, cnt, npass, tails, hx, eidx, lrank, n_tok, n_slots):
    tok = lambda i, *_: (0, i)
    return pl.pallas_call(
        _dispatch_body,
        grid_spec=pltpu.PrefetchScalarGridSpec(
            num_scalar_prefetch=4,
            grid=(n_tok // DP_TM,),
            in_specs=[
                pl.BlockSpec((DP_TM, D), lambda i, *_: (i, 0)),
                pl.BlockSpec((TOP_K, DP_TM), tok),
                pl.BlockSpec((TOP_K, DP_TM), tok),
            ],
            out_specs=pl.BlockSpec(memory_space=pl.ANY),
            scratch_shapes=[
                pltpu.VMEM((2, DP_ROWS, D // 2), U32),
                pltpu.VMEM((MOE_BLK, D // 2), U32),
                pltpu.SemaphoreType.DMA,
                pltpu.SMEM((2,), I32),
            ],
        ),
        out_shape=jax.ShapeDtypeStruct((n_slots, D // 2), U32),
        compiler_params=_cparams(1, vmem_mb=56),
        name="moe_dispatch",
    )(off, cnt, npass, tails, hx, eidx, lrank)


def _experts_body(bexp_ref, xblk_ref, nused_ref, xs_ref, wg_ref, wu_ref, wd_ref, ys_ref):
    j = pl.program_id(0)

    @pl.when(j < nused_ref[0])
    def _():
        bits = xs_ref[...]
        lo = lax.bitcast_convert_type(bits << 16, F32)
        hi = lax.bitcast_convert_type(bits & jnp.uint32(0xFFFF0000), F32)
        x = jnp.concatenate([lo, hi], axis=1).astype(BF16)
        g = jnp.dot(x, wg_ref[...].astype(BF16), preferred_element_type=F32)
        u = jnp.dot(x, wu_ref[...].astype(BF16), preferred_element_type=F32)
        h = (g * jax.nn.sigmoid(g) * u).astype(BF16)
        ys_ref[...] = jnp.dot(h, wd_ref[...].astype(BF16), preferred_element_type=F32)

    @pl.when(j >= nused_ref[0])
    def _():
        ys_ref[...] = jnp.zeros_like(ys_ref)


def _experts(bexp, xblk, nused, xs, wg, wu, wd, n_blk, layer):
    w_idx = lambda j, be, xb, nu: (layer, be[j], 0, 0)
    return pl.pallas_call(
        _experts_body,
        grid_spec=pltpu.PrefetchScalarGridSpec(
            num_scalar_prefetch=3,
            grid=(n_blk,),
            in_specs=[
                pl.BlockSpec((MOE_BLK, D // 2), lambda j, be, xb, nu: (xb[j], 0)),
                pl.BlockSpec((None, None, D, EXP_FF), w_idx),
                pl.BlockSpec((None, None, D, EXP_FF), w_idx),
                pl.BlockSpec((None, None, EXP_FF, D), w_idx),
            ],
            out_specs=pl.BlockSpec((MOE_BLK, D), lambda j, be, xb, nu: (j, 0)),
        ),
        out_shape=jax.ShapeDtypeStruct((n_blk * MOE_BLK, D), F32),
        compiler_params=_cparams(1),
        name="moe_experts",
    )(bexp, xblk, nused, xs, wg, wu, wd)


CB_TM = 128


def _combine_body(dest_ref, ys_ref, x_ref, hx_ref, w_ref, m_ref, sg_ref, su_ref, sd_ref, fg_ref,
                  o_ref, buf_ref, sem, *, final_norm):
    def row_copy(t, k):
        return pltpu.make_async_copy(
            ys_ref.at[pl.ds(dest_ref[k, t], 1)], buf_ref.at[k, pl.ds(t, 1)], sem)

    def issue(t, carry):
        for k in range(TOP_K):
            row_copy(t, k).start()
        return carry

    lax.fori_loop(0, CB_TM, issue, 0)

    hx = hx_ref[...].astype(BF16)
    g = jnp.dot(hx, sg_ref[...], preferred_element_type=F32)
    u = jnp.dot(hx, su_ref[...], preferred_element_type=F32)
    h = (g * jax.nn.sigmoid(g) * u).astype(BF16)
    y = jnp.dot(h, sd_ref[...], preferred_element_type=F32)

    def drain(t, carry):
        for k in range(TOP_K):
            row_copy(t, k).wait()
        return carry

    lax.fori_loop(0, CB_TM, drain, 0)

    w = w_ref[...]
    for k in range(TOP_K):
        y = y + buf_ref[k] * w[:, k:k + 1]
    out = x_ref[...] + m_ref[:, 5 * D:6 * D] * y
    if final_norm:
        out = _rms(out) * fg_ref[...]
    o_ref[...] = out


def _combine(dest, ys, x, hx, w_rows, mods, sg_bf, su_bf, sd_bf, fg, n_tok, final_norm):
    tm = CB_TM
    row = lambda i: (i, 0)
    const2 = lambda i: (0, 0)
    return pl.pallas_call(
        functools.partial(_combine_body, final_norm=final_norm),
        grid=(n_tok // tm,),
        in_specs=[
            pl.BlockSpec((TOP_K, tm), lambda i: (0, i), memory_space=pltpu.SMEM),
            pl.BlockSpec(memory_space=pl.ANY),
            pl.BlockSpec((tm, D), row),
            pl.BlockSpec((tm, D), row),
            pl.BlockSpec((tm, TOP_K), row),
            pl.BlockSpec((None, 1, 6 * D), lambda i: (_mod_row(i, tm), 0, 0)),
            pl.BlockSpec((D, EXP_FF), const2),
            pl.BlockSpec((D, EXP_FF), const2),
            pl.BlockSpec((EXP_FF, D), const2),
            pl.BlockSpec((1, D), const2),
        ],
        out_specs=pl.BlockSpec((tm, D), row),
        out_shape=jax.ShapeDtypeStruct((n_tok, D), F32),
        scratch_shapes=[pltpu.VMEM((TOP_K, tm, D), F32), pltpu.SemaphoreType.DMA],
        compiler_params=_cparams(1),
        name="moe_combine",
    )(dest, ys, x, hx, w_rows, mods, sg_bf, su_bf, sd_bf, fg)


def _moe(x_new, hx, logits_t, mods, rbias, wg, wu, wd, sg_bf, su_bf, sd_bf, fg, n_tok, layer,
         final_norm):
    eidx, w, rank, lrank, tile_cnt, counts = _route(logits_t, rbias, n_tok)
    pad_blk = N_EXP * DP_CAP // MOE_BLK
    n_tiles = n_tok // RT_TM
    align_blk = pl.cdiv((SLOT_ALIGN - 1) * N_EXP * n_tiles, MOE_BLK)
    n_blk = n_tok * TOP_K // MOE_BLK + N_EXP + pad_blk + align_blk
    counts = counts[:, 0].astype(I32)
    tcnt = tile_cnt[:, :, 0].astype(I32)
    tcnt_al = (tcnt + SLOT_ALIGN - 1) // SLOT_ALIGN * SLOT_ALIGN
    nblk_e = (counts + DP_CAP + MOE_BLK - 1) // MOE_BLK
    blk_end = jnp.cumsum(nblk_e)
    pstart = (blk_end - nblk_e) * MOE_BLK
    expert_ids = jnp.arange(N_EXP, dtype=I32)
    dest = rank + jnp.sum(
        jnp.where(eidx[..., None] == expert_ids, pstart, 0), axis=-1).astype(I32)
    nused = blk_end[-1:].astype(I32)
    xblk = jnp.minimum(jnp.arange(n_blk, dtype=I32), nused[0] - 1)
    bexp = jnp.minimum(
        jnp.sum(blk_end[None, :] <= xblk[:, None], axis=1), N_EXP - 1).astype(I32)
    off = (pstart[None, :] + jnp.cumsum(tcnt_al, axis=0) - tcnt_al).reshape(-1).astype(I32)
    npass = ((jnp.max(tcnt, axis=1) + DP_CAP - 1) // DP_CAP).astype(I32)
    last_blk = pstart + (nblk_e - 1) * MOE_BLK
    tails = jnp.concatenate(
        [last_blk, jnp.where(nblk_e > 1, last_blk - MOE_BLK, -1), nused]).astype(I32)
    xs = _dispatch(off, tcnt.reshape(-1), npass, tails, hx, eidx, lrank, n_tok,
                   (n_blk + pad_blk) * MOE_BLK)
    ys = _experts(bexp, xblk, nused, xs, wg, wu, wd, n_blk, layer)
    return _combine(dest, ys, x_new, hx, w.T, mods, sg_bf, su_bf, sd_bf, fg, n_tok, final_norm)


def _rope_tables():
    t = np.arange(S)
    row = (t // GRID_W).astype(np.float32)
    col = (t % GRID_W).astype(np.float32)
    half = HD // 2
    inv = jnp.asarray(ROPE_BASE, F32) ** (-jnp.arange(0, half, 2, dtype=F32) / half)
    ang_r = jnp.asarray(row)[:, None] * inv
    ang_c = jnp.asarray(col)[:, None] * inv
    ang = jnp.concatenate([ang_r, ang_r, ang_c, ang_c], axis=-1)
    n_rep = ROT_W // HD
    cos = jnp.tile(jnp.cos(ang), (1, n_rep))
    sin = jnp.tile(jnp.sin(ang), (1, n_rep))
    cos = jnp.concatenate([cos, jnp.ones((IN_TM, ROT_W), F32)], axis=0)
    sin = jnp.concatenate([sin, jnp.zeros((IN_TM, ROT_W), F32)], axis=0)
    return cos, sin


def kernel(x, c, ctx, c_ctx, mod_w, mod_b, norm1_g, w_in, attn_sink, sgu_norm_g, sgu_w, sgu_b, na_rpb,
           group_norm_g, w_out, norm2_g, router_w, router_bias, exp_w_gate, exp_w_up, exp_w_down,
           shared_w_gate, shared_w_up, shared_w_down, final_g):
    xc = jnp.concatenate([x.reshape(N_LAT, D), ctx.reshape(N_CTX, D)], axis=0)
    cc = jnp.concatenate([c, c_ctx[None, :], jnp.zeros((16 - B - 1, D), F32)], axis=0)
    mods_all = _modulation(cc, mod_w, mod_b)
    cos_t, sin_t = _rope_tables()
    seg = np.arange(B_W) // HD
    bd = jnp.asarray((seg[:, None] == seg[None, :]).astype(np.float32) / HD)
    fg = final_g.reshape(1, D)

    out = None
    for l in range(DEPTH):
        last = l == DEPTH - 1
        mods = mods_all[l].reshape(16, 1, 6 * D)
        za, zc, uv = _in_proj(xc, mods, norm1_g[l].reshape(1, D), w_in[l].astype(BF16), cos_t, sin_t)
        ya = _attn_a(za, attn_sink[l], with_ctx=not last)
        yc = _na(zc, _na_bias_table(na_rpb[l]), with_ctx=not last)
        n_tok = N_LAT if last else N_ALL
        sb_tab = jnp.repeat(sgu_b[l].T, HD, axis=1)
        x_new, hx, logits_t = _merge(
            xc, ya, yc, uv, mods, group_norm_g[l].reshape(1, D), sgu_norm_g[l].reshape(1, B_W),
            sgu_w[l].astype(BF16), sb_tab, bd, w_out[l].astype(BF16), norm2_g[l].reshape(1, D),
            router_w[l].T, n_tok)
        res = _moe(x_new, hx, logits_t, mods, router_bias[l].reshape(N_EXP, 1),
                   exp_w_gate, exp_w_up, exp_w_down,
                   shared_w_gate[l].astype(BF16), shared_w_up[l].astype(BF16),
                   shared_w_down[l].astype(BF16), fg, n_tok, layer=l, final_norm=last)
        if last:
            out = res.reshape(B, S, D)
        else:
            xc = res
    return out
```

```python
import functools

import jax
import jax.numpy as jnp
import numpy as np
from jax import lax
from jax.experimental import pallas as pl
from jax.experimental.pallas import tpu as pltpu

F32 = jnp.float32
BF16 = jnp.bfloat16
I32 = jnp.int32
U32 = jnp.uint32

D = 1024
B = 8
S = 2048
C = 256
DEPTH = 2
GRID_W = 64
HD = 64
A_HEADS = 6
A_KV = 2
A_WIN = 128
A_BLK = 128
ROPE_BASE = 10000.0
SGU_GROUPS = 4
SGU_CHUNK = 128
C_HEADS = 6
C_WIN_R = 8
C_WIN_C = 16
A_W = A_HEADS * HD
B_W = SGU_GROUPS * HD
C_W = C_HEADS * HD
KV_W = A_KV * HD
IN_W = A_W + 2 * KV_W + 2 * B_W + 3 * C_W
N_EXP = 64
TOP_K = 8
N_GRP = 8
TOPK_GRP = 4
EXP_FF = 256
ROUTED_SCALE = 2.5
MOE_BLK = 256
EPS = 1e-6
NEG = -1e30
SCALE = HD ** -0.5

N_LAT = B * S
N_CTX = B * C
N_ALL = N_LAT + N_CTX
ROWS = S // GRID_W

HIGHEST = lax.Precision.HIGHEST
ARB = pltpu.ARBITRARY

NT_DIMS = (((1,), (1,)), ((), ()))


def _cparams(n_axes, vmem_mb=48):
    return pltpu.CompilerParams(
        dimension_semantics=(ARB,) * n_axes, vmem_limit_bytes=vmem_mb * 1024 * 1024)


def _mod_row(i, tm):
    return jnp.where(i < N_LAT // tm, i // (S // tm), B)


def _rms(x):
    return x * lax.rsqrt(jnp.mean(x * x, axis=-1, keepdims=True) + EPS)


MOD_TN = 1024


def _mod_body(cc_ref, w_ref, b_ref, o_ref):
    a = cc_ref[...]
    a = a * jax.nn.sigmoid(a)
    o_ref[...] = jnp.dot(a.astype(BF16), w_ref[...].astype(BF16),
                         preferred_element_type=F32) + b_ref[...]


def _modulation(cc, mod_w, mod_b):
    n_col = 6 * D // MOD_TN
    return pl.pallas_call(
        _mod_body,
        grid=(DEPTH, n_col),
        in_specs=[
            pl.BlockSpec((16, D), lambda l, j: (0, 0)),
            pl.BlockSpec((None, D, MOD_TN), lambda l, j: (l, 0, j)),
            pl.BlockSpec((None, 1, MOD_TN), lambda l, j: (l, 0, j)),
        ],
        out_specs=pl.BlockSpec((None, 16, MOD_TN), lambda l, j: (l, 0, j)),
        out_shape=jax.ShapeDtypeStruct((DEPTH, 16, 6 * D), F32),
        compiler_params=_cparams(2),
        name="modulation",
    )(cc, mod_w, mod_b.reshape(DEPTH, 1, 6 * D))


IN_TM = 256
ROT_W = A_W + KV_W


def _in_body(x_ref, m_ref, g_ref, w_ref, cos_ref, sin_ref, za_ref, zc_ref, uv_ref):
    x = x_ref[...]
    h = _rms(x) * g_ref[...]
    h = h * (1.0 + m_ref[:, D:2 * D]) + m_ref[:, 0:D]
    z = jnp.dot(h.astype(BF16), w_ref[...], preferred_element_type=F32)
    qk = z[:, :ROT_W]
    lane = lax.broadcasted_iota(I32, qk.shape, 1)
    rot = jnp.where((lane & 16) == 0,
                    -pltpu.roll(qk, ROT_W - 16, 1), pltpu.roll(qk, 16, 1))
    qk = qk * cos_ref[...] + rot * sin_ref[...]
    za_ref[...] = jnp.concatenate(
        [qk[:, :A_W] * SCALE, qk[:, A_W:], z[:, ROT_W:ROT_W + KV_W]], axis=1).astype(BF16)
    u0 = ROT_W + KV_W
    uv_ref[...] = jax.nn.gelu(z[:, u0:u0 + 2 * B_W])
    c0 = u0 + 2 * B_W
    zc_ref[...] = jnp.concatenate(
        [z[:, c0:c0 + C_W] * SCALE, z[:, c0 + C_W:]], axis=1).astype(BF16)


def _in_proj(xc, mods, g, w_bf, cos_t, sin_t):
    tm = IN_TM
    n_t = N_ALL // tm

    def tab_idx(i):
        return (jnp.where(i < N_LAT // tm, i % (S // tm), S // tm), 0)

    return pl.pallas_call(
        _in_body,
        grid=(n_t,),
        in_specs=[
            pl.BlockSpec((tm, D), lambda i: (i, 0)),
            pl.BlockSpec((None, 1, 6 * D), lambda i: (_mod_row(i, tm), 0, 0)),
            pl.BlockSpec((1, D), lambda i: (0, 0)),
            pl.BlockSpec((D, IN_W), lambda i: (0, 0)),
            pl.BlockSpec((tm, ROT_W), tab_idx),
            pl.BlockSpec((tm, ROT_W), tab_idx),
        ],
        out_specs=[
            pl.BlockSpec((tm, A_W + 2 * KV_W), lambda i: (i, 0)),
            pl.BlockSpec((tm, 3 * C_W), lambda i: (i, 0)),
            pl.BlockSpec((tm, 2 * B_W), lambda i: (i, 0)),
        ],
        out_shape=[
            jax.ShapeDtypeStruct((N_ALL, A_W + 2 * KV_W), BF16),
            jax.ShapeDtypeStruct((N_ALL, 3 * C_W), BF16),
            jax.ShapeDtypeStruct((N_ALL, 2 * B_W), F32),
        ],
        compiler_params=_cparams(1),
        name="in_proj",
    )(xc, mods, g, w_bf, cos_t, sin_t)


A_BAND = 3 * A_BLK
N_QB_LAT = N_LAT // A_BLK
QB_PER_SEQ = S // A_BLK
QB_PER_CTX = C // A_BLK


def _softmax_pv(s_list, v_list, extra_logit=None):
    m = s_list[0].max(axis=-1, keepdims=True)
    for s in s_list[1:]:
        m = jnp.maximum(m, s.max(axis=-1, keepdims=True))
    if extra_logit is not None:
        m = jnp.maximum(m, extra_logit)
    den = None
    out = None
    for s, v in zip(s_list, v_list):
        p = jnp.exp(s - m)
        d = p.sum(axis=-1, keepdims=True)
        o = jnp.dot(p.astype(BF16), v, preferred_element_type=F32)
        den = d if den is None else den + d
        out = o if out is None else out + o
    if extra_logit is not None:
        den = den + jnp.exp(extra_logit - m)
    return out / den


def _attn_a_body(sink_ref, q_ref, kb_ref, vb_ref, kc_ref, vc_ref, o_ref):
    i = pl.program_id(0)
    is_lat = i < N_QB_LAT
    n = i % QB_PER_SEQ
    start = pl.multiple_of(jnp.clip((n - 1) * A_BLK, 0, S - A_BAND), A_BLK)
    qpos = n * A_BLK + lax.broadcasted_iota(I32, (A_BLK, A_BAND), 0)
    kpos = start + lax.broadcasted_iota(I32, (A_BLK, A_BAND), 1)
    mask = jnp.abs(kpos - qpos) <= jnp.where(is_lat, A_WIN, -1)
    kb = kb_ref[pl.ds(start, A_BAND), :]
    vb = vb_ref[pl.ds(start, A_BAND), :]
    kc = kc_ref[...]
    vc = vc_ref[...]
    outs = []
    for h in range(A_HEADS):
        kv = h // (A_HEADS // A_KV)
        sl = slice(kv * HD, (kv + 1) * HD)
        q = q_ref[:, h * HD:(h + 1) * HD]
        s_b = lax.dot_general(q, kb[:, sl], NT_DIMS, preferred_element_type=F32)
        s_b = jnp.where(mask, s_b, NEG)
        s_c = lax.dot_general(q, kc[:, sl], NT_DIMS, preferred_element_type=F32)
        outs.append(_softmax_pv([s_c, s_b], [vc[:, sl], vb[:, sl]], sink_ref[h]))
    o_ref[...] = jnp.concatenate(outs, axis=1)


def _attn_a(za, sink, with_ctx):
    n_qb = N_QB_LAT + (N_CTX // A_BLK if with_ctx else 0)

    def bidx(i):
        return jnp.where(i < N_QB_LAT, i // QB_PER_SEQ, (i - N_QB_LAT) // QB_PER_CTX)

    k_col = A_W // KV_W
    v_col = k_col + 1
    return pl.pallas_call(
        _attn_a_body,
        grid=(n_qb,),
        in_specs=[
            pl.BlockSpec(memory_space=pltpu.SMEM),
            pl.BlockSpec((A_BLK, A_W), lambda i: (i, 0)),
            pl.BlockSpec((S, KV_W), lambda i: (bidx(i), k_col)),
            pl.BlockSpec((S, KV_W), lambda i: (bidx(i), v_col)),
            pl.BlockSpec((C, KV_W), lambda i: (N_LAT // C + bidx(i), k_col)),
            pl.BlockSpec((C, KV_W), lambda i: (N_LAT // C + bidx(i), v_col)),
        ],
        out_specs=pl.BlockSpec((A_BLK, A_W), lambda i: (i, 0)),
        out_shape=jax.ShapeDtypeStruct((n_qb * A_BLK, A_W), F32),
        compiler_params=_cparams(1),
        name="attn_window",
    )(sink, za, za, za, za, za)


NA_R = 4
NA_TQ = NA_R * GRID_W
NA_WIN = C_WIN_R * GRID_W
NA_STEPS = ROWS // NA_R


def _na_body(q_ref, k_ref, v_ref, kc_ref, vc_ref, tab_ref, o_ref):
    j = pl.program_id(1)
    ctx_penalty = jnp.where(j < NA_STEPS, 0.0, NEG)
    kc = kc_ref[...]
    vc = vc_ref[...]

    def row(rr, carry):
        r = jnp.minimum(j * NA_R + rr, ROWS - 1)
        start = jnp.clip(r - C_WIN_R // 2, 0, ROWS - C_WIN_R)
        base = start - r + (C_WIN_R - 1)
        k0 = pl.multiple_of(start * GRID_W, GRID_W)
        kw = k_ref[pl.ds(k0, NA_WIN), :]
        vw = v_ref[pl.ds(k0, NA_WIN), :]
        q0 = pl.multiple_of(rr * GRID_W, GRID_W)
        q = q_ref[pl.ds(q0, GRID_W), :]
        outs = []
        for h in range(C_HEADS):
            sl = slice(h * HD, (h + 1) * HD)
            qh = q[:, sl]
            s_w = lax.dot_general(qh, kw[:, sl], NT_DIMS, preferred_element_type=F32)
            bias = jnp.concatenate(
                [tab_ref[h, pl.ds(base + 2 * jj, 1)][0] for jj in range(C_WIN_R // 2)],
                axis=1)
            s_w = s_w + bias + ctx_penalty
            s_c = lax.dot_general(qh, kc[:, sl], NT_DIMS, preferred_element_type=F32)
            outs.append(_softmax_pv([s_c, s_w], [vc[:, sl], vw[:, sl]]))
        o_ref[pl.ds(q0, GRID_W), :] = jnp.concatenate(outs, axis=1)
        return carry

    lax.fori_loop(0, NA_R, row, 0)


def _na(zc, tab, with_ctx):
    n_j = NA_STEPS + (1 if with_ctx else 0)

    def qidx(b, j):
        return jnp.where(j < NA_STEPS, b * NA_STEPS + j, N_LAT // NA_TQ + b)

    n_out = N_LAT + (N_CTX if with_ctx else 0)
    return pl.pallas_call(
        _na_body,
        grid=(B, n_j),
        in_specs=[
            pl.BlockSpec((NA_TQ, C_W), lambda b, j: (qidx(b, j), 0)),
            pl.BlockSpec((S, C_W), lambda b, j: (b, 1)),
            pl.BlockSpec((S, C_W), lambda b, j: (b, 2)),
            pl.BlockSpec((C, C_W), lambda b, j: (N_LAT // C + b, 1)),
            pl.BlockSpec((C, C_W), lambda b, j: (N_LAT // C + b, 2)),
            pl.BlockSpec((C_HEADS, 2 * C_WIN_R, GRID_W, 2 * GRID_W), lambda b, j: (0, 0, 0, 0)),
        ],
        out_specs=pl.BlockSpec((NA_TQ, C_W), lambda b, j: (qidx(b, j), 0)),
        out_shape=jax.ShapeDtypeStruct((n_out, C_W), F32),
        compiler_params=_cparams(2),
        name="attn_neighbourhood",
    )(zc, zc, zc, zc, zc, tab)


def _na_bias_table(rpb):
    cq = np.arange(GRID_W)
    col_start = np.clip(cq - C_WIN_C // 2, 0, GRID_W - C_WIN_C)
    col_ok = (cq[None, :] >= col_start[:, None]) & (cq[None, :] < col_start[:, None] + C_WIN_C)
    dc = np.clip(cq[None, :] - cq[:, None], -(C_WIN_C - 1), C_WIN_C - 1) + (C_WIN_C - 1)
    t = rpb.astype(F32)[:, :, dc]
    t = jnp.where(col_ok[None, None], t, NEG)
    n_dr = 2 * C_WIN_R - 1
    lo = t[:, np.minimum(np.arange(2 * C_WIN_R), n_dr - 1)]
    hi = t[:, np.minimum(np.arange(2 * C_WIN_R) + 1, n_dr - 1)]
    return jnp.concatenate([lo, hi], axis=-1)


MG_TM = 256


def _merge_body(x_ref, ya_ref, yc_ref, uv_ref, m_ref, gg_ref, sg_ref, sw_ref, sb_ref, bd_ref,
                wo_ref, g2_ref, rw_ref, xo_ref, hx_ref, lg_ref):
    u = uv_ref[:, :B_W]
    v = uv_ref[:, B_W:]
    ms = jnp.dot(v * v, bd_ref[...], precision=HIGHEST, preferred_element_type=F32)
    vn = (v * lax.rsqrt(ms + EPS) * sg_ref[...]).astype(BF16)
    lane_grp = lax.broadcasted_iota(I32, (SGU_CHUNK, B_W), 1) // HD
    gates = []
    for c in range(MG_TM // SGU_CHUNK):
        vc = vn[c * SGU_CHUNK:(c + 1) * SGU_CHUNK, :]
        gate = sb_ref[...]
        acc = jnp.zeros((SGU_CHUNK, B_W), F32)
        for g in range(SGU_GROUPS):
            r = jnp.dot(sw_ref[g], vc, preferred_element_type=F32)
            acc = jnp.where(lane_grp == g, r, acc)
        gates.append(acc + gate)
    yb = u * jnp.concatenate(gates, axis=0)
    gg = gg_ref[...]
    ycat = jnp.concatenate([
        _rms(ya_ref[...]) * gg[:, :A_W],
        _rms(yb) * gg[:, A_W:A_W + B_W],
        _rms(yc_ref[...]) * gg[:, A_W + B_W:],
    ], axis=1)
    proj = jnp.dot(ycat.astype(BF16), wo_ref[...], preferred_element_type=F32)
    xn = x_ref[...] + m_ref[:, 2 * D:3 * D] * proj
    xo_ref[...] = xn
    hx = _rms(xn) * g2_ref[...]
    hx = hx * (1.0 + m_ref[:, 4 * D:5 * D]) + m_ref[:, 3 * D:4 * D]
    hx_ref[...] = hx
    lg_ref[...] = lax.dot_general(rw_ref[...], hx, NT_DIMS, precision=HIGHEST,
                                  preferred_element_type=F32)


def _merge(xc, ya, yc, uv, mods, gg, sg, sw_bf, sb_tab, bd, wo_bf, g2, rw_t, n_rows):
    tm = MG_TM
    n_t = n_rows // tm
    const2 = lambda i: (0, 0)
    row = lambda i: (i, 0)
    return pl.pallas_call(
        _merge_body,
        grid=(n_t,),
        in_specs=[
            pl.BlockSpec((tm, D), row),
            pl.BlockSpec((tm, A_W), row),
            pl.BlockSpec((tm, C_W), row),
            pl.BlockSpec((tm, 2 * B_W), row),
            pl.BlockSpec((None, 1, 6 * D), lambda i: (_mod_row(i, tm), 0, 0)),
            pl.BlockSpec((1, D), const2),
            pl.BlockSpec((1, B_W), const2),
            pl.BlockSpec((SGU_GROUPS, SGU_CHUNK, SGU_CHUNK), lambda i: (0, 0, 0)),
            pl.BlockSpec((SGU_CHUNK, B_W), const2),
            pl.BlockSpec((B_W, B_W), const2),
            pl.BlockSpec((D, D), const2),
            pl.BlockSpec((1, D), const2),
            pl.BlockSpec((N_EXP, D), const2),
        ],
        out_specs=[
            pl.BlockSpec((tm, D), row),
            pl.BlockSpec((tm, D), row),
            pl.BlockSpec((N_EXP, tm), lambda i: (0, i)),
        ],
        out_shape=[
            jax.ShapeDtypeStruct((n_rows, D), F32),
            jax.ShapeDtypeStruct((n_rows, D), F32),
            jax.ShapeDtypeStruct((N_EXP, n_rows), F32),
        ],
        compiler_params=_cparams(1),
        name="merge",
    )(xc, ya, yc, uv, mods, gg, sg, sw_bf, sb_tab, bd, wo_bf, g2, rw_t)


RT_TM = 256
GRP_SZ = N_EXP // N_GRP
SLOT_ALIGN = 8


def _first_argmax(v, iota):
    m = v.max(axis=0, keepdims=True)
    idx = jnp.where(v == m, iota, float(v.shape[0])).min(axis=0, keepdims=True)
    return m, idx


def _stack_rows(rows, iota):
    out = jnp.zeros(iota.shape, F32)
    for r, v in enumerate(rows):
        out = jnp.where(iota == float(r), v, out)
    return out


def _route_body(lg_ref, rb_ref, w_ref, lp_ref, tc_ref):
    tm = RT_TM
    scores = jax.nn.sigmoid(lg_ref[...])
    sel = scores + rb_ref[...]
    iota_g = lax.broadcasted_iota(I32, (GRP_SZ, tm), 0).astype(F32)
    gs = []
    for g in range(N_GRP):
        v = sel[g * GRP_SZ:(g + 1) * GRP_SZ, :]
        m1, i1 = _first_argmax(v, iota_g)
        m2 = jnp.where(iota_g == i1, -jnp.inf, v).max(axis=0, keepdims=True)
        gs.append(m1 + m2)
    iota_n = lax.broadcasted_iota(I32, (N_GRP, tm), 0).astype(F32)
    gscore = _stack_rows(gs, iota_n)
    gsel = jnp.zeros((N_GRP, tm), F32)
    for _ in range(TOPK_GRP):
        _, gi = _first_argmax(gscore, iota_n)
        hit = iota_n == gi
        gsel = jnp.where(hit, 1.0, gsel)
        gscore = jnp.where(hit, -jnp.inf, gscore)
    emask = jnp.concatenate(
        [jnp.broadcast_to(gsel[g:g + 1, :], (GRP_SZ, tm)) for g in range(N_GRP)], axis=0)
    cand = jnp.where(emask > 0.5, sel, NEG)
    iota_e = lax.broadcasted_iota(I32, (N_EXP, tm), 0).astype(F32)
    hits = []
    ws = []
    member = jnp.zeros((N_EXP, tm), F32)
    for _ in range(TOP_K):
        _, ei = _first_argmax(cand, iota_e)
        hit = iota_e == ei
        hits.append(hit)
        ws.append(jnp.where(hit, scores, 0.0).sum(axis=0, keepdims=True))
        member = jnp.where(hit, 1.0, member)
        cand = jnp.where(hit, -jnp.inf, cand)
    wsum = ws[0]
    for w in ws[1:]:
        wsum = wsum + w
    iota_k = lax.broadcasted_iota(I32, (TOP_K, tm), 0).astype(F32)
    w_ref[...] = _stack_rows(ws, iota_k) / wsum * ROUTED_SCALE
    r_i = lax.broadcasted_iota(I32, (tm, tm), 0)
    c_i = lax.broadcasted_iota(I32, (tm, tm), 1)
    tri = jnp.where(r_i < c_i, 1.0, 0.0).astype(BF16)
    local = jnp.dot(member.astype(BF16), tri, preferred_element_type=F32)
    tile_cnt = member.sum(axis=1, keepdims=True)
    tc_ref[...] = tile_cnt
    aligned = jnp.ceil(tile_cnt / SLOT_ALIGN) * SLOT_ALIGN
    e_r = lax.broadcasted_iota(I32, (N_EXP, N_EXP), 0)
    e_c = lax.broadcasted_iota(I32, (N_EXP, N_EXP), 1)
    below = jnp.where(e_c < e_r, 1.0, 0.0)
    start = jnp.dot(below, jnp.broadcast_to(aligned, (N_EXP, 128)), precision=HIGHEST,
                    preferred_element_type=F32)[:, 0:1]
    pos = local + start
    lp_ref[...] = _stack_rows(
        [jnp.where(hit, pos, 0.0).sum(axis=0, keepdims=True) for hit in hits],
        iota_k).astype(I32)


def _route(logits_t, rbias, n_tok):
    tm = RT_TM
    tok = lambda i: (0, i)
    return pl.pallas_call(
        _route_body,
        grid=(n_tok // tm,),
        in_specs=[
            pl.BlockSpec((N_EXP, tm), tok),
            pl.BlockSpec((N_EXP, 1), lambda i: (0, 0)),
        ],
        out_specs=[
            pl.BlockSpec((TOP_K, tm), tok),
            pl.BlockSpec((TOP_K, tm), tok),
            pl.BlockSpec((None, N_EXP, 1), lambda i: (i, 0, 0)),
        ],
        out_shape=[
            jax.ShapeDtypeStruct((TOP_K, n_tok), F32),
            jax.ShapeDtypeStruct((TOP_K, n_tok), I32),
            jax.ShapeDtypeStruct((n_tok // tm, N_EXP, 1), F32),
        ],
        compiler_params=_cparams(1),
        name="route",
    )(logits_t, rbias)


DP_TM = RT_TM
SORT_ROWS = 2560
N_GRPS = SORT_ROWS // SLOT_ALIGN
HI_MASK = 0xFFFF0000


def _pack_bf16_pairs(v):
    bits = lax.bitcast_convert_type(v, U32)
    return (bits[:, :D // 2] >> 16) | (bits[:, D // 2:] & jnp.uint32(HI_MASK))


def _unpack_bf16_pairs(bits):
    lo = lax.bitcast_convert_type(bits << 16, F32)
    hi = lax.bitcast_convert_type(bits & jnp.uint32(HI_MASK), F32)
    return jnp.concatenate([lo, hi], axis=1).astype(BF16)


def _group_copies(ngrp, row_of, make_copy):
    def start(g, carry):
        make_copy(g, row_of(g)).start()
        return carry

    def wait(g, carry):
        make_copy(0, 0).wait()
        return carry

    return (lambda: lax.fori_loop(0, ngrp, start, 0)), (lambda: lax.fori_loop(0, ngrp, wait, 0))


def _dispatch_body(ngrp_ref, fill_ref, grow_ref, hx_ref, lp_ref, xs_ref, zbuf_ref, zero_ref, sem):
    i = pl.program_id(0)
    n_tiles = pl.num_programs(0)
    slot = i % 2

    def zero_copy(b):
        dst = xs_ref.at[pl.ds(pl.multiple_of(b * MOE_BLK, MOE_BLK), MOE_BLK)]
        return pltpu.make_async_copy(zero_ref, dst, sem)

    @pl.when(i == 0)
    def _():
        zero_ref[...] = jnp.zeros_like(zero_ref)
        n_fill = fill_ref[0]

        def z_start(q, carry):
            @pl.when(fill_ref[1 + q] >= 0)
            def _():
                zero_copy(fill_ref[1 + q]).start()
            return carry

        def z_wait(q, carry):
            @pl.when(fill_ref[1 + q] >= 0)
            def _():
                zero_copy(0).wait()
            return carry

        lax.fori_loop(0, n_fill, z_start, 0)
        lax.fori_loop(0, n_fill, z_wait, 0)

    def group_copy(s, g, row):
        src = zbuf_ref.at[s, pl.ds(pl.multiple_of(g * SLOT_ALIGN, SLOT_ALIGN), SLOT_ALIGN)]
        dst = xs_ref.at[pl.ds(pl.multiple_of(row, SLOT_ALIGN), SLOT_ALIGN)]
        return pltpu.make_async_copy(src, dst, sem)

    x = hx_ref[...].astype(BF16)
    lpos = lp_ref[...]
    row_iota = lax.broadcasted_iota(I32, (SORT_ROWS, DP_TM), 0)
    onehot = jnp.zeros((SORT_ROWS, DP_TM), F32)
    for k in range(TOP_K):
        onehot = jnp.where(row_iota == lpos[k:k + 1, :], 1.0, onehot)
    z = jnp.dot(onehot.astype(BF16), x, preferred_element_type=F32)
    zbuf_ref[slot] = _pack_bf16_pairs(z)

    @pl.when(i > 0)
    def _():
        _, wait_prev = _group_copies(ngrp_ref[jnp.maximum(i - 1, 0)], None, lambda g, r: group_copy(0, 0, 0))
        wait_prev()

    start_cur, wait_cur = _group_copies(
        ngrp_ref[i], lambda g: grow_ref[0, g], lambda g, r: group_copy(slot, g, r))
    start_cur()

    @pl.when(i == n_tiles - 1)
    def _():
        wait_cur()


def _dispatch(ngrp, fill, grp_row, hx, lpos, n_tok, n_slots):
    return pl.pallas_call(
        _dispatch_body,
        grid_spec=pltpu.PrefetchScalarGridSpec(
            num_scalar_prefetch=2,
            grid=(n_tok // DP_TM,),
            in_specs=[
                pl.BlockSpec((None, 1, N_GRPS), lambda i, *_: (i, 0, 0), memory_space=pltpu.SMEM),
                pl.BlockSpec((DP_TM, D), lambda i, *_: (i, 0)),
                pl.BlockSpec((TOP_K, DP_TM), lambda i, *_: (0, i)),
            ],
            out_specs=pl.BlockSpec(memory_space=pl.ANY),
            scratch_shapes=[
                pltpu.VMEM((2, SORT_ROWS, D // 2), U32),
                pltpu.VMEM((MOE_BLK, D // 2), U32),
                pltpu.SemaphoreType.DMA,
            ],
        ),
        out_shape=jax.ShapeDtypeStruct((n_slots, D // 2), U32),
        compiler_params=_cparams(1, vmem_mb=56),
        name="moe_dispatch",
    )(ngrp, fill, grp_row, hx, lpos)


def _experts_body(bexp_ref, xblk_ref, nused_ref, xs_ref, wg_ref, wu_ref, wd_ref, ys_ref):
    j = pl.program_id(0)

    @pl.when(j < nused_ref[0])
    def _():
        x = _unpack_bf16_pairs(xs_ref[...])
        g = jnp.dot(x, wg_ref[...].astype(BF16), preferred_element_type=F32)
        u = jnp.dot(x, wu_ref[...].astype(BF16), preferred_element_type=F32)
        h = (g * jax.nn.sigmoid(g) * u).astype(BF16)
        y = jnp.dot(h, wd_ref[...].astype(BF16), preferred_element_type=F32)
        ys_ref[...] = _pack_bf16_pairs(y.astype(BF16).astype(F32))

    @pl.when(j >= nused_ref[0])
    def _():
        ys_ref[...] = jnp.zeros_like(ys_ref)


def _experts(bexp, xblk, nused, xs, wg, wu, wd, n_blk, layer):
    w_idx = lambda j, be, xb, nu: (layer, be[j], 0, 0)
    return pl.pallas_call(
        _experts_body,
        grid_spec=pltpu.PrefetchScalarGridSpec(
            num_scalar_prefetch=3,
            grid=(n_blk,),
            in_specs=[
                pl.BlockSpec((MOE_BLK, D // 2), lambda j, be, xb, nu: (xb[j], 0)),
                pl.BlockSpec((None, None, D, EXP_FF), w_idx),
                pl.BlockSpec((None, None, D, EXP_FF), w_idx),
                pl.BlockSpec((None, None, EXP_FF, D), w_idx),
            ],
            out_specs=pl.BlockSpec((MOE_BLK, D // 2), lambda j, be, xb, nu: (j, 0)),
        ),
        out_shape=jax.ShapeDtypeStruct((n_blk * MOE_BLK, D // 2), U32),
        compiler_params=_cparams(1),
        name="moe_experts",
    )(bexp, xblk, nused, xs, wg, wu, wd)


def _combine_body(ngrp_ref, grow_ref, ys_ref, lp_ref, w_ref, x_ref, hx_ref, m_ref, sg_ref, su_ref,
                  sd_ref, fg_ref, o_ref, ybuf_ref, sem, *, final_norm):
    i = pl.program_id(0)

    @pl.when(i == 0)
    def _():
        ybuf_ref[...] = jnp.zeros_like(ybuf_ref)

    def group_copy(g, row):
        src = ys_ref.at[pl.ds(pl.multiple_of(row, SLOT_ALIGN), SLOT_ALIGN)]
        dst = ybuf_ref.at[pl.ds(pl.multiple_of(g * SLOT_ALIGN, SLOT_ALIGN), SLOT_ALIGN)]
        return pltpu.make_async_copy(src, dst, sem)

    start, wait = _group_copies(ngrp_ref[i], lambda g: grow_ref[0, g], group_copy)
    start()

    hx = hx_ref[...].astype(BF16)
    g = jnp.dot(hx, sg_ref[...], preferred_element_type=F32)
    u = jnp.dot(hx, su_ref[...], preferred_element_type=F32)
    h = (g * jax.nn.sigmoid(g) * u).astype(BF16)
    y = jnp.dot(h, sd_ref[...], preferred_element_type=F32)

    lpos = lp_ref[...]
    w = w_ref[...]
    col_iota = lax.broadcasted_iota(I32, (DP_TM, SORT_ROWS), 1)
    unsort = jnp.zeros((DP_TM, SORT_ROWS), F32)
    for k in range(TOP_K):
        unsort = jnp.where(col_iota == lpos[:, k:k + 1], w[:, k:k + 1], unsort)

    wait()
    y = y + jnp.dot(unsort.astype(BF16), _unpack_bf16_pairs(ybuf_ref[...]),
                    preferred_element_type=F32)
    out = x_ref[...] + m_ref[:, 5 * D:6 * D] * y
    if final_norm:
        out = _rms(out) * fg_ref[...]
    o_ref[...] = out


def _combine(ngrp, grp_row, ys, lpos_t, w_t, x, hx, mods, sg_bf, su_bf, sd_bf, fg, n_tok, final_norm):
    tm = DP_TM
    row = lambda i, *_: (i, 0)
    const2 = lambda i, *_: (0, 0)
    return pl.pallas_call(
        functools.partial(_combine_body, final_norm=final_norm),
        grid_spec=pltpu.PrefetchScalarGridSpec(
            num_scalar_prefetch=1,
            grid=(n_tok // tm,),
            in_specs=[
                pl.BlockSpec((None, 1, N_GRPS), lambda i, *_: (i, 0, 0), memory_space=pltpu.SMEM),
                pl.BlockSpec(memory_space=pl.ANY),
                pl.BlockSpec((tm, TOP_K), row),
                pl.BlockSpec((tm, TOP_K), row),
                pl.BlockSpec((tm, D), row),
                pl.BlockSpec((tm, D), row),
                pl.BlockSpec((None, 1, 6 * D), lambda i, *_: (_mod_row(i, tm), 0, 0)),
                pl.BlockSpec((D, EXP_FF), const2),
                pl.BlockSpec((D, EXP_FF), const2),
                pl.BlockSpec((EXP_FF, D), const2),
                pl.BlockSpec((1, D), const2),
            ],
            out_specs=pl.BlockSpec((tm, D), row),
            scratch_shapes=[pltpu.VMEM((SORT_ROWS, D // 2), U32), pltpu.SemaphoreType.DMA],
        ),
        out_shape=jax.ShapeDtypeStruct((n_tok, D), F32),
        compiler_params=_cparams(1, vmem_mb=56),
        name="moe_combine",
    )(ngrp, grp_row, ys, lpos_t, w_t, x, hx, mods, sg_bf, su_bf, sd_bf, fg)


def _moe(x_new, hx, logits_t, mods, rbias, wg, wu, wd, sg_bf, su_bf, sd_bf, fg, n_tok, layer,
         final_norm):
    w, lpos, tile_cnt = _route(logits_t, rbias, n_tok)
    n_tiles = n_tok // DP_TM
    n_blk = (n_tok * TOP_K + (SLOT_ALIGN - 1) * N_EXP * n_tiles) // MOE_BLK + N_EXP
    tcnt = tile_cnt[:, :, 0].astype(I32)
    cnt_al = (tcnt + SLOT_ALIGN - 1) // SLOT_ALIGN * SLOT_ALIGN
    loc_end = jnp.cumsum(cnt_al, axis=1)
    loc = loc_end - cnt_al
    slots_e = jnp.sum(cnt_al, axis=0)
    nblk_e = (slots_e + MOE_BLK - 1) // MOE_BLK
    blk_end = jnp.cumsum(nblk_e)
    pstart = (blk_end - nblk_e) * MOE_BLK
    nused = blk_end[-1:].astype(I32)
    blk_ids = jnp.arange(n_blk, dtype=I32)
    xblk = jnp.minimum(blk_ids, nused[0] - 1)
    bexp = jnp.minimum(
        jnp.sum(blk_end[None, :] <= xblk[:, None], axis=1), N_EXP - 1).astype(I32)
    off = pstart[None, :] + jnp.cumsum(cnt_al, axis=0) - cnt_al
    g_row = jnp.arange(N_GRPS, dtype=I32) * SLOT_ALIGN
    e_of_g = jnp.minimum(
        jnp.sum(loc_end[:, None, :] <= g_row[None, :, None], axis=2), N_EXP - 1)
    pick = e_of_g[..., None] == jnp.arange(N_EXP, dtype=I32)
    grp_row = (jnp.sum(jnp.where(pick, (off - loc)[:, None, :], 0), axis=2)
               + g_row[None, :]).astype(I32).reshape(n_tiles, 1, N_GRPS)
    ngrp = (loc_end[:, -1] // SLOT_ALIGN).astype(I32)
    unused = nused[0] + blk_ids
    fill = jnp.concatenate([
        (N_EXP + n_blk - nused[0])[None],
        jnp.where(nblk_e > 0, blk_end - 1, -1),
        jnp.where(unused < n_blk, unused, -1)]).astype(I32)
    xs = _dispatch(ngrp, fill, grp_row, hx, lpos, n_tok, n_blk * MOE_BLK)
    ys = _experts(bexp, xblk, nused, xs, wg, wu, wd, n_blk, layer)
    return _combine(ngrp, grp_row, ys, lpos.T, w.T, x_new, hx, mods, sg_bf, su_bf, sd_bf, fg,
                    n_tok, final_norm)


def _rope_tables():
    t = np.arange(S)
    row = (t // GRID_W).astype(np.float32)
    col = (t % GRID_W).astype(np.float32)
    half = HD // 2
    inv = jnp.asarray(ROPE_BASE, F32) ** (-jnp.arange(0, half, 2, dtype=F32) / half)
    ang_r = jnp.asarray(row)[:, None] * inv
    ang_c = jnp.asarray(col)[:, None] * inv
    ang = jnp.concatenate([ang_r, ang_r, ang_c, ang_c], axis=-1)
    n_rep = ROT_W // HD
    cos = jnp.tile(jnp.cos(ang), (1, n_rep))
    sin = jnp.tile(jnp.sin(ang), (1, n_rep))
    cos = jnp.concatenate([cos, jnp.ones((IN_TM, ROT_W), F32)], axis=0)
    sin = jnp.concatenate([sin, jnp.zeros((IN_TM, ROT_W), F32)], axis=0)
    return cos, sin


def kernel(x, c, ctx, c_ctx, mod_w, mod_b, norm1_g, w_in, attn_sink, sgu_norm_g, sgu_w, sgu_b, na_rpb,
           group_norm_g, w_out, norm2_g, router_w, router_bias, exp_w_gate, exp_w_up, exp_w_down,
           shared_w_gate, shared_w_up, shared_w_down, final_g):
    xc = jnp.concatenate([x.reshape(N_LAT, D), ctx.reshape(N_CTX, D)], axis=0)
    cc = jnp.concatenate([c, c_ctx[None, :], jnp.zeros((16 - B - 1, D), F32)], axis=0)
    mods_all = _modulation(cc, mod_w, mod_b)
    cos_t, sin_t = _rope_tables()
    seg = np.arange(B_W) // HD
    bd = jnp.asarray((seg[:, None] == seg[None, :]).astype(np.float32) / HD)
    fg = final_g.reshape(1, D)

    out = None
    for l in range(DEPTH):
        last = l == DEPTH - 1
        mods = mods_all[l].reshape(16, 1, 6 * D)
        za, zc, uv = _in_proj(xc, mods, norm1_g[l].reshape(1, D), w_in[l].astype(BF16), cos_t, sin_t)
        ya = _attn_a(za, attn_sink[l], with_ctx=not last)
        yc = _na(zc, _na_bias_table(na_rpb[l]), with_ctx=not last)
        n_tok = N_LAT if last else N_ALL
        sb_tab = jnp.repeat(sgu_b[l].T, HD, axis=1)
        x_new, hx, logits_t = _merge(
            xc, ya, yc, uv, mods, group_norm_g[l].reshape(1, D), sgu_norm_g[l].reshape(1, B_W),
            sgu_w[l].astype(BF16), sb_tab, bd, w_out[l].astype(BF16), norm2_g[l].reshape(1, D),
            router_w[l].T, n_tok)
        res = _moe(x_new, hx, logits_t, mods, router_bias[l].reshape(N_EXP, 1),
                   exp_w_gate, exp_w_up, exp_w_down,
                   shared_w_gate[l].astype(BF16), shared_w_up[l].astype(BF16),
                   shared_w_down[l].astype(BF16), fg, n_tok, layer=l, final_norm=last)
        if last:
            out = res.reshape(B, S, D)
        else:
            xc = res
    return out
```

```python
import functools

import jax
import jax.numpy as jnp
import numpy as np
from jax import lax
from jax.experimental import pallas as pl
from jax.experimental.pallas import tpu as pltpu

F32 = jnp.float32
BF16 = jnp.bfloat16
I32 = jnp.int32
U32 = jnp.uint32

D = 1024
B = 8
S = 2048
C = 256
DEPTH = 2
GRID_W = 64
HD = 64
A_HEADS = 6
A_KV = 2
A_WIN = 128
A_BLK = 128
ROPE_BASE = 10000.0
SGU_GROUPS = 4
SGU_CHUNK = 128
C_HEADS = 6
C_WIN_R = 8
C_WIN_C = 16
A_W = A_HEADS * HD
B_W = SGU_GROUPS * HD
C_W = C_HEADS * HD
KV_W = A_KV * HD
IN_W = A_W + 2 * KV_W + 2 * B_W + 3 * C_W
N_EXP = 64
TOP_K = 8
N_GRP = 8
TOPK_GRP = 4
EXP_FF = 256
ROUTED_SCALE = 2.5
MOE_BLK = 512
EPS = 1e-6
NEG = -1e30
SCALE = HD ** -0.5

N_LAT = B * S
N_CTX = B * C
N_ALL = N_LAT + N_CTX
ROWS = S // GRID_W

HIGHEST = lax.Precision.HIGHEST
ARB = pltpu.ARBITRARY

NT_DIMS = (((1,), (1,)), ((), ()))


def _cparams(n_axes, vmem_mb=48):
    return pltpu.CompilerParams(
        dimension_semantics=(ARB,) * n_axes, vmem_limit_bytes=vmem_mb * 1024 * 1024)


def _mod_row(i, tm):
    return jnp.where(i < N_LAT // tm, i // (S // tm), B)


def _rms(x):
    return x * lax.rsqrt(jnp.mean(x * x, axis=-1, keepdims=True) + EPS)


MOD_TN = 1024


def _mod_body(cc_ref, w_ref, b_ref, o_ref):
    a = cc_ref[...]
    a = a * jax.nn.sigmoid(a)
    o_ref[...] = jnp.dot(a.astype(BF16), w_ref[...].astype(BF16),
                         preferred_element_type=F32) + b_ref[...]


def _modulation(cc, mod_w, mod_b):
    n_col = 6 * D // MOD_TN
    return pl.pallas_call(
        _mod_body,
        grid=(DEPTH, n_col),
        in_specs=[
            pl.BlockSpec((16, D), lambda l, j: (0, 0)),
            pl.BlockSpec((None, D, MOD_TN), lambda l, j: (l, 0, j)),
            pl.BlockSpec((None, 1, MOD_TN), lambda l, j: (l, 0, j)),
        ],
        out_specs=pl.BlockSpec((None, 16, MOD_TN), lambda l, j: (l, 0, j)),
        out_shape=jax.ShapeDtypeStruct((DEPTH, 16, 6 * D), F32),
        compiler_params=_cparams(2),
        name="modulation",
    )(cc, mod_w, mod_b.reshape(DEPTH, 1, 6 * D))


IN_TM = 256
ROT_W = A_W + KV_W


def _in_body(x_ref, m_ref, g_ref, w_ref, cos_ref, sin_ref, za_ref, zc_ref, uv_ref):
    x = x_ref[...]
    h = _rms(x) * g_ref[...]
    h = h * (1.0 + m_ref[:, D:2 * D]) + m_ref[:, 0:D]
    z = jnp.dot(h.astype(BF16), w_ref[...], preferred_element_type=F32)
    qk = z[:, :ROT_W]
    lane = lax.broadcasted_iota(I32, qk.shape, 1)
    rot = jnp.where((lane & 16) == 0,
                    -pltpu.roll(qk, ROT_W - 16, 1), pltpu.roll(qk, 16, 1))
    qk = qk * cos_ref[...] + rot * sin_ref[...]
    za_ref[...] = jnp.concatenate(
        [qk[:, :A_W] * SCALE, qk[:, A_W:], z[:, ROT_W:ROT_W + KV_W]], axis=1).astype(BF16)
    u0 = ROT_W + KV_W
    uv_ref[...] = jax.nn.gelu(z[:, u0:u0 + 2 * B_W])
    c0 = u0 + 2 * B_W
    zc_ref[...] = jnp.concatenate(
        [z[:, c0:c0 + C_W] * SCALE, z[:, c0 + C_W:]], axis=1).astype(BF16)


def _in_proj(xc, mods, g, w_bf, cos_t, sin_t):
    tm = IN_TM
    n_t = N_ALL // tm

    def tab_idx(i):
        return (jnp.where(i < N_LAT // tm, i % (S // tm), S // tm), 0)

    return pl.pallas_call(
        _in_body,
        grid=(n_t,),
        in_specs=[
            pl.BlockSpec((tm, D), lambda i: (i, 0)),
            pl.BlockSpec((None, 1, 6 * D), lambda i: (_mod_row(i, tm), 0, 0)),
            pl.BlockSpec((1, D), lambda i: (0, 0)),
            pl.BlockSpec((D, IN_W), lambda i: (0, 0)),
            pl.BlockSpec((tm, ROT_W), tab_idx),
            pl.BlockSpec((tm, ROT_W), tab_idx),
        ],
        out_specs=[
            pl.BlockSpec((tm, A_W + 2 * KV_W), lambda i: (i, 0)),
            pl.BlockSpec((tm, 3 * C_W), lambda i: (i, 0)),
            pl.BlockSpec((tm, 2 * B_W), lambda i: (i, 0)),
        ],
        out_shape=[
            jax.ShapeDtypeStruct((N_ALL, A_W + 2 * KV_W), BF16),
            jax.ShapeDtypeStruct((N_ALL, 3 * C_W), BF16),
            jax.ShapeDtypeStruct((N_ALL, 2 * B_W), F32),
        ],
        compiler_params=_cparams(1),
        name="in_proj",
    )(xc, mods, g, w_bf, cos_t, sin_t)


A_BAND = 3 * A_BLK
N_QB_LAT = N_LAT // A_BLK
QB_PER_SEQ = S // A_BLK
QB_PER_CTX = C // A_BLK


def _softmax_pv(s_list, v_list, extra_logit=None):
    m = s_list[0].max(axis=-1, keepdims=True)
    for s in s_list[1:]:
        m = jnp.maximum(m, s.max(axis=-1, keepdims=True))
    if extra_logit is not None:
        m = jnp.maximum(m, extra_logit)
    den = None
    out = None
    for s, v in zip(s_list, v_list):
        p = jnp.exp(s - m)
        d = p.sum(axis=-1, keepdims=True)
        o = jnp.dot(p.astype(BF16), v, preferred_element_type=F32)
        den = d if den is None else den + d
        out = o if out is None else out + o
    if extra_logit is not None:
        den = den + jnp.exp(extra_logit - m)
    return out / den


def _attn_a_body(sink_ref, q_ref, kb_ref, vb_ref, kc_ref, vc_ref, o_ref):
    i = pl.program_id(0)
    is_lat = i < N_QB_LAT
    n = i % QB_PER_SEQ
    start = pl.multiple_of(jnp.clip((n - 1) * A_BLK, 0, S - A_BAND), A_BLK)
    qpos = n * A_BLK + lax.broadcasted_iota(I32, (A_BLK, A_BAND), 0)
    kpos = start + lax.broadcasted_iota(I32, (A_BLK, A_BAND), 1)
    mask = jnp.abs(kpos - qpos) <= jnp.where(is_lat, A_WIN, -1)
    kb = kb_ref[pl.ds(start, A_BAND), :]
    vb = vb_ref[pl.ds(start, A_BAND), :]
    kc = kc_ref[...]
    vc = vc_ref[...]
    outs = []
    for h in range(A_HEADS):
        kv = h // (A_HEADS // A_KV)
        sl = slice(kv * HD, (kv + 1) * HD)
        q = q_ref[:, h * HD:(h + 1) * HD]
        s_b = lax.dot_general(q, kb[:, sl], NT_DIMS, preferred_element_type=F32)
        s_b = jnp.where(mask, s_b, NEG)
        s_c = lax.dot_general(q, kc[:, sl], NT_DIMS, preferred_element_type=F32)
        outs.append(_softmax_pv([s_c, s_b], [vc[:, sl], vb[:, sl]], sink_ref[h]))
    o_ref[...] = jnp.concatenate(outs, axis=1)


def _attn_a(za, sink, with_ctx):
    n_qb = N_QB_LAT + (N_CTX // A_BLK if with_ctx else 0)

    def bidx(i):
        return jnp.where(i < N_QB_LAT, i // QB_PER_SEQ, (i - N_QB_LAT) // QB_PER_CTX)

    k_col = A_W // KV_W
    v_col = k_col + 1
    return pl.pallas_call(
        _attn_a_body,
        grid=(n_qb,),
        in_specs=[
            pl.BlockSpec(memory_space=pltpu.SMEM),
            pl.BlockSpec((A_BLK, A_W), lambda i: (i, 0)),
            pl.BlockSpec((S, KV_W), lambda i: (bidx(i), k_col)),
            pl.BlockSpec((S, KV_W), lambda i: (bidx(i), v_col)),
            pl.BlockSpec((C, KV_W), lambda i: (N_LAT // C + bidx(i), k_col)),
            pl.BlockSpec((C, KV_W), lambda i: (N_LAT // C + bidx(i), v_col)),
        ],
        out_specs=pl.BlockSpec((A_BLK, A_W), lambda i: (i, 0)),
        out_shape=jax.ShapeDtypeStruct((n_qb * A_BLK, A_W), F32),
        compiler_params=_cparams(1),
        name="attn_window",
    )(sink, za, za, za, za, za)


NA_R = 4
NA_TQ = NA_R * GRID_W
NA_KROWS = 12
NA_WIN = NA_KROWS * GRID_W
NA_STEPS = ROWS // NA_R
NA_PAIRS = NA_KROWS // 2
N_DR = 2 * C_WIN_R - 1


def _na_body(q_ref, k_ref, v_ref, kc_ref, vc_ref, tab_ref, o_ref):
    j = pl.program_id(1)
    is_lat = j < NA_STEPS
    r0 = jnp.minimum(j, NA_STEPS - 1) * NA_R
    u0 = jnp.clip(r0 - C_WIN_R // 2, 0, ROWS - NA_KROWS)
    k0 = pl.multiple_of(u0 * GRID_W, GRID_W)
    kw = k_ref[pl.ds(k0, NA_WIN), :]
    vw = v_ref[pl.ds(k0, NA_WIN), :]
    kc = kc_ref[...]
    vc = vc_ref[...]
    q = q_ref[...]
    left = lax.broadcasted_iota(I32, (1, 2 * GRID_W), 1) < GRID_W

    tab_idx = []
    penalty = []
    for rr in range(NA_R):
        r = r0 + rr
        start = jnp.clip(r - C_WIN_R // 2, 0, ROWS - C_WIN_R)
        idx_row = []
        pen_row = []
        for p in range(NA_PAIRS):
            kr = u0 + 2 * p
            idx_row.append(jnp.clip(kr - r + (C_WIN_R - 1), -1, N_DR - 1) + 1)
            pens = []
            for half in range(2):
                ok = jnp.logical_and(is_lat, jnp.logical_and(kr + half >= start,
                                                             kr + half < start + C_WIN_R))
                pens.append(jnp.where(ok, 0.0, NEG))
            pen_row.append(jnp.where(left, pens[0], pens[1]))
        tab_idx.append(idx_row)
        penalty.append(jnp.concatenate(pen_row, axis=1))

    outs = []
    for h in range(C_HEADS):
        sl = slice(h * HD, (h + 1) * HD)
        qh = q[:, sl]
        s_w = lax.dot_general(qh, kw[:, sl], NT_DIMS, preferred_element_type=F32)
        bias = jnp.concatenate([
            jnp.concatenate([tab_ref[h, pl.ds(tab_idx[rr][p], 1)][0] for p in range(NA_PAIRS)],
                            axis=1) + penalty[rr]
            for rr in range(NA_R)], axis=0)
        s_c = lax.dot_general(qh, kc[:, sl], NT_DIMS, preferred_element_type=F32)
        outs.append(_softmax_pv([s_c, s_w + bias], [vc[:, sl], vw[:, sl]]))
    o_ref[...] = jnp.concatenate(outs, axis=1)


def _na(zc, tab, with_ctx):
    n_j = NA_STEPS + (1 if with_ctx else 0)

    def qidx(b, j):
        return jnp.where(j < NA_STEPS, b * NA_STEPS + j, N_LAT // NA_TQ + b)

    n_out = N_LAT + (N_CTX if with_ctx else 0)
    return pl.pallas_call(
        _na_body,
        grid=(B, n_j),
        in_specs=[
            pl.BlockSpec((NA_TQ, C_W), lambda b, j: (qidx(b, j), 0)),
            pl.BlockSpec((S, C_W), lambda b, j: (b, 1)),
            pl.BlockSpec((S, C_W), lambda b, j: (b, 2)),
            pl.BlockSpec((C, C_W), lambda b, j: (N_LAT // C + b, 1)),
            pl.BlockSpec((C, C_W), lambda b, j: (N_LAT // C + b, 2)),
            pl.BlockSpec((C_HEADS, N_DR + 1, GRID_W, 2 * GRID_W), lambda b, j: (0, 0, 0, 0)),
        ],
        out_specs=pl.BlockSpec((NA_TQ, C_W), lambda b, j: (qidx(b, j), 0)),
        out_shape=jax.ShapeDtypeStruct((n_out, C_W), F32),
        compiler_params=_cparams(2),
        name="attn_neighbourhood",
    )(zc, zc, zc, zc, zc, tab)


def _na_bias_table(rpb):
    cq = np.arange(GRID_W)
    col_start = np.clip(cq - C_WIN_C // 2, 0, GRID_W - C_WIN_C)
    col_ok = (cq[None, :] >= col_start[:, None]) & (cq[None, :] < col_start[:, None] + C_WIN_C)
    dc = np.clip(cq[None, :] - cq[:, None], -(C_WIN_C - 1), C_WIN_C - 1) + (C_WIN_C - 1)
    t = rpb.astype(F32)[:, :, dc]
    t = jnp.where(col_ok[None, None], t, NEG)
    zero = jnp.zeros((C_HEADS, 1, GRID_W, GRID_W), F32)
    ext = jnp.concatenate([zero, t, zero], axis=1)
    return jnp.concatenate([ext[:, :-1], ext[:, 1:]], axis=-1)


MG_TM = 256


def _merge_body(x_ref, ya_ref, yc_ref, uv_ref, m_ref, gg_ref, sg_ref, sw_ref, sb_ref, bd_ref,
                wo_ref, g2_ref, rw_ref, xo_ref, hx_ref, lg_ref):
    u = uv_ref[:, :B_W]
    v = uv_ref[:, B_W:]
    ms = jnp.dot(v * v, bd_ref[...], precision=HIGHEST, preferred_element_type=F32)
    vn = (v * lax.rsqrt(ms + EPS) * sg_ref[...]).astype(BF16)
    lane_grp = lax.broadcasted_iota(I32, (SGU_CHUNK, B_W), 1) // HD
    gates = []
    for c in range(MG_TM // SGU_CHUNK):
        vc = vn[c * SGU_CHUNK:(c + 1) * SGU_CHUNK, :]
        gate = sb_ref[...]
        acc = jnp.zeros((SGU_CHUNK, B_W), F32)
        for g in range(SGU_GROUPS):
            r = jnp.dot(sw_ref[g], vc, preferred_element_type=F32)
            acc = jnp.where(lane_grp == g, r, acc)
        gates.append(acc + gate)
    yb = u * jnp.concatenate(gates, axis=0)
    gg = gg_ref[...]
    ycat = jnp.concatenate([
        _rms(ya_ref[...]) * gg[:, :A_W],
        _rms(yb) * gg[:, A_W:A_W + B_W],
        _rms(yc_ref[...]) * gg[:, A_W + B_W:],
    ], axis=1)
    proj = jnp.dot(ycat.astype(BF16), wo_ref[...], preferred_element_type=F32)
    xn = x_ref[...] + m_ref[:, 2 * D:3 * D] * proj
    xo_ref[...] = xn
    hx = _rms(xn) * g2_ref[...]
    hx = hx * (1.0 + m_ref[:, 4 * D:5 * D]) + m_ref[:, 3 * D:4 * D]
    hx_ref[...] = hx
    lg_ref[...] = lax.dot_general(rw_ref[...], hx, NT_DIMS, precision=HIGHEST,
                                  preferred_element_type=F32)


def _merge(xc, ya, yc, uv, mods, gg, sg, sw_bf, sb_tab, bd, wo_bf, g2, rw_t, n_rows):
    tm = MG_TM
    n_t = n_rows // tm
    const2 = lambda i: (0, 0)
    row = lambda i: (i, 0)
    return pl.pallas_call(
        _merge_body,
        grid=(n_t,),
        in_specs=[
            pl.BlockSpec((tm, D), row),
            pl.BlockSpec((tm, A_W), row),
            pl.BlockSpec((tm, C_W), row),
            pl.BlockSpec((tm, 2 * B_W), row),
            pl.BlockSpec((None, 1, 6 * D), lambda i: (_mod_row(i, tm), 0, 0)),
            pl.BlockSpec((1, D), const2),
            pl.BlockSpec((1, B_W), const2),
            pl.BlockSpec((SGU_GROUPS, SGU_CHUNK, SGU_CHUNK), lambda i: (0, 0, 0)),
            pl.BlockSpec((SGU_CHUNK, B_W), const2),
            pl.BlockSpec((B_W, B_W), const2),
            pl.BlockSpec((D, D), const2),
            pl.BlockSpec((1, D), const2),
            pl.BlockSpec((N_EXP, D), const2),
        ],
        out_specs=[
            pl.BlockSpec((tm, D), row),
            pl.BlockSpec((tm, D), row),
            pl.BlockSpec((N_EXP, tm), lambda i: (0, i)),
        ],
        out_shape=[
            jax.ShapeDtypeStruct((n_rows, D), F32),
            jax.ShapeDtypeStruct((n_rows, D), F32),
            jax.ShapeDtypeStruct((N_EXP, n_rows), F32),
        ],
        compiler_params=_cparams(1),
        name="merge",
    )(xc, ya, yc, uv, mods, gg, sg, sw_bf, sb_tab, bd, wo_bf, g2, rw_t)


RT_TM = 256
GRP_SZ = N_EXP // N_GRP
SLOT_ALIGN = 8


def _first_argmax(v, iota):
    m = v.max(axis=0, keepdims=True)
    idx = jnp.where(v == m, iota, float(v.shape[0])).min(axis=0, keepdims=True)
    return m, idx


def _stack_rows(rows, iota):
    out = jnp.zeros(iota.shape, F32)
    for r, v in enumerate(rows):
        out = jnp.where(iota == float(r), v, out)
    return out


def _route_body(lg_ref, rb_ref, w_ref, lp_ref, tc_ref):
    tm = RT_TM
    scores = jax.nn.sigmoid(lg_ref[...])
    sel = scores + rb_ref[...]
    iota_g = lax.broadcasted_iota(I32, (GRP_SZ, tm), 0).astype(F32)
    gs = []
    for g in range(N_GRP):
        v = sel[g * GRP_SZ:(g + 1) * GRP_SZ, :]
        m1, i1 = _first_argmax(v, iota_g)
        m2 = jnp.where(iota_g == i1, -jnp.inf, v).max(axis=0, keepdims=True)
        gs.append(m1 + m2)
    iota_n = lax.broadcasted_iota(I32, (N_GRP, tm), 0).astype(F32)
    gscore = _stack_rows(gs, iota_n)
    gsel = jnp.zeros((N_GRP, tm), F32)
    for _ in range(TOPK_GRP):
        _, gi = _first_argmax(gscore, iota_n)
        hit = iota_n == gi
        gsel = jnp.where(hit, 1.0, gsel)
        gscore = jnp.where(hit, -jnp.inf, gscore)
    emask = jnp.concatenate(
        [jnp.broadcast_to(gsel[g:g + 1, :], (GRP_SZ, tm)) for g in range(N_GRP)], axis=0)
    cand = jnp.where(emask > 0.5, sel, NEG)
    iota_e = lax.broadcasted_iota(I32, (N_EXP, tm), 0).astype(F32)
    hits = []
    ws = []
    member = jnp.zeros((N_EXP, tm), F32)
    for _ in range(TOP_K):
        _, ei = _first_argmax(cand, iota_e)
        hit = iota_e == ei
        hits.append(hit)
        ws.append(jnp.where(hit, scores, 0.0).sum(axis=0, keepdims=True))
        member = jnp.where(hit, 1.0, member)
        cand = jnp.where(hit, -jnp.inf, cand)
    wsum = ws[0]
    for w in ws[1:]:
        wsum = wsum + w
    iota_k = lax.broadcasted_iota(I32, (TOP_K, tm), 0).astype(F32)
    w_ref[...] = _stack_rows(ws, iota_k) / wsum * ROUTED_SCALE
    r_i = lax.broadcasted_iota(I32, (tm, tm), 0)
    c_i = lax.broadcasted_iota(I32, (tm, tm), 1)
    tri = jnp.where(r_i < c_i, 1.0, 0.0).astype(BF16)
    local = jnp.dot(member.astype(BF16), tri, preferred_element_type=F32)
    tile_cnt = member.sum(axis=1, keepdims=True)
    tc_ref[...] = tile_cnt
    aligned = jnp.ceil(tile_cnt / SLOT_ALIGN) * SLOT_ALIGN
    e_r = lax.broadcasted_iota(I32, (N_EXP, N_EXP), 0)
    e_c = lax.broadcasted_iota(I32, (N_EXP, N_EXP), 1)
    below = jnp.where(e_c < e_r, 1.0, 0.0)
    start = jnp.dot(below, jnp.broadcast_to(aligned, (N_EXP, 128)), precision=HIGHEST,
                    preferred_element_type=F32)[:, 0:1]
    pos = local + start
    lp_ref[...] = _stack_rows(
        [jnp.where(hit, pos, 0.0).sum(axis=0, keepdims=True) for hit in hits],
        iota_k).astype(I32)


def _route(logits_t, rbias, n_tok):
    tm = RT_TM
    tok = lambda i: (0, i)
    return pl.pallas_call(
        _route_body,
        grid=(n_tok // tm,),
        in_specs=[
            pl.BlockSpec((N_EXP, tm), tok),
            pl.BlockSpec((N_EXP, 1), lambda i: (0, 0)),
        ],
        out_specs=[
            pl.BlockSpec((TOP_K, tm), tok),
            pl.BlockSpec((TOP_K, tm), tok),
            pl.BlockSpec((None, N_EXP, 1), lambda i: (i, 0, 0)),
        ],
        out_shape=[
            jax.ShapeDtypeStruct((TOP_K, n_tok), F32),
            jax.ShapeDtypeStruct((TOP_K, n_tok), I32),
            jax.ShapeDtypeStruct((n_tok // tm, N_EXP, 1), F32),
        ],
        compiler_params=_cparams(1),
        name="route",
    )(logits_t, rbias)


DP_TM = RT_TM
SORT_ROWS = 2560
N_GRPS = SORT_ROWS // SLOT_ALIGN
HI_MASK = 0xFFFF0000


def _pack_bf16_pairs(v):
    bits = lax.bitcast_convert_type(v, U32)
    return (bits[:, :D // 2] >> 16) | (bits[:, D // 2:] & jnp.uint32(HI_MASK))


def _unpack_bf16_pairs(bits):
    lo = lax.bitcast_convert_type(bits << 16, F32)
    hi = lax.bitcast_convert_type(bits & jnp.uint32(HI_MASK), F32)
    return jnp.concatenate([lo, hi], axis=1).astype(BF16)


def _group_copies(ngrp, row_of, make_copy):
    def start(g, carry):
        make_copy(g, row_of(g)).start()
        return carry

    def wait(g, carry):
        make_copy(0, 0).wait()
        return carry

    return (lambda: lax.fori_loop(0, ngrp, start, 0)), (lambda: lax.fori_loop(0, ngrp, wait, 0))


def _dispatch_body(ngrp_ref, fill_ref, grow_ref, hx_ref, lp_ref, xs_ref, zbuf_ref, zero_ref, sem):
    i = pl.program_id(0)
    n_tiles = pl.num_programs(0)
    slot = i % 2

    def zero_copy(b):
        dst = xs_ref.at[pl.ds(pl.multiple_of(b * MOE_BLK, MOE_BLK), MOE_BLK)]
        return pltpu.make_async_copy(zero_ref, dst, sem)

    @pl.when(i == 0)
    def _():
        zero_ref[...] = jnp.zeros_like(zero_ref)
        n_fill = fill_ref[0]

        def z_start(q, carry):
            @pl.when(fill_ref[1 + q] >= 0)
            def _():
                zero_copy(fill_ref[1 + q]).start()
            return carry

        def z_wait(q, carry):
            @pl.when(fill_ref[1 + q] >= 0)
            def _():
                zero_copy(0).wait()
            return carry

        lax.fori_loop(0, n_fill, z_start, 0)
        lax.fori_loop(0, n_fill, z_wait, 0)

    def group_copy(s, g, row):
        src = zbuf_ref.at[s, pl.ds(pl.multiple_of(g * SLOT_ALIGN, SLOT_ALIGN), SLOT_ALIGN)]
        dst = xs_ref.at[pl.ds(pl.multiple_of(row, SLOT_ALIGN), SLOT_ALIGN)]
        return pltpu.make_async_copy(src, dst, sem)

    x = hx_ref[...].astype(BF16)
    lpos = lp_ref[...]
    row_iota = lax.broadcasted_iota(I32, (SORT_ROWS, DP_TM), 0)
    onehot = jnp.zeros((SORT_ROWS, DP_TM), F32)
    for k in range(TOP_K):
        onehot = jnp.where(row_iota == lpos[k:k + 1, :], 1.0, onehot)
    z = jnp.dot(onehot.astype(BF16), x, preferred_element_type=F32)
    zbuf_ref[slot] = _pack_bf16_pairs(z)

    @pl.when(i > 0)
    def _():
        _, wait_prev = _group_copies(ngrp_ref[jnp.maximum(i - 1, 0)], None, lambda g, r: group_copy(0, 0, 0))
        wait_prev()

    start_cur, wait_cur = _group_copies(
        ngrp_ref[i], lambda g: grow_ref[0, g], lambda g, r: group_copy(slot, g, r))
    start_cur()

    @pl.when(i == n_tiles - 1)
    def _():
        wait_cur()


def _dispatch(ngrp, fill, grp_row, hx, lpos, n_tok, n_slots):
    return pl.pallas_call(
        _dispatch_body,
        grid_spec=pltpu.PrefetchScalarGridSpec(
            num_scalar_prefetch=2,
            grid=(n_tok // DP_TM,),
            in_specs=[
                pl.BlockSpec((None, 1, N_GRPS), lambda i, *_: (i, 0, 0), memory_space=pltpu.SMEM),
                pl.BlockSpec((DP_TM, D), lambda i, *_: (i, 0)),
                pl.BlockSpec((TOP_K, DP_TM), lambda i, *_: (0, i)),
            ],
            out_specs=pl.BlockSpec(memory_space=pl.ANY),
            scratch_shapes=[
                pltpu.VMEM((2, SORT_ROWS, D // 2), U32),
                pltpu.VMEM((MOE_BLK, D // 2), U32),
                pltpu.SemaphoreType.DMA,
            ],
        ),
        out_shape=jax.ShapeDtypeStruct((n_slots, D // 2), U32),
        compiler_params=_cparams(1, vmem_mb=56),
        name="moe_dispatch",
    )(ngrp, fill, grp_row, hx, lpos)


def _experts_body(bexp_ref, xblk_ref, nused_ref, xs_ref, wg_ref, wu_ref, wd_ref, ys_ref):
    j = pl.program_id(0)

    @pl.when(j < nused_ref[0])
    def _():
        x = _unpack_bf16_pairs(xs_ref[...])
        g = jnp.dot(x, wg_ref[...].astype(BF16), preferred_element_type=F32)
        u = jnp.dot(x, wu_ref[...].astype(BF16), preferred_element_type=F32)
        h = (g * jax.nn.sigmoid(g) * u).astype(BF16)
        y = jnp.dot(h, wd_ref[...].astype(BF16), preferred_element_type=F32)
        ys_ref[...] = _pack_bf16_pairs(y.astype(BF16).astype(F32))

    @pl.when(j >= nused_ref[0])
    def _():
        ys_ref[...] = jnp.zeros_like(ys_ref)


def _experts(bexp, xblk, nused, xs, wg, wu, wd, n_blk, layer):
    w_idx = lambda j, be, xb, nu: (layer, be[j], 0, 0)
    return pl.pallas_call(
        _experts_body,
        grid_spec=pltpu.PrefetchScalarGridSpec(
            num_scalar_prefetch=3,
            grid=(n_blk,),
            in_specs=[
                pl.BlockSpec((MOE_BLK, D // 2), lambda j, be, xb, nu: (xb[j], 0)),
                pl.BlockSpec((None, None, D, EXP_FF), w_idx),
                pl.BlockSpec((None, None, D, EXP_FF), w_idx),
                pl.BlockSpec((None, None, EXP_FF, D), w_idx),
            ],
            out_specs=pl.BlockSpec((MOE_BLK, D // 2), lambda j, be, xb, nu: (j, 0)),
        ),
        out_shape=jax.ShapeDtypeStruct((n_blk * MOE_BLK, D // 2), U32),
        compiler_params=_cparams(1),
        name="moe_experts",
    )(bexp, xblk, nused, xs, wg, wu, wd)


def _combine_body(ngrp_ref, grow_ref, ys_ref, lp_ref, w_ref, x_ref, hx_ref, m_ref, sg_ref, su_ref,
                  sd_ref, fg_ref, o_ref, ybuf_ref, sem, *, final_norm):
    i = pl.program_id(0)

    @pl.when(i == 0)
    def _():
        ybuf_ref[...] = jnp.zeros_like(ybuf_ref)

    def group_copy(g, row):
        src = ys_ref.at[pl.ds(pl.multiple_of(row, SLOT_ALIGN), SLOT_ALIGN)]
        dst = ybuf_ref.at[pl.ds(pl.multiple_of(g * SLOT_ALIGN, SLOT_ALIGN), SLOT_ALIGN)]
        return pltpu.make_async_copy(src, dst, sem)

    start, wait = _group_copies(ngrp_ref[i], lambda g: grow_ref[0, g], group_copy)
    start()

    hx = hx_ref[...].astype(BF16)
    g = jnp.dot(hx, sg_ref[...], preferred_element_type=F32)
    u = jnp.dot(hx, su_ref[...], preferred_element_type=F32)
    h = (g * jax.nn.sigmoid(g) * u).astype(BF16)
    y = jnp.dot(h, sd_ref[...], preferred_element_type=F32)

    lpos = lp_ref[...]
    w = w_ref[...]
    col_iota = lax.broadcasted_iota(I32, (DP_TM, SORT_ROWS), 1)
    unsort = jnp.zeros((DP_TM, SORT_ROWS), F32)
    for k in range(TOP_K):
        unsort = jnp.where(col_iota == lpos[:, k:k + 1], w[:, k:k + 1], unsort)

    wait()
    y = y + jnp.dot(unsort.astype(BF16), _unpack_bf16_pairs(ybuf_ref[...]),
                    preferred_element_type=F32)
    out = x_ref[...] + m_ref[:, 5 * D:6 * D] * y
    if final_norm:
        out = _rms(out) * fg_ref[...]
    o_ref[...] = out


def _combine(ngrp, grp_row, ys, lpos_t, w_t, x, hx, mods, sg_bf, su_bf, sd_bf, fg, n_tok, final_norm):
    tm = DP_TM
    row = lambda i, *_: (i, 0)
    const2 = lambda i, *_: (0, 0)
    return pl.pallas_call(
        functools.partial(_combine_body, final_norm=final_norm),
        grid_spec=pltpu.PrefetchScalarGridSpec(
            num_scalar_prefetch=1,
            grid=(n_tok // tm,),
            in_specs=[
                pl.BlockSpec((None, 1, N_GRPS), lambda i, *_: (i, 0, 0), memory_space=pltpu.SMEM),
                pl.BlockSpec(memory_space=pl.ANY),
                pl.BlockSpec((tm, TOP_K), row),
                pl.BlockSpec((tm, TOP_K), row),
                pl.BlockSpec((tm, D), row),
                pl.BlockSpec((tm, D), row),
                pl.BlockSpec((None, 1, 6 * D), lambda i, *_: (_mod_row(i, tm), 0, 0)),
                pl.BlockSpec((D, EXP_FF), const2),
                pl.BlockSpec((D, EXP_FF), const2),
                pl.BlockSpec((EXP_FF, D), const2),
                pl.BlockSpec((1, D), const2),
            ],
            out_specs=pl.BlockSpec((tm, D), row),
            scratch_shapes=[pltpu.VMEM((SORT_ROWS, D // 2), U32), pltpu.SemaphoreType.DMA],
        ),
        out_shape=jax.ShapeDtypeStruct((n_tok, D), F32),
        compiler_params=_cparams(1, vmem_mb=56),
        name="moe_combine",
    )(ngrp, grp_row, ys, lpos_t, w_t, x, hx, mods, sg_bf, su_bf, sd_bf, fg)


def _moe(x_new, hx, logits_t, mods, rbias, wg, wu, wd, sg_bf, su_bf, sd_bf, fg, n_tok, layer,
         final_norm):
    w, lpos, tile_cnt = _route(logits_t, rbias, n_tok)
    n_tiles = n_tok // DP_TM
    n_blk = (n_tok * TOP_K + (SLOT_ALIGN - 1) * N_EXP * n_tiles) // MOE_BLK + N_EXP
    tcnt = tile_cnt[:, :, 0].astype(I32)
    cnt_al = (tcnt + SLOT_ALIGN - 1) // SLOT_ALIGN * SLOT_ALIGN
    loc_end = jnp.cumsum(cnt_al, axis=1)
    loc = loc_end - cnt_al
    slots_e = jnp.sum(cnt_al, axis=0)
    nblk_e = (slots_e + MOE_BLK - 1) // MOE_BLK
    blk_end = jnp.cumsum(nblk_e)
    pstart = (blk_end - nblk_e) * MOE_BLK
    nused = blk_end[-1:].astype(I32)
    blk_ids = jnp.arange(n_blk, dtype=I32)
    xblk = jnp.minimum(blk_ids, nused[0] - 1)
    bexp = jnp.minimum(
        jnp.sum(blk_end[None, :] <= xblk[:, None], axis=1), N_EXP - 1).astype(I32)
    off = pstart[None, :] + jnp.cumsum(cnt_al, axis=0) - cnt_al
    g_row = jnp.arange(N_GRPS, dtype=I32) * SLOT_ALIGN
    e_of_g = jnp.minimum(
        jnp.sum(loc_end[:, None, :] <= g_row[None, :, None], axis=2), N_EXP - 1)
    pick = e_of_g[..., None] == jnp.arange(N_EXP, dtype=I32)
    grp_row = (jnp.sum(jnp.where(pick, (off - loc)[:, None, :], 0), axis=2)
               + g_row[None, :]).astype(I32).reshape(n_tiles, 1, N_GRPS)
    ngrp = (loc_end[:, -1] // SLOT_ALIGN).astype(I32)
    unused = nused[0] + blk_ids
    fill = jnp.concatenate([
        (N_EXP + n_blk - nused[0])[None],
        jnp.where(nblk_e > 0, blk_end - 1, -1),
        jnp.where(unused < n_blk, unused, -1)]).astype(I32)
    xs = _dispatch(ngrp, fill, grp_row, hx, lpos, n_tok, n_blk * MOE_BLK)
    ys = _experts(bexp, xblk, nused, xs, wg, wu, wd, n_blk, layer)
    return _combine(ngrp, grp_row, ys, lpos.T, w.T, x_new, hx, mods, sg_bf, su_bf, sd_bf, fg,
                    n_tok, final_norm)


def _rope_tables():
    t = np.arange(S)
    row = (t // GRID_W).astype(np.float32)
    col = (t % GRID_W).astype(np.float32)
    half = HD // 2
    inv = jnp.asarray(ROPE_BASE, F32) ** (-jnp.arange(0, half, 2, dtype=F32) / half)
    ang_r = jnp.asarray(row)[:, None] * inv
    ang_c = jnp.asarray(col)[:, None] * inv
    ang = jnp.concatenate([ang_r, ang_r, ang_c, ang_c], axis=-1)
    n_rep = ROT_W // HD
    cos = jnp.tile(jnp.cos(ang), (1, n_rep))
    sin = jnp.tile(jnp.sin(ang), (1, n_rep))
    cos = jnp.concatenate([cos, jnp.ones((IN_TM, ROT_W), F32)], axis=0)
    sin = jnp.concatenate([sin, jnp.zeros((IN_TM, ROT_W), F32)], axis=0)
    return cos, sin


def kernel(x, c, ctx, c_ctx, mod_w, mod_b, norm1_g, w_in, attn_sink, sgu_norm_g, sgu_w, sgu_b, na_rpb,
           group_norm_g, w_out, norm2_g, router_w, router_bias, exp_w_gate, exp_w_up, exp_w_down,
           shared_w_gate, shared_w_up, shared_w_down, final_g):
    xc = jnp.concatenate([x.reshape(N_LAT, D), ctx.reshape(N_CTX, D)], axis=0)
    cc = jnp.concatenate([c, c_ctx[None, :], jnp.zeros((16 - B - 1, D), F32)], axis=0)
    mods_all = _modulation(cc, mod_w, mod_b)
    cos_t, sin_t = _rope_tables()
    seg = np.arange(B_W) // HD
    bd = jnp.asarray((seg[:, None] == seg[None, :]).astype(np.float32) / HD)
    fg = final_g.reshape(1, D)

    out = None
    for l in range(DEPTH):
        last = l == DEPTH - 1
        mods = mods_all[l].reshape(16, 1, 6 * D)
        za, zc, uv = _in_proj(xc, mods, norm1_g[l].reshape(1, D), w_in[l].astype(BF16), cos_t, sin_t)
        ya = _attn_a(za, attn_sink[l], with_ctx=not last)
        yc = _na(zc, _na_bias_table(na_rpb[l]), with_ctx=not last)
        n_tok = N_LAT if last else N_ALL
        sb_tab = jnp.repeat(sgu_b[l].T, HD, axis=1)
        x_new, hx, logits_t = _merge(
            xc, ya, yc, uv, mods, group_norm_g[l].reshape(1, D), sgu_norm_g[l].reshape(1, B_W),
            sgu_w[l].astype(BF16), sb_tab, bd, w_out[l].astype(BF16), norm2_g[l].reshape(1, D),
            router_w[l].T, n_tok)
        res = _moe(x_new, hx, logits_t, mods, router_bias[l].reshape(N_EXP, 1),
                   exp_w_gate, exp_w_up, exp_w_down,
                   shared_w_gate[l].astype(BF16), shared_w_up[l].astype(BF16),
                   shared_w_down[l].astype(BF16), fg, n_tok, layer=l, final_norm=last)
        if last:
            out = res.reshape(B, S, D)
        else:
            xc = res
    return out
```

```python
import functools

import jax
import jax.numpy as jnp
import numpy as np
from jax import lax
from jax.experimental import pallas as pl
from jax.experimental.pallas import tpu as pltpu

F32 = jnp.float32
BF16 = jnp.bfloat16
I32 = jnp.int32
U32 = jnp.uint32

D = 1024
B = 8
S = 2048
C = 256
DEPTH = 2
GRID_W = 64
HD = 64
A_HEADS = 6
A_KV = 2
A_WIN = 128
A_BLK = 256
ROPE_BASE = 10000.0
SGU_GROUPS = 4
SGU_CHUNK = 128
C_HEADS = 6
C_WIN_R = 8
C_WIN_C = 16
A_W = A_HEADS * HD
B_W = SGU_GROUPS * HD
C_W = C_HEADS * HD
KV_W = A_KV * HD
IN_W = A_W + 2 * KV_W + 2 * B_W + 3 * C_W
N_EXP = 64
TOP_K = 8
N_GRP = 8
TOPK_GRP = 4
EXP_FF = 256
ROUTED_SCALE = 2.5
MOE_BLK = 512
EPS = 1e-6
NEG = -1e30
SCALE = HD ** -0.5

N_LAT = B * S
N_CTX = B * C
N_ALL = N_LAT + N_CTX
ROWS = S // GRID_W

HIGHEST = lax.Precision.HIGHEST
ARB = pltpu.ARBITRARY

NT_DIMS = (((1,), (1,)), ((), ()))


def _cparams(n_axes, vmem_mb=48):
    return pltpu.CompilerParams(
        dimension_semantics=(ARB,) * n_axes, vmem_limit_bytes=vmem_mb * 1024 * 1024)


def _mod_row(i, tm):
    return jnp.where(i < N_LAT // tm, i // (S // tm), B)


def _rms(x):
    return x * lax.rsqrt(jnp.mean(x * x, axis=-1, keepdims=True) + EPS)


MOD_TN = 1024


def _mod_body(cc_ref, w_ref, b_ref, o_ref):
    a = cc_ref[...]
    a = a * jax.nn.sigmoid(a)
    o_ref[...] = jnp.dot(a.astype(BF16), w_ref[...].astype(BF16),
                         preferred_element_type=F32) + b_ref[...]


def _modulation(cc, mod_w, mod_b):
    n_col = 6 * D // MOD_TN
    return pl.pallas_call(
        _mod_body,
        grid=(DEPTH, n_col),
        in_specs=[
            pl.BlockSpec((16, D), lambda l, j: (0, 0)),
            pl.BlockSpec((None, D, MOD_TN), lambda l, j: (l, 0, j)),
            pl.BlockSpec((None, 1, MOD_TN), lambda l, j: (l, 0, j)),
        ],
        out_specs=pl.BlockSpec((None, 16, MOD_TN), lambda l, j: (l, 0, j)),
        out_shape=jax.ShapeDtypeStruct((DEPTH, 16, 6 * D), F32),
        compiler_params=_cparams(2),
        name="modulation",
    )(cc, mod_w, mod_b.reshape(DEPTH, 1, 6 * D))


IN_TM = 256
ROT_W = A_W + KV_W


def _in_body(x_ref, m_ref, g_ref, w_ref, cos_ref, sin_ref, za_ref, zc_ref, uv_ref):
    x = x_ref[...]
    h = _rms(x) * g_ref[...]
    h = h * (1.0 + m_ref[:, D:2 * D]) + m_ref[:, 0:D]
    z = jnp.dot(h.astype(BF16), w_ref[...], preferred_element_type=F32)
    qk = z[:, :ROT_W]
    lane = lax.broadcasted_iota(I32, qk.shape, 1)
    rot = jnp.where((lane & 16) == 0,
                    -pltpu.roll(qk, ROT_W - 16, 1), pltpu.roll(qk, 16, 1))
    qk = qk * cos_ref[...] + rot * sin_ref[...]
    za_ref[...] = jnp.concatenate(
        [qk[:, :A_W] * SCALE, qk[:, A_W:], z[:, ROT_W:ROT_W + KV_W]], axis=1).astype(BF16)
    u0 = ROT_W + KV_W
    uv_ref[...] = jax.nn.gelu(z[:, u0:u0 + 2 * B_W])
    c0 = u0 + 2 * B_W
    zc_ref[...] = jnp.concatenate(
        [z[:, c0:c0 + C_W] * SCALE, z[:, c0 + C_W:]], axis=1).astype(BF16)


def _in_proj(xc, mods, g, w_bf, cos_t, sin_t):
    tm = IN_TM
    n_t = N_ALL // tm

    def tab_idx(i):
        return (jnp.where(i < N_LAT // tm, i % (S // tm), S // tm), 0)

    return pl.pallas_call(
        _in_body,
        grid=(n_t,),
        in_specs=[
            pl.BlockSpec((tm, D), lambda i: (i, 0)),
            pl.BlockSpec((None, 1, 6 * D), lambda i: (_mod_row(i, tm), 0, 0)),
            pl.BlockSpec((1, D), lambda i: (0, 0)),
            pl.BlockSpec((D, IN_W), lambda i: (0, 0)),
            pl.BlockSpec((tm, ROT_W), tab_idx),
            pl.BlockSpec((tm, ROT_W), tab_idx),
        ],
        out_specs=[
            pl.BlockSpec((tm, A_W + 2 * KV_W), lambda i: (i, 0)),
            pl.BlockSpec((tm, 3 * C_W), lambda i: (i, 0)),
            pl.BlockSpec((tm, 2 * B_W), lambda i: (i, 0)),
        ],
        out_shape=[
            jax.ShapeDtypeStruct((N_ALL, A_W + 2 * KV_W), BF16),
            jax.ShapeDtypeStruct((N_ALL, 3 * C_W), BF16),
            jax.ShapeDtypeStruct((N_ALL, 2 * B_W), F32),
        ],
        compiler_params=_cparams(1),
        name="in_proj",
    )(xc, mods, g, w_bf, cos_t, sin_t)


A_BAND = A_BLK + 2 * A_WIN
N_QB_LAT = N_LAT // A_BLK
QB_PER_SEQ = S // A_BLK
QB_PER_CTX = C // A_BLK


def _softmax_pv(s_list, v_list, extra_logit=None):
    m = s_list[0].max(axis=-1, keepdims=True)
    for s in s_list[1:]:
        m = jnp.maximum(m, s.max(axis=-1, keepdims=True))
    if extra_logit is not None:
        m = jnp.maximum(m, extra_logit)
    den = None
    out = None
    for s, v in zip(s_list, v_list):
        p = jnp.exp(s - m)
        d = p.sum(axis=-1, keepdims=True)
        o = jnp.dot(p.astype(BF16), v, preferred_element_type=F32)
        den = d if den is None else den + d
        out = o if out is None else out + o
    if extra_logit is not None:
        den = den + jnp.exp(extra_logit - m)
    return out / den


def _attn_a_body(sink_ref, q_ref, kb_ref, vb_ref, kc_ref, vc_ref, o_ref):
    i = pl.program_id(0)
    is_lat = i < N_QB_LAT
    n = i % QB_PER_SEQ
    start = pl.multiple_of(jnp.clip(n * A_BLK - A_WIN, 0, S - A_BAND), A_WIN)
    qpos = n * A_BLK + lax.broadcasted_iota(I32, (A_BLK, A_BAND), 0)
    kpos = start + lax.broadcasted_iota(I32, (A_BLK, A_BAND), 1)
    mask = jnp.abs(kpos - qpos) <= jnp.where(is_lat, A_WIN, -1)
    kb = kb_ref[pl.ds(start, A_BAND), :]
    vb = vb_ref[pl.ds(start, A_BAND), :]
    kc = kc_ref[...]
    vc = vc_ref[...]
    outs = []
    for h in range(A_HEADS):
        kv = h // (A_HEADS // A_KV)
        sl = slice(kv * HD, (kv + 1) * HD)
        q = q_ref[:, h * HD:(h + 1) * HD]
        s_b = lax.dot_general(q, kb[:, sl], NT_DIMS, preferred_element_type=F32)
        s_b = jnp.where(mask, s_b, NEG)
        s_c = lax.dot_general(q, kc[:, sl], NT_DIMS, preferred_element_type=F32)
        outs.append(_softmax_pv([s_c, s_b], [vc[:, sl], vb[:, sl]], sink_ref[h]))
    o_ref[...] = jnp.concatenate(outs, axis=1)


def _attn_a(za, sink, with_ctx):
    n_qb = N_QB_LAT + (N_CTX // A_BLK if with_ctx else 0)

    def bidx(i):
        return jnp.where(i < N_QB_LAT, i // QB_PER_SEQ, (i - N_QB_LAT) // QB_PER_CTX)

    k_col = A_W // KV_W
    v_col = k_col + 1
    return pl.pallas_call(
        _attn_a_body,
        grid=(n_qb,),
        in_specs=[
            pl.BlockSpec(memory_space=pltpu.SMEM),
            pl.BlockSpec((A_BLK, A_W), lambda i: (i, 0)),
            pl.BlockSpec((S, KV_W), lambda i: (bidx(i), k_col)),
            pl.BlockSpec((S, KV_W), lambda i: (bidx(i), v_col)),
            pl.BlockSpec((C, KV_W), lambda i: (N_LAT // C + bidx(i), k_col)),
            pl.BlockSpec((C, KV_W), lambda i: (N_LAT // C + bidx(i), v_col)),
        ],
        out_specs=pl.BlockSpec((A_BLK, A_W), lambda i: (i, 0)),
        out_shape=jax.ShapeDtypeStruct((n_qb * A_BLK, A_W), F32),
        compiler_params=_cparams(1),
        name="attn_window",
    )(sink, za, za, za, za, za)


NA_R = 4
NA_TQ = NA_R * GRID_W
NA_KROWS = 12
NA_WIN = NA_KROWS * GRID_W
NA_STEPS = ROWS // NA_R
NA_PAIRS = NA_KROWS // 2
N_DR = 2 * C_WIN_R - 1


def _na_body(q_ref, k_ref, v_ref, kc_ref, vc_ref, tab_ref, o_ref):
    j = pl.program_id(1)
    is_lat = j < NA_STEPS
    r0 = jnp.minimum(j, NA_STEPS - 1) * NA_R
    u0 = jnp.clip(r0 - C_WIN_R // 2, 0, ROWS - NA_KROWS)
    k0 = pl.multiple_of(u0 * GRID_W, GRID_W)
    kw = k_ref[pl.ds(k0, NA_WIN), :]
    vw = v_ref[pl.ds(k0, NA_WIN), :]
    kc = kc_ref[...]
    vc = vc_ref[...]
    q = q_ref[...]
    left = lax.broadcasted_iota(I32, (1, 2 * GRID_W), 1) < GRID_W

    tab_idx = []
    penalty = []
    for rr in range(NA_R):
        r = r0 + rr
        start = jnp.clip(r - C_WIN_R // 2, 0, ROWS - C_WIN_R)
        idx_row = []
        pen_row = []
        for p in range(NA_PAIRS):
            kr = u0 + 2 * p
            idx_row.append(jnp.clip(kr - r + (C_WIN_R - 1), -1, N_DR - 1) + 1)
            pens = []
            for half in range(2):
                ok = jnp.logical_and(is_lat, jnp.logical_and(kr + half >= start,
                                                             kr + half < start + C_WIN_R))
                pens.append(jnp.where(ok, 0.0, NEG))
            pen_row.append(jnp.where(left, pens[0], pens[1]))
        tab_idx.append(idx_row)
        penalty.append(jnp.concatenate(pen_row, axis=1))

    outs = []
    for h in range(C_HEADS):
        sl = slice(h * HD, (h + 1) * HD)
        qh = q[:, sl]
        s_w = lax.dot_general(qh, kw[:, sl], NT_DIMS, preferred_element_type=F32)
        bias = jnp.concatenate([
            jnp.concatenate([tab_ref[h, pl.ds(tab_idx[rr][p], 1)][0] for p in range(NA_PAIRS)],
                            axis=1) + penalty[rr]
            for rr in range(NA_R)], axis=0)
        s_c = lax.dot_general(qh, kc[:, sl], NT_DIMS, preferred_element_type=F32)
        outs.append(_softmax_pv([s_c, s_w + bias], [vc[:, sl], vw[:, sl]]))
    o_ref[...] = jnp.concatenate(outs, axis=1)


def _na(zc, tab, with_ctx):
    n_j = NA_STEPS + (1 if with_ctx else 0)

    def qidx(b, j):
        return jnp.where(j < NA_STEPS, b * NA_STEPS + j, N_LAT // NA_TQ + b)

    n_out = N_LAT + (N_CTX if with_ctx else 0)
    return pl.pallas_call(
        _na_body,
        grid=(B, n_j),
        in_specs=[
            pl.BlockSpec((NA_TQ, C_W), lambda b, j: (qidx(b, j), 0)),
            pl.BlockSpec((S, C_W), lambda b, j: (b, 1)),
            pl.BlockSpec((S, C_W), lambda b, j: (b, 2)),
            pl.BlockSpec((C, C_W), lambda b, j: (N_LAT // C + b, 1)),
            pl.BlockSpec((C, C_W), lambda b, j: (N_LAT // C + b, 2)),
            pl.BlockSpec((C_HEADS, N_DR + 1, GRID_W, 2 * GRID_W), lambda b, j: (0, 0, 0, 0)),
        ],
        out_specs=pl.BlockSpec((NA_TQ, C_W), lambda b, j: (qidx(b, j), 0)),
        out_shape=jax.ShapeDtypeStruct((n_out, C_W), F32),
        compiler_params=_cparams(2),
        name="attn_neighbourhood",
    )(zc, zc, zc, zc, zc, tab)


def _na_bias_table(rpb):
    cq = np.arange(GRID_W)
    col_start = np.clip(cq - C_WIN_C // 2, 0, GRID_W - C_WIN_C)
    col_ok = (cq[None, :] >= col_start[:, None]) & (cq[None, :] < col_start[:, None] + C_WIN_C)
    dc = np.clip(cq[None, :] - cq[:, None], -(C_WIN_C - 1), C_WIN_C - 1) + (C_WIN_C - 1)
    t = rpb.astype(F32)[:, :, dc]
    t = jnp.where(col_ok[None, None], t, NEG)
    zero = jnp.zeros((C_HEADS, 1, GRID_W, GRID_W), F32)
    ext = jnp.concatenate([zero, t, zero], axis=1)
    return jnp.concatenate([ext[:, :-1], ext[:, 1:]], axis=-1)


MG_TM = 256


def _merge_body(x_ref, ya_ref, yc_ref, uv_ref, m_ref, gg_ref, sg_ref, sw_ref, sb_ref, bd_ref,
                wo_ref, g2_ref, rw_ref, xo_ref, hx_ref, lg_ref):
    u = uv_ref[:, :B_W]
    v = uv_ref[:, B_W:]
    ms = jnp.dot(v * v, bd_ref[...], precision=HIGHEST, preferred_element_type=F32)
    vn = (v * lax.rsqrt(ms + EPS) * sg_ref[...]).astype(BF16)
    lane_grp = lax.broadcasted_iota(I32, (SGU_CHUNK, B_W), 1) // HD
    gates = []
    for c in range(MG_TM // SGU_CHUNK):
        vc = vn[c * SGU_CHUNK:(c + 1) * SGU_CHUNK, :]
        gate = sb_ref[...]
        acc = jnp.zeros((SGU_CHUNK, B_W), F32)
        for g in range(SGU_GROUPS):
            r = jnp.dot(sw_ref[g], vc, preferred_element_type=F32)
            acc = jnp.where(lane_grp == g, r, acc)
        gates.append(acc + gate)
    yb = u * jnp.concatenate(gates, axis=0)
    gg = gg_ref[...]
    ycat = jnp.concatenate([
        _rms(ya_ref[...]) * gg[:, :A_W],
        _rms(yb) * gg[:, A_W:A_W + B_W],
        _rms(yc_ref[...]) * gg[:, A_W + B_W:],
    ], axis=1)
    proj = jnp.dot(ycat.astype(BF16), wo_ref[...], preferred_element_type=F32)
    xn = x_ref[...] + m_ref[:, 2 * D:3 * D] * proj
    xo_ref[...] = xn
    hx = _rms(xn) * g2_ref[...]
    hx = hx * (1.0 + m_ref[:, 4 * D:5 * D]) + m_ref[:, 3 * D:4 * D]
    hx_ref[...] = hx
    lg_ref[...] = lax.dot_general(rw_ref[...], hx, NT_DIMS, precision=HIGHEST,
                                  preferred_element_type=F32)


def _merge(xc, ya, yc, uv, mods, gg, sg, sw_bf, sb_tab, bd, wo_bf, g2, rw_t, n_rows):
    tm = MG_TM
    n_t = n_rows // tm
    const2 = lambda i: (0, 0)
    row = lambda i: (i, 0)
    return pl.pallas_call(
        _merge_body,
        grid=(n_t,),
        in_specs=[
            pl.BlockSpec((tm, D), row),
            pl.BlockSpec((tm, A_W), row),
            pl.BlockSpec((tm, C_W), row),
            pl.BlockSpec((tm, 2 * B_W), row),
            pl.BlockSpec((None, 1, 6 * D), lambda i: (_mod_row(i, tm), 0, 0)),
            pl.BlockSpec((1, D), const2),
            pl.BlockSpec((1, B_W), const2),
            pl.BlockSpec((SGU_GROUPS, SGU_CHUNK, SGU_CHUNK), lambda i: (0, 0, 0)),
            pl.BlockSpec((SGU_CHUNK, B_W), const2),
            pl.BlockSpec((B_W, B_W), const2),
            pl.BlockSpec((D, D), const2),
            pl.BlockSpec((1, D), const2),
            pl.BlockSpec((N_EXP, D), const2),
        ],
        out_specs=[
            pl.BlockSpec((tm, D), row),
            pl.BlockSpec((tm, D), row),
            pl.BlockSpec((N_EXP, tm), lambda i: (0, i)),
        ],
        out_shape=[
            jax.ShapeDtypeStruct((n_rows, D), F32),
            jax.ShapeDtypeStruct((n_rows, D), F32),
            jax.ShapeDtypeStruct((N_EXP, n_rows), F32),
        ],
        compiler_params=_cparams(1),
        name="merge",
    )(xc, ya, yc, uv, mods, gg, sg, sw_bf, sb_tab, bd, wo_bf, g2, rw_t)


RT_TM = 256
GRP_SZ = N_EXP // N_GRP
SLOT_ALIGN = 8


def _first_argmax(v, iota):
    m = v.max(axis=0, keepdims=True)
    idx = jnp.where(v == m, iota, float(v.shape[0])).min(axis=0, keepdims=True)
    return m, idx


def _stack_rows(rows, iota):
    out = jnp.zeros(iota.shape, F32)
    for r, v in enumerate(rows):
        out = jnp.where(iota == float(r), v, out)
    return out


def _route_body(lg_ref, rb_ref, w_ref, lp_ref, tc_ref):
    tm = RT_TM
    scores = jax.nn.sigmoid(lg_ref[...])
    sel = scores + rb_ref[...]
    iota_g = lax.broadcasted_iota(I32, (GRP_SZ, tm), 0).astype(F32)
    gs = []
    for g in range(N_GRP):
        v = sel[g * GRP_SZ:(g + 1) * GRP_SZ, :]
        m1, i1 = _first_argmax(v, iota_g)
        m2 = jnp.where(iota_g == i1, -jnp.inf, v).max(axis=0, keepdims=True)
        gs.append(m1 + m2)
    iota_n = lax.broadcasted_iota(I32, (N_GRP, tm), 0).astype(F32)
    gscore = _stack_rows(gs, iota_n)
    gsel = jnp.zeros((N_GRP, tm), F32)
    for _ in range(TOPK_GRP):
        _, gi = _first_argmax(gscore, iota_n)
        hit = iota_n == gi
        gsel = jnp.where(hit, 1.0, gsel)
        gscore = jnp.where(hit, -jnp.inf, gscore)
    emask = jnp.concatenate(
        [jnp.broadcast_to(gsel[g:g + 1, :], (GRP_SZ, tm)) for g in range(N_GRP)], axis=0)
    cand = jnp.where(emask > 0.5, sel, NEG)
    iota_e = lax.broadcasted_iota(I32, (N_EXP, tm), 0).astype(F32)
    hits = []
    ws = []
    member = jnp.zeros((N_EXP, tm), F32)
    for _ in range(TOP_K):
        _, ei = _first_argmax(cand, iota_e)
        hit = iota_e == ei
        hits.append(hit)
        ws.append(jnp.where(hit, scores, 0.0).sum(axis=0, keepdims=True))
        member = jnp.where(hit, 1.0, member)
        cand = jnp.where(hit, -jnp.inf, cand)
    wsum = ws[0]
    for w in ws[1:]:
        wsum = wsum + w
    iota_k = lax.broadcasted_iota(I32, (TOP_K, tm), 0).astype(F32)
    w_ref[...] = _stack_rows(ws, iota_k) / wsum * ROUTED_SCALE
    r_i = lax.broadcasted_iota(I32, (tm, tm), 0)
    c_i = lax.broadcasted_iota(I32, (tm, tm), 1)
    tri = jnp.where(r_i < c_i, 1.0, 0.0).astype(BF16)
    local = jnp.dot(member.astype(BF16), tri, preferred_element_type=F32)
    tile_cnt = member.sum(axis=1, keepdims=True)
    tc_ref[...] = tile_cnt
    aligned = jnp.ceil(tile_cnt / SLOT_ALIGN) * SLOT_ALIGN
    e_r = lax.broadcasted_iota(I32, (N_EXP, N_EXP), 0)
    e_c = lax.broadcasted_iota(I32, (N_EXP, N_EXP), 1)
    below = jnp.where(e_c < e_r, 1.0, 0.0)
    start = jnp.dot(below, jnp.broadcast_to(aligned, (N_EXP, 128)), precision=HIGHEST,
                    preferred_element_type=F32)[:, 0:1]
    pos = local + start
    lp_ref[...] = _stack_rows(
        [jnp.where(hit, pos, 0.0).sum(axis=0, keepdims=True) for hit in hits],
        iota_k).astype(I32)


def _route(logits_t, rbias, n_tok):
    tm = RT_TM
    tok = lambda i: (0, i)
    return pl.pallas_call(
        _route_body,
        grid=(n_tok // tm,),
        in_specs=[
            pl.BlockSpec((N_EXP, tm), tok),
            pl.BlockSpec((N_EXP, 1), lambda i: (0, 0)),
        ],
        out_specs=[
            pl.BlockSpec((TOP_K, tm), tok),
            pl.BlockSpec((TOP_K, tm), tok),
            pl.BlockSpec((None, N_EXP, 1), lambda i: (i, 0, 0)),
        ],
        out_shape=[
            jax.ShapeDtypeStruct((TOP_K, n_tok), F32),
            jax.ShapeDtypeStruct((TOP_K, n_tok), I32),
            jax.ShapeDtypeStruct((n_tok // tm, N_EXP, 1), F32),
        ],
        compiler_params=_cparams(1),
        name="route",
    )(logits_t, rbias)


DP_TM = RT_TM
SORT_ROWS = 2560
N_GRPS = SORT_ROWS // SLOT_ALIGN
HI_MASK = 0xFFFF0000


def _pack_bf16_pairs(v):
    bits = lax.bitcast_convert_type(v, U32)
    return (bits[:, :D // 2] >> 16) | (bits[:, D // 2:] & jnp.uint32(HI_MASK))


def _unpack_bf16_pairs(bits):
    lo = lax.bitcast_convert_type(bits << 16, F32)
    hi = lax.bitcast_convert_type(bits & jnp.uint32(HI_MASK), F32)
    return jnp.concatenate([lo, hi], axis=1).astype(BF16)


def _group_copies(ngrp, row_of, make_copy):
    def start(g, carry):
        make_copy(g, row_of(g)).start()
        return carry

    def wait(g, carry):
        make_copy(0, 0).wait()
        return carry

    return (lambda: lax.fori_loop(0, ngrp, start, 0)), (lambda: lax.fori_loop(0, ngrp, wait, 0))


def _dispatch_body(ngrp_ref, fill_ref, grow_ref, hx_ref, lp_ref, xs_ref, zbuf_ref, zero_ref, sem):
    i = pl.program_id(0)
    n_tiles = pl.num_programs(0)
    slot = i % 2

    def zero_copy(b):
        dst = xs_ref.at[pl.ds(pl.multiple_of(b * MOE_BLK, MOE_BLK), MOE_BLK)]
        return pltpu.make_async_copy(zero_ref, dst, sem)

    @pl.when(i == 0)
    def _():
        zero_ref[...] = jnp.zeros_like(zero_ref)
        n_fill = fill_ref[0]

        def z_start(q, carry):
            @pl.when(fill_ref[1 + q] >= 0)
            def _():
                zero_copy(fill_ref[1 + q]).start()
            return carry

        def z_wait(q, carry):
            @pl.when(fill_ref[1 + q] >= 0)
            def _():
                zero_copy(0).wait()
            return carry

        lax.fori_loop(0, n_fill, z_start, 0)
        lax.fori_loop(0, n_fill, z_wait, 0)

    def group_copy(s, g, row):
        src = zbuf_ref.at[s, pl.ds(pl.multiple_of(g * SLOT_ALIGN, SLOT_ALIGN), SLOT_ALIGN)]
        dst = xs_ref.at[pl.ds(pl.multiple_of(row, SLOT_ALIGN), SLOT_ALIGN)]
        return pltpu.make_async_copy(src, dst, sem)

    x = hx_ref[...].astype(BF16)
    lpos = lp_ref[...]
    row_iota = lax.broadcasted_iota(I32, (SORT_ROWS, DP_TM), 0).astype(jnp.int16)
    onehot = jnp.zeros((SORT_ROWS, DP_TM), BF16)
    for k in range(TOP_K):
        pos_k = jnp.broadcast_to(lpos[k:k + 1, :], (16, DP_TM)).astype(jnp.int16)
        onehot = jnp.where(row_iota == jnp.tile(pos_k, (SORT_ROWS // 16, 1)),
                           jnp.ones((), BF16), onehot)
    z = jnp.dot(onehot, x, preferred_element_type=F32)
    zbuf_ref[slot] = _pack_bf16_pairs(z)

    @pl.when(i > 0)
    def _():
        _, wait_prev = _group_copies(ngrp_ref[jnp.maximum(i - 1, 0)], None, lambda g, r: group_copy(0, 0, 0))
        wait_prev()

    start_cur, wait_cur = _group_copies(
        ngrp_ref[i], lambda g: grow_ref[0, g], lambda g, r: group_copy(slot, g, r))
    start_cur()

    @pl.when(i == n_tiles - 1)
    def _():
        wait_cur()


def _dispatch(ngrp, fill, grp_row, hx, lpos, n_tok, n_slots):
    return pl.pallas_call(
        _dispatch_body,
        grid_spec=pltpu.PrefetchScalarGridSpec(
            num_scalar_prefetch=2,
            grid=(n_tok // DP_TM,),
            in_specs=[
                pl.BlockSpec((None, 1, N_GRPS), lambda i, *_: (i, 0, 0), memory_space=pltpu.SMEM),
                pl.BlockSpec((DP_TM, D), lambda i, *_: (i, 0)),
                pl.BlockSpec((TOP_K, DP_TM), lambda i, *_: (0, i)),
            ],
            out_specs=pl.BlockSpec(memory_space=pl.ANY),
            scratch_shapes=[
                pltpu.VMEM((2, SORT_ROWS, D // 2), U32),
                pltpu.VMEM((MOE_BLK, D // 2), U32),
                pltpu.SemaphoreType.DMA,
            ],
        ),
        out_shape=jax.ShapeDtypeStruct((n_slots, D // 2), U32),
        compiler_params=_cparams(1, vmem_mb=56),
        name="moe_dispatch",
    )(ngrp, fill, grp_row, hx, lpos)


def _experts_body(bexp_ref, xblk_ref, nused_ref, xs_ref, wg_ref, wu_ref, wd_ref, ys_ref):
    j = pl.program_id(0)

    @pl.when(j < nused_ref[0])
    def _():
        x = _unpack_bf16_pairs(xs_ref[...])
        g = jnp.dot(x, wg_ref[...].astype(BF16), preferred_element_type=F32)
        u = jnp.dot(x, wu_ref[...].astype(BF16), preferred_element_type=F32)
        h = (g * jax.nn.sigmoid(g) * u).astype(BF16)
        y = jnp.dot(h, wd_ref[...].astype(BF16), preferred_element_type=F32)
        ys_ref[...] = _pack_bf16_pairs(y.astype(BF16).astype(F32))

    @pl.when(j >= nused_ref[0])
    def _():
        ys_ref[...] = jnp.zeros_like(ys_ref)


def _experts(bexp, xblk, nused, xs, wg, wu, wd, n_blk, layer):
    w_idx = lambda j, be, xb, nu: (layer, be[j], 0, 0)
    return pl.pallas_call(
        _experts_body,
        grid_spec=pltpu.PrefetchScalarGridSpec(
            num_scalar_prefetch=3,
            grid=(n_blk,),
            in_specs=[
                pl.BlockSpec((MOE_BLK, D // 2), lambda j, be, xb, nu: (xb[j], 0)),
                pl.BlockSpec((None, None, D, EXP_FF), w_idx),
                pl.BlockSpec((None, None, D, EXP_FF), w_idx),
                pl.BlockSpec((None, None, EXP_FF, D), w_idx),
            ],
            out_specs=pl.BlockSpec((MOE_BLK, D // 2), lambda j, be, xb, nu: (j, 0)),
        ),
        out_shape=jax.ShapeDtypeStruct((n_blk * MOE_BLK, D // 2), U32),
        compiler_params=_cparams(1),
        name="moe_experts",
    )(bexp, xblk, nused, xs, wg, wu, wd)


def _combine_body(ngrp_ref, grow_ref, ys_ref, lp_ref, w_ref, x_ref, hx_ref, m_ref, sg_ref, su_ref,
                  sd_ref, fg_ref, o_ref, ybuf_ref, sem, *, final_norm):
    i = pl.program_id(0)

    @pl.when(i == 0)
    def _():
        ybuf_ref[...] = jnp.zeros_like(ybuf_ref)

    def group_copy(g, row):
        src = ys_ref.at[pl.ds(pl.multiple_of(row, SLOT_ALIGN), SLOT_ALIGN)]
        dst = ybuf_ref.at[pl.ds(pl.multiple_of(g * SLOT_ALIGN, SLOT_ALIGN), SLOT_ALIGN)]
        return pltpu.make_async_copy(src, dst, sem)

    start, wait = _group_copies(ngrp_ref[i], lambda g: grow_ref[0, g], group_copy)
    start()

    hx = hx_ref[...].astype(BF16)
    g = jnp.dot(hx, sg_ref[...], preferred_element_type=F32)
    u = jnp.dot(hx, su_ref[...], preferred_element_type=F32)
    h = (g * jax.nn.sigmoid(g) * u).astype(BF16)
    y = jnp.dot(h, sd_ref[...], preferred_element_type=F32)

    lpos = lp_ref[...]
    w = w_ref[...]
    col_iota = lax.broadcasted_iota(I32, (DP_TM, SORT_ROWS), 1).astype(jnp.int16)
    unsort = jnp.zeros((DP_TM, SORT_ROWS), BF16)
    for k in range(TOP_K):
        pos_k = jnp.broadcast_to(lpos[:, k:k + 1], (DP_TM, 128)).astype(jnp.int16)
        w_k = jnp.broadcast_to(w[:, k:k + 1], (DP_TM, 128)).astype(BF16)
        unsort = jnp.where(col_iota == jnp.tile(pos_k, (1, SORT_ROWS // 128)),
                           jnp.tile(w_k, (1, SORT_ROWS // 128)), unsort)

    wait()
    y = y + jnp.dot(unsort, _unpack_bf16_pairs(ybuf_ref[...]), preferred_element_type=F32)
    out = x_ref[...] + m_ref[:, 5 * D:6 * D] * y
    if final_norm:
        out = _rms(out) * fg_ref[...]
    o_ref[...] = out


def _combine(ngrp, grp_row, ys, lpos_t, w_t, x, hx, mods, sg_bf, su_bf, sd_bf, fg, n_tok, final_norm):
    tm = DP_TM
    row = lambda i, *_: (i, 0)
    const2 = lambda i, *_: (0, 0)
    return pl.pallas_call(
        functools.partial(_combine_body, final_norm=final_norm),
        grid_spec=pltpu.PrefetchScalarGridSpec(
            num_scalar_prefetch=1,
            grid=(n_tok // tm,),
            in_specs=[
                pl.BlockSpec((None, 1, N_GRPS), lambda i, *_: (i, 0, 0), memory_space=pltpu.SMEM),
                pl.BlockSpec(memory_space=pl.ANY),
                pl.BlockSpec((tm, TOP_K), row),
                pl.BlockSpec((tm, TOP_K), row),
                pl.BlockSpec((tm, D), row),
                pl.BlockSpec((tm, D), row),
                pl.BlockSpec((None, 1, 6 * D), lambda i, *_: (_mod_row(i, tm), 0, 0)),
                pl.BlockSpec((D, EXP_FF), const2),
                pl.BlockSpec((D, EXP_FF), const2),
                pl.BlockSpec((EXP_FF, D), const2),
                pl.BlockSpec((1, D), const2),
            ],
            out_specs=pl.BlockSpec((tm, D), row),
            scratch_shapes=[pltpu.VMEM((SORT_ROWS, D // 2), U32), pltpu.SemaphoreType.DMA],
        ),
        out_shape=jax.ShapeDtypeStruct((n_tok, D), F32),
        compiler_params=_cparams(1, vmem_mb=56),
        name="moe_combine",
    )(ngrp, grp_row, ys, lpos_t, w_t, x, hx, mods, sg_bf, su_bf, sd_bf, fg)


def _moe(x_new, hx, logits_t, mods, rbias, wg, wu, wd, sg_bf, su_bf, sd_bf, fg, n_tok, layer,
         final_norm):
    w, lpos, tile_cnt = _route(logits_t, rbias, n_tok)
    n_tiles = n_tok // DP_TM
    n_blk = (n_tok * TOP_K + (SLOT_ALIGN - 1) * N_EXP * n_tiles) // MOE_BLK + N_EXP
    tcnt = tile_cnt[:, :, 0].astype(I32)
    cnt_al = (tcnt + SLOT_ALIGN - 1) // SLOT_ALIGN * SLOT_ALIGN
    loc_end = jnp.cumsum(cnt_al, axis=1)
    loc = loc_end - cnt_al
    slots_e = jnp.sum(cnt_al, axis=0)
    nblk_e = (slots_e + MOE_BLK - 1) // MOE_BLK
    blk_end = jnp.cumsum(nblk_e)
    pstart = (blk_end - nblk_e) * MOE_BLK
    nused = blk_end[-1:].astype(I32)
    blk_ids = jnp.arange(n_blk, dtype=I32)
    xblk = jnp.minimum(blk_ids, nused[0] - 1)
    bexp = jnp.minimum(
        jnp.sum(blk_end[None, :] <= xblk[:, None], axis=1), N_EXP - 1).astype(I32)
    off = pstart[None, :] + jnp.cumsum(cnt_al, axis=0) - cnt_al
    g_row = jnp.arange(N_GRPS, dtype=I32) * SLOT_ALIGN
    e_of_g = jnp.minimum(
        jnp.sum(loc_end[:, None, :] <= g_row[None, :, None], axis=2), N_EXP - 1)
    pick = e_of_g[..., None] == jnp.arange(N_EXP, dtype=I32)
    grp_row = (jnp.sum(jnp.where(pick, (off - loc)[:, None, :], 0), axis=2)
               + g_row[None, :]).astype(I32).reshape(n_tiles, 1, N_GRPS)
    ngrp = (loc_end[:, -1] // SLOT_ALIGN).astype(I32)
    unused = nused[0] + blk_ids
    fill = jnp.concatenate([
        (N_EXP + n_blk - nused[0])[None],
        jnp.where(nblk_e > 0, blk_end - 1, -1),
        jnp.where(unused < n_blk, unused, -1)]).astype(I32)
    xs = _dispatch(ngrp, fill, grp_row, hx, lpos, n_tok, n_blk * MOE_BLK)
    ys = _experts(bexp, xblk, nused, xs, wg, wu, wd, n_blk, layer)
    return _combine(ngrp, grp_row, ys, lpos.T, w.T, x_new, hx, mods, sg_bf, su_bf, sd_bf, fg,
                    n_tok, final_norm)


def _rope_tables():
    t = np.arange(S)
    row = (t // GRID_W).astype(np.float32)
    col = (t % GRID_W).astype(np.float32)
    half = HD // 2
    inv = jnp.asarray(ROPE_BASE, F32) ** (-jnp.arange(0, half, 2, dtype=F32) / half)
    ang_r = jnp.asarray(row)[:, None] * inv
    ang_c = jnp.asarray(col)[:, None] * inv
    ang = jnp.concatenate([ang_r, ang_r, ang_c, ang_c], axis=-1)
    n_rep = ROT_W // HD
    cos = jnp.tile(jnp.cos(ang), (1, n_rep))
    sin = jnp.tile(jnp.sin(ang), (1, n_rep))
    cos = jnp.concatenate([cos, jnp.ones((IN_TM, ROT_W), F32)], axis=0)
    sin = jnp.concatenate([sin, jnp.zeros((IN_TM, ROT_W), F32)], axis=0)
    return cos, sin


def kernel(x, c, ctx, c_ctx, mod_w, mod_b, norm1_g, w_in, attn_sink, sgu_norm_g, sgu_w, sgu_b, na_rpb,
           group_norm_g, w_out, norm2_g, router_w, router_bias, exp_w_gate, exp_w_up, exp_w_down,
           shared_w_gate, shared_w_up, shared_w_down, final_g):
    xc = jnp.concatenate([x.reshape(N_LAT, D), ctx.reshape(N_CTX, D)], axis=0)
    cc = jnp.concatenate([c, c_ctx[None, :], jnp.zeros((16 - B - 1, D), F32)], axis=0)
    mods_all = _modulation(cc, mod_w, mod_b)
    cos_t, sin_t = _rope_tables()
    seg = np.arange(B_W) // HD
    bd = jnp.asarray((seg[:, None] == seg[None, :]).astype(np.float32) / HD)
    fg = final_g.reshape(1, D)

    out = None
    for l in range(DEPTH):
        last = l == DEPTH - 1
        mods = mods_all[l].reshape(16, 1, 6 * D)
        za, zc, uv = _in_proj(xc, mods, norm1_g[l].reshape(1, D), w_in[l].astype(BF16), cos_t, sin_t)
        ya = _attn_a(za, attn_sink[l], with_ctx=not last)
        yc = _na(zc, _na_bias_table(na_rpb[l]), with_ctx=not last)
        n_tok = N_LAT if last else N_ALL
        sb_tab = jnp.repeat(sgu_b[l].T, HD, axis=1)
        x_new, hx, logits_t = _merge(
            xc, ya, yc, uv, mods, group_norm_g[l].reshape(1, D), sgu_norm_g[l].reshape(1, B_W),
            sgu_w[l].astype(BF16), sb_tab, bd, w_out[l].astype(BF16), norm2_g[l].reshape(1, D),
            router_w[l].T, n_tok)
        res = _moe(x_new, hx, logits_t, mods, router_bias[l].reshape(N_EXP, 1),
                   exp_w_gate, exp_w_up, exp_w_down,
                   shared_w_gate[l].astype(BF16), shared_w_up[l].astype(BF16),
                   shared_w_down[l].astype(BF16), fg, n_tok, layer=l, final_norm=last)
        if last:
            out = res.reshape(B, S, D)
        else:
            xc = res
    return out
```

```python
import functools

import jax
import jax.numpy as jnp
import numpy as np
from jax import lax
from jax.experimental import pallas as pl
from jax.experimental.pallas import tpu as pltpu

F32 = jnp.float32
BF16 = jnp.bfloat16
I32 = jnp.int32
U32 = jnp.uint32

D = 1024
B = 8
S = 2048
C = 256
DEPTH = 2
GRID_W = 64
HD = 64
A_HEADS = 6
A_KV = 2
A_WIN = 128
A_BLK = 256
ROPE_BASE = 10000.0
SGU_GROUPS = 4
SGU_CHUNK = 128
C_HEADS = 6
C_WIN_R = 8
C_WIN_C = 16
A_W = A_HEADS * HD
B_W = SGU_GROUPS * HD
C_W = C_HEADS * HD
KV_W = A_KV * HD
IN_W = A_W + 2 * KV_W + 2 * B_W + 3 * C_W
N_EXP = 64
TOP_K = 8
N_GRP = 8
TOPK_GRP = 4
EXP_FF = 256
ROUTED_SCALE = 2.5
MOE_BLK = 1024
EPS = 1e-6
NEG = -1e30
SCALE = HD ** -0.5

N_LAT = B * S
N_CTX = B * C
N_ALL = N_LAT + N_CTX
ROWS = S // GRID_W

HIGHEST = lax.Precision.HIGHEST
ARB = pltpu.ARBITRARY

NT_DIMS = (((1,), (1,)), ((), ()))


def _cparams(n_axes, vmem_mb=48):
    return pltpu.CompilerParams(
        dimension_semantics=(ARB,) * n_axes, vmem_limit_bytes=vmem_mb * 1024 * 1024)


def _mod_row(i, tm):
    return jnp.where(i < N_LAT // tm, i // (S // tm), B)


def _rms(x):
    return x * lax.rsqrt(jnp.mean(x * x, axis=-1, keepdims=True) + EPS)


def _split_bf16(x):
    hi = x.astype(BF16)
    return hi, (x - hi.astype(F32)).astype(BF16)


def _router_split(rw):
    hi, lo = _split_bf16(jnp.pad(rw, ((0, 0), (0, 128 - N_EXP))))
    return jnp.stack([hi, lo])


MOD_TN = 1024


def _mod_body(cc_ref, w_ref, b_ref, o_ref):
    a = cc_ref[...]
    a = a * jax.nn.sigmoid(a)
    o_ref[...] = jnp.dot(a.astype(BF16), w_ref[...].astype(BF16),
                         preferred_element_type=F32) + b_ref[...]


def _modulation(cc, mod_w, mod_b):
    n_col = 6 * D // MOD_TN
    return pl.pallas_call(
        _mod_body,
        grid=(DEPTH, n_col),
        in_specs=[
            pl.BlockSpec((16, D), lambda l, j: (0, 0)),
            pl.BlockSpec((None, D, MOD_TN), lambda l, j: (l, 0, j)),
            pl.BlockSpec((None, 1, MOD_TN), lambda l, j: (l, 0, j)),
        ],
        out_specs=pl.BlockSpec((None, 16, MOD_TN), lambda l, j: (l, 0, j)),
        out_shape=jax.ShapeDtypeStruct((DEPTH, 16, 6 * D), F32),
        compiler_params=_cparams(2),
        name="modulation",
    )(cc, mod_w, mod_b.reshape(DEPTH, 1, 6 * D))


IN_TM = 256
ROT_W = A_W + KV_W


def _in_body(x_ref, m_ref, g_ref, w_ref, cos_ref, sin_ref, za_ref, zc_ref, uv_ref):
    x = x_ref[...]
    h = _rms(x) * g_ref[...]
    h = h * (1.0 + m_ref[:, D:2 * D]) + m_ref[:, 0:D]
    z = jnp.dot(h.astype(BF16), w_ref[...], preferred_element_type=F32)
    qk = z[:, :ROT_W]
    lane = lax.broadcasted_iota(I32, qk.shape, 1)
    rot = jnp.where((lane & 16) == 0,
                    -pltpu.roll(qk, ROT_W - 16, 1), pltpu.roll(qk, 16, 1))
    qk = qk * cos_ref[...] + rot * sin_ref[...]
    za_ref[...] = jnp.concatenate(
        [qk[:, :A_W] * SCALE, qk[:, A_W:], z[:, ROT_W:ROT_W + KV_W]], axis=1).astype(BF16)
    u0 = ROT_W + KV_W
    uv_ref[...] = jax.nn.gelu(z[:, u0:u0 + 2 * B_W])
    c0 = u0 + 2 * B_W
    zc_ref[...] = jnp.concatenate(
        [z[:, c0:c0 + C_W] * SCALE, z[:, c0 + C_W:]], axis=1).astype(BF16)


def _in_proj(xc, mods, g, w_bf, cos_t, sin_t):
    tm = IN_TM
    n_t = N_ALL // tm

    def tab_idx(i):
        return (jnp.where(i < N_LAT // tm, i % (S // tm), S // tm), 0)

    return pl.pallas_call(
        _in_body,
        grid=(n_t,),
        in_specs=[
            pl.BlockSpec((tm, D), lambda i: (i, 0)),
            pl.BlockSpec((None, 1, 6 * D), lambda i: (_mod_row(i, tm), 0, 0)),
            pl.BlockSpec((1, D), lambda i: (0, 0)),
            pl.BlockSpec((D, IN_W), lambda i: (0, 0)),
            pl.BlockSpec((tm, ROT_W), tab_idx),
            pl.BlockSpec((tm, ROT_W), tab_idx),
        ],
        out_specs=[
            pl.BlockSpec((tm, A_W + 2 * KV_W), lambda i: (i, 0)),
            pl.BlockSpec((tm, 3 * C_W), lambda i: (i, 0)),
            pl.BlockSpec((tm, 2 * B_W), lambda i: (i, 0)),
        ],
        out_shape=[
            jax.ShapeDtypeStruct((N_ALL, A_W + 2 * KV_W), BF16),
            jax.ShapeDtypeStruct((N_ALL, 3 * C_W), BF16),
            jax.ShapeDtypeStruct((N_ALL, 2 * B_W), F32),
        ],
        compiler_params=_cparams(1),
        name="in_proj",
    )(xc, mods, g, w_bf, cos_t, sin_t)


A_BAND = A_BLK + 2 * A_WIN
N_QB_LAT = N_LAT // A_BLK
QB_PER_SEQ = S // A_BLK
QB_PER_CTX = C // A_BLK


def _softmax_pv(s_list, v_list, extra_logit=None):
    m = s_list[0].max(axis=-1, keepdims=True)
    for s in s_list[1:]:
        m = jnp.maximum(m, s.max(axis=-1, keepdims=True))
    if extra_logit is not None:
        m = jnp.maximum(m, extra_logit)
    den = None
    out = None
    for s, v in zip(s_list, v_list):
        p = jnp.exp(s - m)
        d = p.sum(axis=-1, keepdims=True)
        o = jnp.dot(p.astype(BF16), v, preferred_element_type=F32)
        den = d if den is None else den + d
        out = o if out is None else out + o
    if extra_logit is not None:
        den = den + jnp.exp(extra_logit - m)
    return out / den


def _attn_a_body(sink_ref, q_ref, kb_ref, vb_ref, kc_ref, vc_ref, o_ref):
    i = pl.program_id(0)
    is_lat = i < N_QB_LAT
    n = i % QB_PER_SEQ
    start = pl.multiple_of(jnp.clip(n * A_BLK - A_WIN, 0, S - A_BAND), A_WIN)
    qpos = n * A_BLK + lax.broadcasted_iota(I32, (A_BLK, A_BAND), 0)
    kpos = start + lax.broadcasted_iota(I32, (A_BLK, A_BAND), 1)
    mask = jnp.abs(kpos - qpos) <= jnp.where(is_lat, A_WIN, -1)
    kb = kb_ref[pl.ds(start, A_BAND), :]
    vb = vb_ref[pl.ds(start, A_BAND), :]
    kc = kc_ref[...]
    vc = vc_ref[...]
    outs = []
    for h in range(A_HEADS):
        kv = h // (A_HEADS // A_KV)
        sl = slice(kv * HD, (kv + 1) * HD)
        q = q_ref[:, h * HD:(h + 1) * HD]
        s_b = lax.dot_general(q, kb[:, sl], NT_DIMS, preferred_element_type=F32)
        s_b = jnp.where(mask, s_b, NEG)
        s_c = lax.dot_general(q, kc[:, sl], NT_DIMS, preferred_element_type=F32)
        outs.append(_softmax_pv([s_c, s_b], [vc[:, sl], vb[:, sl]], sink_ref[h]))
    o_ref[...] = jnp.concatenate(outs, axis=1)


def _attn_a(za, sink, with_ctx):
    n_qb = N_QB_LAT + (N_CTX // A_BLK if with_ctx else 0)

    def bidx(i):
        return jnp.where(i < N_QB_LAT, i // QB_PER_SEQ, (i - N_QB_LAT) // QB_PER_CTX)

    k_col = A_W // KV_W
    v_col = k_col + 1
    return pl.pallas_call(
        _attn_a_body,
        grid=(n_qb,),
        in_specs=[
            pl.BlockSpec(memory_space=pltpu.SMEM),
            pl.BlockSpec((A_BLK, A_W), lambda i: (i, 0)),
            pl.BlockSpec((S, KV_W), lambda i: (bidx(i), k_col)),
            pl.BlockSpec((S, KV_W), lambda i: (bidx(i), v_col)),
            pl.BlockSpec((C, KV_W), lambda i: (N_LAT // C + bidx(i), k_col)),
            pl.BlockSpec((C, KV_W), lambda i: (N_LAT // C + bidx(i), v_col)),
        ],
        out_specs=pl.BlockSpec((A_BLK, A_W), lambda i: (i, 0)),
        out_shape=jax.ShapeDtypeStruct((n_qb * A_BLK, A_W), F32),
        compiler_params=_cparams(1),
        name="attn_window",
    )(sink, za, za, za, za, za)


NA_R = 4
NA_TQ = NA_R * GRID_W
NA_KROWS = 12
NA_WIN = NA_KROWS * GRID_W
NA_STEPS = ROWS // NA_R
NA_PAIRS = NA_KROWS // 2
N_DR = 2 * C_WIN_R - 1


def _na_body(q_ref, k_ref, v_ref, kc_ref, vc_ref, tab_ref, o_ref):
    j = pl.program_id(1)
    is_lat = j < NA_STEPS
    r0 = jnp.minimum(j, NA_STEPS - 1) * NA_R
    u0 = jnp.clip(r0 - C_WIN_R // 2, 0, ROWS - NA_KROWS)
    k0 = pl.multiple_of(u0 * GRID_W, GRID_W)
    kw = k_ref[pl.ds(k0, NA_WIN), :]
    vw = v_ref[pl.ds(k0, NA_WIN), :]
    kc = kc_ref[...]
    vc = vc_ref[...]
    q = q_ref[...]
    left = lax.broadcasted_iota(I32, (1, 2 * GRID_W), 1) < GRID_W

    tab_idx = []
    penalty = []
    for rr in range(NA_R):
        r = r0 + rr
        start = jnp.clip(r - C_WIN_R // 2, 0, ROWS - C_WIN_R)
        idx_row = []
        pen_row = []
        for p in range(NA_PAIRS):
            kr = u0 + 2 * p
            idx_row.append(jnp.clip(kr - r + (C_WIN_R - 1), -1, N_DR - 1) + 1)
            pens = []
            for half in range(2):
                ok = jnp.logical_and(is_lat, jnp.logical_and(kr + half >= start,
                                                             kr + half < start + C_WIN_R))
                pens.append(jnp.where(ok, 0.0, NEG))
            pen_row.append(jnp.where(left, pens[0], pens[1]))
        tab_idx.append(idx_row)
        penalty.append(jnp.concatenate(pen_row, axis=1))

    outs = []
    for h in range(C_HEADS):
        sl = slice(h * HD, (h + 1) * HD)
        qh = q[:, sl]
        s_w = lax.dot_general(qh, kw[:, sl], NT_DIMS, preferred_element_type=F32)
        bias = jnp.concatenate([
            jnp.concatenate([tab_ref[h, pl.ds(tab_idx[rr][p], 1)][0] for p in range(NA_PAIRS)],
                            axis=1) + penalty[rr]
            for rr in range(NA_R)], axis=0)
        s_c = lax.dot_general(qh, kc[:, sl], NT_DIMS, preferred_element_type=F32)
        outs.append(_softmax_pv([s_c, s_w + bias], [vc[:, sl], vw[:, sl]]))
    o_ref[...] = jnp.concatenate(outs, axis=1)


def _na(zc, tab, with_ctx):
    n_j = NA_STEPS + (1 if with_ctx else 0)

    def qidx(b, j):
        return jnp.where(j < NA_STEPS, b * NA_STEPS + j, N_LAT // NA_TQ + b)

    n_out = N_LAT + (N_CTX if with_ctx else 0)
    return pl.pallas_call(
        _na_body,
        grid=(B, n_j),
        in_specs=[
            pl.BlockSpec((NA_TQ, C_W), lambda b, j: (qidx(b, j), 0)),
            pl.BlockSpec((S, C_W), lambda b, j: (b, 1)),
            pl.BlockSpec((S, C_W), lambda b, j: (b, 2)),
            pl.BlockSpec((C, C_W), lambda b, j: (N_LAT // C + b, 1)),
            pl.BlockSpec((C, C_W), lambda b, j: (N_LAT // C + b, 2)),
            pl.BlockSpec((C_HEADS, N_DR + 1, GRID_W, 2 * GRID_W), lambda b, j: (0, 0, 0, 0)),
        ],
        out_specs=pl.BlockSpec((NA_TQ, C_W), lambda b, j: (qidx(b, j), 0)),
        out_shape=jax.ShapeDtypeStruct((n_out, C_W), F32),
        compiler_params=_cparams(2),
        name="attn_neighbourhood",
    )(zc, zc, zc, zc, zc, tab)


def _na_bias_table(rpb):
    cq = np.arange(GRID_W)
    col_start = np.clip(cq - C_WIN_C // 2, 0, GRID_W - C_WIN_C)
    col_ok = (cq[None, :] >= col_start[:, None]) & (cq[None, :] < col_start[:, None] + C_WIN_C)
    dc = np.clip(cq[None, :] - cq[:, None], -(C_WIN_C - 1), C_WIN_C - 1) + (C_WIN_C - 1)
    t = rpb.astype(F32)[:, :, dc]
    t = jnp.where(col_ok[None, None], t, NEG)
    zero = jnp.zeros((C_HEADS, 1, GRID_W, GRID_W), F32)
    ext = jnp.concatenate([zero, t, zero], axis=1)
    return jnp.concatenate([ext[:, :-1], ext[:, 1:]], axis=-1)


MG_TM = 256


def _merge_body(x_ref, ya_ref, yc_ref, uv_ref, m_ref, gg_ref, sg_ref, sw_ref, sb_ref, bd_ref,
                wo_ref, g2_ref, rw_ref, xo_ref, hx_ref, lg_ref):
    u = uv_ref[:, :B_W]
    v = uv_ref[:, B_W:]
    vv_hi, vv_lo = _split_bf16(v * v)
    ms = (jnp.dot(vv_hi, bd_ref[...], preferred_element_type=F32)
          + jnp.dot(vv_lo, bd_ref[...], preferred_element_type=F32))
    vn = (v * lax.rsqrt(ms + EPS) * sg_ref[...]).astype(BF16)
    lane_grp = lax.broadcasted_iota(I32, (SGU_CHUNK, B_W), 1) // HD
    gates = []
    for c in range(MG_TM // SGU_CHUNK):
        vc = vn[c * SGU_CHUNK:(c + 1) * SGU_CHUNK, :]
        gate = sb_ref[...]
        acc = jnp.zeros((SGU_CHUNK, B_W), F32)
        for g in range(SGU_GROUPS):
            r = jnp.dot(sw_ref[g], vc, preferred_element_type=F32)
            acc = jnp.where(lane_grp == g, r, acc)
        gates.append(acc + gate)
    yb = u * jnp.concatenate(gates, axis=0)
    gg = gg_ref[...]
    ycat = jnp.concatenate([
        _rms(ya_ref[...]) * gg[:, :A_W],
        _rms(yb) * gg[:, A_W:A_W + B_W],
        _rms(yc_ref[...]) * gg[:, A_W + B_W:],
    ], axis=1)
    proj = jnp.dot(ycat.astype(BF16), wo_ref[...], preferred_element_type=F32)
    xn = x_ref[...] + m_ref[:, 2 * D:3 * D] * proj
    xo_ref[...] = xn
    hx = _rms(xn) * g2_ref[...]
    hx = hx * (1.0 + m_ref[:, 4 * D:5 * D]) + m_ref[:, 3 * D:4 * D]
    hx_ref[...] = hx
    hx_hi, hx_lo = _split_bf16(hx)
    logits = (jnp.dot(hx_hi, rw_ref[0], preferred_element_type=F32)
              + jnp.dot(hx_hi, rw_ref[1], preferred_element_type=F32)
              + jnp.dot(hx_lo, rw_ref[0], preferred_element_type=F32))
    lg_ref[...] = logits.T[:N_EXP, :]


def _merge(xc, ya, yc, uv, mods, gg, sg, sw_bf, sb_tab, bd, wo_bf, g2, rw_t, n_rows):
    tm = MG_TM
    n_t = n_rows // tm
    const2 = lambda i: (0, 0)
    row = lambda i: (i, 0)
    return pl.pallas_call(
        _merge_body,
        grid=(n_t,),
        in_specs=[
            pl.BlockSpec((tm, D), row),
            pl.BlockSpec((tm, A_W), row),
            pl.BlockSpec((tm, C_W), row),
            pl.BlockSpec((tm, 2 * B_W), row),
            pl.BlockSpec((None, 1, 6 * D), lambda i: (_mod_row(i, tm), 0, 0)),
            pl.BlockSpec((1, D), const2),
            pl.BlockSpec((1, B_W), const2),
            pl.BlockSpec((SGU_GROUPS, SGU_CHUNK, SGU_CHUNK), lambda i: (0, 0, 0)),
            pl.BlockSpec((SGU_CHUNK, B_W), const2),
            pl.BlockSpec((B_W, B_W), const2),
            pl.BlockSpec((D, D), const2),
            pl.BlockSpec((1, D), const2),
            pl.BlockSpec((2, D, 128), lambda i: (0, 0, 0)),
        ],
        out_specs=[
            pl.BlockSpec((tm, D), row),
            pl.BlockSpec((tm, D), row),
            pl.BlockSpec((N_EXP, tm), lambda i: (0, i)),
        ],
        out_shape=[
            jax.ShapeDtypeStruct((n_rows, D), F32),
            jax.ShapeDtypeStruct((n_rows, D), F32),
            jax.ShapeDtypeStruct((N_EXP, n_rows), F32),
        ],
        compiler_params=_cparams(1),
        name="merge",
    )(xc, ya, yc, uv, mods, gg, sg, sw_bf, sb_tab, bd, wo_bf, g2, rw_t)


RT_TM = 256
GRP_SZ = N_EXP // N_GRP
SLOT_ALIGN = 8


def _first_argmax(v, iota):
    m = v.max(axis=0, keepdims=True)
    idx = jnp.where(v == m, iota, float(v.shape[0])).min(axis=0, keepdims=True)
    return m, idx


def _stack_rows(rows, iota):
    out = jnp.zeros(iota.shape, F32)
    for r, v in enumerate(rows):
        out = jnp.where(iota == float(r), v, out)
    return out


def _route_body(lg_ref, rb_ref, w_ref, lp_ref, tc_ref):
    tm = RT_TM
    scores = jax.nn.sigmoid(lg_ref[...])
    sel = scores + rb_ref[...]
    iota_g = lax.broadcasted_iota(I32, (GRP_SZ, tm), 0).astype(F32)
    gs = []
    for g in range(N_GRP):
        v = sel[g * GRP_SZ:(g + 1) * GRP_SZ, :]
        m1, i1 = _first_argmax(v, iota_g)
        m2 = jnp.where(iota_g == i1, -jnp.inf, v).max(axis=0, keepdims=True)
        gs.append(m1 + m2)
    iota_n = lax.broadcasted_iota(I32, (N_GRP, tm), 0).astype(F32)
    gscore = _stack_rows(gs, iota_n)
    gsel = jnp.zeros((N_GRP, tm), F32)
    for _ in range(TOPK_GRP):
        _, gi = _first_argmax(gscore, iota_n)
        hit = iota_n == gi
        gsel = jnp.where(hit, 1.0, gsel)
        gscore = jnp.where(hit, -jnp.inf, gscore)
    emask = jnp.concatenate(
        [jnp.broadcast_to(gsel[g:g + 1, :], (GRP_SZ, tm)) for g in range(N_GRP)], axis=0)
    cand = jnp.where(emask > 0.5, sel, NEG)
    iota_e = lax.broadcasted_iota(I32, (N_EXP, tm), 0).astype(F32)
    hits = []
    ws = []
    member = jnp.zeros((N_EXP, tm), F32)
    for _ in range(TOP_K):
        _, ei = _first_argmax(cand, iota_e)
        hit = iota_e == ei
        hits.append(hit)
        ws.append(jnp.where(hit, scores, 0.0).sum(axis=0, keepdims=True))
        member = jnp.where(hit, 1.0, member)
        cand = jnp.where(hit, -jnp.inf, cand)
    wsum = ws[0]
    for w in ws[1:]:
        wsum = wsum + w
    iota_k = lax.broadcasted_iota(I32, (TOP_K, tm), 0).astype(F32)
    w_ref[...] = _stack_rows(ws, iota_k) / wsum * ROUTED_SCALE
    r_i = lax.broadcasted_iota(I32, (tm, tm), 0)
    c_i = lax.broadcasted_iota(I32, (tm, tm), 1)
    tri = jnp.where(r_i < c_i, 1.0, 0.0).astype(BF16)
    local = jnp.dot(member.astype(BF16), tri, preferred_element_type=F32)
    tile_cnt = member.sum(axis=1, keepdims=True)
    tc_ref[...] = tile_cnt
    aligned = jnp.ceil(tile_cnt / SLOT_ALIGN) * SLOT_ALIGN
    e_r = lax.broadcasted_iota(I32, (N_EXP, N_EXP), 0)
    e_c = lax.broadcasted_iota(I32, (N_EXP, N_EXP), 1)
    below = jnp.where(e_c < e_r, 1.0, 0.0)
    start = jnp.dot(below, jnp.broadcast_to(aligned, (N_EXP, 128)), precision=HIGHEST,
                    preferred_element_type=F32)[:, 0:1]
    pos = local + start
    lp_ref[...] = _stack_rows(
        [jnp.where(hit, pos, 0.0).sum(axis=0, keepdims=True) for hit in hits],
        iota_k).astype(I32)


def _route(logits_t, rbias, n_tok):
    tm = RT_TM
    tok = lambda i: (0, i)
    return pl.pallas_call(
        _route_body,
        grid=(n_tok // tm,),
        in_specs=[
            pl.BlockSpec((N_EXP, tm), tok),
            pl.BlockSpec((N_EXP, 1), lambda i: (0, 0)),
        ],
        out_specs=[
            pl.BlockSpec((TOP_K, tm), tok),
            pl.BlockSpec((TOP_K, tm), tok),
            pl.BlockSpec((None, N_EXP, 1), lambda i: (i, 0, 0)),
        ],
        out_shape=[
            jax.ShapeDtypeStruct((TOP_K, n_tok), F32),
            jax.ShapeDtypeStruct((TOP_K, n_tok), I32),
            jax.ShapeDtypeStruct((n_tok // tm, N_EXP, 1), F32),
        ],
        compiler_params=_cparams(1),
        name="route",
    )(logits_t, rbias)


DP_TM = RT_TM
SORT_ROWS = 2560
N_GRPS = SORT_ROWS // SLOT_ALIGN
HI_MASK = 0xFFFF0000


def _pack_bf16_pairs(v):
    bits = lax.bitcast_convert_type(v, U32)
    return (bits[:, :D // 2] >> 16) | (bits[:, D // 2:] & jnp.uint32(HI_MASK))


def _unpack_bf16_pairs(bits):
    lo = lax.bitcast_convert_type(bits << 16, F32)
    hi = lax.bitcast_convert_type(bits & jnp.uint32(HI_MASK), F32)
    return jnp.concatenate([lo, hi], axis=1).astype(BF16)


ISSUE_UNROLL = 4
WAIT_GROUPS = 32


def _group_copies(ngrp, row_of, make_copy, make_bulk):
    def start_all():
        def body(q, carry):
            for s in range(ISSUE_UNROLL):
                g = q * ISSUE_UNROLL + s
                make_copy(g, row_of(g)).start()
            return carry

        n_full = ngrp // ISSUE_UNROLL
        lax.fori_loop(0, n_full, body, 0)

        def tail(g, carry):
            make_copy(g, row_of(g)).start()
            return carry

        lax.fori_loop(n_full * ISSUE_UNROLL, ngrp, tail, 0)

    def wait_all():
        def bulk(q, carry):
            make_bulk().wait()
            return carry

        def single(g, carry):
            make_copy(0, 0).wait()
            return carry

        n_bulk = ngrp // WAIT_GROUPS
        lax.fori_loop(0, n_bulk, bulk, 0)
        lax.fori_loop(n_bulk * WAIT_GROUPS, ngrp, single, 0)

    return start_all, wait_all


def _dispatch_body(ngrp_ref, fill_ref, grow_ref, hx_ref, lp_ref, xs_ref, zbuf_ref, zero_ref, sem):
    i = pl.program_id(0)
    n_tiles = pl.num_programs(0)
    slot = i % 2

    def zero_copy(b):
        dst = xs_ref.at[pl.ds(pl.multiple_of(b * MOE_BLK, MOE_BLK), MOE_BLK)]
        return pltpu.make_async_copy(zero_ref, dst, sem)

    @pl.when(i == 0)
    def _():
        zero_ref[...] = jnp.zeros_like(zero_ref)
        n_fill = fill_ref[0]

        def z_start(q, carry):
            @pl.when(fill_ref[1 + q] >= 0)
            def _():
                zero_copy(fill_ref[1 + q]).start()
            return carry

        def z_wait(q, carry):
            @pl.when(fill_ref[1 + q] >= 0)
            def _():
                zero_copy(0).wait()
            return carry

        lax.fori_loop(0, n_fill, z_start, 0)
        lax.fori_loop(0, n_fill, z_wait, 0)

    def group_copy(s, g, row):
        src = zbuf_ref.at[s, pl.ds(pl.multiple_of(g * SLOT_ALIGN, SLOT_ALIGN), SLOT_ALIGN)]
        dst = xs_ref.at[pl.ds(pl.multiple_of(row, SLOT_ALIGN), SLOT_ALIGN)]
        return pltpu.make_async_copy(src, dst, sem)

    x = hx_ref[...].astype(BF16)
    lpos = lp_ref[...]
    row_iota = lax.broadcasted_iota(I32, (SORT_ROWS, DP_TM), 0).astype(jnp.int16)
    onehot = jnp.zeros((SORT_ROWS, DP_TM), BF16)
    for k in range(TOP_K):
        pos_k = jnp.broadcast_to(lpos[k:k + 1, :], (16, DP_TM)).astype(jnp.int16)
        onehot = jnp.where(row_iota == jnp.tile(pos_k, (SORT_ROWS // 16, 1)),
                           jnp.ones((), BF16), onehot)
    z = jnp.dot(onehot, x, preferred_element_type=F32)
    zbuf_ref[slot] = _pack_bf16_pairs(z)

    def bulk_copy():
        n_rows = WAIT_GROUPS * SLOT_ALIGN
        return pltpu.make_async_copy(
            zbuf_ref.at[0, pl.ds(0, n_rows)], xs_ref.at[pl.ds(0, n_rows)], sem)

    @pl.when(i > 0)
    def _():
        _, wait_prev = _group_copies(ngrp_ref[jnp.maximum(i - 1, 0)], None,
                                     lambda g, r: group_copy(0, 0, 0), bulk_copy)
        wait_prev()

    start_cur, wait_cur = _group_copies(
        ngrp_ref[i], lambda g: grow_ref[0, g], lambda g, r: group_copy(slot, g, r), bulk_copy)
    start_cur()

    @pl.when(i == n_tiles - 1)
    def _():
        wait_cur()


def _dispatch(ngrp, fill, grp_row, hx, lpos, n_tok, n_slots):
    return pl.pallas_call(
        _dispatch_body,
        grid_spec=pltpu.PrefetchScalarGridSpec(
            num_scalar_prefetch=2,
            grid=(n_tok // DP_TM,),
            in_specs=[
                pl.BlockSpec((None, 1, N_GRPS), lambda i, *_: (i, 0, 0), memory_space=pltpu.SMEM),
                pl.BlockSpec((DP_TM, D), lambda i, *_: (i, 0)),
                pl.BlockSpec((TOP_K, DP_TM), lambda i, *_: (0, i)),
            ],
            out_specs=pl.BlockSpec(memory_space=pl.ANY),
            scratch_shapes=[
                pltpu.VMEM((2, SORT_ROWS, D // 2), U32),
                pltpu.VMEM((MOE_BLK, D // 2), U32),
                pltpu.SemaphoreType.DMA,
            ],
        ),
        out_shape=jax.ShapeDtypeStruct((n_slots, D // 2), U32),
        compiler_params=_cparams(1, vmem_mb=56),
        name="moe_dispatch",
    )(ngrp, fill, grp_row, hx, lpos)


def _experts_body(bexp_ref, xblk_ref, nused_ref, xs_ref, wg_ref, wu_ref, wd_ref, ys_ref):
    j = pl.program_id(0)

    @pl.when(j < nused_ref[0])
    def _():
        x = _unpack_bf16_pairs(xs_ref[...])
        g = jnp.dot(x, wg_ref[...].astype(BF16), preferred_element_type=F32)
        u = jnp.dot(x, wu_ref[...].astype(BF16), preferred_element_type=F32)
        h = (g * jax.nn.sigmoid(g) * u).astype(BF16)
        y = jnp.dot(h, wd_ref[...].astype(BF16), preferred_element_type=F32)
        ys_ref[...] = _pack_bf16_pairs(y.astype(BF16).astype(F32))

    @pl.when(j >= nused_ref[0])
    def _():
        ys_ref[...] = jnp.zeros_like(ys_ref)


def _experts(bexp, xblk, nused, xs, wg, wu, wd, n_blk, layer):
    w_idx = lambda j, be, xb, nu: (layer, be[j], 0, 0)
    return pl.pallas_call(
        _experts_body,
        grid_spec=pltpu.PrefetchScalarGridSpec(
            num_scalar_prefetch=3,
            grid=(n_blk,),
            in_specs=[
                pl.BlockSpec((MOE_BLK, D // 2), lambda j, be, xb, nu: (xb[j], 0)),
                pl.BlockSpec((None, None, D, EXP_FF), w_idx),
                pl.BlockSpec((None, None, D, EXP_FF), w_idx),
                pl.BlockSpec((None, None, EXP_FF, D), w_idx),
            ],
            out_specs=pl.BlockSpec((MOE_BLK, D // 2), lambda j, be, xb, nu: (j, 0)),
        ),
        out_shape=jax.ShapeDtypeStruct((n_blk * MOE_BLK, D // 2), U32),
        compiler_params=_cparams(1),
        name="moe_experts",
    )(bexp, xblk, nused, xs, wg, wu, wd)


def _combine_body(ngrp_ref, grow_ref, ys_ref, lp_ref, w_ref, x_ref, hx_ref, m_ref, sg_ref, su_ref,
                  sd_ref, fg_ref, o_ref, ybuf_ref, sem, *, final_norm):
    i = pl.program_id(0)

    @pl.when(i == 0)
    def _():
        ybuf_ref[...] = jnp.zeros_like(ybuf_ref)

    def group_copy(g, row):
        src = ys_ref.at[pl.ds(pl.multiple_of(row, SLOT_ALIGN), SLOT_ALIGN)]
        dst = ybuf_ref.at[pl.ds(pl.multiple_of(g * SLOT_ALIGN, SLOT_ALIGN), SLOT_ALIGN)]
        return pltpu.make_async_copy(src, dst, sem)

    def bulk_copy():
        n_rows = WAIT_GROUPS * SLOT_ALIGN
        return pltpu.make_async_copy(
            ys_ref.at[pl.ds(0, n_rows)], ybuf_ref.at[pl.ds(0, n_rows)], sem)

    start, wait = _group_copies(ngrp_ref[i], lambda g: grow_ref[0, g], group_copy, bulk_copy)
    start()

    hx = hx_ref[...].astype(BF16)
    g = jnp.dot(hx, sg_ref[...], preferred_element_type=F32)
    u = jnp.dot(hx, su_ref[...], preferred_element_type=F32)
    h = (g * jax.nn.sigmoid(g) * u).astype(BF16)
    y = jnp.dot(h, sd_ref[...], preferred_element_type=F32)

    lpos = lp_ref[...]
    w = w_ref[...]
    col_iota = lax.broadcasted_iota(I32, (DP_TM, SORT_ROWS), 1).astype(jnp.int16)
    unsort = jnp.zeros((DP_TM, SORT_ROWS), BF16)
    for k in range(TOP_K):
        pos_k = jnp.broadcast_to(lpos[:, k:k + 1], (DP_TM, 128)).astype(jnp.int16)
        w_k = jnp.broadcast_to(w[:, k:k + 1], (DP_TM, 128)).astype(BF16)
        unsort = jnp.where(col_iota == jnp.tile(pos_k, (1, SORT_ROWS // 128)),
                           jnp.tile(w_k, (1, SORT_ROWS // 128)), unsort)

    wait()
    y = y + jnp.dot(unsort, _unpack_bf16_pairs(ybuf_ref[...]), preferred_element_type=F32)
    out = x_ref[...] + m_ref[:, 5 * D:6 * D] * y
    if final_norm:
        out = _rms(out) * fg_ref[...]
    o_ref[...] = out


def _combine(ngrp, grp_row, ys, lpos_t, w_t, x, hx, mods, sg_bf, su_bf, sd_bf, fg, n_tok, final_norm):
    tm = DP_TM
    row = lambda i, *_: (i, 0)
    const2 = lambda i, *_: (0, 0)
    return pl.pallas_call(
        functools.partial(_combine_body, final_norm=final_norm),
        grid_spec=pltpu.PrefetchScalarGridSpec(
            num_scalar_prefetch=1,
            grid=(n_tok // tm,),
            in_specs=[
                pl.BlockSpec((None, 1, N_GRPS), lambda i, *_: (i, 0, 0), memory_space=pltpu.SMEM),
                pl.BlockSpec(memory_space=pl.ANY),
                pl.BlockSpec((tm, TOP_K), row),
                pl.BlockSpec((tm, TOP_K), row),
                pl.BlockSpec((tm, D), row),
                pl.BlockSpec((tm, D), row),
                pl.BlockSpec((None, 1, 6 * D), lambda i, *_: (_mod_row(i, tm), 0, 0)),
                pl.BlockSpec((D, EXP_FF), const2),
                pl.BlockSpec((D, EXP_FF), const2),
                pl.BlockSpec((EXP_FF, D), const2),
                pl.BlockSpec((1, D), const2),
            ],
            out_specs=pl.BlockSpec((tm, D), row),
            scratch_shapes=[pltpu.VMEM((SORT_ROWS, D // 2), U32), pltpu.SemaphoreType.DMA],
        ),
        out_shape=jax.ShapeDtypeStruct((n_tok, D), F32),
        compiler_params=_cparams(1, vmem_mb=56),
        name="moe_combine",
    )(ngrp, grp_row, ys, lpos_t, w_t, x, hx, mods, sg_bf, su_bf, sd_bf, fg)


def _moe(x_new, hx, logits_t, mods, rbias, wg, wu, wd, sg_bf, su_bf, sd_bf, fg, n_tok, layer,
         final_norm):
    w, lpos, tile_cnt = _route(logits_t, rbias, n_tok)
    n_tiles = n_tok // DP_TM
    n_blk = (n_tok * TOP_K + (SLOT_ALIGN - 1) * N_EXP * n_tiles) // MOE_BLK + N_EXP
    tcnt = tile_cnt[:, :, 0].astype(I32)
    cnt_al = (tcnt + SLOT_ALIGN - 1) // SLOT_ALIGN * SLOT_ALIGN
    loc_end = jnp.cumsum(cnt_al, axis=1)
    loc = loc_end - cnt_al
    slots_e = jnp.sum(cnt_al, axis=0)
    nblk_e = (slots_e + MOE_BLK - 1) // MOE_BLK
    blk_end = jnp.cumsum(nblk_e)
    pstart = (blk_end - nblk_e) * MOE_BLK
    nused = blk_end[-1:].astype(I32)
    blk_ids = jnp.arange(n_blk, dtype=I32)
    xblk = jnp.minimum(blk_ids, nused[0] - 1)
    bexp = jnp.minimum(
        jnp.sum(blk_end[None, :] <= xblk[:, None], axis=1), N_EXP - 1).astype(I32)
    off = pstart[None, :] + jnp.cumsum(cnt_al, axis=0) - cnt_al
    g_row = jnp.arange(N_GRPS, dtype=I32) * SLOT_ALIGN
    e_of_g = jnp.minimum(
        jnp.sum(loc_end[:, None, :] <= g_row[None, :, None], axis=2), N_EXP - 1)
    pick = e_of_g[..., None] == jnp.arange(N_EXP, dtype=I32)
    grp_row = (jnp.sum(jnp.where(pick, (off - loc)[:, None, :], 0), axis=2)
               + g_row[None, :]).astype(I32).reshape(n_tiles, 1, N_GRPS)
    ngrp = (loc_end[:, -1] // SLOT_ALIGN).astype(I32)
    unused = nused[0] + blk_ids
    fill = jnp.concatenate([
        (N_EXP + n_blk - nused[0])[None],
        jnp.where(nblk_e > 0, blk_end - 1, -1),
        jnp.where(unused < n_blk, unused, -1)]).astype(I32)
    xs = _dispatch(ngrp, fill, grp_row, hx, lpos, n_tok, n_blk * MOE_BLK)
    ys = _experts(bexp, xblk, nused, xs, wg, wu, wd, n_blk, layer)
    return _combine(ngrp, grp_row, ys, lpos.T, w.T, x_new, hx, mods, sg_bf, su_bf, sd_bf, fg,
                    n_tok, final_norm)


def _rope_tables():
    t = np.arange(S)
    row = (t // GRID_W).astype(np.float32)
    col = (t % GRID_W).astype(np.float32)
    half = HD // 2
    inv = jnp.asarray(ROPE_BASE, F32) ** (-jnp.arange(0, half, 2, dtype=F32) / half)
    ang_r = jnp.asarray(row)[:, None] * inv
    ang_c = jnp.asarray(col)[:, None] * inv
    ang = jnp.concatenate([ang_r, ang_r, ang_c, ang_c], axis=-1)
    n_rep = ROT_W // HD
    cos = jnp.tile(jnp.cos(ang), (1, n_rep))
    sin = jnp.tile(jnp.sin(ang), (1, n_rep))
    cos = jnp.concatenate([cos, jnp.ones((IN_TM, ROT_W), F32)], axis=0)
    sin = jnp.concatenate([sin, jnp.zeros((IN_TM, ROT_W), F32)], axis=0)
    return cos, sin


def kernel(x, c, ctx, c_ctx, mod_w, mod_b, norm1_g, w_in, attn_sink, sgu_norm_g, sgu_w, sgu_b, na_rpb,
           group_norm_g, w_out, norm2_g, router_w, router_bias, exp_w_gate, exp_w_up, exp_w_down,
           shared_w_gate, shared_w_up, shared_w_down, final_g):
    xc = jnp.concatenate([x.reshape(N_LAT, D), ctx.reshape(N_CTX, D)], axis=0)
    cc = jnp.concatenate([c, c_ctx[None, :], jnp.zeros((16 - B - 1, D), F32)], axis=0)
    mods_all = _modulation(cc, mod_w, mod_b)
    cos_t, sin_t = _rope_tables()
    seg = np.arange(B_W) // HD
    bd = jnp.asarray((seg[:, None] == seg[None, :]).astype(np.float32) / HD, BF16)
    fg = final_g.reshape(1, D)

    out = None
    for l in range(DEPTH):
        last = l == DEPTH - 1
        mods = mods_all[l].reshape(16, 1, 6 * D)
        za, zc, uv = _in_proj(xc, mods, norm1_g[l].reshape(1, D), w_in[l].astype(BF16), cos_t, sin_t)
        ya = _attn_a(za, attn_sink[l], with_ctx=not last)
        yc = _na(zc, _na_bias_table(na_rpb[l]), with_ctx=not last)
        n_tok = N_LAT if last else N_ALL
        sb_tab = jnp.repeat(sgu_b[l].T, HD, axis=1)
        x_new, hx, logits_t = _merge(
            xc, ya, yc, uv, mods, group_norm_g[l].reshape(1, D), sgu_norm_g[l].reshape(1, B_W),
            sgu_w[l].astype(BF16), sb_tab, bd, w_out[l].astype(BF16), norm2_g[l].reshape(1, D),
            _router_split(router_w[l]), n_tok)
        res = _moe(x_new, hx, logits_t, mods, router_bias[l].reshape(N_EXP, 1),
                   exp_w_gate, exp_w_up, exp_w_down,
                   shared_w_gate[l].astype(BF16), shared_w_up[l].astype(BF16),
                   shared_w_down[l].astype(BF16), fg, n_tok, layer=l, final_norm=last)
        if last:
            out = res.reshape(B, S, D)
        else:
            xc = res
    return out
```

```python
import functools

import jax
import jax.numpy as jnp
import numpy as np
from jax import lax
from jax.experimental import pallas as pl
from jax.experimental.pallas import tpu as pltpu

F32 = jnp.float32
BF16 = jnp.bfloat16
I32 = jnp.int32

D = 1024
B = 8
S = 2048
C = 256
DEPTH = 2
GRID_W = 64
HD = 64
A_HEADS = 6
A_KV = 2
A_WIN = 128
A_BLK = 256
ROPE_BASE = 10000.0
SGU_GROUPS = 4
SGU_CHUNK = 128
C_HEADS = 6
C_WIN_R = 8
C_WIN_C = 16
A_W = A_HEADS * HD
B_W = SGU_GROUPS * HD
C_W = C_HEADS * HD
KV_W = A_KV * HD
IN_W = A_W + 2 * KV_W + 2 * B_W + 3 * C_W
N_EXP = 64
TOP_K = 8
N_GRP = 8
TOPK_GRP = 4
EXP_FF = 256
ROUTED_SCALE = 2.5
MOE_BLK = 1024
EPS = 1e-6
NEG = -1e30
SCALE = HD ** -0.5

N_LAT = B * S
N_CTX = B * C
N_ALL = N_LAT + N_CTX
ROWS = S // GRID_W

HIGHEST = lax.Precision.HIGHEST
ARB = pltpu.ARBITRARY

NT_DIMS = (((1,), (1,)), ((), ()))


def _cparams(n_axes, vmem_mb=48):
    return pltpu.CompilerParams(
        dimension_semantics=(ARB,) * n_axes, vmem_limit_bytes=vmem_mb * 1024 * 1024)


def _mod_row(i, tm):
    return jnp.where(i < N_LAT // tm, i // (S // tm), B)


def _rms(x):
    return x * lax.rsqrt(jnp.mean(x * x, axis=-1, keepdims=True) + EPS)


def _split_bf16(x):
    hi = x.astype(BF16)
    return hi, (x - hi.astype(F32)).astype(BF16)


def _router_split(rw):
    hi, lo = _split_bf16(jnp.pad(rw, ((0, 0), (0, 128 - N_EXP))))
    return jnp.stack([hi, lo])


MOD_TN = 1024


def _mod_body(cc_ref, w_ref, b_ref, o_ref):
    a = cc_ref[...]
    a = a * jax.nn.sigmoid(a)
    o_ref[...] = jnp.dot(a.astype(BF16), w_ref[...].astype(BF16),
                         preferred_element_type=F32) + b_ref[...]


def _modulation(cc, mod_w, mod_b):
    n_col = 6 * D // MOD_TN
    return pl.pallas_call(
        _mod_body,
        grid=(DEPTH, n_col),
        in_specs=[
            pl.BlockSpec((16, D), lambda l, j: (0, 0)),
            pl.BlockSpec((None, D, MOD_TN), lambda l, j: (l, 0, j)),
            pl.BlockSpec((None, 1, MOD_TN), lambda l, j: (l, 0, j)),
        ],
        out_specs=pl.BlockSpec((None, 16, MOD_TN), lambda l, j: (l, 0, j)),
        out_shape=jax.ShapeDtypeStruct((DEPTH, 16, 6 * D), F32),
        compiler_params=_cparams(2),
        name="modulation",
    )(cc, mod_w, mod_b.reshape(DEPTH, 1, 6 * D))


IN_TM = 256
ROT_W = A_W + KV_W


def _in_body(x_ref, m_ref, g_ref, w_ref, cos_ref, sin_ref, za_ref, zc_ref, uv_ref):
    x = x_ref[...]
    h = _rms(x) * g_ref[...]
    h = h * (1.0 + m_ref[:, D:2 * D]) + m_ref[:, 0:D]
    z = jnp.dot(h.astype(BF16), w_ref[...], preferred_element_type=F32)
    qk = z[:, :ROT_W]
    lane = lax.broadcasted_iota(I32, qk.shape, 1)
    rot = jnp.where((lane & 16) == 0,
                    -pltpu.roll(qk, ROT_W - 16, 1), pltpu.roll(qk, 16, 1))
    qk = qk * cos_ref[...] + rot * sin_ref[...]
    za_ref[...] = jnp.concatenate(
        [qk[:, :A_W] * SCALE, qk[:, A_W:], z[:, ROT_W:ROT_W + KV_W]], axis=1).astype(BF16)
    u0 = ROT_W + KV_W
    uv_ref[...] = jax.nn.gelu(z[:, u0:u0 + 2 * B_W])
    c0 = u0 + 2 * B_W
    zc_ref[...] = jnp.concatenate(
        [z[:, c0:c0 + C_W] * SCALE, z[:, c0 + C_W:]], axis=1).astype(BF16)


def _in_proj(xc, mods, g, w_bf, cos_t, sin_t):
    tm = IN_TM
    n_t = N_ALL // tm

    def tab_idx(i):
        return (jnp.where(i < N_LAT // tm, i % (S // tm), S // tm), 0)

    return pl.pallas_call(
        _in_body,
        grid=(n_t,),
        in_specs=[
            pl.BlockSpec((tm, D), lambda i: (i, 0)),
            pl.BlockSpec((None, 1, 6 * D), lambda i: (_mod_row(i, tm), 0, 0)),
            pl.BlockSpec((1, D), lambda i: (0, 0)),
            pl.BlockSpec((D, IN_W), lambda i: (0, 0)),
            pl.BlockSpec((tm, ROT_W), tab_idx),
            pl.BlockSpec((tm, ROT_W), tab_idx),
        ],
        out_specs=[
            pl.BlockSpec((tm, A_W + 2 * KV_W), lambda i: (i, 0)),
            pl.BlockSpec((tm, 3 * C_W), lambda i: (i, 0)),
            pl.BlockSpec((tm, 2 * B_W), lambda i: (i, 0)),
        ],
        out_shape=[
            jax.ShapeDtypeStruct((N_ALL, A_W + 2 * KV_W), BF16),
            jax.ShapeDtypeStruct((N_ALL, 3 * C_W), BF16),
            jax.ShapeDtypeStruct((N_ALL, 2 * B_W), F32),
        ],
        compiler_params=_cparams(1),
        name="in_proj",
    )(xc, mods, g, w_bf, cos_t, sin_t)


A_BAND = A_BLK + 2 * A_WIN
N_QB_LAT = N_LAT // A_BLK
QB_PER_SEQ = S // A_BLK
QB_PER_CTX = C // A_BLK


def _softmax_pv(s_list, v_list, extra_logit=None):
    m = s_list[0].max(axis=-1, keepdims=True)
    for s in s_list[1:]:
        m = jnp.maximum(m, s.max(axis=-1, keepdims=True))
    if extra_logit is not None:
        m = jnp.maximum(m, extra_logit)
    den = None
    out = None
    for s, v in zip(s_list, v_list):
        p = jnp.exp(s - m)
        d = p.sum(axis=-1, keepdims=True)
        o = jnp.dot(p.astype(BF16), v, preferred_element_type=F32)
        den = d if den is None else den + d
        out = o if out is None else out + o
    if extra_logit is not None:
        den = den + jnp.exp(extra_logit - m)
    return out / den


def _attn_a_body(sink_ref, q_ref, kb_ref, vb_ref, kc_ref, vc_ref, o_ref):
    i = pl.program_id(0)
    is_lat = i < N_QB_LAT
    n = i % QB_PER_SEQ
    start = pl.multiple_of(jnp.clip(n * A_BLK - A_WIN, 0, S - A_BAND), A_WIN)
    qpos = n * A_BLK + lax.broadcasted_iota(I32, (A_BLK, A_BAND), 0)
    kpos = start + lax.broadcasted_iota(I32, (A_BLK, A_BAND), 1)
    mask = jnp.abs(kpos - qpos) <= jnp.where(is_lat, A_WIN, -1)
    kb = kb_ref[pl.ds(start, A_BAND), :]
    vb = vb_ref[pl.ds(start, A_BAND), :]
    kc = kc_ref[...]
    vc = vc_ref[...]
    outs = []
    for h in range(A_HEADS):
        kv = h // (A_HEADS // A_KV)
        sl = slice(kv * HD, (kv + 1) * HD)
        q = q_ref[:, h * HD:(h + 1) * HD]
        s_b = lax.dot_general(q, kb[:, sl], NT_DIMS, preferred_element_type=F32)
        s_b = jnp.where(mask, s_b, NEG)
        s_c = lax.dot_general(q, kc[:, sl], NT_DIMS, preferred_element_type=F32)
        outs.append(_softmax_pv([s_c, s_b], [vc[:, sl], vb[:, sl]], sink_ref[h]))
    o_ref[...] = jnp.concatenate(outs, axis=1)


def _attn_a(za, sink, with_ctx):
    n_qb = N_QB_LAT + (N_CTX // A_BLK if with_ctx else 0)

    def bidx(i):
        return jnp.where(i < N_QB_LAT, i // QB_PER_SEQ, (i - N_QB_LAT) // QB_PER_CTX)

    k_col = A_W // KV_W
    v_col = k_col + 1
    return pl.pallas_call(
        _attn_a_body,
        grid=(n_qb,),
        in_specs=[
            pl.BlockSpec(memory_space=pltpu.SMEM),
            pl.BlockSpec((A_BLK, A_W), lambda i: (i, 0)),
            pl.BlockSpec((S, KV_W), lambda i: (bidx(i), k_col)),
            pl.BlockSpec((S, KV_W), lambda i: (bidx(i), v_col)),
            pl.BlockSpec((C, KV_W), lambda i: (N_LAT // C + bidx(i), k_col)),
            pl.BlockSpec((C, KV_W), lambda i: (N_LAT // C + bidx(i), v_col)),
        ],
        out_specs=pl.BlockSpec((A_BLK, A_W), lambda i: (i, 0)),
        out_shape=jax.ShapeDtypeStruct((n_qb * A_BLK, A_W), F32),
        compiler_params=_cparams(1),
        name="attn_window",
    )(sink, za, za, za, za, za)


NA_R = 4
NA_TQ = NA_R * GRID_W
NA_KROWS = 12
NA_WIN = NA_KROWS * GRID_W
NA_STEPS = ROWS // NA_R
NA_PAIRS = NA_KROWS // 2
N_DR = 2 * C_WIN_R - 1


def _na_body(q_ref, k_ref, v_ref, kc_ref, vc_ref, tab_ref, o_ref):
    j = pl.program_id(1)
    is_lat = j < NA_STEPS
    r0 = jnp.minimum(j, NA_STEPS - 1) * NA_R
    u0 = jnp.clip(r0 - C_WIN_R // 2, 0, ROWS - NA_KROWS)
    k0 = pl.multiple_of(u0 * GRID_W, GRID_W)
    kw = k_ref[pl.ds(k0, NA_WIN), :]
    vw = v_ref[pl.ds(k0, NA_WIN), :]
    kc = kc_ref[...]
    vc = vc_ref[...]
    q = q_ref[...]
    left = lax.broadcasted_iota(I32, (1, 2 * GRID_W), 1) < GRID_W

    tab_idx = []
    penalty = []
    for rr in range(NA_R):
        r = r0 + rr
        start = jnp.clip(r - C_WIN_R // 2, 0, ROWS - C_WIN_R)
        idx_row = []
        pen_row = []
        for p in range(NA_PAIRS):
            kr = u0 + 2 * p
            idx_row.append(jnp.clip(kr - r + (C_WIN_R - 1), -1, N_DR - 1) + 1)
            pens = []
            for half in range(2):
                ok = jnp.logical_and(is_lat, jnp.logical_and(kr + half >= start,
                                                             kr + half < start + C_WIN_R))
                pens.append(jnp.where(ok, 0.0, NEG))
            pen_row.append(jnp.where(left, pens[0], pens[1]))
        tab_idx.append(idx_row)
        penalty.append(jnp.concatenate(pen_row, axis=1))

    outs = []
    for h in range(C_HEADS):
        sl = slice(h * HD, (h + 1) * HD)
        qh = q[:, sl]
        s_w = lax.dot_general(qh, kw[:, sl], NT_DIMS, preferred_element_type=F32)
        bias = jnp.concatenate([
            jnp.concatenate([tab_ref[h, pl.ds(tab_idx[rr][p], 1)][0] for p in range(NA_PAIRS)],
                            axis=1) + penalty[rr]
            for rr in range(NA_R)], axis=0)
        s_c = lax.dot_general(qh, kc[:, sl], NT_DIMS, preferred_element_type=F32)
        outs.append(_softmax_pv([s_c, s_w + bias], [vc[:, sl], vw[:, sl]]))
    o_ref[...] = jnp.concatenate(outs, axis=1)


def _na(zc, tab, with_ctx):
    n_j = NA_STEPS + (1 if with_ctx else 0)

    def qidx(b, j):
        return jnp.where(j < NA_STEPS, b * NA_STEPS + j, N_LAT // NA_TQ + b)

    n_out = N_LAT + (N_CTX if with_ctx else 0)
    return pl.pallas_call(
        _na_body,
        grid=(B, n_j),
        in_specs=[
            pl.BlockSpec((NA_TQ, C_W), lambda b, j: (qidx(b, j), 0)),
            pl.BlockSpec((S, C_W), lambda b, j: (b, 1)),
            pl.BlockSpec((S, C_W), lambda b, j: (b, 2)),
            pl.BlockSpec((C, C_W), lambda b, j: (N_LAT // C + b, 1)),
            pl.BlockSpec((C, C_W), lambda b, j: (N_LAT // C + b, 2)),
            pl.BlockSpec((C_HEADS, N_DR + 1, GRID_W, 2 * GRID_W), lambda b, j: (0, 0, 0, 0)),
        ],
        out_specs=pl.BlockSpec((NA_TQ, C_W), lambda b, j: (qidx(b, j), 0)),
        out_shape=jax.ShapeDtypeStruct((n_out, C_W), F32),
        compiler_params=_cparams(2),
        name="attn_neighbourhood",
    )(zc, zc, zc, zc, zc, tab)


def _na_bias_table(rpb):
    cq = np.arange(GRID_W)
    col_start = np.clip(cq - C_WIN_C // 2, 0, GRID_W - C_WIN_C)
    col_ok = (cq[None, :] >= col_start[:, None]) & (cq[None, :] < col_start[:, None] + C_WIN_C)
    dc = np.clip(cq[None, :] - cq[:, None], -(C_WIN_C - 1), C_WIN_C - 1) + (C_WIN_C - 1)
    t = rpb.astype(F32)[:, :, dc]
    t = jnp.where(col_ok[None, None], t, NEG)
    zero = jnp.zeros((C_HEADS, 1, GRID_W, GRID_W), F32)
    ext = jnp.concatenate([zero, t, zero], axis=1)
    return jnp.concatenate([ext[:, :-1], ext[:, 1:]], axis=-1)


MG_TM = 256


def _merge_body(x_ref, ya_ref, yc_ref, uv_ref, m_ref, gg_ref, sg_ref, sw_ref, sb_ref, bd_ref,
                wo_ref, g2_ref, rw_ref, xo_ref, hx_ref, lg_ref):
    u = uv_ref[:, :B_W]
    v = uv_ref[:, B_W:]
    vv_hi, vv_lo = _split_bf16(v * v)
    ms = (jnp.dot(vv_hi, bd_ref[...], preferred_element_type=F32)
          + jnp.dot(vv_lo, bd_ref[...], preferred_element_type=F32))
    vn = (v * lax.rsqrt(ms + EPS) * sg_ref[...]).astype(BF16)
    lane_grp = lax.broadcasted_iota(I32, (SGU_CHUNK, B_W), 1) // HD
    gates = []
    for c in range(MG_TM // SGU_CHUNK):
        vc = vn[c * SGU_CHUNK:(c + 1) * SGU_CHUNK, :]
        gate = sb_ref[...]
        acc = jnp.zeros((SGU_CHUNK, B_W), F32)
        for g in range(SGU_GROUPS):
            r = jnp.dot(sw_ref[g], vc, preferred_element_type=F32)
            acc = jnp.where(lane_grp == g, r, acc)
        gates.append(acc + gate)
    yb = u * jnp.concatenate(gates, axis=0)
    gg = gg_ref[...]
    ycat = jnp.concatenate([
        _rms(ya_ref[...]) * gg[:, :A_W],
        _rms(yb) * gg[:, A_W:A_W + B_W],
        _rms(yc_ref[...]) * gg[:, A_W + B_W:],
    ], axis=1)
    proj = jnp.dot(ycat.astype(BF16), wo_ref[...], preferred_element_type=F32)
    xn = x_ref[...] + m_ref[:, 2 * D:3 * D] * proj
    xo_ref[...] = xn
    hx = _rms(xn) * g2_ref[...]
    hx = hx * (1.0 + m_ref[:, 4 * D:5 * D]) + m_ref[:, 3 * D:4 * D]
    hx_ref[...] = hx
    hx_hi, hx_lo = _split_bf16(hx)
    logits = (jnp.dot(hx_hi, rw_ref[0], preferred_element_type=F32)
              + jnp.dot(hx_hi, rw_ref[1], preferred_element_type=F32)
              + jnp.dot(hx_lo, rw_ref[0], preferred_element_type=F32))
    lg_ref[...] = logits.T[:N_EXP, :]


def _merge(xc, ya, yc, uv, mods, gg, sg, sw_bf, sb_tab, bd, wo_bf, g2, rw_t, n_rows):
    tm = MG_TM
    n_t = n_rows // tm
    const2 = lambda i: (0, 0)
    row = lambda i: (i, 0)
    return pl.pallas_call(
        _merge_body,
        grid=(n_t,),
        in_specs=[
            pl.BlockSpec((tm, D), row),
            pl.BlockSpec((tm, A_W), row),
            pl.BlockSpec((tm, C_W), row),
            pl.BlockSpec((tm, 2 * B_W), row),
            pl.BlockSpec((None, 1, 6 * D), lambda i: (_mod_row(i, tm), 0, 0)),
            pl.BlockSpec((1, D), const2),
            pl.BlockSpec((1, B_W), const2),
            pl.BlockSpec((SGU_GROUPS, SGU_CHUNK, SGU_CHUNK), lambda i: (0, 0, 0)),
            pl.BlockSpec((SGU_CHUNK, B_W), const2),
            pl.BlockSpec((B_W, B_W), const2),
            pl.BlockSpec((D, D), const2),
            pl.BlockSpec((1, D), const2),
            pl.BlockSpec((2, D, 128), lambda i: (0, 0, 0)),
        ],
        out_specs=[
            pl.BlockSpec((tm, D), row),
            pl.BlockSpec((tm, D), row),
            pl.BlockSpec((N_EXP, tm), lambda i: (0, i)),
        ],
        out_shape=[
            jax.ShapeDtypeStruct((n_rows, D), F32),
            jax.ShapeDtypeStruct((n_rows, D), F32),
            jax.ShapeDtypeStruct((N_EXP, n_rows), F32),
        ],
        compiler_params=_cparams(1),
        name="merge",
    )(xc, ya, yc, uv, mods, gg, sg, sw_bf, sb_tab, bd, wo_bf, g2, rw_t)


RT_TM = 256
GRP_SZ = N_EXP // N_GRP
SLOT_ALIGN = 16


def _first_argmax(v, iota):
    m = v.max(axis=0, keepdims=True)
    idx = jnp.where(v == m, iota, float(v.shape[0])).min(axis=0, keepdims=True)
    return m, idx


def _stack_rows(rows, iota):
    out = jnp.zeros(iota.shape, F32)
    for r, v in enumerate(rows):
        out = jnp.where(iota == float(r), v, out)
    return out


def _route_body(lg_ref, rb_ref, w_ref, lp_ref, tc_ref):
    tm = RT_TM
    scores = jax.nn.sigmoid(lg_ref[...])
    sel = scores + rb_ref[...]
    iota_g = lax.broadcasted_iota(I32, (GRP_SZ, tm), 0).astype(F32)
    gs = []
    for g in range(N_GRP):
        v = sel[g * GRP_SZ:(g + 1) * GRP_SZ, :]
        m1, i1 = _first_argmax(v, iota_g)
        m2 = jnp.where(iota_g == i1, -jnp.inf, v).max(axis=0, keepdims=True)
        gs.append(m1 + m2)
    iota_n = lax.broadcasted_iota(I32, (N_GRP, tm), 0).astype(F32)
    gscore = _stack_rows(gs, iota_n)
    gsel = jnp.zeros((N_GRP, tm), F32)
    for _ in range(TOPK_GRP):
        _, gi = _first_argmax(gscore, iota_n)
        hit = iota_n == gi
        gsel = jnp.where(hit, 1.0, gsel)
        gscore = jnp.where(hit, -jnp.inf, gscore)
    emask = jnp.concatenate(
        [jnp.broadcast_to(gsel[g:g + 1, :], (GRP_SZ, tm)) for g in range(N_GRP)], axis=0)
    cand = jnp.where(emask > 0.5, sel, NEG)
    iota_e = lax.broadcasted_iota(I32, (N_EXP, tm), 0).astype(F32)
    hits = []
    ws = []
    member = jnp.zeros((N_EXP, tm), F32)
    for _ in range(TOP_K):
        _, ei = _first_argmax(cand, iota_e)
        hit = iota_e == ei
        hits.append(hit)
        ws.append(jnp.where(hit, scores, 0.0).sum(axis=0, keepdims=True))
        member = jnp.where(hit, 1.0, member)
        cand = jnp.where(hit, -jnp.inf, cand)
    wsum = ws[0]
    for w in ws[1:]:
        wsum = wsum + w
    iota_k = lax.broadcasted_iota(I32, (TOP_K, tm), 0).astype(F32)
    w_ref[...] = _stack_rows(ws, iota_k) / wsum * ROUTED_SCALE
    r_i = lax.broadcasted_iota(I32, (tm, tm), 0)
    c_i = lax.broadcasted_iota(I32, (tm, tm), 1)
    tri = jnp.where(r_i < c_i, 1.0, 0.0).astype(BF16)
    local = jnp.dot(member.astype(BF16), tri, preferred_element_type=F32)
    tile_cnt = member.sum(axis=1, keepdims=True)
    tc_ref[...] = tile_cnt
    aligned = jnp.ceil(tile_cnt / SLOT_ALIGN) * SLOT_ALIGN
    e_r = lax.broadcasted_iota(I32, (N_EXP, N_EXP), 0)
    e_c = lax.broadcasted_iota(I32, (N_EXP, N_EXP), 1)
    below = jnp.where(e_c < e_r, 1.0, 0.0)
    start = jnp.dot(below, jnp.broadcast_to(aligned, (N_EXP, 128)), precision=HIGHEST,
                    preferred_element_type=F32)[:, 0:1]
    pos = local + start
    lp_ref[...] = _stack_rows(
        [jnp.where(hit, pos, 0.0).sum(axis=0, keepdims=True) for hit in hits],
        iota_k).astype(I32)


def _route(logits_t, rbias, n_tok):
    tm = RT_TM
    tok = lambda i: (0, i)
    return pl.pallas_call(
        _route_body,
        grid=(n_tok // tm,),
        in_specs=[
            pl.BlockSpec((N_EXP, tm), tok),
            pl.BlockSpec((N_EXP, 1), lambda i: (0, 0)),
        ],
        out_specs=[
            pl.BlockSpec((TOP_K, tm), tok),
            pl.BlockSpec((TOP_K, tm), tok),
            pl.BlockSpec((None, N_EXP, 1), lambda i: (i, 0, 0)),
        ],
        out_shape=[
            jax.ShapeDtypeStruct((TOP_K, n_tok), F32),
            jax.ShapeDtypeStruct((TOP_K, n_tok), I32),
            jax.ShapeDtypeStruct((n_tok // tm, N_EXP, 1), F32),
        ],
        compiler_params=_cparams(1),
        name="route",
    )(logits_t, rbias)


DP_TM = RT_TM
SORT_ROWS = 3072
N_GRPS = SORT_ROWS // SLOT_ALIGN
ISSUE_UNROLL = 4
WAIT_GROUPS = 32
N_BURSTS = 4
BURST_ROWS = SORT_ROWS // N_BURSTS
BURST_GRPS = BURST_ROWS // SLOT_ALIGN


def _start_groups(g_lo, g_hi, row_of, make_copy):
    n_full = (g_hi - g_lo) // ISSUE_UNROLL

    def body(q, carry):
        for s in range(ISSUE_UNROLL):
            g = g_lo + q * ISSUE_UNROLL + s
            make_copy(g, row_of(g)).start()
        return carry

    def tail(g, carry):
        make_copy(g, row_of(g)).start()
        return carry

    lax.fori_loop(0, n_full, body, 0)
    lax.fori_loop(g_lo + n_full * ISSUE_UNROLL, g_hi, tail, 0)


def _start_burst(q, ngrp, row_of, make_copy):
    _start_groups(jnp.minimum(q * BURST_GRPS, ngrp), jnp.minimum((q + 1) * BURST_GRPS, ngrp),
                  row_of, make_copy)


def _wait_groups(ngrp, make_copy, make_bulk):
    def bulk(q, carry):
        make_bulk().wait()
        return carry

    def single(g, carry):
        make_copy(0, 0).wait()
        return carry

    n_bulk = ngrp // WAIT_GROUPS
    lax.fori_loop(0, n_bulk, bulk, 0)
    lax.fori_loop(n_bulk * WAIT_GROUPS, ngrp, single, 0)


def _dispatch_body(ngrp_ref, fill_ref, grow_ref, hx_ref, lp_ref, xs_ref, zbuf_ref, zero_ref, sems):
    i = pl.program_id(0)
    n_tiles = pl.num_programs(0)
    slot = i % 2

    def zero_copy(b):
        dst = xs_ref.at[pl.ds(pl.multiple_of(b * MOE_BLK, MOE_BLK), MOE_BLK)]
        return pltpu.make_async_copy(zero_ref, dst, sems.at[0])

    @pl.when(i == 0)
    def _():
        zero_ref[...] = jnp.zeros_like(zero_ref)
        n_fill = fill_ref[0]

        def z_start(q, carry):
            @pl.when(fill_ref[1 + q] >= 0)
            def _():
                zero_copy(fill_ref[1 + q]).start()
            return carry

        def z_wait(q, carry):
            @pl.when(fill_ref[1 + q] >= 0)
            def _():
                zero_copy(0).wait()
            return carry

        lax.fori_loop(0, n_fill, z_start, 0)
        lax.fori_loop(0, n_fill, z_wait, 0)

    def group_copy(s, g, row):
        src = zbuf_ref.at[s, pl.ds(pl.multiple_of(g * SLOT_ALIGN, SLOT_ALIGN), SLOT_ALIGN)]
        dst = xs_ref.at[pl.ds(pl.multiple_of(row, SLOT_ALIGN), SLOT_ALIGN)]
        return pltpu.make_async_copy(src, dst, sems.at[s])

    def bulk_copy(s):
        n_rows = WAIT_GROUPS * SLOT_ALIGN
        return pltpu.make_async_copy(
            zbuf_ref.at[s, pl.ds(0, n_rows)], xs_ref.at[pl.ds(0, n_rows)], sems.at[s])

    x = hx_ref[...].astype(BF16)
    lpos = lp_ref[...]
    n_cur = ngrp_ref[i]
    pos = [jnp.tile(jnp.broadcast_to(lpos[k:k + 1, :], (16, DP_TM)).astype(jnp.int16),
                    (BURST_ROWS // 16, 1)) for k in range(TOP_K)]
    for q in range(N_BURSTS):
        row_iota = (lax.broadcasted_iota(I32, (BURST_ROWS, DP_TM), 0)
                    + q * BURST_ROWS).astype(jnp.int16)
        onehot = jnp.zeros((BURST_ROWS, DP_TM), BF16)
        for k in range(TOP_K):
            onehot = jnp.where(row_iota == pos[k], jnp.ones((), BF16), onehot)
        z = jnp.dot(onehot, x, preferred_element_type=F32)
        zbuf_ref[slot, pl.ds(q * BURST_ROWS, BURST_ROWS), :] = z.astype(BF16)
        _start_burst(q, n_cur, lambda g: grow_ref[0, g], lambda g, r: group_copy(slot, g, r))

    @pl.when(i > 0)
    def _():
        _wait_groups(ngrp_ref[jnp.maximum(i - 1, 0)], lambda g, r: group_copy(1 - slot, 0, 0),
                     lambda: bulk_copy(1 - slot))

    @pl.when(i == n_tiles - 1)
    def _():
        _wait_groups(n_cur, lambda g, r: group_copy(slot, 0, 0), lambda: bulk_copy(slot))


def _dispatch(ngrp, fill, grp_row, hx, lpos, n_tok, n_slots):
    return pl.pallas_call(
        _dispatch_body,
        grid_spec=pltpu.PrefetchScalarGridSpec(
            num_scalar_prefetch=2,
            grid=(n_tok // DP_TM,),
            in_specs=[
                pl.BlockSpec((None, 1, N_GRPS), lambda i, *_: (i, 0, 0), memory_space=pltpu.SMEM),
                pl.BlockSpec((DP_TM, D), lambda i, *_: (i, 0)),
                pl.BlockSpec((TOP_K, DP_TM), lambda i, *_: (0, i)),
            ],
            out_specs=pl.BlockSpec(memory_space=pl.ANY),
            scratch_shapes=[
                pltpu.VMEM((2, SORT_ROWS, D), BF16),
                pltpu.VMEM((MOE_BLK, D), BF16),
                pltpu.SemaphoreType.DMA((2,)),
            ],
        ),
        out_shape=jax.ShapeDtypeStruct((n_slots, D), BF16),
        compiler_params=_cparams(1, vmem_mb=56),
        name="moe_dispatch",
    )(ngrp, fill, grp_row, hx, lpos)


def _experts_body(bexp_ref, xblk_ref, nused_ref, xs_ref, wg_ref, wu_ref, wd_ref, ys_ref):
    j = pl.program_id(0)

    @pl.when(j < nused_ref[0])
    def _():
        x = xs_ref[...]
        g = jnp.dot(x, wg_ref[...].astype(BF16), preferred_element_type=F32)
        u = jnp.dot(x, wu_ref[...].astype(BF16), preferred_element_type=F32)
        h = (g * jax.nn.sigmoid(g) * u).astype(BF16)
        y = jnp.dot(h, wd_ref[...].astype(BF16), preferred_element_type=F32)
        ys_ref[...] = y.astype(BF16)

    @pl.when(j >= nused_ref[0])
    def _():
        ys_ref[...] = jnp.zeros_like(ys_ref)


def _experts(bexp, xblk, nused, xs, wg, wu, wd, n_blk, layer):
    w_idx = lambda j, be, xb, nu: (layer, be[j], 0, 0)
    return pl.pallas_call(
        _experts_body,
        grid_spec=pltpu.PrefetchScalarGridSpec(
            num_scalar_prefetch=3,
            grid=(n_blk,),
            in_specs=[
                pl.BlockSpec((MOE_BLK, D), lambda j, be, xb, nu: (xb[j], 0)),
                pl.BlockSpec((None, None, D, EXP_FF), w_idx),
                pl.BlockSpec((None, None, D, EXP_FF), w_idx),
                pl.BlockSpec((None, None, EXP_FF, D), w_idx),
            ],
            out_specs=pl.BlockSpec((MOE_BLK, D), lambda j, be, xb, nu: (j, 0)),
        ),
        out_shape=jax.ShapeDtypeStruct((n_blk * MOE_BLK, D), BF16),
        compiler_params=_cparams(1),
        name="moe_experts",
    )(bexp, xblk, nused, xs, wg, wu, wd)


def _combine_body(ngrp_ref, grow_ref, grow_next_ref, ys_ref, lp_ref, w_ref, x_ref, hx_ref, m_ref,
                  sg_ref, su_ref, sd_ref, fg_ref, o_ref, ybuf_ref, sems, *, final_norm):
    i = pl.program_id(0)
    n_tiles = pl.num_programs(0)
    slot = i % 2

    def group_copy(s, g, row):
        src = ys_ref.at[pl.ds(pl.multiple_of(row, SLOT_ALIGN), SLOT_ALIGN)]
        dst = ybuf_ref.at[s, pl.ds(pl.multiple_of(g * SLOT_ALIGN, SLOT_ALIGN), SLOT_ALIGN)]
        return pltpu.make_async_copy(src, dst, sems.at[s])

    def bulk_copy(s):
        n_rows = WAIT_GROUPS * SLOT_ALIGN
        return pltpu.make_async_copy(
            ys_ref.at[pl.ds(0, n_rows)], ybuf_ref.at[s, pl.ds(0, n_rows)], sems.at[s])

    @pl.when(i == 0)
    def _():
        ybuf_ref[...] = jnp.zeros_like(ybuf_ref)
        _start_groups(0, ngrp_ref[0], lambda g: grow_ref[0, g], lambda g, r: group_copy(0, g, r))

    n_next = jnp.where(i + 1 < n_tiles, ngrp_ref[jnp.minimum(i + 1, n_tiles - 1)], 0)

    def prefetch(q):
        _start_burst(q, n_next, lambda g: grow_next_ref[0, g],
                     lambda g, r: group_copy(1 - slot, g, r))

    hx = hx_ref[...].astype(BF16)
    g = jnp.dot(hx, sg_ref[...], preferred_element_type=F32)
    u = jnp.dot(hx, su_ref[...], preferred_element_type=F32)
    h = (g * jax.nn.sigmoid(g) * u).astype(BF16)
    y = jnp.dot(h, sd_ref[...], preferred_element_type=F32)
    prefetch(0)

    lpos = lp_ref[...]
    w = w_ref[...]
    col_iota = lax.broadcasted_iota(I32, (DP_TM, SORT_ROWS), 1).astype(jnp.int16)
    unsort = jnp.zeros((DP_TM, SORT_ROWS), BF16)
    for k in range(TOP_K):
        pos_k = jnp.broadcast_to(lpos[:, k:k + 1], (DP_TM, 128)).astype(jnp.int16)
        w_k = jnp.broadcast_to(w[:, k:k + 1], (DP_TM, 128)).astype(BF16)
        unsort = jnp.where(col_iota == jnp.tile(pos_k, (1, SORT_ROWS // 128)),
                           jnp.tile(w_k, (1, SORT_ROWS // 128)), unsort)
    prefetch(1)

    _wait_groups(ngrp_ref[i], lambda g, r: group_copy(slot, 0, 0), lambda: bulk_copy(slot))
    half = SORT_ROWS // 2
    y = y + jnp.dot(unsort[:, :half], ybuf_ref[slot, pl.ds(0, half), :],
                    preferred_element_type=F32)
    prefetch(2)
    y = y + jnp.dot(unsort[:, half:], ybuf_ref[slot, pl.ds(half, half), :],
                    preferred_element_type=F32)
    prefetch(3)
    out = x_ref[...] + m_ref[:, 5 * D:6 * D] * y
    if final_norm:
        out = _rms(out) * fg_ref[...]
    o_ref[...] = out


def _combine(ngrp, grp_row, ys, lpos_t, w_t, x, hx, mods, sg_bf, su_bf, sd_bf, fg, n_tok, final_norm):
    tm = DP_TM
    row = lambda i, *_: (i, 0)
    const2 = lambda i, *_: (0, 0)
    return pl.pallas_call(
        functools.partial(_combine_body, final_norm=final_norm),
        grid_spec=pltpu.PrefetchScalarGridSpec(
            num_scalar_prefetch=1,
            grid=(n_tok // tm,),
            in_specs=[
                pl.BlockSpec((None, 1, N_GRPS), lambda i, *_: (i, 0, 0), memory_space=pltpu.SMEM),
                pl.BlockSpec((None, 1, N_GRPS), lambda i, *_: (jnp.minimum(i + 1, n_tok // tm - 1), 0, 0),
                             memory_space=pltpu.SMEM),
                pl.BlockSpec(memory_space=pl.ANY),
                pl.BlockSpec((tm, TOP_K), row),
                pl.BlockSpec((tm, TOP_K), row),
                pl.BlockSpec((tm, D), row),
                pl.BlockSpec((tm, D), row),
                pl.BlockSpec((None, 1, 6 * D), lambda i, *_: (_mod_row(i, tm), 0, 0)),
                pl.BlockSpec((D, EXP_FF), const2),
                pl.BlockSpec((D, EXP_FF), const2),
                pl.BlockSpec((EXP_FF, D), const2),
                pl.BlockSpec((1, D), const2),
            ],
            out_specs=pl.BlockSpec((tm, D), row),
            scratch_shapes=[pltpu.VMEM((2, SORT_ROWS, D), BF16), pltpu.SemaphoreType.DMA((2,))],
        ),
        out_shape=jax.ShapeDtypeStruct((n_tok, D), F32),
        compiler_params=_cparams(1, vmem_mb=56),
        name="moe_combine",
    )(ngrp, grp_row, grp_row, ys, lpos_t, w_t, x, hx, mods, sg_bf, su_bf, sd_bf, fg)


def _moe(x_new, hx, logits_t, mods, rbias, wg, wu, wd, sg_bf, su_bf, sd_bf, fg, n_tok, layer,
         final_norm):
    w, lpos, tile_cnt = _route(logits_t, rbias, n_tok)
    n_tiles = n_tok // DP_TM
    n_blk = (n_tok * TOP_K + (SLOT_ALIGN - 1) * N_EXP * n_tiles) // MOE_BLK + N_EXP
    tcnt = tile_cnt[:, :, 0].astype(I32)
    cnt_al = (tcnt + SLOT_ALIGN - 1) // SLOT_ALIGN * SLOT_ALIGN
    loc_end = jnp.cumsum(cnt_al, axis=1)
    loc = loc_end - cnt_al
    slots_e = jnp.sum(cnt_al, axis=0)
    nblk_e = (slots_e + MOE_BLK - 1) // MOE_BLK
    blk_end = jnp.cumsum(nblk_e)
    pstart = (blk_end - nblk_e) * MOE_BLK
    nused = blk_end[-1:].astype(I32)
    blk_ids = jnp.arange(n_blk, dtype=I32)
    xblk = jnp.minimum(blk_ids, nused[0] - 1)
    bexp = jnp.minimum(
        jnp.sum(blk_end[None, :] <= xblk[:, None], axis=1), N_EXP - 1).astype(I32)
    off = pstart[None, :] + jnp.cumsum(cnt_al, axis=0) - cnt_al
    g_row = jnp.arange(N_GRPS, dtype=I32) * SLOT_ALIGN
    e_of_g = jnp.minimum(
        jnp.sum(loc_end[:, None, :] <= g_row[None, :, None], axis=2), N_EXP - 1)
    pick = e_of_g[..., None] == jnp.arange(N_EXP, dtype=I32)
    grp_row = (jnp.sum(jnp.where(pick, (off - loc)[:, None, :], 0), axis=2)
               + g_row[None, :]).astype(I32).reshape(n_tiles, 1, N_GRPS)
    ngrp = (loc_end[:, -1] // SLOT_ALIGN).astype(I32)
    unused = nused[0] + blk_ids
    fill = jnp.concatenate([
        (N_EXP + n_blk - nused[0])[None],
        jnp.where(nblk_e > 0, blk_end - 1, -1),
        jnp.where(unused < n_blk, unused, -1)]).astype(I32)
    xs = _dispatch(ngrp, fill, grp_row, hx, lpos, n_tok, n_blk * MOE_BLK)
    ys = _experts(bexp, xblk, nused, xs, wg, wu, wd, n_blk, layer)
    return _combine(ngrp, grp_row, ys, lpos.T, w.T, x_new, hx, mods, sg_bf, su_bf, sd_bf, fg,
                    n_tok, final_norm)


def _rope_tables():
    t = np.arange(S)
    row = (t // GRID_W).astype(np.float32)
    col = (t % GRID_W).astype(np.float32)
    half = HD // 2
    inv = jnp.asarray(ROPE_BASE, F32) ** (-jnp.arange(0, half, 2, dtype=F32) / half)
    ang_r = jnp.asarray(row)[:, None] * inv
    ang_c = jnp.asarray(col)[:, None] * inv
    ang = jnp.concatenate([ang_r, ang_r, ang_c, ang_c], axis=-1)
    n_rep = ROT_W // HD
    cos = jnp.tile(jnp.cos(ang), (1, n_rep))
    sin = jnp.tile(jnp.sin(ang), (1, n_rep))
    cos = jnp.concatenate([cos, jnp.ones((IN_TM, ROT_W), F32)], axis=0)
    sin = jnp.concatenate([sin, jnp.zeros((IN_TM, ROT_W), F32)], axis=0)
    return cos, sin


def kernel(x, c, ctx, c_ctx, mod_w, mod_b, norm1_g, w_in, attn_sink, sgu_norm_g, sgu_w, sgu_b, na_rpb,
           group_norm_g, w_out, norm2_g, router_w, router_bias, exp_w_gate, exp_w_up, exp_w_down,
           shared_w_gate, shared_w_up, shared_w_down, final_g):
    xc = jnp.concatenate([x.reshape(N_LAT, D), ctx.reshape(N_CTX, D)], axis=0)
    cc = jnp.concatenate([c, c_ctx[None, :], jnp.zeros((16 - B - 1, D), F32)], axis=0)
    mods_all = _modulation(cc, mod_w, mod_b)
    cos_t, sin_t = _rope_tables()
    seg = np.arange(B_W) // HD
    bd = jnp.asarray((seg[:, None] == seg[None, :]).astype(np.float32) / HD, BF16)
    fg = final_g.reshape(1, D)

    out = None
    for l in range(DEPTH):
        last = l == DEPTH - 1
        mods = mods_all[l].reshape(16, 1, 6 * D)
        za, zc, uv = _in_proj(xc, mods, norm1_g[l].reshape(1, D), w_in[l].astype(BF16), cos_t, sin_t)
        ya = _attn_a(za, attn_sink[l], with_ctx=not last)
        yc = _na(zc, _na_bias_table(na_rpb[l]), with_ctx=not last)
        n_tok = N_LAT if last else N_ALL
        sb_tab = jnp.repeat(sgu_b[l].T, HD, axis=1)
        x_new, hx, logits_t = _merge(
            xc, ya, yc, uv, mods, group_norm_g[l].reshape(1, D), sgu_norm_g[l].reshape(1, B_W),
            sgu_w[l].astype(BF16), sb_tab, bd, w_out[l].astype(BF16), norm2_g[l].reshape(1, D),
            _router_split(router_w[l]), n_tok)
        res = _moe(x_new, hx, logits_t, mods, router_bias[l].reshape(N_EXP, 1),
                   exp_w_gate, exp_w_up, exp_w_down,
                   shared_w_gate[l].astype(BF16), shared_w_up[l].astype(BF16),
                   shared_w_down[l].astype(BF16), fg, n_tok, layer=l, final_norm=last)
        if last:
            out = res.reshape(B, S, D)
        else:
            xc = res
    return out
```

```python
import functools

import jax
import jax.numpy as jnp
import numpy as np
from jax import lax
from jax.experimental import pallas as pl
from jax.experimental.pallas import tpu as pltpu

F32 = jnp.float32
BF16 = jnp.bfloat16
I32 = jnp.int32

D = 1024
B = 8
S = 2048
C = 256
DEPTH = 2
GRID_W = 64
HD = 64
A_HEADS = 6
A_KV = 2
A_WIN = 128
A_BLK = 256
ROPE_BASE = 10000.0
SGU_GROUPS = 4
SGU_CHUNK = 128
C_HEADS = 6
C_WIN_R = 8
C_WIN_C = 16
A_W = A_HEADS * HD
B_W = SGU_GROUPS * HD
C_W = C_HEADS * HD
KV_W = A_KV * HD
IN_W = A_W + 2 * KV_W + 2 * B_W + 3 * C_W
N_EXP = 64
TOP_K = 8
N_GRP = 8
TOPK_GRP = 4
EXP_FF = 256
ROUTED_SCALE = 2.5
MOE_BLK = 1024
EPS = 1e-6
NEG = -1e30
SCALE = HD ** -0.5

N_LAT = B * S
N_CTX = B * C
N_ALL = N_LAT + N_CTX
ROWS = S // GRID_W

HIGHEST = lax.Precision.HIGHEST
ARB = pltpu.ARBITRARY

NT_DIMS = (((1,), (1,)), ((), ()))


def _cparams(n_axes, vmem_mb=48):
    return pltpu.CompilerParams(
        dimension_semantics=(ARB,) * n_axes, vmem_limit_bytes=vmem_mb * 1024 * 1024)


def _mod_row(i, tm):
    return jnp.where(i < N_LAT // tm, i // (S // tm), B)


def _rms(x):
    return x * lax.rsqrt(jnp.mean(x * x, axis=-1, keepdims=True) + EPS)


def _split_bf16(x):
    hi = x.astype(BF16)
    return hi, (x - hi.astype(F32)).astype(BF16)


def _router_split(rw):
    hi, lo = _split_bf16(jnp.pad(rw, ((0, 0), (0, 128 - N_EXP))))
    return jnp.stack([hi, lo])


MOD_TN = 1024


def _mod_body(cc_ref, w_ref, b_ref, o_ref):
    a = cc_ref[...]
    a = a * jax.nn.sigmoid(a)
    o_ref[...] = jnp.dot(a.astype(BF16), w_ref[...].astype(BF16),
                         preferred_element_type=F32) + b_ref[...]


def _modulation(cc, mod_w, mod_b):
    n_col = 6 * D // MOD_TN
    return pl.pallas_call(
        _mod_body,
        grid=(DEPTH, n_col),
        in_specs=[
            pl.BlockSpec((16, D), lambda l, j: (0, 0)),
            pl.BlockSpec((None, D, MOD_TN), lambda l, j: (l, 0, j)),
            pl.BlockSpec((None, 1, MOD_TN), lambda l, j: (l, 0, j)),
        ],
        out_specs=pl.BlockSpec((None, 16, MOD_TN), lambda l, j: (l, 0, j)),
        out_shape=jax.ShapeDtypeStruct((DEPTH, 16, 6 * D), F32),
        compiler_params=_cparams(2),
        name="modulation",
    )(cc, mod_w, mod_b.reshape(DEPTH, 1, 6 * D))


IN_TM = 256
ROT_W = A_W + KV_W


def _in_body(x_ref, m_ref, g_ref, w_ref, cos_ref, sin_ref, za_ref, zc_ref, uv_ref):
    x = x_ref[...]
    h = _rms(x) * g_ref[...]
    h = h * (1.0 + m_ref[:, D:2 * D]) + m_ref[:, 0:D]
    z = jnp.dot(h.astype(BF16), w_ref[...], preferred_element_type=F32)
    qk = z[:, :ROT_W]
    lane = lax.broadcasted_iota(I32, qk.shape, 1)
    rot = jnp.where((lane & 16) == 0,
                    -pltpu.roll(qk, ROT_W - 16, 1), pltpu.roll(qk, 16, 1))
    qk = qk * cos_ref[...] + rot * sin_ref[...]
    za_ref[...] = jnp.concatenate(
        [qk[:, :A_W] * SCALE, qk[:, A_W:], z[:, ROT_W:ROT_W + KV_W]], axis=1).astype(BF16)
    u0 = ROT_W + KV_W
    uv_ref[...] = jax.nn.gelu(z[:, u0:u0 + 2 * B_W])
    c0 = u0 + 2 * B_W
    zc_ref[...] = jnp.concatenate(
        [z[:, c0:c0 + C_W] * SCALE, z[:, c0 + C_W:]], axis=1).astype(BF16)


def _in_proj(xc, mods, g, w_bf, cos_t, sin_t):
    tm = IN_TM
    n_t = N_ALL // tm

    def tab_idx(i):
        return (jnp.where(i < N_LAT // tm, i % (S // tm), S // tm), 0)

    return pl.pallas_call(
        _in_body,
        grid=(n_t,),
        in_specs=[
            pl.BlockSpec((tm, D), lambda i: (i, 0)),
            pl.BlockSpec((None, 1, 6 * D), lambda i: (_mod_row(i, tm), 0, 0)),
            pl.BlockSpec((1, D), lambda i: (0, 0)),
            pl.BlockSpec((D, IN_W), lambda i: (0, 0)),
            pl.BlockSpec((tm, ROT_W), tab_idx),
            pl.BlockSpec((tm, ROT_W), tab_idx),
        ],
        out_specs=[
            pl.BlockSpec((tm, A_W + 2 * KV_W), lambda i: (i, 0)),
            pl.BlockSpec((tm, 3 * C_W), lambda i: (i, 0)),
            pl.BlockSpec((tm, 2 * B_W), lambda i: (i, 0)),
        ],
        out_shape=[
            jax.ShapeDtypeStruct((N_ALL, A_W + 2 * KV_W), BF16),
            jax.ShapeDtypeStruct((N_ALL, 3 * C_W), BF16),
            jax.ShapeDtypeStruct((N_ALL, 2 * B_W), F32),
        ],
        compiler_params=_cparams(1),
        name="in_proj",
    )(xc, mods, g, w_bf, cos_t, sin_t)


A_BAND = A_BLK + 2 * A_WIN
N_QB_LAT = N_LAT // A_BLK
QB_PER_SEQ = S // A_BLK
QB_PER_CTX = C // A_BLK


def _softmax_pv(s_list, v_list, extra_logit=None):
    m = s_list[0].max(axis=-1, keepdims=True)
    for s in s_list[1:]:
        m = jnp.maximum(m, s.max(axis=-1, keepdims=True))
    if extra_logit is not None:
        m = jnp.maximum(m, extra_logit)
    den = None
    out = None
    for s, v in zip(s_list, v_list):
        p = jnp.exp(s - m)
        d = p.sum(axis=-1, keepdims=True)
        o = jnp.dot(p.astype(BF16), v, preferred_element_type=F32)
        den = d if den is None else den + d
        out = o if out is None else out + o
    if extra_logit is not None:
        den = den + jnp.exp(extra_logit - m)
    return out / den


def _attn_a_body(sink_ref, q_ref, kb_ref, vb_ref, kc_ref, vc_ref, o_ref):
    i = pl.program_id(0)
    is_lat = i < N_QB_LAT
    n = i % QB_PER_SEQ
    start = pl.multiple_of(jnp.clip(n * A_BLK - A_WIN, 0, S - A_BAND), A_WIN)
    qpos = n * A_BLK + lax.broadcasted_iota(I32, (A_BLK, A_BAND), 0)
    kpos = start + lax.broadcasted_iota(I32, (A_BLK, A_BAND), 1)
    mask = jnp.abs(kpos - qpos) <= jnp.where(is_lat, A_WIN, -1)
    kb = kb_ref[pl.ds(start, A_BAND), :]
    vb = vb_ref[pl.ds(start, A_BAND), :]
    kc = kc_ref[...]
    vc = vc_ref[...]
    outs = []
    for h in range(A_HEADS):
        kv = h // (A_HEADS // A_KV)
        sl = slice(kv * HD, (kv + 1) * HD)
        q = q_ref[:, h * HD:(h + 1) * HD]
        s_b = lax.dot_general(q, kb[:, sl], NT_DIMS, preferred_element_type=F32)
        s_b = jnp.where(mask, s_b, NEG)
        s_c = lax.dot_general(q, kc[:, sl], NT_DIMS, preferred_element_type=F32)
        outs.append(_softmax_pv([s_c, s_b], [vc[:, sl], vb[:, sl]], sink_ref[h]))
    o_ref[...] = jnp.concatenate(outs, axis=1)


def _attn_a(za, sink, with_ctx):
    n_qb = N_QB_LAT + (N_CTX // A_BLK if with_ctx else 0)

    def bidx(i):
        return jnp.where(i < N_QB_LAT, i // QB_PER_SEQ, (i - N_QB_LAT) // QB_PER_CTX)

    k_col = A_W // KV_W
    v_col = k_col + 1
    return pl.pallas_call(
        _attn_a_body,
        grid=(n_qb,),
        in_specs=[
            pl.BlockSpec(memory_space=pltpu.SMEM),
            pl.BlockSpec((A_BLK, A_W), lambda i: (i, 0)),
            pl.BlockSpec((S, KV_W), lambda i: (bidx(i), k_col)),
            pl.BlockSpec((S, KV_W), lambda i: (bidx(i), v_col)),
            pl.BlockSpec((C, KV_W), lambda i: (N_LAT // C + bidx(i), k_col)),
            pl.BlockSpec((C, KV_W), lambda i: (N_LAT // C + bidx(i), v_col)),
        ],
        out_specs=pl.BlockSpec((A_BLK, A_W), lambda i: (i, 0)),
        out_shape=jax.ShapeDtypeStruct((n_qb * A_BLK, A_W), F32),
        compiler_params=_cparams(1),
        name="attn_window",
    )(sink, za, za, za, za, za)


NA_R = 4
NA_TQ = NA_R * GRID_W
NA_KROWS = 12
NA_WIN = NA_KROWS * GRID_W
NA_STEPS = ROWS // NA_R
NA_PAIRS = NA_KROWS // 2
N_DR = 2 * C_WIN_R - 1


def _na_body(q_ref, k_ref, v_ref, kc_ref, vc_ref, tab_ref, o_ref):
    j = pl.program_id(1)
    is_lat = j < NA_STEPS
    r0 = jnp.minimum(j, NA_STEPS - 1) * NA_R
    u0 = jnp.clip(r0 - C_WIN_R // 2, 0, ROWS - NA_KROWS)
    k0 = pl.multiple_of(u0 * GRID_W, GRID_W)
    kw = k_ref[pl.ds(k0, NA_WIN), :]
    vw = v_ref[pl.ds(k0, NA_WIN), :]
    kc = kc_ref[...]
    vc = vc_ref[...]
    q = q_ref[...]
    left = lax.broadcasted_iota(I32, (1, 2 * GRID_W), 1) < GRID_W

    tab_idx = []
    penalty = []
    for rr in range(NA_R):
        r = r0 + rr
        start = jnp.clip(r - C_WIN_R // 2, 0, ROWS - C_WIN_R)
        idx_row = []
        pen_row = []
        for p in range(NA_PAIRS):
            kr = u0 + 2 * p
            idx_row.append(jnp.clip(kr - r + (C_WIN_R - 1), -1, N_DR - 1) + 1)
            pens = []
            for half in range(2):
                ok = jnp.logical_and(is_lat, jnp.logical_and(kr + half >= start,
                                                             kr + half < start + C_WIN_R))
                pens.append(jnp.where(ok, 0.0, NEG))
            pen_row.append(jnp.where(left, pens[0], pens[1]))
        tab_idx.append(idx_row)
        penalty.append(jnp.concatenate(pen_row, axis=1))

    outs = []
    for h in range(C_HEADS):
        sl = slice(h * HD, (h + 1) * HD)
        qh = q[:, sl]
        s_w = lax.dot_general(qh, kw[:, sl], NT_DIMS, preferred_element_type=F32)
        bias = jnp.concatenate([
            jnp.concatenate([tab_ref[h, pl.ds(tab_idx[rr][p], 1)][0] for p in range(NA_PAIRS)],
                            axis=1) + penalty[rr]
            for rr in range(NA_R)], axis=0)
        s_c = lax.dot_general(qh, kc[:, sl], NT_DIMS, preferred_element_type=F32)
        outs.append(_softmax_pv([s_c, s_w + bias], [vc[:, sl], vw[:, sl]]))
    o_ref[...] = jnp.concatenate(outs, axis=1)


def _na(zc, tab, with_ctx):
    n_j = NA_STEPS + (1 if with_ctx else 0)

    def qidx(b, j):
        return jnp.where(j < NA_STEPS, b * NA_STEPS + j, N_LAT // NA_TQ + b)

    n_out = N_LAT + (N_CTX if with_ctx else 0)
    return pl.pallas_call(
        _na_body,
        grid=(B, n_j),
        in_specs=[
            pl.BlockSpec((NA_TQ, C_W), lambda b, j: (qidx(b, j), 0)),
            pl.BlockSpec((S, C_W), lambda b, j: (b, 1)),
            pl.BlockSpec((S, C_W), lambda b, j: (b, 2)),
            pl.BlockSpec((C, C_W), lambda b, j: (N_LAT // C + b, 1)),
            pl.BlockSpec((C, C_W), lambda b, j: (N_LAT // C + b, 2)),
            pl.BlockSpec((C_HEADS, N_DR + 1, GRID_W, 2 * GRID_W), lambda b, j: (0, 0, 0, 0)),
        ],
        out_specs=pl.BlockSpec((NA_TQ, C_W), lambda b, j: (qidx(b, j), 0)),
        out_shape=jax.ShapeDtypeStruct((n_out, C_W), F32),
        compiler_params=_cparams(2),
        name="attn_neighbourhood",
    )(zc, zc, zc, zc, zc, tab)


def _na_bias_table(rpb):
    cq = np.arange(GRID_W)
    col_start = np.clip(cq - C_WIN_C // 2, 0, GRID_W - C_WIN_C)
    col_ok = (cq[None, :] >= col_start[:, None]) & (cq[None, :] < col_start[:, None] + C_WIN_C)
    dc = np.clip(cq[None, :] - cq[:, None], -(C_WIN_C - 1), C_WIN_C - 1) + (C_WIN_C - 1)
    n_dc = 2 * C_WIN_C - 1
    pick = (dc.reshape(-1)[None, :] == np.arange(n_dc)[:, None]).astype(np.float32)
    t = jnp.dot(rpb.astype(F32).reshape(-1, n_dc), jnp.asarray(pick), precision=HIGHEST)
    t = t.reshape(C_HEADS, N_DR, GRID_W, GRID_W)
    t = jnp.where(col_ok[None, None], t, NEG)
    zero = jnp.zeros((C_HEADS, 1, GRID_W, GRID_W), F32)
    ext = jnp.concatenate([zero, t, zero], axis=1)
    return jnp.concatenate([ext[:, :-1], ext[:, 1:]], axis=-1)


MG_TM = 256


def _merge_body(x_ref, ya_ref, yc_ref, uv_ref, m_ref, gg_ref, sg_ref, sw_ref, sb_ref, bd_ref,
                wo_ref, g2_ref, rw_ref, xo_ref, hx_ref, lg_ref):
    u = uv_ref[:, :B_W]
    v = uv_ref[:, B_W:]
    vv_hi, vv_lo = _split_bf16(v * v)
    ms = (jnp.dot(vv_hi, bd_ref[...], preferred_element_type=F32)
          + jnp.dot(vv_lo, bd_ref[...], preferred_element_type=F32))
    vn = (v * lax.rsqrt(ms + EPS) * sg_ref[...]).astype(BF16)
    lane_grp = lax.broadcasted_iota(I32, (SGU_CHUNK, B_W), 1) // HD
    gates = []
    for c in range(MG_TM // SGU_CHUNK):
        vc = vn[c * SGU_CHUNK:(c + 1) * SGU_CHUNK, :]
        gate = sb_ref[...]
        acc = jnp.zeros((SGU_CHUNK, B_W), F32)
        for g in range(SGU_GROUPS):
            r = jnp.dot(sw_ref[g], vc, preferred_element_type=F32)
            acc = jnp.where(lane_grp == g, r, acc)
        gates.append(acc + gate)
    yb = u * jnp.concatenate(gates, axis=0)
    gg = gg_ref[...]
    ycat = jnp.concatenate([
        _rms(ya_ref[...]) * gg[:, :A_W],
        _rms(yb) * gg[:, A_W:A_W + B_W],
        _rms(yc_ref[...]) * gg[:, A_W + B_W:],
    ], axis=1)
    proj = jnp.dot(ycat.astype(BF16), wo_ref[...], preferred_element_type=F32)
    xn = x_ref[...] + m_ref[:, 2 * D:3 * D] * proj
    xo_ref[...] = xn
    hx = _rms(xn) * g2_ref[...]
    hx = hx * (1.0 + m_ref[:, 4 * D:5 * D]) + m_ref[:, 3 * D:4 * D]
    hx_ref[...] = hx
    hx_hi, hx_lo = _split_bf16(hx)
    logits = (jnp.dot(hx_hi, rw_ref[0], preferred_element_type=F32)
              + jnp.dot(hx_hi, rw_ref[1], preferred_element_type=F32)
              + jnp.dot(hx_lo, rw_ref[0], preferred_element_type=F32))
    lg_ref[...] = logits.T[:N_EXP, :]


def _merge(xc, ya, yc, uv, mods, gg, sg, sw_bf, sb_tab, bd, wo_bf, g2, rw_t, n_rows):
    tm = MG_TM
    n_t = n_rows // tm
    const2 = lambda i: (0, 0)
    row = lambda i: (i, 0)
    return pl.pallas_call(
        _merge_body,
        grid=(n_t,),
        in_specs=[
            pl.BlockSpec((tm, D), row),
            pl.BlockSpec((tm, A_W), row),
            pl.BlockSpec((tm, C_W), row),
            pl.BlockSpec((tm, 2 * B_W), row),
            pl.BlockSpec((None, 1, 6 * D), lambda i: (_mod_row(i, tm), 0, 0)),
            pl.BlockSpec((1, D), const2),
            pl.BlockSpec((1, B_W), const2),
            pl.BlockSpec((SGU_GROUPS, SGU_CHUNK, SGU_CHUNK), lambda i: (0, 0, 0)),
            pl.BlockSpec((SGU_CHUNK, B_W), const2),
            pl.BlockSpec((B_W, B_W), const2),
            pl.BlockSpec((D, D), const2),
            pl.BlockSpec((1, D), const2),
            pl.BlockSpec((2, D, 128), lambda i: (0, 0, 0)),
        ],
        out_specs=[
            pl.BlockSpec((tm, D), row),
            pl.BlockSpec((tm, D), row),
            pl.BlockSpec((N_EXP, tm), lambda i: (0, i)),
        ],
        out_shape=[
            jax.ShapeDtypeStruct((n_rows, D), F32),
            jax.ShapeDtypeStruct((n_rows, D), F32),
            jax.ShapeDtypeStruct((N_EXP, n_rows), F32),
        ],
        compiler_params=_cparams(1),
        name="merge",
    )(xc, ya, yc, uv, mods, gg, sg, sw_bf, sb_tab, bd, wo_bf, g2, rw_t)


RT_TM = 256
GRP_SZ = N_EXP // N_GRP
SLOT_ALIGN = 16


def _first_argmax(v, iota):
    m = v.max(axis=0, keepdims=True)
    idx = jnp.where(v == m, iota, float(v.shape[0])).min(axis=0, keepdims=True)
    return m, idx


def _stack_rows(rows, iota):
    out = jnp.zeros(iota.shape, F32)
    for r, v in enumerate(rows):
        out = jnp.where(iota == float(r), v, out)
    return out


def _route_body(lg_ref, rb_ref, w_ref, lp_ref, tc_ref):
    tm = RT_TM
    scores = jax.nn.sigmoid(lg_ref[...])
    sel = scores + rb_ref[...]
    iota_g = lax.broadcasted_iota(I32, (GRP_SZ, tm), 0).astype(F32)
    gs = []
    for g in range(N_GRP):
        v = sel[g * GRP_SZ:(g + 1) * GRP_SZ, :]
        m1, i1 = _first_argmax(v, iota_g)
        m2 = jnp.where(iota_g == i1, -jnp.inf, v).max(axis=0, keepdims=True)
        gs.append(m1 + m2)
    iota_n = lax.broadcasted_iota(I32, (N_GRP, tm), 0).astype(F32)
    gscore = _stack_rows(gs, iota_n)
    gsel = jnp.zeros((N_GRP, tm), F32)
    for _ in range(TOPK_GRP):
        _, gi = _first_argmax(gscore, iota_n)
        hit = iota_n == gi
        gsel = jnp.where(hit, 1.0, gsel)
        gscore = jnp.where(hit, -jnp.inf, gscore)
    emask = jnp.concatenate(
        [jnp.broadcast_to(gsel[g:g + 1, :], (GRP_SZ, tm)) for g in range(N_GRP)], axis=0)
    cand = jnp.where(emask > 0.5, sel, NEG)
    iota_e = lax.broadcasted_iota(I32, (N_EXP, tm), 0).astype(F32)
    hits = []
    ws = []
    member = jnp.zeros((N_EXP, tm), F32)
    for _ in range(TOP_K):
        _, ei = _first_argmax(cand, iota_e)
        hit = iota_e == ei
        hits.append(hit)
        ws.append(jnp.where(hit, scores, 0.0).sum(axis=0, keepdims=True))
        member = jnp.where(hit, 1.0, member)
        cand = jnp.where(hit, -jnp.inf, cand)
    wsum = ws[0]
    for w in ws[1:]:
        wsum = wsum + w
    iota_k = lax.broadcasted_iota(I32, (TOP_K, tm), 0).astype(F32)
    w_ref[...] = _stack_rows(ws, iota_k) / wsum * ROUTED_SCALE
    r_i = lax.broadcasted_iota(I32, (tm, tm), 0)
    c_i = lax.broadcasted_iota(I32, (tm, tm), 1)
    tri = jnp.where(r_i < c_i, 1.0, 0.0).astype(BF16)
    local = jnp.dot(member.astype(BF16), tri, preferred_element_type=F32)
    tile_cnt = member.sum(axis=1, keepdims=True)
    tc_ref[...] = tile_cnt
    aligned = jnp.ceil(tile_cnt / SLOT_ALIGN) * SLOT_ALIGN
    e_r = lax.broadcasted_iota(I32, (N_EXP, N_EXP), 0)
    e_c = lax.broadcasted_iota(I32, (N_EXP, N_EXP), 1)
    below = jnp.where(e_c < e_r, 1.0, 0.0)
    start = jnp.dot(below, jnp.broadcast_to(aligned, (N_EXP, 128)), precision=HIGHEST,
                    preferred_element_type=F32)[:, 0:1]
    pos = local + start
    lp_ref[...] = _stack_rows(
        [jnp.where(hit, pos, 0.0).sum(axis=0, keepdims=True) for hit in hits],
        iota_k).astype(I32)


def _route(logits_t, rbias, n_tok):
    tm = RT_TM
    tok = lambda i: (0, i)
    return pl.pallas_call(
        _route_body,
        grid=(n_tok // tm,),
        in_specs=[
            pl.BlockSpec((N_EXP, tm), tok),
            pl.BlockSpec((N_EXP, 1), lambda i: (0, 0)),
        ],
        out_specs=[
            pl.BlockSpec((TOP_K, tm), tok),
            pl.BlockSpec((TOP_K, tm), tok),
            pl.BlockSpec((None, N_EXP, 1), lambda i: (i, 0, 0)),
        ],
        out_shape=[
            jax.ShapeDtypeStruct((TOP_K, n_tok), F32),
            jax.ShapeDtypeStruct((TOP_K, n_tok), I32),
            jax.ShapeDtypeStruct((n_tok // tm, N_EXP, 1), F32),
        ],
        compiler_params=_cparams(1),
        name="route",
    )(logits_t, rbias)


DP_TM = RT_TM
SORT_ROWS = 3072
N_GRPS = SORT_ROWS // SLOT_ALIGN
ISSUE_UNROLL = 4
WAIT_GROUPS = 32
N_BURSTS = 4
BURST_ROWS = SORT_ROWS // N_BURSTS
BURST_GRPS = BURST_ROWS // SLOT_ALIGN


def _start_groups(g_lo, g_hi, row_of, make_copy):
    n_full = (g_hi - g_lo) // ISSUE_UNROLL

    def body(q, carry):
        for s in range(ISSUE_UNROLL):
            g = g_lo + q * ISSUE_UNROLL + s
            make_copy(g, row_of(g)).start(priority=s % 2)
        return carry

    def tail(g, carry):
        make_copy(g, row_of(g)).start()
        return carry

    lax.fori_loop(0, n_full, body, 0)
    lax.fori_loop(g_lo + n_full * ISSUE_UNROLL, g_hi, tail, 0)


def _start_burst(q, ngrp, row_of, make_copy):
    _start_groups(jnp.minimum(q * BURST_GRPS, ngrp), jnp.minimum((q + 1) * BURST_GRPS, ngrp),
                  row_of, make_copy)


def _wait_groups(ngrp, make_copy, make_bulk):
    def bulk(q, carry):
        make_bulk().wait()
        return carry

    def single(g, carry):
        make_copy(0, 0).wait()
        return carry

    n_bulk = ngrp // WAIT_GROUPS
    lax.fori_loop(0, n_bulk, bulk, 0)
    lax.fori_loop(n_bulk * WAIT_GROUPS, ngrp, single, 0)


def _dispatch_body(ngrp_ref, fill_ref, grow_ref, hx_ref, lp_ref, xs_ref, zbuf_ref, zero_ref, sems):
    i = pl.program_id(0)
    n_tiles = pl.num_programs(0)
    slot = i % 2

    def zero_copy(b):
        dst = xs_ref.at[pl.ds(pl.multiple_of(b * MOE_BLK, MOE_BLK), MOE_BLK)]
        return pltpu.make_async_copy(zero_ref, dst, sems.at[0])

    @pl.when(i == 0)
    def _():
        zero_ref[...] = jnp.zeros_like(zero_ref)
        n_fill = fill_ref[0]

        def z_start(q, carry):
            @pl.when(fill_ref[1 + q] >= 0)
            def _():
                zero_copy(fill_ref[1 + q]).start()
            return carry

        def z_wait(q, carry):
            @pl.when(fill_ref[1 + q] >= 0)
            def _():
                zero_copy(0).wait()
            return carry

        lax.fori_loop(0, n_fill, z_start, 0)
        lax.fori_loop(0, n_fill, z_wait, 0)

    def group_copy(s, g, row):
        src = zbuf_ref.at[s, pl.ds(pl.multiple_of(g * SLOT_ALIGN, SLOT_ALIGN), SLOT_ALIGN)]
        dst = xs_ref.at[pl.ds(pl.multiple_of(row, SLOT_ALIGN), SLOT_ALIGN)]
        return pltpu.make_async_copy(src, dst, sems.at[s])

    def bulk_copy(s):
        n_rows = WAIT_GROUPS * SLOT_ALIGN
        return pltpu.make_async_copy(
            zbuf_ref.at[s, pl.ds(0, n_rows)], xs_ref.at[pl.ds(0, n_rows)], sems.at[s])

    x = hx_ref[...].astype(BF16)
    lpos = lp_ref[...]
    n_cur = ngrp_ref[i]
    pos = [jnp.tile(jnp.broadcast_to(lpos[k:k + 1, :], (16, DP_TM)).astype(jnp.int16),
                    (BURST_ROWS // 16, 1)) for k in range(TOP_K)]
    for q in range(N_BURSTS):
        row_iota = (lax.broadcasted_iota(I32, (BURST_ROWS, DP_TM), 0)
                    + q * BURST_ROWS).astype(jnp.int16)
        onehot = jnp.zeros((BURST_ROWS, DP_TM), BF16)
        for k in range(TOP_K):
            onehot = jnp.where(row_iota == pos[k], jnp.ones((), BF16), onehot)
        z = jnp.dot(onehot, x, preferred_element_type=F32)
        zbuf_ref[slot, pl.ds(q * BURST_ROWS, BURST_ROWS), :] = z.astype(BF16)
        _start_burst(q, n_cur, lambda g: grow_ref[0, g], lambda g, r: group_copy(slot, g, r))

    @pl.when(i > 0)
    def _():
        _wait_groups(ngrp_ref[jnp.maximum(i - 1, 0)], lambda g, r: group_copy(1 - slot, 0, 0),
                     lambda: bulk_copy(1 - slot))

    @pl.when(i == n_tiles - 1)
    def _():
        _wait_groups(n_cur, lambda g, r: group_copy(slot, 0, 0), lambda: bulk_copy(slot))


def _dispatch(ngrp, fill, grp_row, hx, lpos, n_tok, n_slots):
    return pl.pallas_call(
        _dispatch_body,
        grid_spec=pltpu.PrefetchScalarGridSpec(
            num_scalar_prefetch=2,
            grid=(n_tok // DP_TM,),
            in_specs=[
                pl.BlockSpec((None, 1, N_GRPS), lambda i, *_: (i, 0, 0), memory_space=pltpu.SMEM),
                pl.BlockSpec((DP_TM, D), lambda i, *_: (i, 0)),
                pl.BlockSpec((TOP_K, DP_TM), lambda i, *_: (0, i)),
            ],
            out_specs=pl.BlockSpec(memory_space=pl.ANY),
            scratch_shapes=[
                pltpu.VMEM((2, SORT_ROWS, D), BF16),
                pltpu.VMEM((MOE_BLK, D), BF16),
                pltpu.SemaphoreType.DMA((2,)),
            ],
        ),
        out_shape=jax.ShapeDtypeStruct((n_slots, D), BF16),
        compiler_params=_cparams(1, vmem_mb=56),
        name="moe_dispatch",
    )(ngrp, fill, grp_row, hx, lpos)


def _experts_body(bexp_ref, xblk_ref, nused_ref, xs_ref, wg_ref, wu_ref, wd_ref, ys_ref):
    j = pl.program_id(0)

    @pl.when(j < nused_ref[0])
    def _():
        x = xs_ref[...]
        g = jnp.dot(x, wg_ref[...].astype(BF16), preferred_element_type=F32)
        u = jnp.dot(x, wu_ref[...].astype(BF16), preferred_element_type=F32)
        h = (g * jax.nn.sigmoid(g) * u).astype(BF16)
        y = jnp.dot(h, wd_ref[...].astype(BF16), preferred_element_type=F32)
        ys_ref[...] = y.astype(BF16)

    @pl.when(j >= nused_ref[0])
    def _():
        ys_ref[...] = jnp.zeros_like(ys_ref)


def _experts(bexp, xblk, nused, xs, wg, wu, wd, n_blk, layer):
    w_idx = lambda j, be, xb, nu: (layer, be[j], 0, 0)
    return pl.pallas_call(
        _experts_body,
        grid_spec=pltpu.PrefetchScalarGridSpec(
            num_scalar_prefetch=3,
            grid=(n_blk,),
            in_specs=[
                pl.BlockSpec((MOE_BLK, D), lambda j, be, xb, nu: (xb[j], 0)),
                pl.BlockSpec((None, None, D, EXP_FF), w_idx),
                pl.BlockSpec((None, None, D, EXP_FF), w_idx),
                pl.BlockSpec((None, None, EXP_FF, D), w_idx),
            ],
            out_specs=pl.BlockSpec((MOE_BLK, D), lambda j, be, xb, nu: (j, 0)),
        ),
        out_shape=jax.ShapeDtypeStruct((n_blk * MOE_BLK, D), BF16),
        compiler_params=_cparams(1),
        name="moe_experts",
    )(bexp, xblk, nused, xs, wg, wu, wd)


def _combine_body(ngrp_ref, grow_ref, grow_next_ref, ys_ref, lp_ref, w_ref, x_ref, hx_ref, m_ref,
                  sg_ref, su_ref, sd_ref, fg_ref, o_ref, ybuf_ref, sems, *, final_norm):
    i = pl.program_id(0)
    n_tiles = pl.num_programs(0)
    slot = i % 2

    def group_copy(s, g, row):
        src = ys_ref.at[pl.ds(pl.multiple_of(row, SLOT_ALIGN), SLOT_ALIGN)]
        dst = ybuf_ref.at[s, pl.ds(pl.multiple_of(g * SLOT_ALIGN, SLOT_ALIGN), SLOT_ALIGN)]
        return pltpu.make_async_copy(src, dst, sems.at[s])

    def bulk_copy(s):
        n_rows = WAIT_GROUPS * SLOT_ALIGN
        return pltpu.make_async_copy(
            ys_ref.at[pl.ds(0, n_rows)], ybuf_ref.at[s, pl.ds(0, n_rows)], sems.at[s])

    @pl.when(i == 0)
    def _():
        ybuf_ref[...] = jnp.zeros_like(ybuf_ref)
        _start_groups(0, ngrp_ref[0], lambda g: grow_ref[0, g], lambda g, r: group_copy(0, g, r))

    n_next = jnp.where(i + 1 < n_tiles, ngrp_ref[jnp.minimum(i + 1, n_tiles - 1)], 0)

    def prefetch(q):
        _start_burst(q, n_next, lambda g: grow_next_ref[0, g],
                     lambda g, r: group_copy(1 - slot, g, r))

    hx = hx_ref[...].astype(BF16)
    g = jnp.dot(hx, sg_ref[...], preferred_element_type=F32)
    u = jnp.dot(hx, su_ref[...], preferred_element_type=F32)
    h = (g * jax.nn.sigmoid(g) * u).astype(BF16)
    y = jnp.dot(h, sd_ref[...], preferred_element_type=F32)
    prefetch(0)

    lpos = lp_ref[...]
    w = w_ref[...]
    col_iota = lax.broadcasted_iota(I32, (DP_TM, SORT_ROWS), 1).astype(jnp.int16)
    unsort = jnp.zeros((DP_TM, SORT_ROWS), BF16)
    for k in range(TOP_K):
        pos_k = jnp.broadcast_to(lpos[:, k:k + 1], (DP_TM, 128)).astype(jnp.int16)
        w_k = jnp.broadcast_to(w[:, k:k + 1], (DP_TM, 128)).astype(BF16)
        unsort = jnp.where(col_iota == jnp.tile(pos_k, (1, SORT_ROWS // 128)),
                           jnp.tile(w_k, (1, SORT_ROWS // 128)), unsort)
    prefetch(1)

    _wait_groups(ngrp_ref[i], lambda g, r: group_copy(slot, 0, 0), lambda: bulk_copy(slot))
    half = SORT_ROWS // 2
    y = y + jnp.dot(unsort[:, :half], ybuf_ref[slot, pl.ds(0, half), :],
                    preferred_element_type=F32)
    prefetch(2)
    y = y + jnp.dot(unsort[:, half:], ybuf_ref[slot, pl.ds(half, half), :],
                    preferred_element_type=F32)
    prefetch(3)
    out = x_ref[...] + m_ref[:, 5 * D:6 * D] * y
    if final_norm:
        out = _rms(out) * fg_ref[...]
    o_ref[...] = out


def _combine(ngrp, grp_row, ys, lpos_t, w_t, x, hx, mods, sg_bf, su_bf, sd_bf, fg, n_tok, final_norm):
    tm = DP_TM
    row = lambda i, *_: (i, 0)
    const2 = lambda i, *_: (0, 0)
    return pl.pallas_call(
        functools.partial(_combine_body, final_norm=final_norm),
        grid_spec=pltpu.PrefetchScalarGridSpec(
            num_scalar_prefetch=1,
            grid=(n_tok // tm,),
            in_specs=[
                pl.BlockSpec((None, 1, N_GRPS), lambda i, *_: (i, 0, 0), memory_space=pltpu.SMEM),
                pl.BlockSpec((None, 1, N_GRPS), lambda i, *_: (jnp.minimum(i + 1, n_tok // tm - 1), 0, 0),
                             memory_space=pltpu.SMEM),
                pl.BlockSpec(memory_space=pl.ANY),
                pl.BlockSpec((tm, TOP_K), row),
                pl.BlockSpec((tm, TOP_K), row),
                pl.BlockSpec((tm, D), row),
                pl.BlockSpec((tm, D), row),
                pl.BlockSpec((None, 1, 6 * D), lambda i, *_: (_mod_row(i, tm), 0, 0)),
                pl.BlockSpec((D, EXP_FF), const2),
                pl.BlockSpec((D, EXP_FF), const2),
                pl.BlockSpec((EXP_FF, D), const2),
                pl.BlockSpec((1, D), const2),
            ],
            out_specs=pl.BlockSpec((tm, D), row),
            scratch_shapes=[pltpu.VMEM((2, SORT_ROWS, D), BF16), pltpu.SemaphoreType.DMA((2,))],
        ),
        out_shape=jax.ShapeDtypeStruct((n_tok, D), F32),
        compiler_params=_cparams(1, vmem_mb=56),
        name="moe_combine",
    )(ngrp, grp_row, grp_row, ys, lpos_t, w_t, x, hx, mods, sg_bf, su_bf, sd_bf, fg)


def _moe(x_new, hx, logits_t, mods, rbias, wg, wu, wd, sg_bf, su_bf, sd_bf, fg, n_tok, layer,
         final_norm):
    w, lpos, tile_cnt = _route(logits_t, rbias, n_tok)
    n_tiles = n_tok // DP_TM
    n_blk = (n_tok * TOP_K + (SLOT_ALIGN - 1) * N_EXP * n_tiles) // MOE_BLK + N_EXP
    tcnt = tile_cnt[:, :, 0].astype(I32)
    cnt_al = (tcnt + SLOT_ALIGN - 1) // SLOT_ALIGN * SLOT_ALIGN
    loc_end = jnp.cumsum(cnt_al, axis=1)
    loc = loc_end - cnt_al
    slots_e = jnp.sum(cnt_al, axis=0)
    nblk_e = (slots_e + MOE_BLK - 1) // MOE_BLK
    blk_end = jnp.cumsum(nblk_e)
    pstart = (blk_end - nblk_e) * MOE_BLK
    nused = blk_end[-1:].astype(I32)
    blk_ids = jnp.arange(n_blk, dtype=I32)
    xblk = jnp.minimum(blk_ids, nused[0] - 1)
    bexp = jnp.minimum(
        jnp.sum(blk_end[None, :] <= xblk[:, None], axis=1), N_EXP - 1).astype(I32)
    off = pstart[None, :] + jnp.cumsum(cnt_al, axis=0) - cnt_al
    g_row = jnp.arange(N_GRPS, dtype=I32) * SLOT_ALIGN
    e_of_g = jnp.minimum(
        jnp.sum(loc_end[:, None, :] <= g_row[None, :, None], axis=2), N_EXP - 1)
    pick = e_of_g[..., None] == jnp.arange(N_EXP, dtype=I32)
    grp_row = (jnp.sum(jnp.where(pick, (off - loc)[:, None, :], 0), axis=2)
               + g_row[None, :]).astype(I32).reshape(n_tiles, 1, N_GRPS)
    ngrp = (loc_end[:, -1] // SLOT_ALIGN).astype(I32)
    unused = nused[0] + blk_ids
    fill = jnp.concatenate([
        (N_EXP + n_blk - nused[0])[None],
        jnp.where(nblk_e > 0, blk_end - 1, -1),
        jnp.where(unused < n_blk, unused, -1)]).astype(I32)
    xs = _dispatch(ngrp, fill, grp_row, hx, lpos, n_tok, n_blk * MOE_BLK)
    ys = _experts(bexp, xblk, nused, xs, wg, wu, wd, n_blk, layer)
    return _combine(ngrp, grp_row, ys, lpos.T, w.T, x_new, hx, mods, sg_bf, su_bf, sd_bf, fg,
                    n_tok, final_norm)


def _rope_tables():
    t = np.arange(S)
    row = (t // GRID_W).astype(np.float32)
    col = (t % GRID_W).astype(np.float32)
    half = HD // 2
    inv = jnp.asarray(ROPE_BASE, F32) ** (-jnp.arange(0, half, 2, dtype=F32) / half)
    ang_r = jnp.asarray(row)[:, None] * inv
    ang_c = jnp.asarray(col)[:, None] * inv
    ang = jnp.concatenate([ang_r, ang_r, ang_c, ang_c], axis=-1)
    n_rep = ROT_W // HD
    cos = jnp.tile(jnp.cos(ang), (1, n_rep))
    sin = jnp.tile(jnp.sin(ang), (1, n_rep))
    cos = jnp.concatenate([cos, jnp.ones((IN_TM, ROT_W), F32)], axis=0)
    sin = jnp.concatenate([sin, jnp.zeros((IN_TM, ROT_W), F32)], axis=0)
    return cos, sin


def kernel(x, c, ctx, c_ctx, mod_w, mod_b, norm1_g, w_in, attn_sink, sgu_norm_g, sgu_w, sgu_b, na_rpb,
           group_norm_g, w_out, norm2_g, router_w, router_bias, exp_w_gate, exp_w_up, exp_w_down,
           shared_w_gate, shared_w_up, shared_w_down, final_g):
    xc = jnp.concatenate([x.reshape(N_LAT, D), ctx.reshape(N_CTX, D)], axis=0)
    cc = jnp.concatenate([c, c_ctx[None, :], jnp.zeros((16 - B - 1, D), F32)], axis=0)
    mods_all = _modulation(cc, mod_w, mod_b)
    cos_t, sin_t = _rope_tables()
    seg = np.arange(B_W) // HD
    bd = jnp.asarray((seg[:, None] == seg[None, :]).astype(np.float32) / HD, BF16)
    fg = final_g.reshape(1, D)

    out = None
    for l in range(DEPTH):
        last = l == DEPTH - 1
        mods = mods_all[l].reshape(16, 1, 6 * D)
        za, zc, uv = _in_proj(xc, mods, norm1_g[l].reshape(1, D), w_in[l].astype(BF16), cos_t, sin_t)
        ya = _attn_a(za, attn_sink[l], with_ctx=not last)
        yc = _na(zc, _na_bias_table(na_rpb[l]), with_ctx=not last)
        n_tok = N_LAT if last else N_ALL
        sb_tab = jnp.repeat(sgu_b[l].T, HD, axis=1)
        x_new, hx, logits_t = _merge(
            xc, ya, yc, uv, mods, group_norm_g[l].reshape(1, D), sgu_norm_g[l].reshape(1, B_W),
            sgu_w[l].astype(BF16), sb_tab, bd, w_out[l].astype(BF16), norm2_g[l].reshape(1, D),
            _router_split(router_w[l]), n_tok)
        res = _moe(x_new, hx, logits_t, mods, router_bias[l].reshape(N_EXP, 1),
                   exp_w_gate, exp_w_up, exp_w_down,
                   shared_w_gate[l].astype(BF16), shared_w_up[l].astype(BF16),
                   shared_w_down[l].astype(BF16), fg, n_tok, layer=l, final_norm=last)
        if last:
            out = res.reshape(B, S, D)
        else:
            xc = res
    return out
```

```python
import functools

import jax
import jax.numpy as jnp
import numpy as np
from jax import lax
from jax.experimental import pallas as pl
from jax.experimental.pallas import tpu as pltpu

F32 = jnp.float32
BF16 = jnp.bfloat16
I32 = jnp.int32

D = 1024
B = 8
S = 2048
C = 256
DEPTH = 2
GRID_W = 64
HD = 64
A_HEADS = 6
A_KV = 2
A_WIN = 128
A_BLK = 256
ROPE_BASE = 10000.0
SGU_GROUPS = 4
SGU_CHUNK = 128
C_HEADS = 6
C_WIN_R = 8
C_WIN_C = 16
A_W = A_HEADS * HD
B_W = SGU_GROUPS * HD
C_W = C_HEADS * HD
KV_W = A_KV * HD
IN_W = A_W + 2 * KV_W + 2 * B_W + 3 * C_W
N_EXP = 64
TOP_K = 8
N_GRP = 8
TOPK_GRP = 4
EXP_FF = 256
ROUTED_SCALE = 2.5
MOE_BLK = 1024
EPS = 1e-6
NEG = -1e30
SCALE = HD ** -0.5

N_LAT = B * S
N_CTX = B * C
N_ALL = N_LAT + N_CTX
ROWS = S // GRID_W

HIGHEST = lax.Precision.HIGHEST
ARB = pltpu.ARBITRARY

NT_DIMS = (((1,), (1,)), ((), ()))


def _cparams(n_axes, vmem_mb=48):
    return pltpu.CompilerParams(
        dimension_semantics=(ARB,) * n_axes, vmem_limit_bytes=vmem_mb * 1024 * 1024)


def _mod_row(i, tm):
    return jnp.where(i < N_LAT // tm, i // (S // tm), B)


def _rms(x):
    return x * lax.rsqrt(jnp.mean(x * x, axis=-1, keepdims=True) + EPS)


def _split_bf16(x):
    hi = x.astype(BF16)
    return hi, (x - hi.astype(F32)).astype(BF16)


def _router_split(rw):
    hi, lo = _split_bf16(jnp.pad(rw, ((0, 0), (0, 128 - N_EXP))))
    return jnp.stack([hi, lo])


MOD_TN = 1024


def _mod_body(cc_ref, w_ref, b_ref, o_ref):
    a = cc_ref[...]
    a = a * jax.nn.sigmoid(a)
    o_ref[...] = jnp.dot(a.astype(BF16), w_ref[...].astype(BF16),
                         preferred_element_type=F32) + b_ref[...]


def _modulation(cc, mod_w, mod_b):
    n_col = 6 * D // MOD_TN
    return pl.pallas_call(
        _mod_body,
        grid=(DEPTH, n_col),
        in_specs=[
            pl.BlockSpec((16, D), lambda l, j: (0, 0)),
            pl.BlockSpec((None, D, MOD_TN), lambda l, j: (l, 0, j)),
            pl.BlockSpec((None, 1, MOD_TN), lambda l, j: (l, 0, j)),
        ],
        out_specs=pl.BlockSpec((None, 16, MOD_TN), lambda l, j: (l, 0, j)),
        out_shape=jax.ShapeDtypeStruct((DEPTH, 16, 6 * D), F32),
        compiler_params=_cparams(2),
        name="modulation",
    )(cc, mod_w, mod_b.reshape(DEPTH, 1, 6 * D))


IN_TM = 512
ROT_W = A_W + KV_W


def _in_body(x_ref, m_ref, g_ref, w_ref, cos_ref, sin_ref, za_ref, zc_ref, uv_ref):
    x = x_ref[...]
    h = _rms(x) * g_ref[...]
    h = h * (1.0 + m_ref[:, D:2 * D]) + m_ref[:, 0:D]
    z = jnp.dot(h.astype(BF16), w_ref[...], preferred_element_type=F32)
    qk = z[:, :ROT_W]
    lane = lax.broadcasted_iota(I32, qk.shape, 1)
    rot = jnp.where((lane & 16) == 0,
                    -pltpu.roll(qk, ROT_W - 16, 1), pltpu.roll(qk, 16, 1))
    qk = qk * cos_ref[...] + rot * sin_ref[...]
    za_ref[...] = jnp.concatenate(
        [qk[:, :A_W] * SCALE, qk[:, A_W:], z[:, ROT_W:ROT_W + KV_W]], axis=1).astype(BF16)
    u0 = ROT_W + KV_W
    uv_ref[...] = jax.nn.gelu(z[:, u0:u0 + 2 * B_W])
    c0 = u0 + 2 * B_W
    zc_ref[...] = jnp.concatenate(
        [z[:, c0:c0 + C_W] * SCALE, z[:, c0 + C_W:]], axis=1).astype(BF16)


def _in_proj(xc, mods, g, w_bf, cos_t, sin_t):
    tm = IN_TM
    n_t = N_ALL // tm

    def tab_idx(i):
        return (jnp.where(i < N_LAT // tm, i % (S // tm), S // tm), 0)

    return pl.pallas_call(
        _in_body,
        grid=(n_t,),
        in_specs=[
            pl.BlockSpec((tm, D), lambda i: (i, 0)),
            pl.BlockSpec((None, 1, 6 * D), lambda i: (_mod_row(i, tm), 0, 0)),
            pl.BlockSpec((1, D), lambda i: (0, 0)),
            pl.BlockSpec((D, IN_W), lambda i: (0, 0)),
            pl.BlockSpec((tm, ROT_W), tab_idx),
            pl.BlockSpec((tm, ROT_W), tab_idx),
        ],
        out_specs=[
            pl.BlockSpec((tm, A_W + 2 * KV_W), lambda i: (i, 0)),
            pl.BlockSpec((tm, 3 * C_W), lambda i: (i, 0)),
            pl.BlockSpec((tm, 2 * B_W), lambda i: (i, 0)),
        ],
        out_shape=[
            jax.ShapeDtypeStruct((N_ALL, A_W + 2 * KV_W), BF16),
            jax.ShapeDtypeStruct((N_ALL, 3 * C_W), BF16),
            jax.ShapeDtypeStruct((N_ALL, 2 * B_W), F32),
        ],
        compiler_params=_cparams(1),
        name="in_proj",
    )(xc, mods, g, w_bf, cos_t, sin_t)


A_BAND = A_BLK + 2 * A_WIN
N_QB_LAT = N_LAT // A_BLK
QB_PER_SEQ = S // A_BLK
QB_PER_CTX = C // A_BLK


def _softmax_pv(s_list, v_list, extra_logit=None):
    m = s_list[0].max(axis=-1, keepdims=True)
    for s in s_list[1:]:
        m = jnp.maximum(m, s.max(axis=-1, keepdims=True))
    if extra_logit is not None:
        m = jnp.maximum(m, extra_logit)
    den = None
    out = None
    for s, v in zip(s_list, v_list):
        p = jnp.exp(s - m)
        d = p.sum(axis=-1, keepdims=True)
        o = jnp.dot(p.astype(BF16), v, preferred_element_type=F32)
        den = d if den is None else den + d
        out = o if out is None else out + o
    if extra_logit is not None:
        den = den + jnp.exp(extra_logit - m)
    return out / den


def _attn_a_body(sink_ref, q_ref, kb_ref, vb_ref, kc_ref, vc_ref, o_ref):
    i = pl.program_id(0)
    is_lat = i < N_QB_LAT
    n = i % QB_PER_SEQ
    start = pl.multiple_of(jnp.clip(n * A_BLK - A_WIN, 0, S - A_BAND), A_WIN)
    qpos = n * A_BLK + lax.broadcasted_iota(I32, (A_BLK, A_BAND), 0)
    kpos = start + lax.broadcasted_iota(I32, (A_BLK, A_BAND), 1)
    mask = jnp.abs(kpos - qpos) <= jnp.where(is_lat, A_WIN, -1)
    kb = kb_ref[pl.ds(start, A_BAND), :]
    vb = vb_ref[pl.ds(start, A_BAND), :]
    kc = kc_ref[...]
    vc = vc_ref[...]
    outs = []
    for h in range(A_HEADS):
        kv = h // (A_HEADS // A_KV)
        sl = slice(kv * HD, (kv + 1) * HD)
        q = q_ref[:, h * HD:(h + 1) * HD]
        s_b = lax.dot_general(q, kb[:, sl], NT_DIMS, preferred_element_type=F32)
        s_b = jnp.where(mask, s_b, NEG)
        s_c = lax.dot_general(q, kc[:, sl], NT_DIMS, preferred_element_type=F32)
        outs.append(_softmax_pv([s_c, s_b], [vc[:, sl], vb[:, sl]], sink_ref[h]))
    o_ref[...] = jnp.concatenate(outs, axis=1)


def _attn_a(za, sink, with_ctx):
    n_qb = N_QB_LAT + (N_CTX // A_BLK if with_ctx else 0)

    def bidx(i):
        return jnp.where(i < N_QB_LAT, i // QB_PER_SEQ, (i - N_QB_LAT) // QB_PER_CTX)

    k_col = A_W // KV_W
    v_col = k_col + 1
    return pl.pallas_call(
        _attn_a_body,
        grid=(n_qb,),
        in_specs=[
            pl.BlockSpec(memory_space=pltpu.SMEM),
            pl.BlockSpec((A_BLK, A_W), lambda i: (i, 0)),
            pl.BlockSpec((S, KV_W), lambda i: (bidx(i), k_col)),
            pl.BlockSpec((S, KV_W), lambda i: (bidx(i), v_col)),
            pl.BlockSpec((C, KV_W), lambda i: (N_LAT // C + bidx(i), k_col)),
            pl.BlockSpec((C, KV_W), lambda i: (N_LAT // C + bidx(i), v_col)),
        ],
        out_specs=pl.BlockSpec((A_BLK, A_W), lambda i: (i, 0)),
        out_shape=jax.ShapeDtypeStruct((n_qb * A_BLK, A_W), F32),
        compiler_params=_cparams(1),
        name="attn_window",
    )(sink, za, za, za, za, za)


NA_R = 4
NA_TQ = NA_R * GRID_W
NA_KROWS = 12
NA_WIN = NA_KROWS * GRID_W
NA_STEPS = ROWS // NA_R
NA_PAIRS = NA_KROWS // 2
N_DR = 2 * C_WIN_R - 1


def _na_body(q_ref, k_ref, v_ref, kc_ref, vc_ref, tab_ref, o_ref):
    j = pl.program_id(1)
    is_lat = j < NA_STEPS
    r0 = jnp.minimum(j, NA_STEPS - 1) * NA_R
    u0 = jnp.clip(r0 - C_WIN_R // 2, 0, ROWS - NA_KROWS)
    k0 = pl.multiple_of(u0 * GRID_W, GRID_W)
    kw = k_ref[pl.ds(k0, NA_WIN), :]
    vw = v_ref[pl.ds(k0, NA_WIN), :]
    kc = kc_ref[...]
    vc = vc_ref[...]
    q = q_ref[...]
    left = lax.broadcasted_iota(I32, (1, 2 * GRID_W), 1) < GRID_W

    tab_idx = []
    penalty = []
    for rr in range(NA_R):
        r = r0 + rr
        start = jnp.clip(r - C_WIN_R // 2, 0, ROWS - C_WIN_R)
        idx_row = []
        pen_row = []
        for p in range(NA_PAIRS):
            kr = u0 + 2 * p
            idx_row.append(jnp.clip(kr - r + (C_WIN_R - 1), -1, N_DR - 1) + 1)
            pens = []
            for half in range(2):
                ok = jnp.logical_and(is_lat, jnp.logical_and(kr + half >= start,
                                                             kr + half < start + C_WIN_R))
                pens.append(jnp.where(ok, 0.0, NEG))
            pen_row.append(jnp.where(left, pens[0], pens[1]))
        tab_idx.append(idx_row)
        penalty.append(jnp.concatenate(pen_row, axis=1))

    outs = []
    for h in range(C_HEADS):
        sl = slice(h * HD, (h + 1) * HD)
        qh = q[:, sl]
        s_w = lax.dot_general(qh, kw[:, sl], NT_DIMS, preferred_element_type=F32)
        bias = jnp.concatenate([
            jnp.concatenate([tab_ref[h, pl.ds(tab_idx[rr][p], 1)][0] for p in range(NA_PAIRS)],
                            axis=1) + penalty[rr]
            for rr in range(NA_R)], axis=0)
        s_c = lax.dot_general(qh, kc[:, sl], NT_DIMS, preferred_element_type=F32)
        outs.append(_softmax_pv([s_c, s_w + bias], [vc[:, sl], vw[:, sl]]))
    o_ref[...] = jnp.concatenate(outs, axis=1)


def _na(zc, tab, with_ctx):
    n_j = NA_STEPS + (1 if with_ctx else 0)

    def qidx(b, j):
        return jnp.where(j < NA_STEPS, b * NA_STEPS + j, N_LAT // NA_TQ + b)

    n_out = N_LAT + (N_CTX if with_ctx else 0)
    return pl.pallas_call(
        _na_body,
        grid=(B, n_j),
        in_specs=[
            pl.BlockSpec((NA_TQ, C_W), lambda b, j: (qidx(b, j), 0)),
            pl.BlockSpec((S, C_W), lambda b, j: (b, 1)),
            pl.BlockSpec((S, C_W), lambda b, j: (b, 2)),
            pl.BlockSpec((C, C_W), lambda b, j: (N_LAT // C + b, 1)),
            pl.BlockSpec((C, C_W), lambda b, j: (N_LAT // C + b, 2)),
            pl.BlockSpec((C_HEADS, N_DR + 1, GRID_W, 2 * GRID_W), lambda b, j: (0, 0, 0, 0)),
        ],
        out_specs=pl.BlockSpec((NA_TQ, C_W), lambda b, j: (qidx(b, j), 0)),
        out_shape=jax.ShapeDtypeStruct((n_out, C_W), F32),
        compiler_params=_cparams(2),
        name="attn_neighbourhood",
    )(zc, zc, zc, zc, zc, tab)


def _na_bias_table(rpb):
    cq = np.arange(GRID_W)
    col_start = np.clip(cq - C_WIN_C // 2, 0, GRID_W - C_WIN_C)
    col_ok = (cq[None, :] >= col_start[:, None]) & (cq[None, :] < col_start[:, None] + C_WIN_C)
    dc = np.clip(cq[None, :] - cq[:, None], -(C_WIN_C - 1), C_WIN_C - 1) + (C_WIN_C - 1)
    n_dc = 2 * C_WIN_C - 1
    pick = (dc.reshape(-1)[None, :] == np.arange(n_dc)[:, None]).astype(np.float32)
    t = jnp.dot(rpb.astype(F32).reshape(-1, n_dc), jnp.asarray(pick), precision=HIGHEST)
    t = t.reshape(C_HEADS, N_DR, GRID_W, GRID_W)
    t = jnp.where(col_ok[None, None], t, NEG)
    zero = jnp.zeros((C_HEADS, 1, GRID_W, GRID_W), F32)
    ext = jnp.concatenate([zero, t, zero], axis=1)
    return jnp.concatenate([ext[:, :-1], ext[:, 1:]], axis=-1)


MG_TM = 256


def _merge_body(x_ref, ya_ref, yc_ref, uv_ref, m_ref, gg_ref, sg_ref, sw_ref, sb_ref, bd_ref,
                wo_ref, g2_ref, rw_ref, xo_ref, hx_ref, lg_ref):
    u = uv_ref[:, :B_W]
    v = uv_ref[:, B_W:]
    vv_hi, vv_lo = _split_bf16(v * v)
    ms = (jnp.dot(vv_hi, bd_ref[...], preferred_element_type=F32)
          + jnp.dot(vv_lo, bd_ref[...], preferred_element_type=F32))
    vn = (v * lax.rsqrt(ms + EPS) * sg_ref[...]).astype(BF16)
    lane_grp = lax.broadcasted_iota(I32, (SGU_CHUNK, B_W), 1) // HD
    gates = []
    for c in range(MG_TM // SGU_CHUNK):
        vc = vn[c * SGU_CHUNK:(c + 1) * SGU_CHUNK, :]
        gate = sb_ref[...]
        acc = jnp.zeros((SGU_CHUNK, B_W), F32)
        for g in range(SGU_GROUPS):
            r = jnp.dot(sw_ref[g], vc, preferred_element_type=F32)
            acc = jnp.where(lane_grp == g, r, acc)
        gates.append(acc + gate)
    yb = u * jnp.concatenate(gates, axis=0)
    gg = gg_ref[...]
    ycat = jnp.concatenate([
        _rms(ya_ref[...]) * gg[:, :A_W],
        _rms(yb) * gg[:, A_W:A_W + B_W],
        _rms(yc_ref[...]) * gg[:, A_W + B_W:],
    ], axis=1)
    proj = jnp.dot(ycat.astype(BF16), wo_ref[...], preferred_element_type=F32)
    xn = x_ref[...] + m_ref[:, 2 * D:3 * D] * proj
    xo_ref[...] = xn
    hx = _rms(xn) * g2_ref[...]
    hx = hx * (1.0 + m_ref[:, 4 * D:5 * D]) + m_ref[:, 3 * D:4 * D]
    hx_ref[...] = hx
    hx_hi, hx_lo = _split_bf16(hx)
    logits = (jnp.dot(hx_hi, rw_ref[0], preferred_element_type=F32)
              + jnp.dot(hx_hi, rw_ref[1], preferred_element_type=F32)
              + jnp.dot(hx_lo, rw_ref[0], preferred_element_type=F32))
    lg_ref[...] = logits.T[:N_EXP, :]


def _merge(xc, ya, yc, uv, mods, gg, sg, sw_bf, sb_tab, bd, wo_bf, g2, rw_t, n_rows):
    tm = MG_TM
    n_t = n_rows // tm
    const2 = lambda i: (0, 0)
    row = lambda i: (i, 0)
    return pl.pallas_call(
        _merge_body,
        grid=(n_t,),
        in_specs=[
            pl.BlockSpec((tm, D), row),
            pl.BlockSpec((tm, A_W), row),
            pl.BlockSpec((tm, C_W), row),
            pl.BlockSpec((tm, 2 * B_W), row),
            pl.BlockSpec((None, 1, 6 * D), lambda i: (_mod_row(i, tm), 0, 0)),
            pl.BlockSpec((1, D), const2),
            pl.BlockSpec((1, B_W), const2),
            pl.BlockSpec((SGU_GROUPS, SGU_CHUNK, SGU_CHUNK), lambda i: (0, 0, 0)),
            pl.BlockSpec((SGU_CHUNK, B_W), const2),
            pl.BlockSpec((B_W, B_W), const2),
            pl.BlockSpec((D, D), const2),
            pl.BlockSpec((1, D), const2),
            pl.BlockSpec((2, D, 128), lambda i: (0, 0, 0)),
        ],
        out_specs=[
            pl.BlockSpec((tm, D), row),
            pl.BlockSpec((tm, D), row),
            pl.BlockSpec((N_EXP, tm), lambda i: (0, i)),
        ],
        out_shape=[
            jax.ShapeDtypeStruct((n_rows, D), F32),
            jax.ShapeDtypeStruct((n_rows, D), F32),
            jax.ShapeDtypeStruct((N_EXP, n_rows), F32),
        ],
        compiler_params=_cparams(1),
        name="merge",
    )(xc, ya, yc, uv, mods, gg, sg, sw_bf, sb_tab, bd, wo_bf, g2, rw_t)


RT_TM = 256
GRP_SZ = N_EXP // N_GRP
SLOT_ALIGN = 16


def _first_argmax(v, iota):
    m = v.max(axis=0, keepdims=True)
    idx = jnp.where(v == m, iota, float(v.shape[0])).min(axis=0, keepdims=True)
    return m, idx


def _stack_rows(rows, iota):
    out = jnp.zeros(iota.shape, F32)
    for r, v in enumerate(rows):
        out = jnp.where(iota == float(r), v, out)
    return out


def _route_body(lg_ref, rb_ref, w_ref, lp_ref, tc_ref):
    tm = RT_TM
    scores = jax.nn.sigmoid(lg_ref[...])
    sel = scores + rb_ref[...]
    iota_g = lax.broadcasted_iota(I32, (GRP_SZ, tm), 0).astype(F32)
    gs = []
    for g in range(N_GRP):
        v = sel[g * GRP_SZ:(g + 1) * GRP_SZ, :]
        m1, i1 = _first_argmax(v, iota_g)
        m2 = jnp.where(iota_g == i1, -jnp.inf, v).max(axis=0, keepdims=True)
        gs.append(m1 + m2)
    iota_n = lax.broadcasted_iota(I32, (N_GRP, tm), 0).astype(F32)
    gscore = _stack_rows(gs, iota_n)
    gsel = jnp.zeros((N_GRP, tm), F32)
    for _ in range(TOPK_GRP):
        _, gi = _first_argmax(gscore, iota_n)
        hit = iota_n == gi
        gsel = jnp.where(hit, 1.0, gsel)
        gscore = jnp.where(hit, -jnp.inf, gscore)
    emask = jnp.concatenate(
        [jnp.broadcast_to(gsel[g:g + 1, :], (GRP_SZ, tm)) for g in range(N_GRP)], axis=0)
    cand = jnp.where(emask > 0.5, sel, NEG)
    iota_e = lax.broadcasted_iota(I32, (N_EXP, tm), 0).astype(F32)
    hits = []
    ws = []
    member = jnp.zeros((N_EXP, tm), F32)
    for _ in range(TOP_K):
        _, ei = _first_argmax(cand, iota_e)
        hit = iota_e == ei
        hits.append(hit)
        ws.append(jnp.where(hit, scores, 0.0).sum(axis=0, keepdims=True))
        member = jnp.where(hit, 1.0, member)
        cand = jnp.where(hit, -jnp.inf, cand)
    wsum = ws[0]
    for w in ws[1:]:
        wsum = wsum + w
    iota_k = lax.broadcasted_iota(I32, (TOP_K, tm), 0).astype(F32)
    w_ref[...] = _stack_rows(ws, iota_k) / wsum * ROUTED_SCALE
    r_i = lax.broadcasted_iota(I32, (tm, tm), 0)
    c_i = lax.broadcasted_iota(I32, (tm, tm), 1)
    tri = jnp.where(r_i < c_i, 1.0, 0.0).astype(BF16)
    local = jnp.dot(member.astype(BF16), tri, preferred_element_type=F32)
    tile_cnt = member.sum(axis=1, keepdims=True)
    tc_ref[...] = tile_cnt
    aligned = jnp.ceil(tile_cnt / SLOT_ALIGN) * SLOT_ALIGN
    e_r = lax.broadcasted_iota(I32, (N_EXP, N_EXP), 0)
    e_c = lax.broadcasted_iota(I32, (N_EXP, N_EXP), 1)
    below = jnp.where(e_c < e_r, 1.0, 0.0)
    start = jnp.dot(below, jnp.broadcast_to(aligned, (N_EXP, 128)), precision=HIGHEST,
                    preferred_element_type=F32)[:, 0:1]
    pos = local + start
    lp_ref[...] = _stack_rows(
        [jnp.where(hit, pos, 0.0).sum(axis=0, keepdims=True) for hit in hits],
        iota_k).astype(I32)


def _route(logits_t, rbias, n_tok):
    tm = RT_TM
    tok = lambda i: (0, i)
    return pl.pallas_call(
        _route_body,
        grid=(n_tok // tm,),
        in_specs=[
            pl.BlockSpec((N_EXP, tm), tok),
            pl.BlockSpec((N_EXP, 1), lambda i: (0, 0)),
        ],
        out_specs=[
            pl.BlockSpec((TOP_K, tm), tok),
            pl.BlockSpec((TOP_K, tm), tok),
            pl.BlockSpec((None, N_EXP, 1), lambda i: (i, 0, 0)),
        ],
        out_shape=[
            jax.ShapeDtypeStruct((TOP_K, n_tok), F32),
            jax.ShapeDtypeStruct((TOP_K, n_tok), I32),
            jax.ShapeDtypeStruct((n_tok // tm, N_EXP, 1), F32),
        ],
        compiler_params=_cparams(1),
        name="route",
    )(logits_t, rbias)


DP_TM = RT_TM
SORT_ROWS = 3072
N_GRPS = SORT_ROWS // SLOT_ALIGN
ISSUE_UNROLL = 4
WAIT_GROUPS = 32
N_BURSTS = 6
BURST_ROWS = SORT_ROWS // N_BURSTS
BURST_GRPS = BURST_ROWS // SLOT_ALIGN


def _start_groups(g_lo, g_hi, row_of, make_copy):
    n_full = (g_hi - g_lo) // ISSUE_UNROLL

    def body(q, carry):
        for s in range(ISSUE_UNROLL):
            g = g_lo + q * ISSUE_UNROLL + s
            make_copy(g, row_of(g)).start(priority=s % 2)
        return carry

    def tail(g, carry):
        make_copy(g, row_of(g)).start()
        return carry

    lax.fori_loop(0, n_full, body, 0)
    lax.fori_loop(g_lo + n_full * ISSUE_UNROLL, g_hi, tail, 0)


def _start_burst(q, ngrp, row_of, make_copy):
    _start_groups(jnp.minimum(q * BURST_GRPS, ngrp), jnp.minimum((q + 1) * BURST_GRPS, ngrp),
                  row_of, make_copy)


def _wait_groups(ngrp, make_copy, make_bulk):
    def bulk(q, carry):
        make_bulk().wait()
        return carry

    def single(g, carry):
        make_copy(0, 0).wait()
        return carry

    n_bulk = ngrp // WAIT_GROUPS
    lax.fori_loop(0, n_bulk, bulk, 0)
    lax.fori_loop(n_bulk * WAIT_GROUPS, ngrp, single, 0)


def _dispatch_body(ngrp_ref, fill_ref, grow_ref, hx_ref, lp_ref, xs_ref, zbuf_ref, zero_ref, sems):
    i = pl.program_id(0)
    n_tiles = pl.num_programs(0)
    slot = i % 2

    def zero_copy(b):
        dst = xs_ref.at[pl.ds(pl.multiple_of(b * MOE_BLK, MOE_BLK), MOE_BLK)]
        return pltpu.make_async_copy(zero_ref, dst, sems.at[0])

    @pl.when(i == 0)
    def _():
        zero_ref[...] = jnp.zeros_like(zero_ref)
        n_fill = fill_ref[0]

        def z_start(q, carry):
            @pl.when(fill_ref[1 + q] >= 0)
            def _():
                zero_copy(fill_ref[1 + q]).start()
            return carry

        def z_wait(q, carry):
            @pl.when(fill_ref[1 + q] >= 0)
            def _():
                zero_copy(0).wait()
            return carry

        lax.fori_loop(0, n_fill, z_start, 0)
        lax.fori_loop(0, n_fill, z_wait, 0)

    def group_copy(s, g, row):
        src = zbuf_ref.at[s, pl.ds(pl.multiple_of(g * SLOT_ALIGN, SLOT_ALIGN), SLOT_ALIGN)]
        dst = xs_ref.at[pl.ds(pl.multiple_of(row, SLOT_ALIGN), SLOT_ALIGN)]
        return pltpu.make_async_copy(src, dst, sems.at[s])

    def bulk_copy(s):
        n_rows = WAIT_GROUPS * SLOT_ALIGN
        return pltpu.make_async_copy(
            zbuf_ref.at[s, pl.ds(0, n_rows)], xs_ref.at[pl.ds(0, n_rows)], sems.at[s])

    x = hx_ref[...].astype(BF16)
    lpos = lp_ref[...]
    n_cur = ngrp_ref[i]
    pos = [jnp.tile(jnp.broadcast_to(lpos[k:k + 1, :], (16, DP_TM)).astype(jnp.int16),
                    (BURST_ROWS // 16, 1)) for k in range(TOP_K)]
    for q in range(N_BURSTS):
        @pl.when(q * BURST_GRPS < n_cur)
        def _():
            row_iota = (lax.broadcasted_iota(I32, (BURST_ROWS, DP_TM), 0)
                        + q * BURST_ROWS).astype(jnp.int16)
            onehot = jnp.zeros((BURST_ROWS, DP_TM), BF16)
            for k in range(TOP_K):
                onehot = jnp.where(row_iota == pos[k], jnp.ones((), BF16), onehot)
            z = jnp.dot(onehot, x, preferred_element_type=F32)
            zbuf_ref[slot, pl.ds(q * BURST_ROWS, BURST_ROWS), :] = z.astype(BF16)
            _start_burst(q, n_cur, lambda g: grow_ref[0, g],
                         lambda g, r: group_copy(slot, g, r))

    @pl.when(i > 0)
    def _():
        _wait_groups(ngrp_ref[jnp.maximum(i - 1, 0)], lambda g, r: group_copy(1 - slot, 0, 0),
                     lambda: bulk_copy(1 - slot))

    @pl.when(i == n_tiles - 1)
    def _():
        _wait_groups(n_cur, lambda g, r: group_copy(slot, 0, 0), lambda: bulk_copy(slot))


def _dispatch(ngrp, fill, grp_row, hx, lpos, n_tok, n_slots):
    return pl.pallas_call(
        _dispatch_body,
        grid_spec=pltpu.PrefetchScalarGridSpec(
            num_scalar_prefetch=2,
            grid=(n_tok // DP_TM,),
            in_specs=[
                pl.BlockSpec((None, 1, N_GRPS), lambda i, *_: (i, 0, 0), memory_space=pltpu.SMEM),
                pl.BlockSpec((DP_TM, D), lambda i, *_: (i, 0)),
                pl.BlockSpec((TOP_K, DP_TM), lambda i, *_: (0, i)),
            ],
            out_specs=pl.BlockSpec(memory_space=pl.ANY),
            scratch_shapes=[
                pltpu.VMEM((2, SORT_ROWS, D), BF16),
                pltpu.VMEM((MOE_BLK, D), BF16),
                pltpu.SemaphoreType.DMA((2,)),
            ],
        ),
        out_shape=jax.ShapeDtypeStruct((n_slots, D), BF16),
        compiler_params=_cparams(1, vmem_mb=56),
        name="moe_dispatch",
    )(ngrp, fill, grp_row, hx, lpos)


def _experts_body(bexp_ref, xblk_ref, nused_ref, xs_ref, wg_ref, wu_ref, wd_ref, ys_ref):
    j = pl.program_id(0)

    @pl.when(j < nused_ref[0])
    def _():
        x = xs_ref[...]
        g = jnp.dot(x, wg_ref[...].astype(BF16), preferred_element_type=F32)
        u = jnp.dot(x, wu_ref[...].astype(BF16), preferred_element_type=F32)
        h = (g * jax.nn.sigmoid(g) * u).astype(BF16)
        y = jnp.dot(h, wd_ref[...].astype(BF16), preferred_element_type=F32)
        ys_ref[...] = y.astype(BF16)

    @pl.when(j >= nused_ref[0])
    def _():
        ys_ref[...] = jnp.zeros_like(ys_ref)


def _experts(bexp, xblk, nused, xs, wg, wu, wd, n_blk, layer):
    w_idx = lambda j, be, xb, nu: (layer, be[j], 0, 0)
    return pl.pallas_call(
        _experts_body,
        grid_spec=pltpu.PrefetchScalarGridSpec(
            num_scalar_prefetch=3,
            grid=(n_blk,),
            in_specs=[
                pl.BlockSpec((MOE_BLK, D), lambda j, be, xb, nu: (xb[j], 0)),
                pl.BlockSpec((None, None, D, EXP_FF), w_idx),
                pl.BlockSpec((None, None, D, EXP_FF), w_idx),
                pl.BlockSpec((None, None, EXP_FF, D), w_idx),
            ],
            out_specs=pl.BlockSpec((MOE_BLK, D), lambda j, be, xb, nu: (j, 0)),
        ),
        out_shape=jax.ShapeDtypeStruct((n_blk * MOE_BLK, D), BF16),
        compiler_params=_cparams(1),
        name="moe_experts",
    )(bexp, xblk, nused, xs, wg, wu, wd)


def _combine_body(ngrp_ref, grow_ref, grow_next_ref, ys_ref, lp_ref, w_ref, x_ref, hx_ref, m_ref,
                  sg_ref, su_ref, sd_ref, fg_ref, o_ref, ybuf_ref, acc_ref, sems, *, final_norm):
    i = pl.program_id(0)
    n_tiles = pl.num_programs(0)
    slot = i % 2

    def group_copy(s, g, row):
        src = ys_ref.at[pl.ds(pl.multiple_of(row, SLOT_ALIGN), SLOT_ALIGN)]
        dst = ybuf_ref.at[s, pl.ds(pl.multiple_of(g * SLOT_ALIGN, SLOT_ALIGN), SLOT_ALIGN)]
        return pltpu.make_async_copy(src, dst, sems.at[s])

    def bulk_copy(s):
        n_rows = WAIT_GROUPS * SLOT_ALIGN
        return pltpu.make_async_copy(
            ys_ref.at[pl.ds(0, n_rows)], ybuf_ref.at[s, pl.ds(0, n_rows)], sems.at[s])

    @pl.when(i == 0)
    def _():
        ybuf_ref[...] = jnp.zeros_like(ybuf_ref)
        _start_groups(0, ngrp_ref[0], lambda g: grow_ref[0, g], lambda g, r: group_copy(0, g, r))

    n_next = jnp.where(i + 1 < n_tiles, ngrp_ref[jnp.minimum(i + 1, n_tiles - 1)], 0)

    def prefetch(q):
        _start_burst(q, n_next, lambda g: grow_next_ref[0, g],
                     lambda g, r: group_copy(1 - slot, g, r))

    hx = hx_ref[...].astype(BF16)
    g = jnp.dot(hx, sg_ref[...], preferred_element_type=F32)
    u = jnp.dot(hx, su_ref[...], preferred_element_type=F32)
    h = (g * jax.nn.sigmoid(g) * u).astype(BF16)
    acc_ref[...] = jnp.dot(h, sd_ref[...], preferred_element_type=F32)

    lpos = lp_ref[...]
    w = w_ref[...]
    n_own = ngrp_ref[i]
    _wait_groups(n_own, lambda g, r: group_copy(slot, 0, 0), lambda: bulk_copy(slot))
    lane_tiles = BURST_ROWS // 128
    pos = [jnp.tile(jnp.broadcast_to(lpos[:, k:k + 1], (DP_TM, 128)).astype(jnp.int16),
                    (1, lane_tiles)) for k in range(TOP_K)]
    wts = [jnp.tile(jnp.broadcast_to(w[:, k:k + 1], (DP_TM, 128)).astype(BF16),
                    (1, lane_tiles)) for k in range(TOP_K)]
    for q in range(N_BURSTS):
        @pl.when(q * BURST_GRPS < n_own)
        def _():
            col_iota = (lax.broadcasted_iota(I32, (DP_TM, BURST_ROWS), 1)
                        + q * BURST_ROWS).astype(jnp.int16)
            unsort = jnp.zeros((DP_TM, BURST_ROWS), BF16)
            for k in range(TOP_K):
                unsort = jnp.where(col_iota == pos[k], wts[k], unsort)
            acc_ref[...] += jnp.dot(unsort, ybuf_ref[slot, pl.ds(q * BURST_ROWS, BURST_ROWS), :],
                                    preferred_element_type=F32)
        prefetch(q)
    out = x_ref[...] + m_ref[:, 5 * D:6 * D] * acc_ref[...]
    if final_norm:
        out = _rms(out) * fg_ref[...]
    o_ref[...] = out


def _combine(ngrp, grp_row, ys, lpos_t, w_t, x, hx, mods, sg_bf, su_bf, sd_bf, fg, n_tok, final_norm):
    tm = DP_TM
    row = lambda i, *_: (i, 0)
    const2 = lambda i, *_: (0, 0)
    return pl.pallas_call(
        functools.partial(_combine_body, final_norm=final_norm),
        grid_spec=pltpu.PrefetchScalarGridSpec(
            num_scalar_prefetch=1,
            grid=(n_tok // tm,),
            in_specs=[
                pl.BlockSpec((None, 1, N_GRPS), lambda i, *_: (i, 0, 0), memory_space=pltpu.SMEM),
                pl.BlockSpec((None, 1, N_GRPS), lambda i, *_: (jnp.minimum(i + 1, n_tok // tm - 1), 0, 0),
                             memory_space=pltpu.SMEM),
                pl.BlockSpec(memory_space=pl.ANY),
                pl.BlockSpec((tm, TOP_K), row),
                pl.BlockSpec((tm, TOP_K), row),
                pl.BlockSpec((tm, D), row),
                pl.BlockSpec((tm, D), row),
                pl.BlockSpec((None, 1, 6 * D), lambda i, *_: (_mod_row(i, tm), 0, 0)),
                pl.BlockSpec((D, EXP_FF), const2),
                pl.BlockSpec((D, EXP_FF), const2),
                pl.BlockSpec((EXP_FF, D), const2),
                pl.BlockSpec((1, D), const2),
            ],
            out_specs=pl.BlockSpec((tm, D), row),
            scratch_shapes=[pltpu.VMEM((2, SORT_ROWS, D), BF16), pltpu.VMEM((tm, D), F32),
                            pltpu.SemaphoreType.DMA((2,))],
        ),
        out_shape=jax.ShapeDtypeStruct((n_tok, D), F32),
        compiler_params=_cparams(1, vmem_mb=56),
        name="moe_combine",
    )(ngrp, grp_row, grp_row, ys, lpos_t, w_t, x, hx, mods, sg_bf, su_bf, sd_bf, fg)


def _moe(x_new, hx, logits_t, mods, rbias, wg, wu, wd, sg_bf, su_bf, sd_bf, fg, n_tok, layer,
         final_norm):
    w, lpos, tile_cnt = _route(logits_t, rbias, n_tok)
    n_tiles = n_tok // DP_TM
    n_blk = (n_tok * TOP_K + (SLOT_ALIGN - 1) * N_EXP * n_tiles) // MOE_BLK + N_EXP
    tcnt = tile_cnt[:, :, 0].astype(I32)
    cnt_al = (tcnt + SLOT_ALIGN - 1) // SLOT_ALIGN * SLOT_ALIGN
    loc_end = jnp.cumsum(cnt_al, axis=1)
    loc = loc_end - cnt_al
    slots_e = jnp.sum(cnt_al, axis=0)
    nblk_e = (slots_e + MOE_BLK - 1) // MOE_BLK
    blk_end = jnp.cumsum(nblk_e)
    pstart = (blk_end - nblk_e) * MOE_BLK
    nused = blk_end[-1:].astype(I32)
    blk_ids = jnp.arange(n_blk, dtype=I32)
    xblk = jnp.minimum(blk_ids, nused[0] - 1)
    bexp = jnp.minimum(
        jnp.sum(blk_end[None, :] <= xblk[:, None], axis=1), N_EXP - 1).astype(I32)
    off = pstart[None, :] + jnp.cumsum(cnt_al, axis=0) - cnt_al
    g_row = jnp.arange(N_GRPS, dtype=I32) * SLOT_ALIGN
    e_of_g = jnp.minimum(
        jnp.sum(loc_end[:, None, :] <= g_row[None, :, None], axis=2), N_EXP - 1)
    pick = e_of_g[..., None] == jnp.arange(N_EXP, dtype=I32)
    grp_row = (jnp.sum(jnp.where(pick, (off - loc)[:, None, :], 0), axis=2)
               + g_row[None, :]).astype(I32).reshape(n_tiles, 1, N_GRPS)
    ngrp = (loc_end[:, -1] // SLOT_ALIGN).astype(I32)
    unused = nused[0] + blk_ids
    fill = jnp.concatenate([
        (N_EXP + n_blk - nused[0])[None],
        jnp.where(nblk_e > 0, blk_end - 1, -1),
        jnp.where(unused < n_blk, unused, -1)]).astype(I32)
    xs = _dispatch(ngrp, fill, grp_row, hx, lpos, n_tok, n_blk * MOE_BLK)
    ys = _experts(bexp, xblk, nused, xs, wg, wu, wd, n_blk, layer)
    return _combine(ngrp, grp_row, ys, lpos.T, w.T, x_new, hx, mods, sg_bf, su_bf, sd_bf, fg,
                    n_tok, final_norm)


def _rope_tables():
    t = np.arange(S)
    row = (t // GRID_W).astype(np.float32)
    col = (t % GRID_W).astype(np.float32)
    half = HD // 2
    inv = jnp.asarray(ROPE_BASE, F32) ** (-jnp.arange(0, half, 2, dtype=F32) / half)
    ang_r = jnp.asarray(row)[:, None] * inv
    ang_c = jnp.asarray(col)[:, None] * inv
    ang = jnp.concatenate([ang_r, ang_r, ang_c, ang_c], axis=-1)
    n_rep = ROT_W // HD
    cos = jnp.tile(jnp.cos(ang), (1, n_rep))
    sin = jnp.tile(jnp.sin(ang), (1, n_rep))
    cos = jnp.concatenate([cos, jnp.ones((IN_TM, ROT_W), F32)], axis=0)
    sin = jnp.concatenate([sin, jnp.zeros((IN_TM, ROT_W), F32)], axis=0)
    return cos, sin


def kernel(x, c, ctx, c_ctx, mod_w, mod_b, norm1_g, w_in, attn_sink, sgu_norm_g, sgu_w, sgu_b, na_rpb,
           group_norm_g, w_out, norm2_g, router_w, router_bias, exp_w_gate, exp_w_up, exp_w_down,
           shared_w_gate, shared_w_up, shared_w_down, final_g):
    xc = jnp.concatenate([x.reshape(N_LAT, D), ctx.reshape(N_CTX, D)], axis=0)
    cc = jnp.concatenate([c, c_ctx[None, :], jnp.zeros((16 - B - 1, D), F32)], axis=0)
    mods_all = _modulation(cc, mod_w, mod_b)
    cos_t, sin_t = _rope_tables()
    seg = np.arange(B_W) // HD
    bd = jnp.asarray((seg[:, None] == seg[None, :]).astype(np.float32) / HD, BF16)
    fg = final_g.reshape(1, D)

    out = None
    for l in range(DEPTH):
        last = l == DEPTH - 1
        mods = mods_all[l].reshape(16, 1, 6 * D)
        za, zc, uv = _in_proj(xc, mods, norm1_g[l].reshape(1, D), w_in[l].astype(BF16), cos_t, sin_t)
        ya = _attn_a(za, attn_sink[l], with_ctx=not last)
        yc = _na(zc, _na_bias_table(na_rpb[l]), with_ctx=not last)
        n_tok = N_LAT if last else N_ALL
        sb_tab = jnp.repeat(sgu_b[l].T, HD, axis=1)
        x_new, hx, logits_t = _merge(
            xc, ya, yc, uv, mods, group_norm_g[l].reshape(1, D), sgu_norm_g[l].reshape(1, B_W),
            sgu_w[l].astype(BF16), sb_tab, bd, w_out[l].astype(BF16), norm2_g[l].reshape(1, D),
            _router_split(router_w[l]), n_tok)
        res = _moe(x_new, hx, logits_t, mods, router_bias[l].reshape(N_EXP, 1),
                   exp_w_gate, exp_w_up, exp_w_down,
                   shared_w_gate[l].astype(BF16), shared_w_up[l].astype(BF16),
                   shared_w_down[l].astype(BF16), fg, n_tok, layer=l, final_norm=last)
        if last:
            out = res.reshape(B, S, D)
        else:
            xc = res
    return out
```

```python
import functools

import jax
import jax.numpy as jnp
import numpy as np
from jax import lax
from jax.experimental import pallas as pl
from jax.experimental.pallas import tpu as pltpu

F32 = jnp.float32
BF16 = jnp.bfloat16
I32 = jnp.int32

D = 1024
B = 8
S = 2048
C = 256
DEPTH = 2
GRID_W = 64
HD = 64
A_HEADS = 6
A_KV = 2
A_WIN = 128
A_BLK = 256
ROPE_BASE = 10000.0
SGU_GROUPS = 4
SGU_CHUNK = 128
C_HEADS = 6
C_WIN_R = 8
C_WIN_C = 16
A_W = A_HEADS * HD
B_W = SGU_GROUPS * HD
C_W = C_HEADS * HD
KV_W = A_KV * HD
IN_W = A_W + 2 * KV_W + 2 * B_W + 3 * C_W
N_EXP = 64
TOP_K = 8
N_GRP = 8
TOPK_GRP = 4
EXP_FF = 256
ROUTED_SCALE = 2.5
MOE_BLK = 1024
EPS = 1e-6
NEG = -1e30
SCALE = HD ** -0.5

N_LAT = B * S
N_CTX = B * C
N_ALL = N_LAT + N_CTX
ROWS = S // GRID_W

HIGHEST = lax.Precision.HIGHEST
ARB = pltpu.ARBITRARY

NT_DIMS = (((1,), (1,)), ((), ()))


def _cparams(n_axes, vmem_mb=48):
    return pltpu.CompilerParams(
        dimension_semantics=(ARB,) * n_axes, vmem_limit_bytes=vmem_mb * 1024 * 1024)


def _mod_row(i, tm):
    return jnp.where(i < N_LAT // tm, i // (S // tm), B)


def _rms(x):
    return x * lax.rsqrt(jnp.mean(x * x, axis=-1, keepdims=True) + EPS)


def _split_bf16(x):
    hi = x.astype(BF16)
    return hi, (x - hi.astype(F32)).astype(BF16)


def _router_split(rw):
    hi, lo = _split_bf16(jnp.pad(rw, ((0, 0), (0, 128 - N_EXP))))
    return jnp.stack([hi, lo])


MOD_TN = 1024


def _mod_body(cc_ref, w_ref, b_ref, o_ref):
    a = cc_ref[...]
    a = a * jax.nn.sigmoid(a)
    o_ref[...] = jnp.dot(a.astype(BF16), w_ref[...].astype(BF16),
                         preferred_element_type=F32) + b_ref[...]


def _modulation(cc, mod_w, mod_b):
    n_col = 6 * D // MOD_TN
    return pl.pallas_call(
        _mod_body,
        grid=(DEPTH, n_col),
        in_specs=[
            pl.BlockSpec((16, D), lambda l, j: (0, 0)),
            pl.BlockSpec((None, D, MOD_TN), lambda l, j: (l, 0, j)),
            pl.BlockSpec((None, 1, MOD_TN), lambda l, j: (l, 0, j)),
        ],
        out_specs=pl.BlockSpec((None, 16, MOD_TN), lambda l, j: (l, 0, j)),
        out_shape=jax.ShapeDtypeStruct((DEPTH, 16, 6 * D), F32),
        compiler_params=_cparams(2),
        name="modulation",
    )(cc, mod_w, mod_b.reshape(DEPTH, 1, 6 * D))


IN_TM = 512
ROT_W = A_W + KV_W


def _in_body(x_ref, m_ref, g_ref, w_ref, cos_ref, sin_ref, za_ref, zc_ref, uv_ref):
    x = x_ref[...]
    h = _rms(x) * g_ref[...]
    h = h * (1.0 + m_ref[:, D:2 * D]) + m_ref[:, 0:D]
    z = jnp.dot(h.astype(BF16), w_ref[...], preferred_element_type=F32)
    qk = z[:, :ROT_W]
    lane = lax.broadcasted_iota(I32, qk.shape, 1)
    rot = jnp.where((lane & 16) == 0,
                    -pltpu.roll(qk, ROT_W - 16, 1), pltpu.roll(qk, 16, 1))
    qk = qk * cos_ref[...] + rot * sin_ref[...]
    za_ref[...] = jnp.concatenate(
        [qk[:, :A_W] * SCALE, qk[:, A_W:], z[:, ROT_W:ROT_W + KV_W]], axis=1).astype(BF16)
    u0 = ROT_W + KV_W
    uv_ref[...] = jax.nn.gelu(z[:, u0:u0 + 2 * B_W]).astype(BF16)
    c0 = u0 + 2 * B_W
    zc_ref[...] = jnp.concatenate(
        [z[:, c0:c0 + C_W] * SCALE, z[:, c0 + C_W:]], axis=1).astype(BF16)


def _in_proj(xc, mods, g, w_bf, cos_t, sin_t):
    tm = IN_TM
    n_t = N_ALL // tm

    def tab_idx(i):
        return (jnp.where(i < N_LAT // tm, i % (S // tm), S // tm), 0)

    return pl.pallas_call(
        _in_body,
        grid=(n_t,),
        in_specs=[
            pl.BlockSpec((tm, D), lambda i: (i, 0)),
            pl.BlockSpec((None, 1, 6 * D), lambda i: (_mod_row(i, tm), 0, 0)),
            pl.BlockSpec((1, D), lambda i: (0, 0)),
            pl.BlockSpec((D, IN_W), lambda i: (0, 0)),
            pl.BlockSpec((tm, ROT_W), tab_idx),
            pl.BlockSpec((tm, ROT_W), tab_idx),
        ],
        out_specs=[
            pl.BlockSpec((tm, A_W + 2 * KV_W), lambda i: (i, 0)),
            pl.BlockSpec((tm, 3 * C_W), lambda i: (i, 0)),
            pl.BlockSpec((tm, 2 * B_W), lambda i: (i, 0)),
        ],
        out_shape=[
            jax.ShapeDtypeStruct((N_ALL, A_W + 2 * KV_W), BF16),
            jax.ShapeDtypeStruct((N_ALL, 3 * C_W), BF16),
            jax.ShapeDtypeStruct((N_ALL, 2 * B_W), BF16),
        ],
        compiler_params=_cparams(1),
        name="in_proj",
    )(xc, mods, g, w_bf, cos_t, sin_t)


A_BAND = A_BLK + 2 * A_WIN
N_QB_LAT = N_LAT // A_BLK
QB_PER_SEQ = S // A_BLK
QB_PER_CTX = C // A_BLK


def _softmax_pv(s_list, v_list, extra_logit=None):
    m = s_list[0].max(axis=-1, keepdims=True)
    for s in s_list[1:]:
        m = jnp.maximum(m, s.max(axis=-1, keepdims=True))
    if extra_logit is not None:
        m = jnp.maximum(m, extra_logit)
    den = None
    out = None
    for s, v in zip(s_list, v_list):
        p = jnp.exp(s - m)
        d = p.sum(axis=-1, keepdims=True)
        o = jnp.dot(p.astype(BF16), v, preferred_element_type=F32)
        den = d if den is None else den + d
        out = o if out is None else out + o
    if extra_logit is not None:
        den = den + jnp.exp(extra_logit - m)
    return out / den


def _attn_a_body(sink_ref, q_ref, kb_ref, vb_ref, kc_ref, vc_ref, o_ref):
    i = pl.program_id(0)
    is_lat = i < N_QB_LAT
    n = i % QB_PER_SEQ
    start = pl.multiple_of(jnp.clip(n * A_BLK - A_WIN, 0, S - A_BAND), A_WIN)
    qpos = n * A_BLK + lax.broadcasted_iota(I32, (A_BLK, A_BAND), 0)
    kpos = start + lax.broadcasted_iota(I32, (A_BLK, A_BAND), 1)
    mask = jnp.abs(kpos - qpos) <= jnp.where(is_lat, A_WIN, -1)
    kb = kb_ref[pl.ds(start, A_BAND), :]
    vb = vb_ref[pl.ds(start, A_BAND), :]
    kc = kc_ref[...]
    vc = vc_ref[...]
    outs = []
    for h in range(A_HEADS):
        kv = h // (A_HEADS // A_KV)
        sl = slice(kv * HD, (kv + 1) * HD)
        q = q_ref[:, h * HD:(h + 1) * HD]
        s_b = lax.dot_general(q, kb[:, sl], NT_DIMS, preferred_element_type=F32)
        s_b = jnp.where(mask, s_b, NEG)
        s_c = lax.dot_general(q, kc[:, sl], NT_DIMS, preferred_element_type=F32)
        outs.append(_softmax_pv([s_c, s_b], [vc[:, sl], vb[:, sl]], sink_ref[h]))
    o_ref[...] = jnp.concatenate(outs, axis=1).astype(BF16)


def _attn_a(za, sink, with_ctx):
    n_qb = N_QB_LAT + (N_CTX // A_BLK if with_ctx else 0)

    def bidx(i):
        return jnp.where(i < N_QB_LAT, i // QB_PER_SEQ, (i - N_QB_LAT) // QB_PER_CTX)

    k_col = A_W // KV_W
    v_col = k_col + 1
    return pl.pallas_call(
        _attn_a_body,
        grid=(n_qb,),
        in_specs=[
            pl.BlockSpec(memory_space=pltpu.SMEM),
            pl.BlockSpec((A_BLK, A_W), lambda i: (i, 0)),
            pl.BlockSpec((S, KV_W), lambda i: (bidx(i), k_col)),
            pl.BlockSpec((S, KV_W), lambda i: (bidx(i), v_col)),
            pl.BlockSpec((C, KV_W), lambda i: (N_LAT // C + bidx(i), k_col)),
            pl.BlockSpec((C, KV_W), lambda i: (N_LAT // C + bidx(i), v_col)),
        ],
        out_specs=pl.BlockSpec((A_BLK, A_W), lambda i: (i, 0)),
        out_shape=jax.ShapeDtypeStruct((n_qb * A_BLK, A_W), BF16),
        compiler_params=_cparams(1),
        name="attn_window",
    )(sink, za, za, za, za, za)


NA_R = 4
NA_TQ = NA_R * GRID_W
NA_KROWS = 12
NA_WIN = NA_KROWS * GRID_W
NA_STEPS = ROWS // NA_R
NA_PAIRS = NA_KROWS // 2
N_DR = 2 * C_WIN_R - 1


def _na_body(q_ref, k_ref, v_ref, kc_ref, vc_ref, tab_ref, o_ref):
    j = pl.program_id(1)
    is_lat = j < NA_STEPS
    r0 = jnp.minimum(j, NA_STEPS - 1) * NA_R
    u0 = jnp.clip(r0 - C_WIN_R // 2, 0, ROWS - NA_KROWS)
    k0 = pl.multiple_of(u0 * GRID_W, GRID_W)
    kw = k_ref[pl.ds(k0, NA_WIN), :]
    vw = v_ref[pl.ds(k0, NA_WIN), :]
    kc = kc_ref[...]
    vc = vc_ref[...]
    q = q_ref[...]
    left = lax.broadcasted_iota(I32, (1, 2 * GRID_W), 1) < GRID_W

    tab_idx = []
    penalty = []
    for rr in range(NA_R):
        r = r0 + rr
        start = jnp.clip(r - C_WIN_R // 2, 0, ROWS - C_WIN_R)
        idx_row = []
        pen_row = []
        for p in range(NA_PAIRS):
            kr = u0 + 2 * p
            idx_row.append(jnp.clip(kr - r + (C_WIN_R - 1), -1, N_DR - 1) + 1)
            pens = []
            for half in range(2):
                ok = jnp.logical_and(is_lat, jnp.logical_and(kr + half >= start,
                                                             kr + half < start + C_WIN_R))
                pens.append(jnp.where(ok, 0.0, NEG))
            pen_row.append(jnp.where(left, pens[0], pens[1]))
        tab_idx.append(idx_row)
        penalty.append(jnp.concatenate(pen_row, axis=1))

    outs = []
    for h in range(C_HEADS):
        sl = slice(h * HD, (h + 1) * HD)
        qh = q[:, sl]
        s_w = lax.dot_general(qh, kw[:, sl], NT_DIMS, preferred_element_type=F32)
        bias = jnp.concatenate([
            jnp.concatenate([tab_ref[h, pl.ds(tab_idx[rr][p], 1)][0] for p in range(NA_PAIRS)],
                            axis=1) + penalty[rr]
            for rr in range(NA_R)], axis=0)
        s_c = lax.dot_general(qh, kc[:, sl], NT_DIMS, preferred_element_type=F32)
        outs.append(_softmax_pv([s_c, s_w + bias], [vc[:, sl], vw[:, sl]]))
    o_ref[...] = jnp.concatenate(outs, axis=1).astype(BF16)


def _na(zc, tab, with_ctx):
    n_j = NA_STEPS + (1 if with_ctx else 0)

    def qidx(b, j):
        return jnp.where(j < NA_STEPS, b * NA_STEPS + j, N_LAT // NA_TQ + b)

    n_out = N_LAT + (N_CTX if with_ctx else 0)
    return pl.pallas_call(
        _na_body,
        grid=(B, n_j),
        in_specs=[
            pl.BlockSpec((NA_TQ, C_W), lambda b, j: (qidx(b, j), 0)),
            pl.BlockSpec((S, C_W), lambda b, j: (b, 1)),
            pl.BlockSpec((S, C_W), lambda b, j: (b, 2)),
            pl.BlockSpec((C, C_W), lambda b, j: (N_LAT // C + b, 1)),
            pl.BlockSpec((C, C_W), lambda b, j: (N_LAT // C + b, 2)),
            pl.BlockSpec((C_HEADS, N_DR + 1, GRID_W, 2 * GRID_W), lambda b, j: (0, 0, 0, 0)),
        ],
        out_specs=pl.BlockSpec((NA_TQ, C_W), lambda b, j: (qidx(b, j), 0)),
        out_shape=jax.ShapeDtypeStruct((n_out, C_W), BF16),
        compiler_params=_cparams(2),
        name="attn_neighbourhood",
    )(zc, zc, zc, zc, zc, tab)


def _na_bias_table(rpb):
    cq = np.arange(GRID_W)
    col_start = np.clip(cq - C_WIN_C // 2, 0, GRID_W - C_WIN_C)
    col_ok = (cq[None, :] >= col_start[:, None]) & (cq[None, :] < col_start[:, None] + C_WIN_C)
    dc = np.clip(cq[None, :] - cq[:, None], -(C_WIN_C - 1), C_WIN_C - 1) + (C_WIN_C - 1)
    n_dc = 2 * C_WIN_C - 1
    pick = (dc.reshape(-1)[None, :] == np.arange(n_dc)[:, None]).astype(np.float32)
    t = jnp.dot(rpb.astype(F32).reshape(-1, n_dc), jnp.asarray(pick), precision=HIGHEST)
    t = t.reshape(C_HEADS, N_DR, GRID_W, GRID_W)
    t = jnp.where(col_ok[None, None], t, NEG)
    zero = jnp.zeros((C_HEADS, 1, GRID_W, GRID_W), F32)
    ext = jnp.concatenate([zero, t, zero], axis=1)
    return jnp.concatenate([ext[:, :-1], ext[:, 1:]], axis=-1)


MG_TM = 256


def _merge_body(x_ref, ya_ref, yc_ref, uv_ref, m_ref, gg_ref, sg_ref, sw_ref, sb_ref, bd_ref,
                wo_ref, g2_ref, rw_ref, xo_ref, hx_ref, lg_ref):
    u = uv_ref[:, :B_W].astype(F32)
    v = uv_ref[:, B_W:].astype(F32)
    vv_hi, vv_lo = _split_bf16(v * v)
    ms = (jnp.dot(vv_hi, bd_ref[...], preferred_element_type=F32)
          + jnp.dot(vv_lo, bd_ref[...], preferred_element_type=F32))
    vn = (v * lax.rsqrt(ms + EPS) * sg_ref[...]).astype(BF16)
    lane_grp = lax.broadcasted_iota(I32, (SGU_CHUNK, B_W), 1) // HD
    gates = []
    for c in range(MG_TM // SGU_CHUNK):
        vc = vn[c * SGU_CHUNK:(c + 1) * SGU_CHUNK, :]
        gate = sb_ref[...]
        acc = jnp.zeros((SGU_CHUNK, B_W), F32)
        for g in range(SGU_GROUPS):
            r = jnp.dot(sw_ref[g], vc, preferred_element_type=F32)
            acc = jnp.where(lane_grp == g, r, acc)
        gates.append(acc + gate)
    yb = u * jnp.concatenate(gates, axis=0)
    gg = gg_ref[...]
    ycat = jnp.concatenate([
        _rms(ya_ref[...].astype(F32)) * gg[:, :A_W],
        _rms(yb) * gg[:, A_W:A_W + B_W],
        _rms(yc_ref[...].astype(F32)) * gg[:, A_W + B_W:],
    ], axis=1)
    proj = jnp.dot(ycat.astype(BF16), wo_ref[...], preferred_element_type=F32)
    xn = x_ref[...] + m_ref[:, 2 * D:3 * D] * proj
    xo_ref[...] = xn
    hx = _rms(xn) * g2_ref[...]
    hx = hx * (1.0 + m_ref[:, 4 * D:5 * D]) + m_ref[:, 3 * D:4 * D]
    hx_ref[...] = hx.astype(BF16)
    hx_hi, hx_lo = _split_bf16(hx)
    logits = (jnp.dot(hx_hi, rw_ref[0], preferred_element_type=F32)
              + jnp.dot(hx_hi, rw_ref[1], preferred_element_type=F32)
              + jnp.dot(hx_lo, rw_ref[0], preferred_element_type=F32))
    lg_ref[...] = logits.T[:N_EXP, :]


def _merge(xc, ya, yc, uv, mods, gg, sg, sw_bf, sb_tab, bd, wo_bf, g2, rw_t, n_rows):
    tm = MG_TM
    n_t = n_rows // tm
    const2 = lambda i: (0, 0)
    row = lambda i: (i, 0)
    return pl.pallas_call(
        _merge_body,
        grid=(n_t,),
        in_specs=[
            pl.BlockSpec((tm, D), row),
            pl.BlockSpec((tm, A_W), row),
            pl.BlockSpec((tm, C_W), row),
            pl.BlockSpec((tm, 2 * B_W), row),
            pl.BlockSpec((None, 1, 6 * D), lambda i: (_mod_row(i, tm), 0, 0)),
            pl.BlockSpec((1, D), const2),
            pl.BlockSpec((1, B_W), const2),
            pl.BlockSpec((SGU_GROUPS, SGU_CHUNK, SGU_CHUNK), lambda i: (0, 0, 0)),
            pl.BlockSpec((SGU_CHUNK, B_W), const2),
            pl.BlockSpec((B_W, B_W), const2),
            pl.BlockSpec((D, D), const2),
            pl.BlockSpec((1, D), const2),
            pl.BlockSpec((2, D, 128), lambda i: (0, 0, 0)),
        ],
        out_specs=[
            pl.BlockSpec((tm, D), row),
            pl.BlockSpec((tm, D), row),
            pl.BlockSpec((N_EXP, tm), lambda i: (0, i)),
        ],
        out_shape=[
            jax.ShapeDtypeStruct((n_rows, D), F32),
            jax.ShapeDtypeStruct((n_rows, D), BF16),
            jax.ShapeDtypeStruct((N_EXP, n_rows), F32),
        ],
        compiler_params=_cparams(1),
        name="merge",
    )(xc, ya, yc, uv, mods, gg, sg, sw_bf, sb_tab, bd, wo_bf, g2, rw_t)


RT_TM = 256
GRP_SZ = N_EXP // N_GRP
SLOT_ALIGN = 16


def _first_argmax(v, iota):
    m = v.max(axis=0, keepdims=True)
    idx = jnp.where(v == m, iota, float(v.shape[0])).min(axis=0, keepdims=True)
    return m, idx


def _stack_rows(rows, iota):
    out = jnp.zeros(iota.shape, F32)
    for r, v in enumerate(rows):
        out = jnp.where(iota == float(r), v, out)
    return out


def _route_body(lg_ref, rb_ref, w_ref, lp_ref, tc_ref):
    tm = RT_TM
    scores = jax.nn.sigmoid(lg_ref[...])
    sel = scores + rb_ref[...]
    iota_g = lax.broadcasted_iota(I32, (GRP_SZ, tm), 0).astype(F32)
    gs = []
    for g in range(N_GRP):
        v = sel[g * GRP_SZ:(g + 1) * GRP_SZ, :]
        m1, i1 = _first_argmax(v, iota_g)
        m2 = jnp.where(iota_g == i1, -jnp.inf, v).max(axis=0, keepdims=True)
        gs.append(m1 + m2)
    iota_n = lax.broadcasted_iota(I32, (N_GRP, tm), 0).astype(F32)
    gscore = _stack_rows(gs, iota_n)
    gsel = jnp.zeros((N_GRP, tm), F32)
    for _ in range(TOPK_GRP):
        _, gi = _first_argmax(gscore, iota_n)
        hit = iota_n == gi
        gsel = jnp.where(hit, 1.0, gsel)
        gscore = jnp.where(hit, -jnp.inf, gscore)
    emask = jnp.concatenate(
        [jnp.broadcast_to(gsel[g:g + 1, :], (GRP_SZ, tm)) for g in range(N_GRP)], axis=0)
    cand = jnp.where(emask > 0.5, sel, NEG)
    iota_e = lax.broadcasted_iota(I32, (N_EXP, tm), 0).astype(F32)
    hits = []
    ws = []
    member = jnp.zeros((N_EXP, tm), F32)
    for _ in range(TOP_K):
        _, ei = _first_argmax(cand, iota_e)
        hit = iota_e == ei
        hits.append(hit)
        ws.append(jnp.where(hit, scores, 0.0).sum(axis=0, keepdims=True))
        member = jnp.where(hit, 1.0, member)
        cand = jnp.where(hit, -jnp.inf, cand)
    wsum = ws[0]
    for w in ws[1:]:
        wsum = wsum + w
    iota_k = lax.broadcasted_iota(I32, (TOP_K, tm), 0).astype(F32)
    w_ref[...] = _stack_rows(ws, iota_k) / wsum * ROUTED_SCALE
    r_i = lax.broadcasted_iota(I32, (tm, tm), 0)
    c_i = lax.broadcasted_iota(I32, (tm, tm), 1)
    tri = jnp.where(r_i < c_i, 1.0, 0.0).astype(BF16)
    local = jnp.dot(member.astype(BF16), tri, preferred_element_type=F32)
    tile_cnt = member.sum(axis=1, keepdims=True)
    tc_ref[...] = tile_cnt
    aligned = jnp.ceil(tile_cnt / SLOT_ALIGN) * SLOT_ALIGN
    e_r = lax.broadcasted_iota(I32, (N_EXP, N_EXP), 0)
    e_c = lax.broadcasted_iota(I32, (N_EXP, N_EXP), 1)
    below = jnp.where(e_c < e_r, 1.0, 0.0)
    start = jnp.dot(below, jnp.broadcast_to(aligned, (N_EXP, 128)), precision=HIGHEST,
                    preferred_element_type=F32)[:, 0:1]
    pos = local + start
    lp_ref[...] = _stack_rows(
        [jnp.where(hit, pos, 0.0).sum(axis=0, keepdims=True) for hit in hits],
        iota_k).astype(I32)


def _route(logits_t, rbias, n_tok):
    tm = RT_TM
    tok = lambda i: (0, i)
    return pl.pallas_call(
        _route_body,
        grid=(n_tok // tm,),
        in_specs=[
            pl.BlockSpec((N_EXP, tm), tok),
            pl.BlockSpec((N_EXP, 1), lambda i: (0, 0)),
        ],
        out_specs=[
            pl.BlockSpec((TOP_K, tm), tok),
            pl.BlockSpec((TOP_K, tm), tok),
            pl.BlockSpec((None, N_EXP, 1), lambda i: (i, 0, 0)),
        ],
        out_shape=[
            jax.ShapeDtypeStruct((TOP_K, n_tok), F32),
            jax.ShapeDtypeStruct((TOP_K, n_tok), I32),
            jax.ShapeDtypeStruct((n_tok // tm, N_EXP, 1), F32),
        ],
        compiler_params=_cparams(1),
        name="route",
    )(logits_t, rbias)


DP_TM = RT_TM
SORT_ROWS = 3072
N_GRPS = SORT_ROWS // SLOT_ALIGN
ISSUE_UNROLL = 4
WAIT_GROUPS = 32
N_BURSTS = 6
BURST_ROWS = SORT_ROWS // N_BURSTS
BURST_GRPS = BURST_ROWS // SLOT_ALIGN


def _start_groups(g_lo, g_hi, row_of, make_copy):
    n_full = (g_hi - g_lo) // ISSUE_UNROLL

    def body(q, carry):
        for s in range(ISSUE_UNROLL):
            g = g_lo + q * ISSUE_UNROLL + s
            make_copy(g, row_of(g)).start(priority=s % 2)
        return carry

    def tail(g, carry):
        make_copy(g, row_of(g)).start()
        return carry

    lax.fori_loop(0, n_full, body, 0)
    lax.fori_loop(g_lo + n_full * ISSUE_UNROLL, g_hi, tail, 0)


def _start_burst(q, ngrp, row_of, make_copy):
    _start_groups(jnp.minimum(q * BURST_GRPS, ngrp), jnp.minimum((q + 1) * BURST_GRPS, ngrp),
                  row_of, make_copy)


def _wait_groups(ngrp, make_copy, make_bulk):
    def bulk(q, carry):
        make_bulk().wait()
        return carry

    def single(g, carry):
        make_copy(0, 0).wait()
        return carry

    n_bulk = ngrp // WAIT_GROUPS
    lax.fori_loop(0, n_bulk, bulk, 0)
    lax.fori_loop(n_bulk * WAIT_GROUPS, ngrp, single, 0)


def _dispatch_body(ngrp_ref, fill_ref, grow_ref, hx_ref, lp_ref, xs_ref, zbuf_ref, zero_ref, sems):
    i = pl.program_id(0)
    n_tiles = pl.num_programs(0)
    slot = i % 2

    def zero_copy(b):
        dst = xs_ref.at[pl.ds(pl.multiple_of(b * MOE_BLK, MOE_BLK), MOE_BLK)]
        return pltpu.make_async_copy(zero_ref, dst, sems.at[0])

    @pl.when(i == 0)
    def _():
        zero_ref[...] = jnp.zeros_like(zero_ref)
        n_fill = fill_ref[0]

        def z_start(q, carry):
            @pl.when(fill_ref[1 + q] >= 0)
            def _():
                zero_copy(fill_ref[1 + q]).start()
            return carry

        def z_wait(q, carry):
            @pl.when(fill_ref[1 + q] >= 0)
            def _():
                zero_copy(0).wait()
            return carry

        lax.fori_loop(0, n_fill, z_start, 0)
        lax.fori_loop(0, n_fill, z_wait, 0)

    def group_copy(s, g, row):
        src = zbuf_ref.at[s, pl.ds(pl.multiple_of(g * SLOT_ALIGN, SLOT_ALIGN), SLOT_ALIGN)]
        dst = xs_ref.at[pl.ds(pl.multiple_of(row, SLOT_ALIGN), SLOT_ALIGN)]
        return pltpu.make_async_copy(src, dst, sems.at[s])

    def bulk_copy(s):
        n_rows = WAIT_GROUPS * SLOT_ALIGN
        return pltpu.make_async_copy(
            zbuf_ref.at[s, pl.ds(0, n_rows)], xs_ref.at[pl.ds(0, n_rows)], sems.at[s])

    x = hx_ref[...]
    lpos = lp_ref[...]
    n_cur = ngrp_ref[i]
    pos = [jnp.tile(jnp.broadcast_to(lpos[k:k + 1, :], (16, DP_TM)).astype(jnp.int16),
                    (BURST_ROWS // 16, 1)) for k in range(TOP_K)]
    for q in range(N_BURSTS):
        @pl.when(q * BURST_GRPS < n_cur)
        def _():
            row_iota = (lax.broadcasted_iota(I32, (BURST_ROWS, DP_TM), 0)
                        + q * BURST_ROWS).astype(jnp.int16)
            onehot = jnp.zeros((BURST_ROWS, DP_TM), BF16)
            for k in range(TOP_K):
                onehot = jnp.where(row_iota == pos[k], jnp.ones((), BF16), onehot)
            z = jnp.dot(onehot, x, preferred_element_type=F32)
            zbuf_ref[slot, pl.ds(q * BURST_ROWS, BURST_ROWS), :] = z.astype(BF16)
            _start_burst(q, n_cur, lambda g: grow_ref[0, g],
                         lambda g, r: group_copy(slot, g, r))

    @pl.when(i > 0)
    def _():
        _wait_groups(ngrp_ref[jnp.maximum(i - 1, 0)], lambda g, r: group_copy(1 - slot, 0, 0),
                     lambda: bulk_copy(1 - slot))

    @pl.when(i == n_tiles - 1)
    def _():
        _wait_groups(n_cur, lambda g, r: group_copy(slot, 0, 0), lambda: bulk_copy(slot))


def _dispatch(ngrp, fill, grp_row, hx, lpos, n_tok, n_slots):
    return pl.pallas_call(
        _dispatch_body,
        grid_spec=pltpu.PrefetchScalarGridSpec(
            num_scalar_prefetch=2,
            grid=(n_tok // DP_TM,),
            in_specs=[
                pl.BlockSpec((None, 1, N_GRPS), lambda i, *_: (i, 0, 0), memory_space=pltpu.SMEM),
                pl.BlockSpec((DP_TM, D), lambda i, *_: (i, 0)),
                pl.BlockSpec((TOP_K, DP_TM), lambda i, *_: (0, i)),
            ],
            out_specs=pl.BlockSpec(memory_space=pl.ANY),
            scratch_shapes=[
                pltpu.VMEM((2, SORT_ROWS, D), BF16),
                pltpu.VMEM((MOE_BLK, D), BF16),
                pltpu.SemaphoreType.DMA((2,)),
            ],
        ),
        out_shape=jax.ShapeDtypeStruct((n_slots, D), BF16),
        compiler_params=_cparams(1, vmem_mb=56),
        name="moe_dispatch",
    )(ngrp, fill, grp_row, hx, lpos)


def _experts_body(bexp_ref, xblk_ref, nused_ref, xs_ref, wg_ref, wu_ref, wd_ref, ys_ref):
    j = pl.program_id(0)

    @pl.when(j < nused_ref[0])
    def _():
        x = xs_ref[...]
        g = jnp.dot(x, wg_ref[...].astype(BF16), preferred_element_type=F32)
        u = jnp.dot(x, wu_ref[...].astype(BF16), preferred_element_type=F32)
        h = (g * jax.nn.sigmoid(g) * u).astype(BF16)
        y = jnp.dot(h, wd_ref[...].astype(BF16), preferred_element_type=F32)
        ys_ref[...] = y.astype(BF16)

    @pl.when(j >= nused_ref[0])
    def _():
        ys_ref[...] = jnp.zeros_like(ys_ref)


def _experts(bexp, xblk, nused, xs, wg, wu, wd, n_blk, layer):
    w_idx = lambda j, be, xb, nu: (layer, be[j], 0, 0)
    return pl.pallas_call(
        _experts_body,
        grid_spec=pltpu.PrefetchScalarGridSpec(
            num_scalar_prefetch=3,
            grid=(n_blk,),
            in_specs=[
                pl.BlockSpec((MOE_BLK, D), lambda j, be, xb, nu: (xb[j], 0)),
                pl.BlockSpec((None, None, D, EXP_FF), w_idx),
                pl.BlockSpec((None, None, D, EXP_FF), w_idx),
                pl.BlockSpec((None, None, EXP_FF, D), w_idx),
            ],
            out_specs=pl.BlockSpec((MOE_BLK, D), lambda j, be, xb, nu: (j, 0)),
        ),
        out_shape=jax.ShapeDtypeStruct((n_blk * MOE_BLK, D), BF16),
        compiler_params=_cparams(1),
        name="moe_experts",
    )(bexp, xblk, nused, xs, wg, wu, wd)


def _combine_body(ngrp_ref, grow_ref, grow_next_ref, ys_ref, lp_ref, w_ref, x_ref, hx_ref, m_ref,
                  sg_ref, su_ref, sd_ref, fg_ref, o_ref, ybuf_ref, acc_ref, sems, *, final_norm):
    i = pl.program_id(0)
    n_tiles = pl.num_programs(0)
    slot = i % 2

    def group_copy(s, g, row):
        src = ys_ref.at[pl.ds(pl.multiple_of(row, SLOT_ALIGN), SLOT_ALIGN)]
        dst = ybuf_ref.at[s, pl.ds(pl.multiple_of(g * SLOT_ALIGN, SLOT_ALIGN), SLOT_ALIGN)]
        return pltpu.make_async_copy(src, dst, sems.at[s])

    def bulk_copy(s):
        n_rows = WAIT_GROUPS * SLOT_ALIGN
        return pltpu.make_async_copy(
            ys_ref.at[pl.ds(0, n_rows)], ybuf_ref.at[s, pl.ds(0, n_rows)], sems.at[s])

    @pl.when(i == 0)
    def _():
        ybuf_ref[...] = jnp.zeros_like(ybuf_ref)
        _start_groups(0, ngrp_ref[0], lambda g: grow_ref[0, g], lambda g, r: group_copy(0, g, r))

    n_next = jnp.where(i + 1 < n_tiles, ngrp_ref[jnp.minimum(i + 1, n_tiles - 1)], 0)

    def prefetch(q):
        _start_burst(q, n_next, lambda g: grow_next_ref[0, g],
                     lambda g, r: group_copy(1 - slot, g, r))

    hx = hx_ref[...]
    g = jnp.dot(hx, sg_ref[...], preferred_element_type=F32)
    u = jnp.dot(hx, su_ref[...], preferred_element_type=F32)
    h = (g * jax.nn.sigmoid(g) * u).astype(BF16)
    acc_ref[...] = jnp.dot(h, sd_ref[...], preferred_element_type=F32)

    lpos = lp_ref[...]
    w = w_ref[...]
    n_own = ngrp_ref[i]
    _wait_groups(n_own, lambda g, r: group_copy(slot, 0, 0), lambda: bulk_copy(slot))
    lane_tiles = BURST_ROWS // 128
    pos = [jnp.tile(jnp.broadcast_to(lpos[:, k:k + 1], (DP_TM, 128)).astype(jnp.int16),
                    (1, lane_tiles)) for k in range(TOP_K)]
    wts = [jnp.tile(jnp.broadcast_to(w[:, k:k + 1], (DP_TM, 128)).astype(BF16),
                    (1, lane_tiles)) for k in range(TOP_K)]
    for q in range(N_BURSTS):
        @pl.when(q * BURST_GRPS < n_own)
        def _():
            col_iota = (lax.broadcasted_iota(I32, (DP_TM, BURST_ROWS), 1)
                        + q * BURST_ROWS).astype(jnp.int16)
            unsort = jnp.zeros((DP_TM, BURST_ROWS), BF16)
            for k in range(TOP_K):
                unsort = jnp.where(col_iota == pos[k], wts[k], unsort)
            acc_ref[...] += jnp.dot(unsort, ybuf_ref[slot, pl.ds(q * BURST_ROWS, BURST_ROWS), :],
                                    preferred_element_type=F32)
        prefetch(q)
    out = x_ref[...] + m_ref[:, 5 * D:6 * D] * acc_ref[...]
    if final_norm:
        out = _rms(out) * fg_ref[...]
    o_ref[...] = out


def _combine(ngrp, grp_row, ys, lpos_t, w_t, x, hx, mods, sg_bf, su_bf, sd_bf, fg, n_tok, final_norm):
    tm = DP_TM
    row = lambda i, *_: (i, 0)
    const2 = lambda i, *_: (0, 0)
    return pl.pallas_call(
        functools.partial(_combine_body, final_norm=final_norm),
        grid_spec=pltpu.PrefetchScalarGridSpec(
            num_scalar_prefetch=1,
            grid=(n_tok // tm,),
            in_specs=[
                pl.BlockSpec((None, 1, N_GRPS), lambda i, *_: (i, 0, 0), memory_space=pltpu.SMEM),
                pl.BlockSpec((None, 1, N_GRPS), lambda i, *_: (jnp.minimum(i + 1, n_tok // tm - 1), 0, 0),
                             memory_space=pltpu.SMEM),
                pl.BlockSpec(memory_space=pl.ANY),
                pl.BlockSpec((tm, TOP_K), row),
                pl.BlockSpec((tm, TOP_K), row),
                pl.BlockSpec((tm, D), row),
                pl.BlockSpec((tm, D), row),
                pl.BlockSpec((None, 1, 6 * D), lambda i, *_: (_mod_row(i, tm), 0, 0)),
                pl.BlockSpec((D, EXP_FF), const2),
                pl.BlockSpec((D, EXP_FF), const2),
                pl.BlockSpec((EXP_FF, D), const2),
                pl.BlockSpec((1, D), const2),
            ],
            out_specs=pl.BlockSpec((tm, D), row),
            scratch_shapes=[pltpu.VMEM((2, SORT_ROWS, D), BF16), pltpu.VMEM((tm, D), F32),
                            pltpu.SemaphoreType.DMA((2,))],
        ),
        out_shape=jax.ShapeDtypeStruct((n_tok, D), F32),
        compiler_params=_cparams(1, vmem_mb=56),
        name="moe_combine",
    )(ngrp, grp_row, grp_row, ys, lpos_t, w_t, x, hx, mods, sg_bf, su_bf, sd_bf, fg)


def _moe(x_new, hx, logits_t, mods, rbias, wg, wu, wd, sg_bf, su_bf, sd_bf, fg, n_tok, layer,
         final_norm):
    w, lpos, tile_cnt = _route(logits_t, rbias, n_tok)
    n_tiles = n_tok // DP_TM
    n_blk = (n_tok * TOP_K + (SLOT_ALIGN - 1) * N_EXP * n_tiles) // MOE_BLK + N_EXP
    tcnt = tile_cnt[:, :, 0].astype(I32)
    cnt_al = (tcnt + SLOT_ALIGN - 1) // SLOT_ALIGN * SLOT_ALIGN
    loc_end = jnp.cumsum(cnt_al, axis=1)
    loc = loc_end - cnt_al
    slots_e = jnp.sum(cnt_al, axis=0)
    nblk_e = (slots_e + MOE_BLK - 1) // MOE_BLK
    blk_end = jnp.cumsum(nblk_e)
    pstart = (blk_end - nblk_e) * MOE_BLK
    nused = blk_end[-1:].astype(I32)
    blk_ids = jnp.arange(n_blk, dtype=I32)
    xblk = jnp.minimum(blk_ids, nused[0] - 1)
    bexp = jnp.minimum(
        jnp.sum(blk_end[None, :] <= xblk[:, None], axis=1), N_EXP - 1).astype(I32)
    off = pstart[None, :] + jnp.cumsum(cnt_al, axis=0) - cnt_al
    g_row = jnp.arange(N_GRPS, dtype=I32) * SLOT_ALIGN
    e_of_g = jnp.minimum(
        jnp.sum(loc_end[:, None, :] <= g_row[None, :, None], axis=2), N_EXP - 1)
    pick = e_of_g[..., None] == jnp.arange(N_EXP, dtype=I32)
    grp_row = (jnp.sum(jnp.where(pick, (off - loc)[:, None, :], 0), axis=2)
               + g_row[None, :]).astype(I32).reshape(n_tiles, 1, N_GRPS)
    ngrp = (loc_end[:, -1] // SLOT_ALIGN).astype(I32)
    unused = nused[0] + blk_ids
    fill = jnp.concatenate([
        (N_EXP + n_blk - nused[0])[None],
        jnp.where(nblk_e > 0, blk_end - 1, -1),
        jnp.where(unused < n_blk, unused, -1)]).astype(I32)
    xs = _dispatch(ngrp, fill, grp_row, hx, lpos, n_tok, n_blk * MOE_BLK)
    ys = _experts(bexp, xblk, nused, xs, wg, wu, wd, n_blk, layer)
    return _combine(ngrp, grp_row, ys, lpos.T, w.T, x_new, hx, mods, sg_bf, su_bf, sd_bf, fg,
                    n_tok, final_norm)


def _rope_tables():
    t = np.arange(S)
    row = (t // GRID_W).astype(np.float32)
    col = (t % GRID_W).astype(np.float32)
    half = HD // 2
    inv = jnp.asarray(ROPE_BASE, F32) ** (-jnp.arange(0, half, 2, dtype=F32) / half)
    ang_r = jnp.asarray(row)[:, None] * inv
    ang_c = jnp.asarray(col)[:, None] * inv
    ang = jnp.concatenate([ang_r, ang_r, ang_c, ang_c], axis=-1)
    n_rep = ROT_W // HD
    cos = jnp.tile(jnp.cos(ang), (1, n_rep))
    sin = jnp.tile(jnp.sin(ang), (1, n_rep))
    cos = jnp.concatenate([cos, jnp.ones((IN_TM, ROT_W), F32)], axis=0)
    sin = jnp.concatenate([sin, jnp.zeros((IN_TM, ROT_W), F32)], axis=0)
    return cos, sin


def kernel(x, c, ctx, c_ctx, mod_w, mod_b, norm1_g, w_in, attn_sink, sgu_norm_g, sgu_w, sgu_b, na_rpb,
           group_norm_g, w_out, norm2_g, router_w, router_bias, exp_w_gate, exp_w_up, exp_w_down,
           shared_w_gate, shared_w_up, shared_w_down, final_g):
    xc = jnp.concatenate([x.reshape(N_LAT, D), ctx.reshape(N_CTX, D)], axis=0)
    cc = jnp.concatenate([c, c_ctx[None, :], jnp.zeros((16 - B - 1, D), F32)], axis=0)
    mods_all = _modulation(cc, mod_w, mod_b)
    cos_t, sin_t = _rope_tables()
    seg = np.arange(B_W) // HD
    bd = jnp.asarray((seg[:, None] == seg[None, :]).astype(np.float32) / HD, BF16)
    fg = final_g.reshape(1, D)

    out = None
    for l in range(DEPTH):
        last = l == DEPTH - 1
        mods = mods_all[l].reshape(16, 1, 6 * D)
        za, zc, uv = _in_proj(xc, mods, norm1_g[l].reshape(1, D), w_in[l].astype(BF16), cos_t, sin_t)
        ya = _attn_a(za, attn_sink[l], with_ctx=not last)
        yc = _na(zc, _na_bias_table(na_rpb[l]), with_ctx=not last)
        n_tok = N_LAT if last else N_ALL
        sb_tab = jnp.repeat(sgu_b[l].T, HD, axis=1)
        x_new, hx, logits_t = _merge(
            xc, ya, yc, uv, mods, group_norm_g[l].reshape(1, D), sgu_norm_g[l].reshape(1, B_W),
            sgu_w[l].astype(BF16), sb_tab, bd, w_out[l].astype(BF16), norm2_g[l].reshape(1, D),
            _router_split(router_w[l]), n_tok)
        res = _moe(x_new, hx, logits_t, mods, router_bias[l].reshape(N_EXP, 1),
                   exp_w_gate, exp_w_up, exp_w_down,
                   shared_w_gate[l].astype(BF16), shared_w_up[l].astype(BF16),
                   shared_w_down[l].astype(BF16), fg, n_tok, layer=l, final_norm=last)
        if last:
            out = res.reshape(B, S, D)
        else:
            xc = res
    return out
```

```python
import functools
import math

import jax
import jax.numpy as jnp
import numpy as np
from jax import lax
from jax.experimental import pallas as pl
from jax.experimental.pallas import tpu as pltpu

F32 = jnp.float32
BF16 = jnp.bfloat16
I32 = jnp.int32

D = 1024
B = 8
S = 2048
C = 256
DEPTH = 2
GRID_W = 64
HD = 64
A_HEADS = 6
A_KV = 2
A_WIN = 128
A_BLK = 256
ROPE_BASE = 10000.0
SGU_GROUPS = 4
SGU_CHUNK = 128
C_HEADS = 6
C_WIN_R = 8
C_WIN_C = 16
A_W = A_HEADS * HD
B_W = SGU_GROUPS * HD
C_W = C_HEADS * HD
KV_W = A_KV * HD
IN_W = A_W + 2 * KV_W + 2 * B_W + 3 * C_W
N_EXP = 64
TOP_K = 8
N_GRP = 8
TOPK_GRP = 4
EXP_FF = 256
ROUTED_SCALE = 2.5
MOE_BLK = 1024
EPS = 1e-6
NEG = -1e30
SCALE = HD ** -0.5
LOG2E = math.log2(math.e)
Q_SCALE = SCALE * LOG2E

N_LAT = B * S
N_CTX = B * C
N_ALL = N_LAT + N_CTX
ROWS = S // GRID_W

HIGHEST = lax.Precision.HIGHEST
ARB = pltpu.ARBITRARY

NT_DIMS = (((1,), (1,)), ((), ()))


def _cparams(n_axes, vmem_mb=48):
    return pltpu.CompilerParams(
        dimension_semantics=(ARB,) * n_axes, vmem_limit_bytes=vmem_mb * 1024 * 1024)


def _mod_row(i, tm):
    return jnp.where(i < N_LAT // tm, i // (S // tm), B)


def _rms(x):
    return x * lax.rsqrt(jnp.mean(x * x, axis=-1, keepdims=True) + EPS)


def _split_bf16(x):
    hi = x.astype(BF16)
    return hi, (x - hi.astype(F32)).astype(BF16)


def _router_split(rw):
    hi, lo = _split_bf16(jnp.pad(rw, ((0, 0), (0, 128 - N_EXP))))
    return jnp.stack([hi, lo])


MOD_TN = 1024


def _mod_body(cc_ref, w_ref, b_ref, o_ref):
    a = cc_ref[...]
    a = a * jax.nn.sigmoid(a)
    o_ref[...] = jnp.dot(a.astype(BF16), w_ref[...].astype(BF16),
                         preferred_element_type=F32) + b_ref[...]


def _modulation(cc, mod_w, mod_b):
    n_col = 6 * D // MOD_TN
    return pl.pallas_call(
        _mod_body,
        grid=(DEPTH, n_col),
        in_specs=[
            pl.BlockSpec((16, D), lambda l, j: (0, 0)),
            pl.BlockSpec((None, D, MOD_TN), lambda l, j: (l, 0, j)),
            pl.BlockSpec((None, 1, MOD_TN), lambda l, j: (l, 0, j)),
        ],
        out_specs=pl.BlockSpec((None, 16, MOD_TN), lambda l, j: (l, 0, j)),
        out_shape=jax.ShapeDtypeStruct((DEPTH, 16, 6 * D), F32),
        compiler_params=_cparams(2),
        name="modulation",
    )(cc, mod_w, mod_b.reshape(DEPTH, 1, 6 * D))


IN_TM = 512
ROT_W = A_W + KV_W


def _in_body(x_ref, m_ref, g_ref, w_ref, cos_ref, sin_ref, za_ref, zc_ref, uv_ref):
    x = x_ref[...]
    h = _rms(x) * g_ref[...]
    h = h * (1.0 + m_ref[:, D:2 * D]) + m_ref[:, 0:D]
    z = jnp.dot(h.astype(BF16), w_ref[...], preferred_element_type=F32)
    qk = z[:, :ROT_W]
    lane = lax.broadcasted_iota(I32, qk.shape, 1)
    rot = jnp.where((lane & 16) == 0,
                    -pltpu.roll(qk, ROT_W - 16, 1), pltpu.roll(qk, 16, 1))
    qk = qk * cos_ref[...] + rot * sin_ref[...]
    za_ref[...] = jnp.concatenate(
        [qk[:, :A_W] * Q_SCALE, qk[:, A_W:], z[:, ROT_W:ROT_W + KV_W]], axis=1).astype(BF16)
    u0 = ROT_W + KV_W
    uv_ref[...] = jax.nn.gelu(z[:, u0:u0 + 2 * B_W]).astype(BF16)
    c0 = u0 + 2 * B_W
    zc_ref[...] = jnp.concatenate(
        [z[:, c0:c0 + C_W] * Q_SCALE, z[:, c0 + C_W:]], axis=1).astype(BF16)


def _in_proj(xc, mods, g, w_bf, cos_t, sin_t):
    tm = IN_TM
    n_t = N_ALL // tm

    def tab_idx(i):
        return (jnp.where(i < N_LAT // tm, i % (S // tm), S // tm), 0)

    return pl.pallas_call(
        _in_body,
        grid=(n_t,),
        in_specs=[
            pl.BlockSpec((tm, D), lambda i: (i, 0)),
            pl.BlockSpec((None, 1, 6 * D), lambda i: (_mod_row(i, tm), 0, 0)),
            pl.BlockSpec((1, D), lambda i: (0, 0)),
            pl.BlockSpec((D, IN_W), lambda i: (0, 0)),
            pl.BlockSpec((tm, ROT_W), tab_idx),
            pl.BlockSpec((tm, ROT_W), tab_idx),
        ],
        out_specs=[
            pl.BlockSpec((tm, A_W + 2 * KV_W), lambda i: (i, 0)),
            pl.BlockSpec((tm, 3 * C_W), lambda i: (i, 0)),
            pl.BlockSpec((tm, 2 * B_W), lambda i: (i, 0)),
        ],
        out_shape=[
            jax.ShapeDtypeStruct((N_ALL, A_W + 2 * KV_W), BF16),
            jax.ShapeDtypeStruct((N_ALL, 3 * C_W), BF16),
            jax.ShapeDtypeStruct((N_ALL, 2 * B_W), BF16),
        ],
        compiler_params=_cparams(1),
        name="in_proj",
    )(xc, mods, g, w_bf, cos_t, sin_t)


A_BAND = A_BLK + 2 * A_WIN
N_QB_LAT = N_LAT // A_BLK
QB_PER_SEQ = S // A_BLK
QB_PER_CTX = C // A_BLK


def _softmax_pv(s_list, v_list, extra_logit=None):
    m = s_list[0].max(axis=-1, keepdims=True)
    for s in s_list[1:]:
        m = jnp.maximum(m, s.max(axis=-1, keepdims=True))
    if extra_logit is not None:
        m = jnp.maximum(m, extra_logit)
    den = None
    out = None
    for s, v in zip(s_list, v_list):
        p = jnp.exp2(s - m)
        d = p.sum(axis=-1, keepdims=True)
        o = jnp.dot(p.astype(BF16), v, preferred_element_type=F32)
        den = d if den is None else den + d
        out = o if out is None else out + o
    if extra_logit is not None:
        den = den + jnp.exp2(extra_logit - m)
    return out / den


def _attn_a_body(sink_ref, q_ref, kb_ref, vb_ref, kc_ref, vc_ref, o_ref):
    i = pl.program_id(0)
    is_lat = i < N_QB_LAT
    n = i % QB_PER_SEQ
    start = pl.multiple_of(jnp.clip(n * A_BLK - A_WIN, 0, S - A_BAND), A_WIN)
    qpos = n * A_BLK + lax.broadcasted_iota(I32, (A_BLK, A_BAND), 0)
    kpos = start + lax.broadcasted_iota(I32, (A_BLK, A_BAND), 1)
    band_penalty = jnp.where(jnp.abs(kpos - qpos) <= jnp.where(is_lat, A_WIN, -1), 0.0, NEG)
    kb = kb_ref[pl.ds(start, A_BAND), :]
    vb = vb_ref[pl.ds(start, A_BAND), :]
    kc = kc_ref[...]
    vc = vc_ref[...]
    outs = []
    for h in range(A_HEADS):
        kv = h // (A_HEADS // A_KV)
        sl = slice(kv * HD, (kv + 1) * HD)
        q = q_ref[:, h * HD:(h + 1) * HD]
        s_b = lax.dot_general(q, kb[:, sl], NT_DIMS, preferred_element_type=F32) + band_penalty
        s_c = lax.dot_general(q, kc[:, sl], NT_DIMS, preferred_element_type=F32)
        outs.append(_softmax_pv([s_c, s_b], [vc[:, sl], vb[:, sl]], sink_ref[h] * LOG2E))
    o_ref[...] = jnp.concatenate(outs, axis=1).astype(BF16)


def _attn_a(za, sink, with_ctx):
    n_qb = N_QB_LAT + (N_CTX // A_BLK if with_ctx else 0)

    def bidx(i):
        return jnp.where(i < N_QB_LAT, i // QB_PER_SEQ, (i - N_QB_LAT) // QB_PER_CTX)

    k_col = A_W // KV_W
    v_col = k_col + 1
    return pl.pallas_call(
        _attn_a_body,
        grid=(n_qb,),
        in_specs=[
            pl.BlockSpec(memory_space=pltpu.SMEM),
            pl.BlockSpec((A_BLK, A_W), lambda i: (i, 0)),
            pl.BlockSpec((S, KV_W), lambda i: (bidx(i), k_col)),
            pl.BlockSpec((S, KV_W), lambda i: (bidx(i), v_col)),
            pl.BlockSpec((C, KV_W), lambda i: (N_LAT // C + bidx(i), k_col)),
            pl.BlockSpec((C, KV_W), lambda i: (N_LAT // C + bidx(i), v_col)),
        ],
        out_specs=pl.BlockSpec((A_BLK, A_W), lambda i: (i, 0)),
        out_shape=jax.ShapeDtypeStruct((n_qb * A_BLK, A_W), BF16),
        compiler_params=_cparams(1),
        name="attn_window",
    )(sink, za, za, za, za, za)


NA_R = 4
NA_TQ = NA_R * GRID_W
NA_KROWS = 12
NA_WIN = NA_KROWS * GRID_W
NA_STEPS = ROWS // NA_R
NA_PAIRS = NA_KROWS // 2
N_DR = 2 * C_WIN_R - 1


def _na_body(q_ref, k_ref, v_ref, kc_ref, vc_ref, tab_ref, o_ref):
    j = pl.program_id(1)
    is_lat = j < NA_STEPS
    r0 = jnp.minimum(j, NA_STEPS - 1) * NA_R
    u0 = jnp.clip(r0 - C_WIN_R // 2, 0, ROWS - NA_KROWS)
    k0 = pl.multiple_of(u0 * GRID_W, GRID_W)
    kw = k_ref[pl.ds(k0, NA_WIN), :]
    vw = v_ref[pl.ds(k0, NA_WIN), :]
    kc = kc_ref[...]
    vc = vc_ref[...]
    q = q_ref[...]
    left = lax.broadcasted_iota(I32, (1, 2 * GRID_W), 1) < GRID_W

    tab_idx = []
    penalty = []
    for rr in range(NA_R):
        r = r0 + rr
        start = jnp.clip(r - C_WIN_R // 2, 0, ROWS - C_WIN_R)
        idx_row = []
        pen_row = []
        for p in range(NA_PAIRS):
            kr = u0 + 2 * p
            idx_row.append(jnp.clip(kr - r + (C_WIN_R - 1), -1, N_DR - 1) + 1)
            pens = []
            for half in range(2):
                ok = jnp.logical_and(is_lat, jnp.logical_and(kr + half >= start,
                                                             kr + half < start + C_WIN_R))
                pens.append(jnp.where(ok, 0.0, NEG))
            pen_row.append(jnp.where(left, pens[0], pens[1]))
        tab_idx.append(idx_row)
        penalty.append(jnp.concatenate(pen_row, axis=1))

    outs = []
    for h in range(C_HEADS):
        sl = slice(h * HD, (h + 1) * HD)
        qh = q[:, sl]
        s_w = lax.dot_general(qh, kw[:, sl], NT_DIMS, preferred_element_type=F32)
        bias = jnp.concatenate([
            jnp.concatenate([tab_ref[h, pl.ds(tab_idx[rr][p], 1)][0] for p in range(NA_PAIRS)],
                            axis=1) + penalty[rr]
            for rr in range(NA_R)], axis=0)
        s_c = lax.dot_general(qh, kc[:, sl], NT_DIMS, preferred_element_type=F32)
        outs.append(_softmax_pv([s_c, s_w + bias], [vc[:, sl], vw[:, sl]]))
    o_ref[...] = jnp.concatenate(outs, axis=1).astype(BF16)


def _na(zc, tab, with_ctx):
    n_j = NA_STEPS + (1 if with_ctx else 0)

    def qidx(b, j):
        return jnp.where(j < NA_STEPS, b * NA_STEPS + j, N_LAT // NA_TQ + b)

    n_out = N_LAT + (N_CTX if with_ctx else 0)
    return pl.pallas_call(
        _na_body,
        grid=(B, n_j),
        in_specs=[
            pl.BlockSpec((NA_TQ, C_W), lambda b, j: (qidx(b, j), 0)),
            pl.BlockSpec((S, C_W), lambda b, j: (b, 1)),
            pl.BlockSpec((S, C_W), lambda b, j: (b, 2)),
            pl.BlockSpec((C, C_W), lambda b, j: (N_LAT // C + b, 1)),
            pl.BlockSpec((C, C_W), lambda b, j: (N_LAT // C + b, 2)),
            pl.BlockSpec((C_HEADS, N_DR + 1, GRID_W, 2 * GRID_W), lambda b, j: (0, 0, 0, 0)),
        ],
        out_specs=pl.BlockSpec((NA_TQ, C_W), lambda b, j: (qidx(b, j), 0)),
        out_shape=jax.ShapeDtypeStruct((n_out, C_W), BF16),
        compiler_params=_cparams(2),
        name="attn_neighbourhood",
    )(zc, zc, zc, zc, zc, tab)


def _na_bias_table(rpb):
    cq = np.arange(GRID_W)
    col_start = np.clip(cq - C_WIN_C // 2, 0, GRID_W - C_WIN_C)
    col_ok = (cq[None, :] >= col_start[:, None]) & (cq[None, :] < col_start[:, None] + C_WIN_C)
    dc = np.clip(cq[None, :] - cq[:, None], -(C_WIN_C - 1), C_WIN_C - 1) + (C_WIN_C - 1)
    n_dc = 2 * C_WIN_C - 1
    pick = (dc.reshape(-1)[None, :] == np.arange(n_dc)[:, None]).astype(np.float32)
    t = jnp.dot(rpb.astype(F32).reshape(-1, n_dc), jnp.asarray(pick), precision=HIGHEST)
    t = t.reshape(C_HEADS, N_DR, GRID_W, GRID_W) * LOG2E
    t = jnp.where(col_ok[None, None], t, NEG)
    zero = jnp.zeros((C_HEADS, 1, GRID_W, GRID_W), F32)
    ext = jnp.concatenate([zero, t, zero], axis=1)
    return jnp.concatenate([ext[:, :-1], ext[:, 1:]], axis=-1)


MG_TM = 256


def _merge_body(x_ref, ya_ref, yc_ref, uv_ref, m_ref, gg_ref, sg_ref, sw_ref, sb_ref, bd_ref,
                wo_ref, g2_ref, rw_ref, xo_ref, hx_ref, lg_ref):
    u = uv_ref[:, :B_W].astype(F32)
    v = uv_ref[:, B_W:].astype(F32)
    vv_hi, vv_lo = _split_bf16(v * v)
    ms = (jnp.dot(vv_hi, bd_ref[...], preferred_element_type=F32)
          + jnp.dot(vv_lo, bd_ref[...], preferred_element_type=F32))
    vn = (v * lax.rsqrt(ms + EPS) * sg_ref[...]).astype(BF16)
    lane_grp = lax.broadcasted_iota(I32, (SGU_CHUNK, B_W), 1) // HD
    gates = []
    for c in range(MG_TM // SGU_CHUNK):
        vc = vn[c * SGU_CHUNK:(c + 1) * SGU_CHUNK, :]
        gate = sb_ref[...]
        acc = jnp.zeros((SGU_CHUNK, B_W), F32)
        for g in range(SGU_GROUPS):
            r = jnp.dot(sw_ref[g], vc, preferred_element_type=F32)
            acc = jnp.where(lane_grp == g, r, acc)
        gates.append(acc + gate)
    yb = u * jnp.concatenate(gates, axis=0)
    gg = gg_ref[...]
    ycat = jnp.concatenate([
        _rms(ya_ref[...].astype(F32)) * gg[:, :A_W],
        _rms(yb) * gg[:, A_W:A_W + B_W],
        _rms(yc_ref[...].astype(F32)) * gg[:, A_W + B_W:],
    ], axis=1)
    proj = jnp.dot(ycat.astype(BF16), wo_ref[...], preferred_element_type=F32)
    xn = x_ref[...] + m_ref[:, 2 * D:3 * D] * proj
    xo_ref[...] = xn
    hx = _rms(xn) * g2_ref[...]
    hx = hx * (1.0 + m_ref[:, 4 * D:5 * D]) + m_ref[:, 3 * D:4 * D]
    hx_ref[...] = hx.astype(BF16)
    hx_hi, hx_lo = _split_bf16(hx)
    logits = (jnp.dot(hx_hi, rw_ref[0], preferred_element_type=F32)
              + jnp.dot(hx_hi, rw_ref[1], preferred_element_type=F32)
              + jnp.dot(hx_lo, rw_ref[0], preferred_element_type=F32))
    lg_ref[...] = logits.T[:N_EXP, :]


def _merge(xc, ya, yc, uv, mods, gg, sg, sw_bf, sb_tab, bd, wo_bf, g2, rw_t, n_rows):
    tm = MG_TM
    n_t = n_rows // tm
    const2 = lambda i: (0, 0)
    row = lambda i: (i, 0)
    return pl.pallas_call(
        _merge_body,
        grid=(n_t,),
        in_specs=[
            pl.BlockSpec((tm, D), row),
            pl.BlockSpec((tm, A_W), row),
            pl.BlockSpec((tm, C_W), row),
            pl.BlockSpec((tm, 2 * B_W), row),
            pl.BlockSpec((None, 1, 6 * D), lambda i: (_mod_row(i, tm), 0, 0)),
            pl.BlockSpec((1, D), const2),
            pl.BlockSpec((1, B_W), const2),
            pl.BlockSpec((SGU_GROUPS, SGU_CHUNK, SGU_CHUNK), lambda i: (0, 0, 0)),
            pl.BlockSpec((SGU_CHUNK, B_W), const2),
            pl.BlockSpec((B_W, B_W), const2),
            pl.BlockSpec((D, D), const2),
            pl.BlockSpec((1, D), const2),
            pl.BlockSpec((2, D, 128), lambda i: (0, 0, 0)),
        ],
        out_specs=[
            pl.BlockSpec((tm, D), row),
            pl.BlockSpec((tm, D), row),
            pl.BlockSpec((N_EXP, tm), lambda i: (0, i)),
        ],
        out_shape=[
            jax.ShapeDtypeStruct((n_rows, D), F32),
            jax.ShapeDtypeStruct((n_rows, D), BF16),
            jax.ShapeDtypeStruct((N_EXP, n_rows), F32),
        ],
        compiler_params=_cparams(1),
        name="merge",
    )(xc, ya, yc, uv, mods, gg, sg, sw_bf, sb_tab, bd, wo_bf, g2, rw_t)


RT_TM = 256
GRP_SZ = N_EXP // N_GRP
SLOT_ALIGN = 16


def _first_argmax(v, iota):
    m = v.max(axis=0, keepdims=True)
    idx = jnp.where(v == m, iota, float(v.shape[0])).min(axis=0, keepdims=True)
    return m, idx


def _stack_rows(rows, iota):
    out = jnp.zeros(iota.shape, F32)
    for r, v in enumerate(rows):
        out = jnp.where(iota == float(r), v, out)
    return out


def _route_body(lg_ref, rb_ref, w_ref, lp_ref, tc_ref):
    tm = RT_TM
    scores = jax.nn.sigmoid(lg_ref[...])
    sel = scores + rb_ref[...]
    iota_g = lax.broadcasted_iota(I32, (GRP_SZ, tm), 0).astype(F32)
    gs = []
    for g in range(N_GRP):
        v = sel[g * GRP_SZ:(g + 1) * GRP_SZ, :]
        m1, i1 = _first_argmax(v, iota_g)
        m2 = jnp.where(iota_g == i1, -jnp.inf, v).max(axis=0, keepdims=True)
        gs.append(m1 + m2)
    iota_n = lax.broadcasted_iota(I32, (N_GRP, tm), 0).astype(F32)
    gscore = _stack_rows(gs, iota_n)
    gsel = jnp.zeros((N_GRP, tm), F32)
    for _ in range(TOPK_GRP):
        _, gi = _first_argmax(gscore, iota_n)
        hit = iota_n == gi
        gsel = jnp.where(hit, 1.0, gsel)
        gscore = jnp.where(hit, -jnp.inf, gscore)
    emask = jnp.concatenate(
        [jnp.broadcast_to(gsel[g:g + 1, :], (GRP_SZ, tm)) for g in range(N_GRP)], axis=0)
    cand = jnp.where(emask > 0.5, sel, NEG)
    iota_e = lax.broadcasted_iota(I32, (N_EXP, tm), 0).astype(F32)
    hits = []
    ws = []
    member = jnp.zeros((N_EXP, tm), F32)
    for _ in range(TOP_K):
        _, ei = _first_argmax(cand, iota_e)
        hit = iota_e == ei
        hits.append(hit)
        ws.append(jnp.where(hit, scores, 0.0).sum(axis=0, keepdims=True))
        member = jnp.where(hit, 1.0, member)
        cand = jnp.where(hit, -jnp.inf, cand)
    wsum = ws[0]
    for w in ws[1:]:
        wsum = wsum + w
    iota_k = lax.broadcasted_iota(I32, (TOP_K, tm), 0).astype(F32)
    w_ref[...] = _stack_rows(ws, iota_k) / wsum * ROUTED_SCALE
    r_i = lax.broadcasted_iota(I32, (tm, tm), 0)
    c_i = lax.broadcasted_iota(I32, (tm, tm), 1)
    tri = jnp.where(r_i < c_i, 1.0, 0.0).astype(BF16)
    local = jnp.dot(member.astype(BF16), tri, preferred_element_type=F32)
    tile_cnt = member.sum(axis=1, keepdims=True)
    tc_ref[...] = tile_cnt
    aligned = jnp.ceil(tile_cnt / SLOT_ALIGN) * SLOT_ALIGN
    e_r = lax.broadcasted_iota(I32, (N_EXP, N_EXP), 0)
    e_c = lax.broadcasted_iota(I32, (N_EXP, N_EXP), 1)
    below = jnp.where(e_c < e_r, 1.0, 0.0)
    start = jnp.dot(below, jnp.broadcast_to(aligned, (N_EXP, 128)), precision=HIGHEST,
                    preferred_element_type=F32)[:, 0:1]
    pos = local + start
    lp_ref[...] = _stack_rows(
        [jnp.where(hit, pos, 0.0).sum(axis=0, keepdims=True) for hit in hits],
        iota_k).astype(I32)


def _route(logits_t, rbias, n_tok):
    tm = RT_TM
    tok = lambda i: (0, i)
    return pl.pallas_call(
        _route_body,
        grid=(n_tok // tm,),
        in_specs=[
            pl.BlockSpec((N_EXP, tm), tok),
            pl.BlockSpec((N_EXP, 1), lambda i: (0, 0)),
        ],
        out_specs=[
            pl.BlockSpec((TOP_K, tm), tok),
            pl.BlockSpec((TOP_K, tm), tok),
            pl.BlockSpec((None, N_EXP, 1), lambda i: (i, 0, 0)),
        ],
        out_shape=[
            jax.ShapeDtypeStruct((TOP_K, n_tok), F32),
            jax.ShapeDtypeStruct((TOP_K, n_tok), I32),
            jax.ShapeDtypeStruct((n_tok // tm, N_EXP, 1), F32),
        ],
        compiler_params=_cparams(1),
        name="route",
    )(logits_t, rbias)


DP_TM = RT_TM
SORT_ROWS = 3072
N_GRPS = SORT_ROWS // SLOT_ALIGN
ISSUE_UNROLL = 4
WAIT_GROUPS = 32
N_BURSTS = 6
BURST_ROWS = SORT_ROWS // N_BURSTS
BURST_GRPS = BURST_ROWS // SLOT_ALIGN


def _start_groups(g_lo, g_hi, row_of, make_copy):
    n_full = (g_hi - g_lo) // ISSUE_UNROLL

    def body(q, carry):
        for s in range(ISSUE_UNROLL):
            g = g_lo + q * ISSUE_UNROLL + s
            make_copy(g, row_of(g)).start(priority=s % 2)
        return carry

    def tail(g, carry):
        make_copy(g, row_of(g)).start()
        return carry

    lax.fori_loop(0, n_full, body, 0)
    lax.fori_loop(g_lo + n_full * ISSUE_UNROLL, g_hi, tail, 0)


def _start_burst(q, ngrp, row_of, make_copy):
    _start_groups(jnp.minimum(q * BURST_GRPS, ngrp), jnp.minimum((q + 1) * BURST_GRPS, ngrp),
                  row_of, make_copy)


def _wait_groups(ngrp, make_copy, make_bulk):
    def bulk(q, carry):
        make_bulk().wait()
        return carry

    def single(g, carry):
        make_copy(0, 0).wait()
        return carry

    n_bulk = ngrp // WAIT_GROUPS
    lax.fori_loop(0, n_bulk, bulk, 0)
    lax.fori_loop(n_bulk * WAIT_GROUPS, ngrp, single, 0)


def _dispatch_body(ngrp_ref, fill_ref, grow_ref, hx_ref, lp_ref, xs_ref, zbuf_ref, zero_ref, sems):
    i = pl.program_id(0)
    n_tiles = pl.num_programs(0)
    slot = i % 2

    def zero_copy(b):
        dst = xs_ref.at[pl.ds(pl.multiple_of(b * MOE_BLK, MOE_BLK), MOE_BLK)]
        return pltpu.make_async_copy(zero_ref, dst, sems.at[0])

    @pl.when(i == 0)
    def _():
        zero_ref[...] = jnp.zeros_like(zero_ref)
        n_fill = fill_ref[0]

        def z_start(q, carry):
            @pl.when(fill_ref[1 + q] >= 0)
            def _():
                zero_copy(fill_ref[1 + q]).start()
            return carry

        def z_wait(q, carry):
            @pl.when(fill_ref[1 + q] >= 0)
            def _():
                zero_copy(0).wait()
            return carry

        lax.fori_loop(0, n_fill, z_start, 0)
        lax.fori_loop(0, n_fill, z_wait, 0)

    def group_copy(s, g, row):
        src = zbuf_ref.at[s, pl.ds(pl.multiple_of(g * SLOT_ALIGN, SLOT_ALIGN), SLOT_ALIGN)]
        dst = xs_ref.at[pl.ds(pl.multiple_of(row, SLOT_ALIGN), SLOT_ALIGN)]
        return pltpu.make_async_copy(src, dst, sems.at[s])

    def bulk_copy(s):
        n_rows = WAIT_GROUPS * SLOT_ALIGN
        return pltpu.make_async_copy(
            zbuf_ref.at[s, pl.ds(0, n_rows)], xs_ref.at[pl.ds(0, n_rows)], sems.at[s])

    x = hx_ref[...]
    lpos = lp_ref[...]
    n_cur = ngrp_ref[i]
    pos = [jnp.tile(jnp.broadcast_to(lpos[k:k + 1, :], (16, DP_TM)).astype(jnp.int16),
                    (BURST_ROWS // 16, 1)) for k in range(TOP_K)]
    for q in range(N_BURSTS):
        @pl.when(q * BURST_GRPS < n_cur)
        def _():
            row_iota = (lax.broadcasted_iota(I32, (BURST_ROWS, DP_TM), 0)
                        + q * BURST_ROWS).astype(jnp.int16)
            onehot = jnp.zeros((BURST_ROWS, DP_TM), BF16)
            for k in range(TOP_K):
                onehot = jnp.where(row_iota == pos[k], jnp.ones((), BF16), onehot)
            z = jnp.dot(onehot, x, preferred_element_type=F32)
            zbuf_ref[slot, pl.ds(q * BURST_ROWS, BURST_ROWS), :] = z.astype(BF16)
            _start_burst(q, n_cur, lambda g: grow_ref[0, g],
                         lambda g, r: group_copy(slot, g, r))

    @pl.when(i > 0)
    def _():
        _wait_groups(ngrp_ref[jnp.maximum(i - 1, 0)], lambda g, r: group_copy(1 - slot, 0, 0),
                     lambda: bulk_copy(1 - slot))

    @pl.when(i == n_tiles - 1)
    def _():
        _wait_groups(n_cur, lambda g, r: group_copy(slot, 0, 0), lambda: bulk_copy(slot))


def _dispatch(ngrp, fill, grp_row, hx, lpos, n_tok, n_slots):
    return pl.pallas_call(
        _dispatch_body,
        grid_spec=pltpu.PrefetchScalarGridSpec(
            num_scalar_prefetch=2,
            grid=(n_tok // DP_TM,),
            in_specs=[
                pl.BlockSpec((None, 1, N_GRPS), lambda i, *_: (i, 0, 0), memory_space=pltpu.SMEM),
                pl.BlockSpec((DP_TM, D), lambda i, *_: (i, 0)),
                pl.BlockSpec((TOP_K, DP_TM), lambda i, *_: (0, i)),
            ],
            out_specs=pl.BlockSpec(memory_space=pl.ANY),
            scratch_shapes=[
                pltpu.VMEM((2, SORT_ROWS, D), BF16),
                pltpu.VMEM((MOE_BLK, D), BF16),
                pltpu.SemaphoreType.DMA((2,)),
            ],
        ),
        out_shape=jax.ShapeDtypeStruct((n_slots, D), BF16),
        compiler_params=_cparams(1, vmem_mb=56),
        name="moe_dispatch",
    )(ngrp, fill, grp_row, hx, lpos)


def _experts_body(bexp_ref, xblk_ref, nused_ref, xs_ref, wg_ref, wu_ref, wd_ref, ys_ref):
    j = pl.program_id(0)

    @pl.when(j < nused_ref[0])
    def _():
        x = xs_ref[...]
        g = jnp.dot(x, wg_ref[...].astype(BF16), preferred_element_type=F32)
        u = jnp.dot(x, wu_ref[...].astype(BF16), preferred_element_type=F32)
        h = (g * jax.nn.sigmoid(g) * u).astype(BF16)
        y = jnp.dot(h, wd_ref[...].astype(BF16), preferred_element_type=F32)
        ys_ref[...] = y.astype(BF16)


def _experts(bexp, xblk, nused, xs, wg, wu, wd, n_blk, layer):
    w_idx = lambda j, be, xb, nu: (layer, be[j], 0, 0)
    return pl.pallas_call(
        _experts_body,
        grid_spec=pltpu.PrefetchScalarGridSpec(
            num_scalar_prefetch=3,
            grid=(n_blk,),
            in_specs=[
                pl.BlockSpec((MOE_BLK, D), lambda j, be, xb, nu: (xb[j], 0)),
                pl.BlockSpec((None, None, D, EXP_FF), w_idx),
                pl.BlockSpec((None, None, D, EXP_FF), w_idx),
                pl.BlockSpec((None, None, EXP_FF, D), w_idx),
            ],
            out_specs=pl.BlockSpec((MOE_BLK, D), lambda j, be, xb, nu: (xb[j], 0)),
        ),
        out_shape=jax.ShapeDtypeStruct((n_blk * MOE_BLK, D), BF16),
        input_output_aliases={3: 0},
        compiler_params=_cparams(1),
        name="moe_experts",
    )(bexp, xblk, nused, xs, wg, wu, wd)


def _combine_body(ngrp_ref, grow_ref, grow_next_ref, ys_ref, lp_ref, w_ref, x_ref, hx_ref, m_ref,
                  sg_ref, su_ref, sd_ref, fg_ref, o_ref, ybuf_ref, acc_ref, sems, *, final_norm):
    i = pl.program_id(0)
    n_tiles = pl.num_programs(0)
    slot = i % 2

    def group_copy(s, g, row):
        src = ys_ref.at[pl.ds(pl.multiple_of(row, SLOT_ALIGN), SLOT_ALIGN)]
        dst = ybuf_ref.at[s, pl.ds(pl.multiple_of(g * SLOT_ALIGN, SLOT_ALIGN), SLOT_ALIGN)]
        return pltpu.make_async_copy(src, dst, sems.at[s])

    def bulk_copy(s):
        n_rows = WAIT_GROUPS * SLOT_ALIGN
        return pltpu.make_async_copy(
            ys_ref.at[pl.ds(0, n_rows)], ybuf_ref.at[s, pl.ds(0, n_rows)], sems.at[s])

    @pl.when(i == 0)
    def _():
        ybuf_ref[...] = jnp.zeros_like(ybuf_ref)
        _start_groups(0, ngrp_ref[0], lambda g: grow_ref[0, g], lambda g, r: group_copy(0, g, r))

    n_next = jnp.where(i + 1 < n_tiles, ngrp_ref[jnp.minimum(i + 1, n_tiles - 1)], 0)

    def prefetch(q):
        _start_burst(q, n_next, lambda g: grow_next_ref[0, g],
                     lambda g, r: group_copy(1 - slot, g, r))

    hx = hx_ref[...]
    g = jnp.dot(hx, sg_ref[...], preferred_element_type=F32)
    u = jnp.dot(hx, su_ref[...], preferred_element_type=F32)
    h = (g * jax.nn.sigmoid(g) * u).astype(BF16)
    acc_ref[...] = jnp.dot(h, sd_ref[...], preferred_element_type=F32)

    lpos = lp_ref[...]
    w = w_ref[...]
    n_own = ngrp_ref[i]
    _wait_groups(n_own, lambda g, r: group_copy(slot, 0, 0), lambda: bulk_copy(slot))
    lane_tiles = BURST_ROWS // 128
    pos = [jnp.tile(jnp.broadcast_to(lpos[:, k:k + 1], (DP_TM, 128)).astype(jnp.int16),
                    (1, lane_tiles)) for k in range(TOP_K)]
    wts = [jnp.tile(jnp.broadcast_to(w[:, k:k + 1], (DP_TM, 128)).astype(BF16),
                    (1, lane_tiles)) for k in range(TOP_K)]
    for q in range(N_BURSTS):
        @pl.when(q * BURST_GRPS < n_own)
        def _():
            col_iota = (lax.broadcasted_iota(I32, (DP_TM, BURST_ROWS), 1)
                        + q * BURST_ROWS).astype(jnp.int16)
            unsort = jnp.zeros((DP_TM, BURST_ROWS), BF16)
            for k in range(TOP_K):
                unsort = jnp.where(col_iota == pos[k], wts[k], unsort)
            acc_ref[...] += jnp.dot(unsort, ybuf_ref[slot, pl.ds(q * BURST_ROWS, BURST_ROWS), :],
                                    preferred_element_type=F32)
        prefetch(q)
    out = x_ref[...] + m_ref[:, 5 * D:6 * D] * acc_ref[...]
    if final_norm:
        out = _rms(out) * fg_ref[...]
    o_ref[...] = out


def _combine(ngrp, grp_row, ys, lpos_t, w_t, x, hx, mods, sg_bf, su_bf, sd_bf, fg, n_tok, final_norm):
    tm = DP_TM
    row = lambda i, *_: (i, 0)
    const2 = lambda i, *_: (0, 0)
    return pl.pallas_call(
        functools.partial(_combine_body, final_norm=final_norm),
        grid_spec=pltpu.PrefetchScalarGridSpec(
            num_scalar_prefetch=1,
            grid=(n_tok // tm,),
            in_specs=[
                pl.BlockSpec((None, 1, N_GRPS), lambda i, *_: (i, 0, 0), memory_space=pltpu.SMEM),
                pl.BlockSpec((None, 1, N_GRPS), lambda i, *_: (jnp.minimum(i + 1, n_tok // tm - 1), 0, 0),
                             memory_space=pltpu.SMEM),
                pl.BlockSpec(memory_space=pl.ANY),
                pl.BlockSpec((tm, TOP_K), row),
                pl.BlockSpec((tm, TOP_K), row),
                pl.BlockSpec((tm, D), row),
                pl.BlockSpec((tm, D), row),
                pl.BlockSpec((None, 1, 6 * D), lambda i, *_: (_mod_row(i, tm), 0, 0)),
                pl.BlockSpec((D, EXP_FF), const2),
                pl.BlockSpec((D, EXP_FF), const2),
                pl.BlockSpec((EXP_FF, D), const2),
                pl.BlockSpec((1, D), const2),
            ],
            out_specs=pl.BlockSpec((tm, D), row),
            scratch_shapes=[pltpu.VMEM((2, SORT_ROWS, D), BF16), pltpu.VMEM((tm, D), F32),
                            pltpu.SemaphoreType.DMA((2,))],
        ),
        out_shape=jax.ShapeDtypeStruct((n_tok, D), F32),
        compiler_params=_cparams(1, vmem_mb=56),
        name="moe_combine",
    )(ngrp, grp_row, grp_row, ys, lpos_t, w_t, x, hx, mods, sg_bf, su_bf, sd_bf, fg)


def _moe(x_new, hx, logits_t, mods, rbias, wg, wu, wd, sg_bf, su_bf, sd_bf, fg, n_tok, layer,
         final_norm):
    w, lpos, tile_cnt = _route(logits_t, rbias, n_tok)
    n_tiles = n_tok // DP_TM
    n_blk = (n_tok * TOP_K + (SLOT_ALIGN - 1) * N_EXP * n_tiles) // MOE_BLK + N_EXP
    tcnt = tile_cnt[:, :, 0].astype(I32)
    cnt_al = (tcnt + SLOT_ALIGN - 1) // SLOT_ALIGN * SLOT_ALIGN
    loc_end = jnp.cumsum(cnt_al, axis=1)
    loc = loc_end - cnt_al
    slots_e = jnp.sum(cnt_al, axis=0)
    nblk_e = (slots_e + MOE_BLK - 1) // MOE_BLK
    blk_end = jnp.cumsum(nblk_e)
    pstart = (blk_end - nblk_e) * MOE_BLK
    nused = blk_end[-1:].astype(I32)
    blk_ids = jnp.arange(n_blk, dtype=I32)
    xblk = jnp.minimum(blk_ids, nused[0] - 1)
    bexp = jnp.minimum(
        jnp.sum(blk_end[None, :] <= xblk[:, None], axis=1), N_EXP - 1).astype(I32)
    off = pstart[None, :] + jnp.cumsum(cnt_al, axis=0) - cnt_al
    g_row = jnp.arange(N_GRPS, dtype=I32) * SLOT_ALIGN
    e_of_g = jnp.minimum(
        jnp.sum(loc_end[:, None, :] <= g_row[None, :, None], axis=2), N_EXP - 1)
    pick = e_of_g[..., None] == jnp.arange(N_EXP, dtype=I32)
    grp_row = (jnp.sum(jnp.where(pick, (off - loc)[:, None, :], 0), axis=2)
               + g_row[None, :]).astype(I32).reshape(n_tiles, 1, N_GRPS)
    ngrp = (loc_end[:, -1] // SLOT_ALIGN).astype(I32)
    unused = nused[0] + blk_ids
    fill = jnp.concatenate([
        (N_EXP + n_blk - nused[0])[None],
        jnp.where(nblk_e > 0, blk_end - 1, -1),
        jnp.where(unused < n_blk, unused, -1)]).astype(I32)
    xs = _dispatch(ngrp, fill, grp_row, hx, lpos, n_tok, n_blk * MOE_BLK)
    ys = _experts(bexp, xblk, nused, xs, wg, wu, wd, n_blk, layer)
    return _combine(ngrp, grp_row, ys, lpos.T, w.T, x_new, hx, mods, sg_bf, su_bf, sd_bf, fg,
                    n_tok, final_norm)


def _rope_tables():
    t = np.arange(S)
    row = (t // GRID_W).astype(np.float32)
    col = (t % GRID_W).astype(np.float32)
    half = HD // 2
    inv = jnp.asarray(ROPE_BASE, F32) ** (-jnp.arange(0, half, 2, dtype=F32) / half)
    ang_r = jnp.asarray(row)[:, None] * inv
    ang_c = jnp.asarray(col)[:, None] * inv
    ang = jnp.concatenate([ang_r, ang_r, ang_c, ang_c], axis=-1)
    n_rep = ROT_W // HD
    cos = jnp.tile(jnp.cos(ang), (1, n_rep))
    sin = jnp.tile(jnp.sin(ang), (1, n_rep))
    cos = jnp.concatenate([cos, jnp.ones((IN_TM, ROT_W), F32)], axis=0)
    sin = jnp.concatenate([sin, jnp.zeros((IN_TM, ROT_W), F32)], axis=0)
    return cos, sin


def kernel(x, c, ctx, c_ctx, mod_w, mod_b, norm1_g, w_in, attn_sink, sgu_norm_g, sgu_w, sgu_b, na_rpb,
           group_norm_g, w_out, norm2_g, router_w, router_bias, exp_w_gate, exp_w_up, exp_w_down,
           shared_w_gate, shared_w_up, shared_w_down, final_g):
    xc = jnp.concatenate([x.reshape(N_LAT, D), ctx.reshape(N_CTX, D)], axis=0)
    cc = jnp.concatenate([c, c_ctx[None, :], jnp.zeros((16 - B - 1, D), F32)], axis=0)
    mods_all = _modulation(cc, mod_w, mod_b)
    cos_t, sin_t = _rope_tables()
    seg = np.arange(B_W) // HD
    bd = jnp.asarray((seg[:, None] == seg[None, :]).astype(np.float32) / HD, BF16)
    fg = final_g.reshape(1, D)

    out = None
    for l in range(DEPTH):
        last = l == DEPTH - 1
        mods = mods_all[l].reshape(16, 1, 6 * D)
        za, zc, uv = _in_proj(xc, mods, norm1_g[l].reshape(1, D), w_in[l].astype(BF16), cos_t, sin_t)
        ya = _attn_a(za, attn_sink[l], with_ctx=not last)
        yc = _na(zc, _na_bias_table(na_rpb[l]), with_ctx=not last)
        n_tok = N_LAT if last else N_ALL
        sb_tab = jnp.repeat(sgu_b[l].T, HD, axis=1)
        x_new, hx, logits_t = _merge(
            xc, ya, yc, uv, mods, group_norm_g[l].reshape(1, D), sgu_norm_g[l].reshape(1, B_W),
            sgu_w[l].astype(BF16), sb_tab, bd, w_out[l].astype(BF16), norm2_g[l].reshape(1, D),
            _router_split(router_w[l]), n_tok)
        res = _moe(x_new, hx, logits_t, mods, router_bias[l].reshape(N_EXP, 1),
                   exp_w_gate, exp_w_up, exp_w_down,
                   shared_w_gate[l].astype(BF16), shared_w_up[l].astype(BF16),
                   shared_w_down[l].astype(BF16), fg, n_tok, layer=l, final_norm=last)
        if last:
            out = res.reshape(B, S, D)
        else:
            xc = res
    return out
```

```python
import functools
import math

import jax
import jax.numpy as jnp
import numpy as np
from jax import lax
from jax.experimental import pallas as pl
from jax.experimental.pallas import tpu as pltpu

F32 = jnp.float32
BF16 = jnp.bfloat16
I32 = jnp.int32

D = 1024
B = 8
S = 2048
C = 256
DEPTH = 2
GRID_W = 64
HD = 64
A_HEADS = 6
A_KV = 2
A_WIN = 128
A_BLK = 256
ROPE_BASE = 10000.0
SGU_GROUPS = 4
SGU_CHUNK = 128
C_HEADS = 6
C_WIN_R = 8
C_WIN_C = 16
A_W = A_HEADS * HD
B_W = SGU_GROUPS * HD
C_W = C_HEADS * HD
KV_W = A_KV * HD
IN_W = A_W + 2 * KV_W + 2 * B_W + 3 * C_W
N_EXP = 64
TOP_K = 8
N_GRP = 8
TOPK_GRP = 4
EXP_FF = 256
ROUTED_SCALE = 2.5
MOE_BLK = 1024
EPS = 1e-6
NEG = -1e30
SCALE = HD ** -0.5
LOG2E = math.log2(math.e)
Q_SCALE = SCALE * LOG2E

N_LAT = B * S
N_CTX = B * C
N_ALL = N_LAT + N_CTX
ROWS = S // GRID_W

HIGHEST = lax.Precision.HIGHEST
ARB = pltpu.ARBITRARY

NT_DIMS = (((1,), (1,)), ((), ()))


def _cparams(n_axes, vmem_mb=48):
    return pltpu.CompilerParams(
        dimension_semantics=(ARB,) * n_axes, vmem_limit_bytes=vmem_mb * 1024 * 1024)


def _mod_row(i, tm):
    return jnp.where(i < N_LAT // tm, i // (S // tm), B)


def _rms(x):
    return x * lax.rsqrt(jnp.mean(x * x, axis=-1, keepdims=True) + EPS)


def _token_specs(tm, ctx_first_tile):
    n_lat = N_LAT // tm
    lat = pl.BlockSpec((tm, D), lambda i: (jnp.minimum(i, n_lat - 1), 0))
    ctx = pl.BlockSpec((tm, D), lambda i: (ctx_first_tile + jnp.maximum(i - n_lat, 0), 0))
    return lat, ctx


def _token_tile(lat_ref, ctx_ref):
    tm = lat_ref.shape[0]
    n_from_lat = jnp.where(pl.program_id(0) < N_LAT // tm, tm, 0)
    from_lat = lax.broadcasted_iota(I32, lat_ref.shape, 0) < n_from_lat
    return jnp.where(from_lat, lat_ref[...], ctx_ref[...])


def _split_bf16(x):
    hi = x.astype(BF16)
    return hi, (x - hi.astype(F32)).astype(BF16)


def _router_split(rw):
    hi, lo = _split_bf16(jnp.pad(rw, ((0, 0), (0, 128 - N_EXP))))
    return jnp.stack([hi, lo])


MOD_TN = 1024


def _mod_body(cc_ref, w_ref, b_ref, o_ref):
    a = cc_ref[...]
    a = a * jax.nn.sigmoid(a)
    o_ref[...] = jnp.dot(a.astype(BF16), w_ref[...].astype(BF16),
                         preferred_element_type=F32) + b_ref[...]


def _modulation(cc, mod_w, mod_b):
    n_col = 6 * D // MOD_TN
    return pl.pallas_call(
        _mod_body,
        grid=(DEPTH, n_col),
        in_specs=[
            pl.BlockSpec((16, D), lambda l, j: (0, 0)),
            pl.BlockSpec((None, D, MOD_TN), lambda l, j: (l, 0, j)),
            pl.BlockSpec((None, 1, MOD_TN), lambda l, j: (l, 0, j)),
        ],
        out_specs=pl.BlockSpec((None, 16, MOD_TN), lambda l, j: (l, 0, j)),
        out_shape=jax.ShapeDtypeStruct((DEPTH, 16, 6 * D), F32),
        compiler_params=_cparams(2),
        name="modulation",
    )(cc, mod_w, mod_b.reshape(DEPTH, 1, 6 * D))


IN_TM = 512
ROT_W = A_W + KV_W


def _in_body(xl_ref, xc_ref, m_ref, g_ref, w_ref, cos_ref, sin_ref, za_ref, zc_ref, uv_ref):
    x = _token_tile(xl_ref, xc_ref)
    h = _rms(x) * g_ref[...]
    h = h * (1.0 + m_ref[:, D:2 * D]) + m_ref[:, 0:D]
    z = jnp.dot(h.astype(BF16), w_ref[...], preferred_element_type=F32)
    qk = z[:, :ROT_W]
    lane = lax.broadcasted_iota(I32, qk.shape, 1)
    rot = jnp.where((lane & 16) == 0,
                    -pltpu.roll(qk, ROT_W - 16, 1), pltpu.roll(qk, 16, 1))
    qk = qk * cos_ref[...] + rot * sin_ref[...]
    za_ref[...] = jnp.concatenate(
        [qk[:, :A_W] * Q_SCALE, qk[:, A_W:], z[:, ROT_W:ROT_W + KV_W]], axis=1).astype(BF16)
    u0 = ROT_W + KV_W
    uv_ref[...] = jax.nn.gelu(z[:, u0:u0 + 2 * B_W]).astype(BF16)
    c0 = u0 + 2 * B_W
    zc_ref[...] = jnp.concatenate(
        [z[:, c0:c0 + C_W] * Q_SCALE, z[:, c0 + C_W:]], axis=1).astype(BF16)


def _in_proj(x_lat, x_ctx, ctx_first_row, mods, g, w_bf, cos_t, sin_t):
    tm = IN_TM
    n_t = N_ALL // tm

    def tab_idx(i):
        return (jnp.where(i < N_LAT // tm, i % (S // tm), S // tm), 0)

    return pl.pallas_call(
        _in_body,
        grid=(n_t,),
        in_specs=[
            *_token_specs(tm, ctx_first_row // tm),
            pl.BlockSpec((None, 1, 6 * D), lambda i: (_mod_row(i, tm), 0, 0)),
            pl.BlockSpec((1, D), lambda i: (0, 0)),
            pl.BlockSpec((D, IN_W), lambda i: (0, 0)),
            pl.BlockSpec((tm, ROT_W), tab_idx),
            pl.BlockSpec((tm, ROT_W), tab_idx),
        ],
        out_specs=[
            pl.BlockSpec((tm, A_W + 2 * KV_W), lambda i: (i, 0)),
            pl.BlockSpec((tm, 3 * C_W), lambda i: (i, 0)),
            pl.BlockSpec((tm, 2 * B_W), lambda i: (i, 0)),
        ],
        out_shape=[
            jax.ShapeDtypeStruct((N_ALL, A_W + 2 * KV_W), BF16),
            jax.ShapeDtypeStruct((N_ALL, 3 * C_W), BF16),
            jax.ShapeDtypeStruct((N_ALL, 2 * B_W), BF16),
        ],
        compiler_params=_cparams(1),
        name="in_proj",
    )(x_lat, x_ctx, mods, g, w_bf, cos_t, sin_t)


A_BAND = A_BLK + 2 * A_WIN
N_QB_LAT = N_LAT // A_BLK
QB_PER_SEQ = S // A_BLK
QB_PER_CTX = C // A_BLK


def _softmax_pv(s_list, v_list, extra_logit=None):
    m = s_list[0].max(axis=-1, keepdims=True)
    for s in s_list[1:]:
        m = jnp.maximum(m, s.max(axis=-1, keepdims=True))
    if extra_logit is not None:
        m = jnp.maximum(m, extra_logit)
    den = None
    out = None
    for s, v in zip(s_list, v_list):
        p = jnp.exp2(s - m)
        d = p.sum(axis=-1, keepdims=True)
        o = jnp.dot(p.astype(BF16), v, preferred_element_type=F32)
        den = d if den is None else den + d
        out = o if out is None else out + o
    if extra_logit is not None:
        den = den + jnp.exp2(extra_logit - m)
    return out / den


def _attn_a_body(sink_ref, q_ref, kb_ref, vb_ref, kc_ref, vc_ref, o_ref):
    i = pl.program_id(0)
    is_lat = i < N_QB_LAT
    n = i % QB_PER_SEQ
    start = pl.multiple_of(jnp.clip(n * A_BLK - A_WIN, 0, S - A_BAND), A_WIN)
    qpos = n * A_BLK + lax.broadcasted_iota(I32, (A_BLK, A_BAND), 0)
    kpos = start + lax.broadcasted_iota(I32, (A_BLK, A_BAND), 1)
    band_penalty = jnp.where(jnp.abs(kpos - qpos) <= jnp.where(is_lat, A_WIN, -1), 0.0, NEG)
    kb = kb_ref[pl.ds(start, A_BAND), :]
    vb = vb_ref[pl.ds(start, A_BAND), :]
    kc = kc_ref[...]
    vc = vc_ref[...]
    outs = []
    for h in range(A_HEADS):
        kv = h // (A_HEADS // A_KV)
        sl = slice(kv * HD, (kv + 1) * HD)
        q = q_ref[:, h * HD:(h + 1) * HD]
        s_b = lax.dot_general(q, kb[:, sl], NT_DIMS, preferred_element_type=F32) + band_penalty
        s_c = lax.dot_general(q, kc[:, sl], NT_DIMS, preferred_element_type=F32)
        outs.append(_softmax_pv([s_c, s_b], [vc[:, sl], vb[:, sl]], sink_ref[h] * LOG2E))
    o_ref[...] = jnp.concatenate(outs, axis=1).astype(BF16)


def _attn_a(za, sink, with_ctx):
    n_qb = N_QB_LAT + (N_CTX // A_BLK if with_ctx else 0)

    def bidx(i):
        return jnp.where(i < N_QB_LAT, i // QB_PER_SEQ, (i - N_QB_LAT) // QB_PER_CTX)

    k_col = A_W // KV_W
    v_col = k_col + 1
    return pl.pallas_call(
        _attn_a_body,
        grid=(n_qb,),
        in_specs=[
            pl.BlockSpec(memory_space=pltpu.SMEM),
            pl.BlockSpec((A_BLK, A_W), lambda i: (i, 0)),
            pl.BlockSpec((S, KV_W), lambda i: (bidx(i), k_col)),
            pl.BlockSpec((S, KV_W), lambda i: (bidx(i), v_col)),
            pl.BlockSpec((C, KV_W), lambda i: (N_LAT // C + bidx(i), k_col)),
            pl.BlockSpec((C, KV_W), lambda i: (N_LAT // C + bidx(i), v_col)),
        ],
        out_specs=pl.BlockSpec((A_BLK, A_W), lambda i: (i, 0)),
        out_shape=jax.ShapeDtypeStruct((n_qb * A_BLK, A_W), BF16),
        compiler_params=_cparams(1),
        name="attn_window",
    )(sink, za, za, za, za, za)


NA_R = 4
NA_TQ = NA_R * GRID_W
NA_KROWS = 12
NA_WIN = NA_KROWS * GRID_W
NA_STEPS = ROWS // NA_R
NA_PAIRS = NA_KROWS // 2
N_DR = 2 * C_WIN_R - 1


def _na_body(q_ref, k_ref, v_ref, kc_ref, vc_ref, tab_ref, o_ref):
    j = pl.program_id(1)
    is_lat = j < NA_STEPS
    r0 = jnp.minimum(j, NA_STEPS - 1) * NA_R
    u0 = jnp.clip(r0 - C_WIN_R // 2, 0, ROWS - NA_KROWS)
    k0 = pl.multiple_of(u0 * GRID_W, GRID_W)
    kw = k_ref[pl.ds(k0, NA_WIN), :]
    vw = v_ref[pl.ds(k0, NA_WIN), :]
    kc = kc_ref[...]
    vc = vc_ref[...]
    q = q_ref[...]
    left = lax.broadcasted_iota(I32, (1, 2 * GRID_W), 1) < GRID_W

    tab_idx = []
    penalty = []
    for rr in range(NA_R):
        r = r0 + rr
        start = jnp.clip(r - C_WIN_R // 2, 0, ROWS - C_WIN_R)
        idx_row = []
        pen_row = []
        for p in range(NA_PAIRS):
            kr = u0 + 2 * p
            idx_row.append(jnp.clip(kr - r + (C_WIN_R - 1), -1, N_DR - 1) + 1)
            pens = []
            for half in range(2):
                ok = jnp.logical_and(is_lat, jnp.logical_and(kr + half >= start,
                                                             kr + half < start + C_WIN_R))
                pens.append(jnp.where(ok, 0.0, NEG))
            pen_row.append(jnp.where(left, pens[0], pens[1]))
        tab_idx.append(idx_row)
        penalty.append(jnp.concatenate(pen_row, axis=1))

    outs = []
    for h in range(C_HEADS):
        sl = slice(h * HD, (h + 1) * HD)
        qh = q[:, sl]
        s_w = lax.dot_general(qh, kw[:, sl], NT_DIMS, preferred_element_type=F32)
        bias = jnp.concatenate([
            jnp.concatenate([tab_ref[h, pl.ds(tab_idx[rr][p], 1)][0] for p in range(NA_PAIRS)],
                            axis=1) + penalty[rr]
            for rr in range(NA_R)], axis=0)
        s_c = lax.dot_general(qh, kc[:, sl], NT_DIMS, preferred_element_type=F32)
        outs.append(_softmax_pv([s_c, s_w + bias], [vc[:, sl], vw[:, sl]]))
    o_ref[...] = jnp.concatenate(outs, axis=1).astype(BF16)


def _na(zc, tab, with_ctx):
    n_j = NA_STEPS + (1 if with_ctx else 0)

    def qidx(b, j):
        return jnp.where(j < NA_STEPS, b * NA_STEPS + j, N_LAT // NA_TQ + b)

    n_out = N_LAT + (N_CTX if with_ctx else 0)
    return pl.pallas_call(
        _na_body,
        grid=(B, n_j),
        in_specs=[
            pl.BlockSpec((NA_TQ, C_W), lambda b, j: (qidx(b, j), 0)),
            pl.BlockSpec((S, C_W), lambda b, j: (b, 1)),
            pl.BlockSpec((S, C_W), lambda b, j: (b, 2)),
            pl.BlockSpec((C, C_W), lambda b, j: (N_LAT // C + b, 1)),
            pl.BlockSpec((C, C_W), lambda b, j: (N_LAT // C + b, 2)),
            pl.BlockSpec((C_HEADS, N_DR + 1, GRID_W, 2 * GRID_W), lambda b, j: (0, 0, 0, 0)),
        ],
        out_specs=pl.BlockSpec((NA_TQ, C_W), lambda b, j: (qidx(b, j), 0)),
        out_shape=jax.ShapeDtypeStruct((n_out, C_W), BF16),
        compiler_params=_cparams(2),
        name="attn_neighbourhood",
    )(zc, zc, zc, zc, zc, tab)


def _na_bias_table(rpb):
    cq = np.arange(GRID_W)
    col_start = np.clip(cq - C_WIN_C // 2, 0, GRID_W - C_WIN_C)
    col_ok = (cq[None, :] >= col_start[:, None]) & (cq[None, :] < col_start[:, None] + C_WIN_C)
    dc = np.clip(cq[None, :] - cq[:, None], -(C_WIN_C - 1), C_WIN_C - 1) + (C_WIN_C - 1)
    n_dc = 2 * C_WIN_C - 1
    pick = (dc.reshape(-1)[None, :] == np.arange(n_dc)[:, None]).astype(np.float32)
    t = jnp.dot(rpb.astype(F32).reshape(-1, n_dc), jnp.asarray(pick), precision=HIGHEST)
    t = t.reshape(C_HEADS, N_DR, GRID_W, GRID_W) * LOG2E
    t = jnp.where(col_ok[None, None], t, NEG)
    zero = jnp.zeros((C_HEADS, 1, GRID_W, GRID_W), F32)
    ext = jnp.concatenate([zero, t, zero], axis=1)
    return jnp.concatenate([ext[:, :-1], ext[:, 1:]], axis=-1)


MG_TM = 256


def _merge_body(xl_ref, xc_ref, ya_ref, yc_ref, uv_ref, m_ref, gg_ref, sg_ref, sw_ref, sb_ref, bd_ref,
                wo_ref, g2_ref, rw_ref, xo_ref, hx_ref, lg_ref):
    u = uv_ref[:, :B_W].astype(F32)
    v = uv_ref[:, B_W:].astype(F32)
    vv_hi, vv_lo = _split_bf16(v * v)
    ms = (jnp.dot(vv_hi, bd_ref[...], preferred_element_type=F32)
          + jnp.dot(vv_lo, bd_ref[...], preferred_element_type=F32))
    vn = (v * lax.rsqrt(ms + EPS) * sg_ref[...]).astype(BF16)
    lane_grp = lax.broadcasted_iota(I32, (SGU_CHUNK, B_W), 1) // HD
    gates = []
    for c in range(MG_TM // SGU_CHUNK):
        vc = vn[c * SGU_CHUNK:(c + 1) * SGU_CHUNK, :]
        gate = sb_ref[...]
        acc = jnp.zeros((SGU_CHUNK, B_W), F32)
        for g in range(SGU_GROUPS):
            r = jnp.dot(sw_ref[g], vc, preferred_element_type=F32)
            acc = jnp.where(lane_grp == g, r, acc)
        gates.append(acc + gate)
    yb = u * jnp.concatenate(gates, axis=0)
    gg = gg_ref[...]
    ycat = jnp.concatenate([
        _rms(ya_ref[...].astype(F32)) * gg[:, :A_W],
        _rms(yb) * gg[:, A_W:A_W + B_W],
        _rms(yc_ref[...].astype(F32)) * gg[:, A_W + B_W:],
    ], axis=1)
    proj = jnp.dot(ycat.astype(BF16), wo_ref[...], preferred_element_type=F32)
    xn = _token_tile(xl_ref, xc_ref) + m_ref[:, 2 * D:3 * D] * proj
    xo_ref[...] = xn
    hx = _rms(xn) * g2_ref[...]
    hx = hx * (1.0 + m_ref[:, 4 * D:5 * D]) + m_ref[:, 3 * D:4 * D]
    hx_ref[...] = hx.astype(BF16)
    hx_hi, hx_lo = _split_bf16(hx)
    logits = (jnp.dot(hx_hi, rw_ref[0], preferred_element_type=F32)
              + jnp.dot(hx_hi, rw_ref[1], preferred_element_type=F32)
              + jnp.dot(hx_lo, rw_ref[0], preferred_element_type=F32))
    lg_ref[...] = logits.T[:N_EXP, :]


def _merge(x_lat, x_ctx, ctx_first_row, ya, yc, uv, mods, gg, sg, sw_bf, sb_tab, bd, wo_bf, g2, rw_t,
           n_rows):
    tm = MG_TM
    n_t = n_rows // tm
    const2 = lambda i: (0, 0)
    row = lambda i: (i, 0)
    return pl.pallas_call(
        _merge_body,
        grid=(n_t,),
        in_specs=[
            *_token_specs(tm, ctx_first_row // tm),
            pl.BlockSpec((tm, A_W), row),
            pl.BlockSpec((tm, C_W), row),
            pl.BlockSpec((tm, 2 * B_W), row),
            pl.BlockSpec((None, 1, 6 * D), lambda i: (_mod_row(i, tm), 0, 0)),
            pl.BlockSpec((1, D), const2),
            pl.BlockSpec((1, B_W), const2),
            pl.BlockSpec((SGU_GROUPS, SGU_CHUNK, SGU_CHUNK), lambda i: (0, 0, 0)),
            pl.BlockSpec((SGU_CHUNK, B_W), const2),
            pl.BlockSpec((B_W, B_W), const2),
            pl.BlockSpec((D, D), const2),
            pl.BlockSpec((1, D), const2),
            pl.BlockSpec((2, D, 128), lambda i: (0, 0, 0)),
        ],
        out_specs=[
            pl.BlockSpec((tm, D), row),
            pl.BlockSpec((tm, D), row),
            pl.BlockSpec((N_EXP, tm), lambda i: (0, i)),
        ],
        out_shape=[
            jax.ShapeDtypeStruct((n_rows, D), F32),
            jax.ShapeDtypeStruct((n_rows, D), BF16),
            jax.ShapeDtypeStruct((N_EXP, n_rows), F32),
        ],
        compiler_params=_cparams(1),
        name="merge",
    )(x_lat, x_ctx, ya, yc, uv, mods, gg, sg, sw_bf, sb_tab, bd, wo_bf, g2, rw_t)


RT_TM = 256
GRP_SZ = N_EXP // N_GRP
SLOT_ALIGN = 16


def _first_argmax(v, iota):
    m = v.max(axis=0, keepdims=True)
    idx = jnp.where(v == m, iota, float(v.shape[0])).min(axis=0, keepdims=True)
    return m, idx


def _stack_rows(rows, iota):
    out = jnp.zeros(iota.shape, F32)
    for r, v in enumerate(rows):
        out = jnp.where(iota == float(r), v, out)
    return out


def _route_body(lg_ref, rb_ref, w_ref, lp_ref, tc_ref):
    tm = RT_TM
    scores = jax.nn.sigmoid(lg_ref[...])
    sel = scores + rb_ref[...]
    iota_g = lax.broadcasted_iota(I32, (GRP_SZ, tm), 0).astype(F32)
    gs = []
    for g in range(N_GRP):
        v = sel[g * GRP_SZ:(g + 1) * GRP_SZ, :]
        m1, i1 = _first_argmax(v, iota_g)
        m2 = jnp.where(iota_g == i1, -jnp.inf, v).max(axis=0, keepdims=True)
        gs.append(m1 + m2)
    iota_n = lax.broadcasted_iota(I32, (N_GRP, tm), 0).astype(F32)
    gscore = _stack_rows(gs, iota_n)
    gsel = jnp.zeros((N_GRP, tm), F32)
    for _ in range(TOPK_GRP):
        _, gi = _first_argmax(gscore, iota_n)
        hit = iota_n == gi
        gsel = jnp.where(hit, 1.0, gsel)
        gscore = jnp.where(hit, -jnp.inf, gscore)
    emask = jnp.concatenate(
        [jnp.broadcast_to(gsel[g:g + 1, :], (GRP_SZ, tm)) for g in range(N_GRP)], axis=0)
    cand = jnp.where(emask > 0.5, sel, NEG)
    iota_e = lax.broadcasted_iota(I32, (N_EXP, tm), 0).astype(F32)
    hits = []
    ws = []
    member = jnp.zeros((N_EXP, tm), F32)
    for _ in range(TOP_K):
        _, ei = _first_argmax(cand, iota_e)
        hit = iota_e == ei
        hits.append(hit)
        ws.append(jnp.where(hit, scores, 0.0).sum(axis=0, keepdims=True))
        member = jnp.where(hit, 1.0, member)
        cand = jnp.where(hit, -jnp.inf, cand)
    wsum = ws[0]
    for w in ws[1:]:
        wsum = wsum + w
    iota_k = lax.broadcasted_iota(I32, (TOP_K, tm), 0).astype(F32)
    w_ref[...] = _stack_rows(ws, iota_k) / wsum * ROUTED_SCALE
    r_i = lax.broadcasted_iota(I32, (tm, tm), 0)
    c_i = lax.broadcasted_iota(I32, (tm, tm), 1)
    tri = jnp.where(r_i < c_i, 1.0, 0.0).astype(BF16)
    local = jnp.dot(member.astype(BF16), tri, preferred_element_type=F32)
    tile_cnt = member.sum(axis=1, keepdims=True)
    tc_ref[...] = tile_cnt
    aligned = jnp.ceil(tile_cnt / SLOT_ALIGN) * SLOT_ALIGN
    e_r = lax.broadcasted_iota(I32, (N_EXP, N_EXP), 0)
    e_c = lax.broadcasted_iota(I32, (N_EXP, N_EXP), 1)
    below = jnp.where(e_c < e_r, 1.0, 0.0)
    start = jnp.dot(below, jnp.broadcast_to(aligned, (N_EXP, 128)), precision=HIGHEST,
                    preferred_element_type=F32)[:, 0:1]
    pos = local + start
    lp_ref[...] = _stack_rows(
        [jnp.where(hit, pos, 0.0).sum(axis=0, keepdims=True) for hit in hits],
        iota_k).astype(I32)


def _route(logits_t, rbias, n_tok):
    tm = RT_TM
    tok = lambda i: (0, i)
    return pl.pallas_call(
        _route_body,
        grid=(n_tok // tm,),
        in_specs=[
            pl.BlockSpec((N_EXP, tm), tok),
            pl.BlockSpec((N_EXP, 1), lambda i: (0, 0)),
        ],
        out_specs=[
            pl.BlockSpec((TOP_K, tm), tok),
            pl.BlockSpec((TOP_K, tm), tok),
            pl.BlockSpec((None, N_EXP, 1), lambda i: (i, 0, 0)),
        ],
        out_shape=[
            jax.ShapeDtypeStruct((TOP_K, n_tok), F32),
            jax.ShapeDtypeStruct((TOP_K, n_tok), I32),
            jax.ShapeDtypeStruct((n_tok // tm, N_EXP, 1), F32),
        ],
        compiler_params=_cparams(1),
        name="route",
    )(logits_t, rbias)


DP_TM = RT_TM
SORT_ROWS = 3072
N_GRPS = SORT_ROWS // SLOT_ALIGN
ISSUE_UNROLL = 4
WAIT_GROUPS = 32
N_BURSTS = 6
BURST_ROWS = SORT_ROWS // N_BURSTS
BURST_GRPS = BURST_ROWS // SLOT_ALIGN


def _start_groups(g_lo, g_hi, row_of, make_copy):
    n_full = (g_hi - g_lo) // ISSUE_UNROLL

    def body(q, carry):
        for s in range(ISSUE_UNROLL):
            g = g_lo + q * ISSUE_UNROLL + s
            make_copy(g, row_of(g)).start(priority=s % 2)
        return carry

    def tail(g, carry):
        make_copy(g, row_of(g)).start()
        return carry

    lax.fori_loop(0, n_full, body, 0)
    lax.fori_loop(g_lo + n_full * ISSUE_UNROLL, g_hi, tail, 0)


def _start_burst(q, ngrp, row_of, make_copy):
    _start_groups(jnp.minimum(q * BURST_GRPS, ngrp), jnp.minimum((q + 1) * BURST_GRPS, ngrp),
                  row_of, make_copy)


def _wait_groups(ngrp, make_copy, make_bulk):
    def bulk(q, carry):
        make_bulk().wait()
        return carry

    def single(g, carry):
        make_copy(0, 0).wait()
        return carry

    n_bulk = ngrp // WAIT_GROUPS
    lax.fori_loop(0, n_bulk, bulk, 0)
    lax.fori_loop(n_bulk * WAIT_GROUPS, ngrp, single, 0)


def _dispatch_body(ngrp_ref, fill_ref, grow_ref, hx_ref, lp_ref, xs_ref, zbuf_ref, zero_ref, sems):
    i = pl.program_id(0)
    n_tiles = pl.num_programs(0)
    slot = i % 2

    def zero_copy(b):
        dst = xs_ref.at[pl.ds(pl.multiple_of(b * MOE_BLK, MOE_BLK), MOE_BLK)]
        return pltpu.make_async_copy(zero_ref, dst, sems.at[0])

    @pl.when(i == 0)
    def _():
        zero_ref[...] = jnp.zeros_like(zero_ref)
        n_fill = fill_ref[0]

        def z_start(q, carry):
            @pl.when(fill_ref[1 + q] >= 0)
            def _():
                zero_copy(fill_ref[1 + q]).start()
            return carry

        def z_wait(q, carry):
            @pl.when(fill_ref[1 + q] >= 0)
            def _():
                zero_copy(0).wait()
            return carry

        lax.fori_loop(0, n_fill, z_start, 0)
        lax.fori_loop(0, n_fill, z_wait, 0)

    def group_copy(s, g, row):
        src = zbuf_ref.at[s, pl.ds(pl.multiple_of(g * SLOT_ALIGN, SLOT_ALIGN), SLOT_ALIGN)]
        dst = xs_ref.at[pl.ds(pl.multiple_of(row, SLOT_ALIGN), SLOT_ALIGN)]
        return pltpu.make_async_copy(src, dst, sems.at[s])

    def bulk_copy(s):
        n_rows = WAIT_GROUPS * SLOT_ALIGN
        return pltpu.make_async_copy(
            zbuf_ref.at[s, pl.ds(0, n_rows)], xs_ref.at[pl.ds(0, n_rows)], sems.at[s])

    x = hx_ref[...]
    lpos = lp_ref[...]
    n_cur = ngrp_ref[i]
    pos = [jnp.tile(jnp.broadcast_to(lpos[k:k + 1, :], (16, DP_TM)).astype(jnp.int16),
                    (BURST_ROWS // 16, 1)) for k in range(TOP_K)]
    for q in range(N_BURSTS):
        @pl.when(q * BURST_GRPS < n_cur)
        def _():
            row_iota = (lax.broadcasted_iota(I32, (BURST_ROWS, DP_TM), 0)
                        + q * BURST_ROWS).astype(jnp.int16)
            onehot = jnp.zeros((BURST_ROWS, DP_TM), BF16)
            for k in range(TOP_K):
                onehot = jnp.where(row_iota == pos[k], jnp.ones((), BF16), onehot)
            z = jnp.dot(onehot, x, preferred_element_type=F32)
            zbuf_ref[slot, pl.ds(q * BURST_ROWS, BURST_ROWS), :] = z.astype(BF16)
            _start_burst(q, n_cur, lambda g: grow_ref[0, g],
                         lambda g, r: group_copy(slot, g, r))

    @pl.when(i > 0)
    def _():
        _wait_groups(ngrp_ref[jnp.maximum(i - 1, 0)], lambda g, r: group_copy(1 - slot, 0, 0),
                     lambda: bulk_copy(1 - slot))

    @pl.when(i == n_tiles - 1)
    def _():
        _wait_groups(n_cur, lambda g, r: group_copy(slot, 0, 0), lambda: bulk_copy(slot))


def _dispatch(ngrp, fill, grp_row, hx, lpos, n_tok, n_slots):
    return pl.pallas_call(
        _dispatch_body,
        grid_spec=pltpu.PrefetchScalarGridSpec(
            num_scalar_prefetch=2,
            grid=(n_tok // DP_TM,),
            in_specs=[
                pl.BlockSpec((None, 1, N_GRPS), lambda i, *_: (i, 0, 0), memory_space=pltpu.SMEM),
                pl.BlockSpec((DP_TM, D), lambda i, *_: (i, 0)),
                pl.BlockSpec((TOP_K, DP_TM), lambda i, *_: (0, i)),
            ],
            out_specs=pl.BlockSpec(memory_space=pl.ANY),
            scratch_shapes=[
                pltpu.VMEM((2, SORT_ROWS, D), BF16),
                pltpu.VMEM((MOE_BLK, D), BF16),
                pltpu.SemaphoreType.DMA((2,)),
            ],
        ),
        out_shape=jax.ShapeDtypeStruct((n_slots, D), BF16),
        compiler_params=_cparams(1, vmem_mb=56),
        name="moe_dispatch",
    )(ngrp, fill, grp_row, hx, lpos)


def _experts_body(bexp_ref, xblk_ref, nused_ref, xs_ref, wg_ref, wu_ref, wd_ref, ys_ref):
    j = pl.program_id(0)

    @pl.when(j < nused_ref[0])
    def _():
        x = xs_ref[...]
        g = jnp.dot(x, wg_ref[...].astype(BF16), preferred_element_type=F32)
        u = jnp.dot(x, wu_ref[...].astype(BF16), preferred_element_type=F32)
        h = (g * jax.nn.sigmoid(g) * u).astype(BF16)
        y = jnp.dot(h, wd_ref[...].astype(BF16), preferred_element_type=F32)
        ys_ref[...] = y.astype(BF16)


def _experts(bexp, xblk, nused, xs, wg, wu, wd, n_blk, layer):
    w_idx = lambda j, be, xb, nu: (layer, be[j], 0, 0)
    return pl.pallas_call(
        _experts_body,
        grid_spec=pltpu.PrefetchScalarGridSpec(
            num_scalar_prefetch=3,
            grid=(n_blk,),
            in_specs=[
                pl.BlockSpec((MOE_BLK, D), lambda j, be, xb, nu: (xb[j], 0)),
                pl.BlockSpec((None, None, D, EXP_FF), w_idx),
                pl.BlockSpec((None, None, D, EXP_FF), w_idx),
                pl.BlockSpec((None, None, EXP_FF, D), w_idx),
            ],
            out_specs=pl.BlockSpec((MOE_BLK, D), lambda j, be, xb, nu: (xb[j], 0)),
        ),
        out_shape=jax.ShapeDtypeStruct((n_blk * MOE_BLK, D), BF16),
        input_output_aliases={3: 0},
        compiler_params=_cparams(1),
        name="moe_experts",
    )(bexp, xblk, nused, xs, wg, wu, wd)


def _combine_body(ngrp_ref, grow_ref, grow_next_ref, ys_ref, lp_ref, w_ref, x_ref, hx_ref, m_ref,
                  sg_ref, su_ref, sd_ref, fg_ref, o_ref, ybuf_ref, acc_ref, sems, *, final_norm):
    i = pl.program_id(0)
    n_tiles = pl.num_programs(0)
    slot = i % 2

    def group_copy(s, g, row):
        src = ys_ref.at[pl.ds(pl.multiple_of(row, SLOT_ALIGN), SLOT_ALIGN)]
        dst = ybuf_ref.at[s, pl.ds(pl.multiple_of(g * SLOT_ALIGN, SLOT_ALIGN), SLOT_ALIGN)]
        return pltpu.make_async_copy(src, dst, sems.at[s])

    def bulk_copy(s):
        n_rows = WAIT_GROUPS * SLOT_ALIGN
        return pltpu.make_async_copy(
            ys_ref.at[pl.ds(0, n_rows)], ybuf_ref.at[s, pl.ds(0, n_rows)], sems.at[s])

    @pl.when(i == 0)
    def _():
        ybuf_ref[...] = jnp.zeros_like(ybuf_ref)
        _start_groups(0, ngrp_ref[0], lambda g: grow_ref[0, g], lambda g, r: group_copy(0, g, r))

    n_next = jnp.where(i + 1 < n_tiles, ngrp_ref[jnp.minimum(i + 1, n_tiles - 1)], 0)

    def prefetch(q):
        _start_burst(q, n_next, lambda g: grow_next_ref[0, g],
                     lambda g, r: group_copy(1 - slot, g, r))

    hx = hx_ref[...]
    g = jnp.dot(hx, sg_ref[...], preferred_element_type=F32)
    u = jnp.dot(hx, su_ref[...], preferred_element_type=F32)
    h = (g * jax.nn.sigmoid(g) * u).astype(BF16)
    acc_ref[...] = jnp.dot(h, sd_ref[...], preferred_element_type=F32)

    lpos = lp_ref[...]
    w = w_ref[...]
    n_own = ngrp_ref[i]
    _wait_groups(n_own, lambda g, r: group_copy(slot, 0, 0), lambda: bulk_copy(slot))
    lane_tiles = BURST_ROWS // 128
    pos = [jnp.tile(jnp.broadcast_to(lpos[:, k:k + 1], (DP_TM, 128)).astype(jnp.int16),
                    (1, lane_tiles)) for k in range(TOP_K)]
    wts = [jnp.tile(jnp.broadcast_to(w[:, k:k + 1], (DP_TM, 128)).astype(BF16),
                    (1, lane_tiles)) for k in range(TOP_K)]
    for q in range(N_BURSTS):
        @pl.when(q * BURST_GRPS < n_own)
        def _():
            col_iota = (lax.broadcasted_iota(I32, (DP_TM, BURST_ROWS), 1)
                        + q * BURST_ROWS).astype(jnp.int16)
            unsort = jnp.zeros((DP_TM, BURST_ROWS), BF16)
            for k in range(TOP_K):
                unsort = jnp.where(col_iota == pos[k], wts[k], unsort)
            acc_ref[...] += jnp.dot(unsort, ybuf_ref[slot, pl.ds(q * BURST_ROWS, BURST_ROWS), :],
                                    preferred_element_type=F32)
        prefetch(q)
    out = x_ref[...] + m_ref[:, 5 * D:6 * D] * acc_ref[...]
    if final_norm:
        out = _rms(out) * fg_ref[...]
    o_ref[...] = out


def _combine(ngrp, grp_row, ys, lpos_t, w_t, x, hx, mods, sg_bf, su_bf, sd_bf, fg, n_tok, final_norm):
    tm = DP_TM
    row = lambda i, *_: (i, 0)
    const2 = lambda i, *_: (0, 0)
    return pl.pallas_call(
        functools.partial(_combine_body, final_norm=final_norm),
        grid_spec=pltpu.PrefetchScalarGridSpec(
            num_scalar_prefetch=1,
            grid=(n_tok // tm,),
            in_specs=[
                pl.BlockSpec((None, 1, N_GRPS), lambda i, *_: (i, 0, 0), memory_space=pltpu.SMEM),
                pl.BlockSpec((None, 1, N_GRPS), lambda i, *_: (jnp.minimum(i + 1, n_tok // tm - 1), 0, 0),
                             memory_space=pltpu.SMEM),
                pl.BlockSpec(memory_space=pl.ANY),
                pl.BlockSpec((tm, TOP_K), row),
                pl.BlockSpec((tm, TOP_K), row),
                pl.BlockSpec((tm, D), row),
                pl.BlockSpec((tm, D), row),
                pl.BlockSpec((None, 1, 6 * D), lambda i, *_: (_mod_row(i, tm), 0, 0)),
                pl.BlockSpec((D, EXP_FF), const2),
                pl.BlockSpec((D, EXP_FF), const2),
                pl.BlockSpec((EXP_FF, D), const2),
                pl.BlockSpec((1, D), const2),
            ],
            out_specs=pl.BlockSpec((tm, D), row),
            scratch_shapes=[pltpu.VMEM((2, SORT_ROWS, D), BF16), pltpu.VMEM((tm, D), F32),
                            pltpu.SemaphoreType.DMA((2,))],
        ),
        out_shape=jax.ShapeDtypeStruct((n_tok, D), F32),
        compiler_params=_cparams(1, vmem_mb=56),
        name="moe_combine",
    )(ngrp, grp_row, grp_row, ys, lpos_t, w_t, x, hx, mods, sg_bf, su_bf, sd_bf, fg)


def _moe(x_new, hx, logits_t, mods, rbias, wg, wu, wd, sg_bf, su_bf, sd_bf, fg, n_tok, layer,
         final_norm):
    w, lpos, tile_cnt = _route(logits_t, rbias, n_tok)
    n_tiles = n_tok // DP_TM
    n_blk = (n_tok * TOP_K + (SLOT_ALIGN - 1) * N_EXP * n_tiles) // MOE_BLK + N_EXP
    tcnt = tile_cnt[:, :, 0].astype(I32)
    cnt_al = (tcnt + SLOT_ALIGN - 1) // SLOT_ALIGN * SLOT_ALIGN
    loc_end = jnp.cumsum(cnt_al, axis=1)
    loc = loc_end - cnt_al
    slots_e = jnp.sum(cnt_al, axis=0)
    nblk_e = (slots_e + MOE_BLK - 1) // MOE_BLK
    blk_end = jnp.cumsum(nblk_e)
    pstart = (blk_end - nblk_e) * MOE_BLK
    nused = blk_end[-1:].astype(I32)
    blk_ids = jnp.arange(n_blk, dtype=I32)
    xblk = jnp.minimum(blk_ids, nused[0] - 1)
    bexp = jnp.minimum(
        jnp.sum(blk_end[None, :] <= xblk[:, None], axis=1), N_EXP - 1).astype(I32)
    off = pstart[None, :] + jnp.cumsum(cnt_al, axis=0) - cnt_al
    g_row = jnp.arange(N_GRPS, dtype=I32) * SLOT_ALIGN
    e_of_g = jnp.minimum(
        jnp.sum(loc_end[:, None, :] <= g_row[None, :, None], axis=2), N_EXP - 1)
    pick = e_of_g[..., None] == jnp.arange(N_EXP, dtype=I32)
    grp_row = (jnp.sum(jnp.where(pick, (off - loc)[:, None, :], 0), axis=2)
               + g_row[None, :]).astype(I32).reshape(n_tiles, 1, N_GRPS)
    ngrp = (loc_end[:, -1] // SLOT_ALIGN).astype(I32)
    unused = nused[0] + blk_ids
    fill = jnp.concatenate([
        (N_EXP + n_blk - nused[0])[None],
        jnp.where(nblk_e > 0, blk_end - 1, -1),
        jnp.where(unused < n_blk, unused, -1)]).astype(I32)
    xs = _dispatch(ngrp, fill, grp_row, hx, lpos, n_tok, n_blk * MOE_BLK)
    ys = _experts(bexp, xblk, nused, xs, wg, wu, wd, n_blk, layer)
    return _combine(ngrp, grp_row, ys, lpos.T, w.T, x_new, hx, mods, sg_bf, su_bf, sd_bf, fg,
                    n_tok, final_norm)


def _rope_tables():
    t = np.arange(S)
    row = (t // GRID_W).astype(np.float32)
    col = (t % GRID_W).astype(np.float32)
    half = HD // 2
    inv = jnp.asarray(ROPE_BASE, F32) ** (-jnp.arange(0, half, 2, dtype=F32) / half)
    ang_r = jnp.asarray(row)[:, None] * inv
    ang_c = jnp.asarray(col)[:, None] * inv
    ang = jnp.concatenate([ang_r, ang_r, ang_c, ang_c], axis=-1)
    n_rep = ROT_W // HD
    cos = jnp.tile(jnp.cos(ang), (1, n_rep))
    sin = jnp.tile(jnp.sin(ang), (1, n_rep))
    cos = jnp.concatenate([cos, jnp.ones((IN_TM, ROT_W), F32)], axis=0)
    sin = jnp.concatenate([sin, jnp.zeros((IN_TM, ROT_W), F32)], axis=0)
    return cos, sin


def kernel(x, c, ctx, c_ctx, mod_w, mod_b, norm1_g, w_in, attn_sink, sgu_norm_g, sgu_w, sgu_b, na_rpb,
           group_norm_g, w_out, norm2_g, router_w, router_bias, exp_w_gate, exp_w_up, exp_w_down,
           shared_w_gate, shared_w_up, shared_w_down, final_g):
    tokens = (x.reshape(N_LAT, D), ctx.reshape(N_CTX, D), 0)
    cc = jnp.concatenate([c, c_ctx[None, :], jnp.zeros((16 - B - 1, D), F32)], axis=0)
    mods_all = _modulation(cc, mod_w, mod_b)
    cos_t, sin_t = _rope_tables()
    seg = np.arange(B_W) // HD
    bd = jnp.asarray((seg[:, None] == seg[None, :]).astype(np.float32) / HD, BF16)
    fg = final_g.reshape(1, D)

    out = None
    for l in range(DEPTH):
        last = l == DEPTH - 1
        mods = mods_all[l].reshape(16, 1, 6 * D)
        za, zc, uv = _in_proj(*tokens, mods, norm1_g[l].reshape(1, D), w_in[l].astype(BF16), cos_t, sin_t)
        ya = _attn_a(za, attn_sink[l], with_ctx=not last)
        yc = _na(zc, _na_bias_table(na_rpb[l]), with_ctx=not last)
        n_tok = N_LAT if last else N_ALL
        sb_tab = jnp.repeat(sgu_b[l].T, HD, axis=1)
        x_new, hx, logits_t = _merge(
            *tokens, ya, yc, uv, mods, group_norm_g[l].reshape(1, D), sgu_norm_g[l].reshape(1, B_W),
            sgu_w[l].astype(BF16), sb_tab, bd, w_out[l].astype(BF16), norm2_g[l].reshape(1, D),
            _router_split(router_w[l]), n_tok)
        res = _moe(x_new, hx, logits_t, mods, router_bias[l].reshape(N_EXP, 1),
                   exp_w_gate, exp_w_up, exp_w_down,
                   shared_w_gate[l].astype(BF16), shared_w_up[l].astype(BF16),
                   shared_w_down[l].astype(BF16), fg, n_tok, layer=l, final_norm=last)
        if last:
            out = res.reshape(B, S, D)
        else:
            tokens = (res, res, N_LAT)
    return out
```

```python
import functools
import math

import jax
import jax.numpy as jnp
import numpy as np
from jax import lax
from jax.experimental import pallas as pl
from jax.experimental.pallas import tpu as pltpu

F32 = jnp.float32
BF16 = jnp.bfloat16
I32 = jnp.int32

D = 1024
B = 8
S = 2048
C = 256
DEPTH = 2
GRID_W = 64
HD = 64
A_HEADS = 6
A_KV = 2
A_WIN = 128
A_BLK = 256
ROPE_BASE = 10000.0
SGU_GROUPS = 4
SGU_CHUNK = 128
C_HEADS = 6
C_WIN_R = 8
C_WIN_C = 16
A_W = A_HEADS * HD
B_W = SGU_GROUPS * HD
C_W = C_HEADS * HD
KV_W = A_KV * HD
IN_W = A_W + 2 * KV_W + 2 * B_W + 3 * C_W
N_EXP = 64
TOP_K = 8
N_GRP = 8
TOPK_GRP = 4
EXP_FF = 256
ROUTED_SCALE = 2.5
MOE_BLK = 1024
EPS = 1e-6
NEG = -1e30
SCALE = HD ** -0.5
LOG2E = math.log2(math.e)
Q_SCALE = SCALE * LOG2E

N_LAT = B * S
N_CTX = B * C
N_ALL = N_LAT + N_CTX
ROWS = S // GRID_W

HIGHEST = lax.Precision.HIGHEST
ARB = pltpu.ARBITRARY

NT_DIMS = (((1,), (1,)), ((), ()))


def _cparams(n_axes, vmem_mb=48):
    return pltpu.CompilerParams(
        dimension_semantics=(ARB,) * n_axes, vmem_limit_bytes=vmem_mb * 1024 * 1024)


def _mod_row(i, tm):
    return jnp.where(i < N_LAT // tm, i // (S // tm), B)


def _rms(x):
    return x * lax.rsqrt(jnp.mean(x * x, axis=-1, keepdims=True) + EPS)


def _token_specs(tm, ctx_first_tile):
    n_lat = N_LAT // tm
    lat = pl.BlockSpec((tm, D), lambda i: (jnp.minimum(i, n_lat - 1), 0))
    ctx = pl.BlockSpec((tm, D), lambda i: (ctx_first_tile + jnp.maximum(i - n_lat, 0), 0))
    return lat, ctx


def _token_tile(lat_ref, ctx_ref):
    tm = lat_ref.shape[0]
    n_from_lat = jnp.where(pl.program_id(0) < N_LAT // tm, tm, 0)
    from_lat = lax.broadcasted_iota(I32, lat_ref.shape, 0) < n_from_lat
    return jnp.where(from_lat, lat_ref[...], ctx_ref[...])


def _split_bf16(x):
    hi = x.astype(BF16)
    return hi, (x - hi.astype(F32)).astype(BF16)


def _router_split(rw):
    hi, lo = _split_bf16(jnp.pad(rw, ((0, 0), (0, 128 - N_EXP))))
    return jnp.stack([hi, lo])


MOD_TN = 1024


def _mod_body(cc_ref, w_ref, b_ref, o_ref):
    a = cc_ref[...]
    a = a * jax.nn.sigmoid(a)
    o_ref[...] = jnp.dot(a.astype(BF16), w_ref[...].astype(BF16),
                         preferred_element_type=F32) + b_ref[...]


def _modulation(cc, mod_w, mod_b):
    n_col = 6 * D // MOD_TN
    return pl.pallas_call(
        _mod_body,
        grid=(DEPTH, n_col),
        in_specs=[
            pl.BlockSpec((16, D), lambda l, j: (0, 0)),
            pl.BlockSpec((None, D, MOD_TN), lambda l, j: (l, 0, j)),
            pl.BlockSpec((None, 1, MOD_TN), lambda l, j: (l, 0, j)),
        ],
        out_specs=pl.BlockSpec((None, 16, MOD_TN), lambda l, j: (l, 0, j)),
        out_shape=jax.ShapeDtypeStruct((DEPTH, 16, 6 * D), F32),
        compiler_params=_cparams(2),
        name="modulation",
    )(cc, mod_w, mod_b.reshape(DEPTH, 1, 6 * D))


IN_TM = 512
ROT_W = A_W + KV_W


def _in_body(xl_ref, xc_ref, m_ref, g_ref, w_ref, cos_ref, sin_ref, za_ref, zc_ref, uv_ref):
    x = _token_tile(xl_ref, xc_ref)
    h = _rms(x) * g_ref[...]
    h = h * (1.0 + m_ref[:, D:2 * D]) + m_ref[:, 0:D]
    z = jnp.dot(h.astype(BF16), w_ref[...], preferred_element_type=F32)
    qk = z[:, :ROT_W]
    lane = lax.broadcasted_iota(I32, qk.shape, 1)
    rot = jnp.where((lane & 16) == 0,
                    -pltpu.roll(qk, ROT_W - 16, 1), pltpu.roll(qk, 16, 1))
    qk = qk * cos_ref[...] + rot * sin_ref[...]
    za_ref[...] = jnp.concatenate(
        [qk[:, :A_W] * Q_SCALE, qk[:, A_W:], z[:, ROT_W:ROT_W + KV_W]], axis=1).astype(BF16)
    u0 = ROT_W + KV_W
    uv_ref[...] = jax.nn.gelu(z[:, u0:u0 + 2 * B_W]).astype(BF16)
    c0 = u0 + 2 * B_W
    zc_ref[...] = jnp.concatenate(
        [z[:, c0:c0 + C_W] * Q_SCALE, z[:, c0 + C_W:]], axis=1).astype(BF16)


def _in_proj(x_lat, x_ctx, ctx_first_row, mods, g, w_bf, cos_t, sin_t):
    tm = IN_TM
    n_t = N_ALL // tm

    def tab_idx(i):
        return (jnp.where(i < N_LAT // tm, i % (S // tm), S // tm), 0)

    return pl.pallas_call(
        _in_body,
        grid=(n_t,),
        in_specs=[
            *_token_specs(tm, ctx_first_row // tm),
            pl.BlockSpec((None, 1, 6 * D), lambda i: (_mod_row(i, tm), 0, 0)),
            pl.BlockSpec((1, D), lambda i: (0, 0)),
            pl.BlockSpec((D, IN_W), lambda i: (0, 0)),
            pl.BlockSpec((tm, ROT_W), tab_idx),
            pl.BlockSpec((tm, ROT_W), tab_idx),
        ],
        out_specs=[
            pl.BlockSpec((tm, A_W + 2 * KV_W), lambda i: (i, 0)),
            pl.BlockSpec((tm, 3 * C_W), lambda i: (i, 0)),
            pl.BlockSpec((tm, 2 * B_W), lambda i: (i, 0)),
        ],
        out_shape=[
            jax.ShapeDtypeStruct((N_ALL, A_W + 2 * KV_W), BF16),
            jax.ShapeDtypeStruct((N_ALL, 3 * C_W), BF16),
            jax.ShapeDtypeStruct((N_ALL, 2 * B_W), BF16),
        ],
        compiler_params=_cparams(1),
        name="in_proj",
    )(x_lat, x_ctx, mods, g, w_bf, cos_t, sin_t)


A_BAND = A_BLK + 2 * A_WIN
N_QB_LAT = N_LAT // A_BLK
QB_PER_SEQ = S // A_BLK
QB_PER_CTX = C // A_BLK


def _softmax_pv(s_list, v_list, extra_logit=None):
    m = s_list[0].max(axis=-1, keepdims=True)
    for s in s_list[1:]:
        m = jnp.maximum(m, s.max(axis=-1, keepdims=True))
    if extra_logit is not None:
        m = jnp.maximum(m, extra_logit)
    den = None
    out = None
    for s, v in zip(s_list, v_list):
        p = jnp.exp2(s - m)
        d = p.sum(axis=-1, keepdims=True)
        o = jnp.dot(p.astype(BF16), v, preferred_element_type=F32)
        den = d if den is None else den + d
        out = o if out is None else out + o
    if extra_logit is not None:
        den = den + jnp.exp2(extra_logit - m)
    return out / den


def _attn_a_body(sink_ref, q_ref, kb_ref, vb_ref, kc_ref, vc_ref, o_ref):
    i = pl.program_id(0)
    is_lat = i < N_QB_LAT
    n = i % QB_PER_SEQ
    start = pl.multiple_of(jnp.clip(n * A_BLK - A_WIN, 0, S - A_BAND), A_WIN)
    qpos = n * A_BLK + lax.broadcasted_iota(I32, (A_BLK, A_BAND), 0)
    kpos = start + lax.broadcasted_iota(I32, (A_BLK, A_BAND), 1)
    band_penalty = jnp.where(jnp.abs(kpos - qpos) <= jnp.where(is_lat, A_WIN, -1), 0.0, NEG)
    kb = kb_ref[pl.ds(start, A_BAND), :]
    vb = vb_ref[pl.ds(start, A_BAND), :]
    kc = kc_ref[...]
    vc = vc_ref[...]
    outs = []
    for h in range(A_HEADS):
        kv = h // (A_HEADS // A_KV)
        sl = slice(kv * HD, (kv + 1) * HD)
        q = q_ref[:, h * HD:(h + 1) * HD]
        s_b = lax.dot_general(q, kb[:, sl], NT_DIMS, preferred_element_type=F32) + band_penalty
        s_c = lax.dot_general(q, kc[:, sl], NT_DIMS, preferred_element_type=F32)
        outs.append(_softmax_pv([s_c, s_b], [vc[:, sl], vb[:, sl]], sink_ref[h] * LOG2E))
    o_ref[...] = jnp.concatenate(outs, axis=1).astype(BF16)


def _attn_a(za, sink, with_ctx):
    n_qb = N_QB_LAT + (N_CTX // A_BLK if with_ctx else 0)

    def bidx(i):
        return jnp.where(i < N_QB_LAT, i // QB_PER_SEQ, (i - N_QB_LAT) // QB_PER_CTX)

    k_col = A_W // KV_W
    v_col = k_col + 1
    return pl.pallas_call(
        _attn_a_body,
        grid=(n_qb,),
        in_specs=[
            pl.BlockSpec(memory_space=pltpu.SMEM),
            pl.BlockSpec((A_BLK, A_W), lambda i: (i, 0)),
            pl.BlockSpec((S, KV_W), lambda i: (bidx(i), k_col)),
            pl.BlockSpec((S, KV_W), lambda i: (bidx(i), v_col)),
            pl.BlockSpec((C, KV_W), lambda i: (N_LAT // C + bidx(i), k_col)),
            pl.BlockSpec((C, KV_W), lambda i: (N_LAT // C + bidx(i), v_col)),
        ],
        out_specs=pl.BlockSpec((A_BLK, A_W), lambda i: (i, 0)),
        out_shape=jax.ShapeDtypeStruct((n_qb * A_BLK, A_W), BF16),
        compiler_params=_cparams(1),
        name="attn_window",
    )(sink, za, za, za, za, za)


NA_R = 4
NA_TQ = NA_R * GRID_W
NA_KROWS = 12
NA_WIN = NA_KROWS * GRID_W
NA_STEPS = ROWS // NA_R
NA_PAIRS = NA_KROWS // 2
N_DR = 2 * C_WIN_R - 1


def _na_body(q_ref, k_ref, v_ref, kc_ref, vc_ref, tab_ref, o_ref):
    j = pl.program_id(1)
    is_lat = j < NA_STEPS
    r0 = jnp.minimum(j, NA_STEPS - 1) * NA_R
    u0 = jnp.clip(r0 - C_WIN_R // 2, 0, ROWS - NA_KROWS)
    k0 = pl.multiple_of(u0 * GRID_W, GRID_W)
    kw = k_ref[pl.ds(k0, NA_WIN), :]
    vw = v_ref[pl.ds(k0, NA_WIN), :]
    kc = kc_ref[...]
    vc = vc_ref[...]
    q = q_ref[...]
    left = lax.broadcasted_iota(I32, (1, 2 * GRID_W), 1) < GRID_W

    tab_idx = []
    penalty = []
    for rr in range(NA_R):
        r = r0 + rr
        start = jnp.clip(r - C_WIN_R // 2, 0, ROWS - C_WIN_R)
        idx_row = []
        pen_row = []
        for p in range(NA_PAIRS):
            kr = u0 + 2 * p
            idx_row.append(jnp.clip(kr - r + (C_WIN_R - 1), -1, N_DR - 1) + 1)
            pens = []
            for half in range(2):
                ok = jnp.logical_and(is_lat, jnp.logical_and(kr + half >= start,
                                                             kr + half < start + C_WIN_R))
                pens.append(jnp.where(ok, 0.0, NEG))
            pen_row.append(jnp.where(left, pens[0], pens[1]))
        tab_idx.append(idx_row)
        penalty.append(jnp.concatenate(pen_row, axis=1))

    outs = []
    for h in range(C_HEADS):
        sl = slice(h * HD, (h + 1) * HD)
        qh = q[:, sl]
        s_w = lax.dot_general(qh, kw[:, sl], NT_DIMS, preferred_element_type=F32)
        bias = jnp.concatenate([
            jnp.concatenate([tab_ref[h, pl.ds(tab_idx[rr][p], 1)][0] for p in range(NA_PAIRS)],
                            axis=1) + penalty[rr]
            for rr in range(NA_R)], axis=0)
        s_c = lax.dot_general(qh, kc[:, sl], NT_DIMS, preferred_element_type=F32)
        outs.append(_softmax_pv([s_c, s_w + bias], [vc[:, sl], vw[:, sl]]))
    o_ref[...] = jnp.concatenate(outs, axis=1).astype(BF16)


def _na(zc, tab, with_ctx):
    n_j = NA_STEPS + (1 if with_ctx else 0)

    def qidx(b, j):
        return jnp.where(j < NA_STEPS, b * NA_STEPS + j, N_LAT // NA_TQ + b)

    n_out = N_LAT + (N_CTX if with_ctx else 0)
    return pl.pallas_call(
        _na_body,
        grid=(B, n_j),
        in_specs=[
            pl.BlockSpec((NA_TQ, C_W), lambda b, j: (qidx(b, j), 0)),
            pl.BlockSpec((S, C_W), lambda b, j: (b, 1)),
            pl.BlockSpec((S, C_W), lambda b, j: (b, 2)),
            pl.BlockSpec((C, C_W), lambda b, j: (N_LAT // C + b, 1)),
            pl.BlockSpec((C, C_W), lambda b, j: (N_LAT // C + b, 2)),
            pl.BlockSpec((C_HEADS, N_DR + 1, GRID_W, 2 * GRID_W), lambda b, j: (0, 0, 0, 0)),
        ],
        out_specs=pl.BlockSpec((NA_TQ, C_W), lambda b, j: (qidx(b, j), 0)),
        out_shape=jax.ShapeDtypeStruct((n_out, C_W), BF16),
        compiler_params=_cparams(2),
        name="attn_neighbourhood",
    )(zc, zc, zc, zc, zc, tab)


def _na_bias_table(rpb):
    cq = np.arange(GRID_W)
    col_start = np.clip(cq - C_WIN_C // 2, 0, GRID_W - C_WIN_C)
    col_ok = (cq[None, :] >= col_start[:, None]) & (cq[None, :] < col_start[:, None] + C_WIN_C)
    dc = np.clip(cq[None, :] - cq[:, None], -(C_WIN_C - 1), C_WIN_C - 1) + (C_WIN_C - 1)
    n_dc = 2 * C_WIN_C - 1
    pick = (dc.reshape(-1)[None, :] == np.arange(n_dc)[:, None]).astype(np.float32)
    t = jnp.dot(rpb.astype(F32).reshape(-1, n_dc), jnp.asarray(pick), precision=HIGHEST)
    t = t.reshape(C_HEADS, N_DR, GRID_W, GRID_W) * LOG2E
    t = jnp.where(col_ok[None, None], t, NEG)
    zero = jnp.zeros((C_HEADS, 1, GRID_W, GRID_W), F32)
    ext = jnp.concatenate([zero, t, zero], axis=1)
    return jnp.concatenate([ext[:, :-1], ext[:, 1:]], axis=-1)


MG_TM = 256


def _merge_body(xl_ref, xc_ref, ya_ref, yc_ref, uv_ref, m_ref, gg_ref, sg_ref, sw_ref, sb_ref, bd_ref,
                wo_ref, g2_ref, rw_ref, xo_ref, hx_ref, lg_ref):
    u = uv_ref[:, :B_W].astype(F32)
    v = uv_ref[:, B_W:].astype(F32)
    vv_hi, vv_lo = _split_bf16(v * v)
    ms = (jnp.dot(vv_hi, bd_ref[...], preferred_element_type=F32)
          + jnp.dot(vv_lo, bd_ref[...], preferred_element_type=F32))
    vn = (v * lax.rsqrt(ms + EPS) * sg_ref[...]).astype(BF16)
    lane_grp = lax.broadcasted_iota(I32, (SGU_CHUNK, B_W), 1) // HD
    gates = []
    for c in range(MG_TM // SGU_CHUNK):
        vc = vn[c * SGU_CHUNK:(c + 1) * SGU_CHUNK, :]
        gate = sb_ref[...]
        acc = jnp.zeros((SGU_CHUNK, B_W), F32)
        for g in range(SGU_GROUPS):
            r = jnp.dot(sw_ref[g], vc, preferred_element_type=F32)
            acc = jnp.where(lane_grp == g, r, acc)
        gates.append(acc + gate)
    yb = u * jnp.concatenate(gates, axis=0)
    gg = gg_ref[...]
    ycat = jnp.concatenate([
        _rms(ya_ref[...].astype(F32)) * gg[:, :A_W],
        _rms(yb) * gg[:, A_W:A_W + B_W],
        _rms(yc_ref[...].astype(F32)) * gg[:, A_W + B_W:],
    ], axis=1)
    proj = jnp.dot(ycat.astype(BF16), wo_ref[...], preferred_element_type=F32)
    xn = _token_tile(xl_ref, xc_ref) + m_ref[:, 2 * D:3 * D] * proj
    xo_ref[...] = xn
    hx = _rms(xn) * g2_ref[...]
    hx = hx * (1.0 + m_ref[:, 4 * D:5 * D]) + m_ref[:, 3 * D:4 * D]
    hx_ref[...] = hx.astype(BF16)
    hx_hi, hx_lo = _split_bf16(hx)
    logits = (jnp.dot(hx_hi, rw_ref[0], preferred_element_type=F32)
              + jnp.dot(hx_hi, rw_ref[1], preferred_element_type=F32)
              + jnp.dot(hx_lo, rw_ref[0], preferred_element_type=F32))
    lg_ref[...] = logits.T[:N_EXP, :]


def _merge(x_lat, x_ctx, ctx_first_row, ya, yc, uv, mods, gg, sg, sw_bf, sb_tab, bd, wo_bf, g2, rw_t,
           n_rows):
    tm = MG_TM
    n_t = n_rows // tm
    const2 = lambda i: (0, 0)
    row = lambda i: (i, 0)
    return pl.pallas_call(
        _merge_body,
        grid=(n_t,),
        in_specs=[
            *_token_specs(tm, ctx_first_row // tm),
            pl.BlockSpec((tm, A_W), row),
            pl.BlockSpec((tm, C_W), row),
            pl.BlockSpec((tm, 2 * B_W), row),
            pl.BlockSpec((None, 1, 6 * D), lambda i: (_mod_row(i, tm), 0, 0)),
            pl.BlockSpec((1, D), const2),
            pl.BlockSpec((1, B_W), const2),
            pl.BlockSpec((SGU_GROUPS, SGU_CHUNK, SGU_CHUNK), lambda i: (0, 0, 0)),
            pl.BlockSpec((SGU_CHUNK, B_W), const2),
            pl.BlockSpec((B_W, B_W), const2),
            pl.BlockSpec((D, D), const2),
            pl.BlockSpec((1, D), const2),
            pl.BlockSpec((2, D, 128), lambda i: (0, 0, 0)),
        ],
        out_specs=[
            pl.BlockSpec((tm, D), row),
            pl.BlockSpec((tm, D), row),
            pl.BlockSpec((N_EXP, tm), lambda i: (0, i)),
        ],
        out_shape=[
            jax.ShapeDtypeStruct((n_rows, D), F32),
            jax.ShapeDtypeStruct((n_rows, D), BF16),
            jax.ShapeDtypeStruct((N_EXP, n_rows), F32),
        ],
        compiler_params=_cparams(1),
        name="merge",
    )(x_lat, x_ctx, ya, yc, uv, mods, gg, sg, sw_bf, sb_tab, bd, wo_bf, g2, rw_t)


RT_TM = 256
GRP_SZ = N_EXP // N_GRP
SLOT_ALIGN = 16


def _first_argmax(v, iota):
    m = v.max(axis=0, keepdims=True)
    idx = jnp.where(v == m, iota, float(v.shape[0])).min(axis=0, keepdims=True)
    return m, idx


def _stack_rows(rows, iota):
    out = jnp.zeros(iota.shape, F32)
    for r, v in enumerate(rows):
        out = jnp.where(iota == float(r), v, out)
    return out


def _route_body(lg_ref, rb_ref, w_ref, lp_ref, tc_ref):
    tm = RT_TM
    scores = jax.nn.sigmoid(lg_ref[...])
    sel = scores + rb_ref[...]
    iota_g = lax.broadcasted_iota(I32, (GRP_SZ, tm), 0).astype(F32)
    gs = []
    for g in range(N_GRP):
        v = sel[g * GRP_SZ:(g + 1) * GRP_SZ, :]
        m1, i1 = _first_argmax(v, iota_g)
        m2 = jnp.where(iota_g == i1, -jnp.inf, v).max(axis=0, keepdims=True)
        gs.append(m1 + m2)
    iota_n = lax.broadcasted_iota(I32, (N_GRP, tm), 0).astype(F32)
    gscore = _stack_rows(gs, iota_n)
    gsel = jnp.zeros((N_GRP, tm), F32)
    for _ in range(TOPK_GRP):
        _, gi = _first_argmax(gscore, iota_n)
        hit = iota_n == gi
        gsel = jnp.where(hit, 1.0, gsel)
        gscore = jnp.where(hit, -jnp.inf, gscore)
    emask = jnp.concatenate(
        [jnp.broadcast_to(gsel[g:g + 1, :], (GRP_SZ, tm)) for g in range(N_GRP)], axis=0)
    cand = jnp.where(emask > 0.5, sel, NEG)
    iota_e = lax.broadcasted_iota(I32, (N_EXP, tm), 0).astype(F32)
    hits = []
    ws = []
    member = jnp.zeros((N_EXP, tm), F32)
    for _ in range(TOP_K):
        _, ei = _first_argmax(cand, iota_e)
        hit = iota_e == ei
        hits.append(hit)
        ws.append(jnp.where(hit, scores, 0.0).sum(axis=0, keepdims=True))
        member = jnp.where(hit, 1.0, member)
        cand = jnp.where(hit, -jnp.inf, cand)
    wsum = ws[0]
    for w in ws[1:]:
        wsum = wsum + w
    iota_k = lax.broadcasted_iota(I32, (TOP_K, tm), 0).astype(F32)
    w_ref[...] = _stack_rows(ws, iota_k) / wsum * ROUTED_SCALE
    r_i = lax.broadcasted_iota(I32, (tm, tm), 0)
    c_i = lax.broadcasted_iota(I32, (tm, tm), 1)
    tri = jnp.where(r_i < c_i, 1.0, 0.0).astype(BF16)
    local = jnp.dot(member.astype(BF16), tri, preferred_element_type=F32)
    tile_cnt = member.sum(axis=1, keepdims=True)
    tc_ref[...] = tile_cnt
    aligned = jnp.ceil(tile_cnt / SLOT_ALIGN) * SLOT_ALIGN
    e_r = lax.broadcasted_iota(I32, (N_EXP, N_EXP), 0)
    e_c = lax.broadcasted_iota(I32, (N_EXP, N_EXP), 1)
    below = jnp.where(e_c < e_r, 1.0, 0.0)
    start = jnp.dot(below, jnp.broadcast_to(aligned, (N_EXP, 128)), precision=HIGHEST,
                    preferred_element_type=F32)[:, 0:1]
    pos = local + start
    lp_ref[...] = _stack_rows(
        [jnp.where(hit, pos, 0.0).sum(axis=0, keepdims=True) for hit in hits],
        iota_k).astype(I32)


def _route(logits_t, rbias, n_tok):
    tm = RT_TM
    tok = lambda i: (0, i)
    return pl.pallas_call(
        _route_body,
        grid=(n_tok // tm,),
        in_specs=[
            pl.BlockSpec((N_EXP, tm), tok),
            pl.BlockSpec((N_EXP, 1), lambda i: (0, 0)),
        ],
        out_specs=[
            pl.BlockSpec((TOP_K, tm), tok),
            pl.BlockSpec((TOP_K, tm), tok),
            pl.BlockSpec((None, N_EXP, 1), lambda i: (i, 0, 0)),
        ],
        out_shape=[
            jax.ShapeDtypeStruct((TOP_K, n_tok), F32),
            jax.ShapeDtypeStruct((TOP_K, n_tok), I32),
            jax.ShapeDtypeStruct((n_tok // tm, N_EXP, 1), F32),
        ],
        compiler_params=_cparams(1),
        name="route",
    )(logits_t, rbias)


DP_TM = RT_TM
SORT_ROWS = 3072
N_GRPS = SORT_ROWS // SLOT_ALIGN
MIN_SORT_ROWS = TOP_K * DP_TM
ISSUE_UNROLL = 4
WAIT_GROUPS = 32
N_BURSTS = 6
BURST_ROWS = SORT_ROWS // N_BURSTS
BURST_GRPS = BURST_ROWS // SLOT_ALIGN


def _start_groups(g_lo, g_hi, row_of, make_copy):
    n_full = (g_hi - g_lo) // ISSUE_UNROLL

    def body(q, carry):
        for s in range(ISSUE_UNROLL):
            g = g_lo + q * ISSUE_UNROLL + s
            make_copy(g, row_of(g)).start(priority=s % 2)
        return carry

    def tail(g, carry):
        make_copy(g, row_of(g)).start()
        return carry

    lax.fori_loop(0, n_full, body, 0)
    lax.fori_loop(g_lo + n_full * ISSUE_UNROLL, g_hi, tail, 0)


def _start_burst(q, ngrp, row_of, make_copy):
    _start_groups(jnp.minimum(q * BURST_GRPS, ngrp), jnp.minimum((q + 1) * BURST_GRPS, ngrp),
                  row_of, make_copy)


def _wait_groups(ngrp, make_copy, make_bulk):
    def bulk(q, carry):
        make_bulk().wait()
        return carry

    def single(g, carry):
        make_copy(0, 0).wait()
        return carry

    n_bulk = ngrp // WAIT_GROUPS
    lax.fori_loop(0, n_bulk, bulk, 0)
    lax.fori_loop(n_bulk * WAIT_GROUPS, ngrp, single, 0)


def _dispatch_body(ngrp_ref, fill_ref, grow_ref, hx_ref, lp_ref, xs_ref, zbuf_ref, zero_ref, sems):
    i = pl.program_id(0)
    n_tiles = pl.num_programs(0)
    slot = i % 2

    def zero_copy(b):
        dst = xs_ref.at[pl.ds(pl.multiple_of(b * MOE_BLK, MOE_BLK), MOE_BLK)]
        return pltpu.make_async_copy(zero_ref, dst, sems.at[0])

    @pl.when(i == 0)
    def _():
        zero_ref[...] = jnp.zeros_like(zero_ref)
        n_fill = fill_ref[0]

        def z_start(q, carry):
            @pl.when(fill_ref[1 + q] >= 0)
            def _():
                zero_copy(fill_ref[1 + q]).start()
            return carry

        def z_wait(q, carry):
            @pl.when(fill_ref[1 + q] >= 0)
            def _():
                zero_copy(0).wait()
            return carry

        lax.fori_loop(0, n_fill, z_start, 0)
        lax.fori_loop(0, n_fill, z_wait, 0)

    def group_copy(s, g, row):
        src = zbuf_ref.at[s, pl.ds(pl.multiple_of(g * SLOT_ALIGN, SLOT_ALIGN), SLOT_ALIGN)]
        dst = xs_ref.at[pl.ds(pl.multiple_of(row, SLOT_ALIGN), SLOT_ALIGN)]
        return pltpu.make_async_copy(src, dst, sems.at[s])

    def bulk_copy(s):
        n_rows = WAIT_GROUPS * SLOT_ALIGN
        return pltpu.make_async_copy(
            zbuf_ref.at[s, pl.ds(0, n_rows)], xs_ref.at[pl.ds(0, n_rows)], sems.at[s])

    x = hx_ref[...]
    lpos = lp_ref[...]
    n_cur = ngrp_ref[i]
    pos = [jnp.tile(jnp.broadcast_to(lpos[k:k + 1, :], (16, DP_TM)).astype(jnp.int16),
                    (BURST_ROWS // 16, 1)) for k in range(TOP_K)]
    def sort_burst(q):
        row_iota = (lax.broadcasted_iota(I32, (BURST_ROWS, DP_TM), 0)
                    + q * BURST_ROWS).astype(jnp.int16)
        onehot = jnp.zeros((BURST_ROWS, DP_TM), BF16)
        for k in range(TOP_K):
            onehot = jnp.where(row_iota == pos[k], jnp.ones((), BF16), onehot)
        z = jnp.dot(onehot, x, preferred_element_type=F32)
        zbuf_ref[slot, pl.ds(q * BURST_ROWS, BURST_ROWS), :] = z.astype(BF16)
        _start_burst(q, n_cur, lambda g: grow_ref[0, g], lambda g, r: group_copy(slot, g, r))

    for q in range(N_BURSTS):
        if (q + 1) * BURST_ROWS <= MIN_SORT_ROWS:
            sort_burst(q)
        else:
            pl.when(q * BURST_GRPS < n_cur)(functools.partial(sort_burst, q))

    @pl.when(i > 0)
    def _():
        _wait_groups(ngrp_ref[jnp.maximum(i - 1, 0)], lambda g, r: group_copy(1 - slot, 0, 0),
                     lambda: bulk_copy(1 - slot))

    @pl.when(i == n_tiles - 1)
    def _():
        _wait_groups(n_cur, lambda g, r: group_copy(slot, 0, 0), lambda: bulk_copy(slot))


def _dispatch(ngrp, fill, grp_row, hx, lpos, n_tok, n_slots):
    return pl.pallas_call(
        _dispatch_body,
        grid_spec=pltpu.PrefetchScalarGridSpec(
            num_scalar_prefetch=2,
            grid=(n_tok // DP_TM,),
            in_specs=[
                pl.BlockSpec((None, 1, N_GRPS), lambda i, *_: (i, 0, 0), memory_space=pltpu.SMEM),
                pl.BlockSpec((DP_TM, D), lambda i, *_: (i, 0)),
                pl.BlockSpec((TOP_K, DP_TM), lambda i, *_: (0, i)),
            ],
            out_specs=pl.BlockSpec(memory_space=pl.ANY),
            scratch_shapes=[
                pltpu.VMEM((2, SORT_ROWS, D), BF16),
                pltpu.VMEM((MOE_BLK, D), BF16),
                pltpu.SemaphoreType.DMA((2,)),
            ],
        ),
        out_shape=jax.ShapeDtypeStruct((n_slots, D), BF16),
        compiler_params=_cparams(1, vmem_mb=56),
        name="moe_dispatch",
    )(ngrp, fill, grp_row, hx, lpos)


def _experts_body(bexp_ref, xblk_ref, nused_ref, xs_ref, wg_ref, wu_ref, wd_ref, ys_ref):
    j = pl.program_id(0)

    @pl.when(j < nused_ref[0])
    def _():
        x = xs_ref[...]
        g = jnp.dot(x, wg_ref[...].astype(BF16), preferred_element_type=F32)
        u = jnp.dot(x, wu_ref[...].astype(BF16), preferred_element_type=F32)
        h = (g * jax.nn.sigmoid(g) * u).astype(BF16)
        y = jnp.dot(h, wd_ref[...].astype(BF16), preferred_element_type=F32)
        ys_ref[...] = y.astype(BF16)


def _experts(bexp, xblk, nused, xs, wg, wu, wd, n_blk, layer):
    w_idx = lambda j, be, xb, nu: (layer, be[j], 0, 0)
    return pl.pallas_call(
        _experts_body,
        grid_spec=pltpu.PrefetchScalarGridSpec(
            num_scalar_prefetch=3,
            grid=(n_blk,),
            in_specs=[
                pl.BlockSpec((MOE_BLK, D), lambda j, be, xb, nu: (xb[j], 0)),
                pl.BlockSpec((None, None, D, EXP_FF), w_idx),
                pl.BlockSpec((None, None, D, EXP_FF), w_idx),
                pl.BlockSpec((None, None, EXP_FF, D), w_idx),
            ],
            out_specs=pl.BlockSpec((MOE_BLK, D), lambda j, be, xb, nu: (xb[j], 0)),
        ),
        out_shape=jax.ShapeDtypeStruct((n_blk * MOE_BLK, D), BF16),
        input_output_aliases={3: 0},
        compiler_params=_cparams(1),
        name="moe_experts",
    )(bexp, xblk, nused, xs, wg, wu, wd)


def _combine_body(ngrp_ref, grow_ref, grow_next_ref, ys_ref, lp_ref, w_ref, x_ref, hx_ref, m_ref,
                  sg_ref, su_ref, sd_ref, fg_ref, o_ref, ybuf_ref, acc_ref, sems, *, final_norm):
    i = pl.program_id(0)
    n_tiles = pl.num_programs(0)
    slot = i % 2

    def group_copy(s, g, row):
        src = ys_ref.at[pl.ds(pl.multiple_of(row, SLOT_ALIGN), SLOT_ALIGN)]
        dst = ybuf_ref.at[s, pl.ds(pl.multiple_of(g * SLOT_ALIGN, SLOT_ALIGN), SLOT_ALIGN)]
        return pltpu.make_async_copy(src, dst, sems.at[s])

    def bulk_copy(s):
        n_rows = WAIT_GROUPS * SLOT_ALIGN
        return pltpu.make_async_copy(
            ys_ref.at[pl.ds(0, n_rows)], ybuf_ref.at[s, pl.ds(0, n_rows)], sems.at[s])

    @pl.when(i == 0)
    def _():
        ybuf_ref[...] = jnp.zeros_like(ybuf_ref)
        _start_groups(0, ngrp_ref[0], lambda g: grow_ref[0, g], lambda g, r: group_copy(0, g, r))

    n_next = jnp.where(i + 1 < n_tiles, ngrp_ref[jnp.minimum(i + 1, n_tiles - 1)], 0)

    def prefetch(q):
        _start_burst(q, n_next, lambda g: grow_next_ref[0, g],
                     lambda g, r: group_copy(1 - slot, g, r))

    hx = hx_ref[...]
    g = jnp.dot(hx, sg_ref[...], preferred_element_type=F32)
    u = jnp.dot(hx, su_ref[...], preferred_element_type=F32)
    h = (g * jax.nn.sigmoid(g) * u).astype(BF16)
    acc_ref[...] = jnp.dot(h, sd_ref[...], preferred_element_type=F32)

    lpos = lp_ref[...]
    w = w_ref[...]
    n_own = ngrp_ref[i]
    _wait_groups(n_own, lambda g, r: group_copy(slot, 0, 0), lambda: bulk_copy(slot))
    lane_tiles = BURST_ROWS // 128
    pos = [jnp.tile(jnp.broadcast_to(lpos[:, k:k + 1], (DP_TM, 128)).astype(jnp.int16),
                    (1, lane_tiles)) for k in range(TOP_K)]
    wts = [jnp.tile(jnp.broadcast_to(w[:, k:k + 1], (DP_TM, 128)).astype(BF16),
                    (1, lane_tiles)) for k in range(TOP_K)]
    def unsort_chunk(q):
        col_iota = (lax.broadcasted_iota(I32, (DP_TM, BURST_ROWS), 1)
                    + q * BURST_ROWS).astype(jnp.int16)
        unsort = jnp.zeros((DP_TM, BURST_ROWS), BF16)
        for k in range(TOP_K):
            unsort = jnp.where(col_iota == pos[k], wts[k], unsort)
        acc_ref[...] += jnp.dot(unsort, ybuf_ref[slot, pl.ds(q * BURST_ROWS, BURST_ROWS), :],
                                preferred_element_type=F32)

    for q in range(N_BURSTS):
        if (q + 1) * BURST_ROWS <= MIN_SORT_ROWS:
            unsort_chunk(q)
        else:
            pl.when(q * BURST_GRPS < n_own)(functools.partial(unsort_chunk, q))
        prefetch(q)
    out = x_ref[...] + m_ref[:, 5 * D:6 * D] * acc_ref[...]
    if final_norm:
        out = _rms(out) * fg_ref[...]
    o_ref[...] = out


def _combine(ngrp, grp_row, ys, lpos_t, w_t, x, hx, mods, sg_bf, su_bf, sd_bf, fg, n_tok, final_norm):
    tm = DP_TM
    row = lambda i, *_: (i, 0)
    const2 = lambda i, *_: (0, 0)
    return pl.pallas_call(
        functools.partial(_combine_body, final_norm=final_norm),
        grid_spec=pltpu.PrefetchScalarGridSpec(
            num_scalar_prefetch=1,
            grid=(n_tok // tm,),
            in_specs=[
                pl.BlockSpec((None, 1, N_GRPS), lambda i, *_: (i, 0, 0), memory_space=pltpu.SMEM),
                pl.BlockSpec((None, 1, N_GRPS), lambda i, *_: (jnp.minimum(i + 1, n_tok // tm - 1), 0, 0),
                             memory_space=pltpu.SMEM),
                pl.BlockSpec(memory_space=pl.ANY),
                pl.BlockSpec((tm, TOP_K), row),
                pl.BlockSpec((tm, TOP_K), row),
                pl.BlockSpec((tm, D), row),
                pl.BlockSpec((tm, D), row),
                pl.BlockSpec((None, 1, 6 * D), lambda i, *_: (_mod_row(i, tm), 0, 0)),
                pl.BlockSpec((D, EXP_FF), const2),
                pl.BlockSpec((D, EXP_FF), const2),
                pl.BlockSpec((EXP_FF, D), const2),
                pl.BlockSpec((1, D), const2),
            ],
            out_specs=pl.BlockSpec((tm, D), row),
            scratch_shapes=[pltpu.VMEM((2, SORT_ROWS, D), BF16), pltpu.VMEM((tm, D), F32),
                            pltpu.SemaphoreType.DMA((2,))],
        ),
        out_shape=jax.ShapeDtypeStruct((n_tok, D), F32),
        compiler_params=_cparams(1, vmem_mb=56),
        name="moe_combine",
    )(ngrp, grp_row, grp_row, ys, lpos_t, w_t, x, hx, mods, sg_bf, su_bf, sd_bf, fg)


def _moe(x_new, hx, logits_t, mods, rbias, wg, wu, wd, sg_bf, su_bf, sd_bf, fg, n_tok, layer,
         final_norm):
    w, lpos, tile_cnt = _route(logits_t, rbias, n_tok)
    n_tiles = n_tok // DP_TM
    n_blk = (n_tok * TOP_K + (SLOT_ALIGN - 1) * N_EXP * n_tiles) // MOE_BLK + N_EXP
    tcnt = tile_cnt[:, :, 0].astype(I32)
    cnt_al = (tcnt + SLOT_ALIGN - 1) // SLOT_ALIGN * SLOT_ALIGN
    loc_end = jnp.cumsum(cnt_al, axis=1)
    loc = loc_end - cnt_al
    slots_e = jnp.sum(cnt_al, axis=0)
    nblk_e = (slots_e + MOE_BLK - 1) // MOE_BLK
    blk_end = jnp.cumsum(nblk_e)
    pstart = (blk_end - nblk_e) * MOE_BLK
    nused = blk_end[-1:].astype(I32)
    blk_ids = jnp.arange(n_blk, dtype=I32)
    xblk = jnp.minimum(blk_ids, nused[0] - 1)
    bexp = jnp.minimum(
        jnp.sum(blk_end[None, :] <= xblk[:, None], axis=1), N_EXP - 1).astype(I32)
    off = pstart[None, :] + jnp.cumsum(cnt_al, axis=0) - cnt_al
    g_row = jnp.arange(N_GRPS, dtype=I32) * SLOT_ALIGN
    e_of_g = jnp.minimum(
        jnp.sum(loc_end[:, None, :] <= g_row[None, :, None], axis=2), N_EXP - 1)
    pick = e_of_g[..., None] == jnp.arange(N_EXP, dtype=I32)
    grp_row = (jnp.sum(jnp.where(pick, (off - loc)[:, None, :], 0), axis=2)
               + g_row[None, :]).astype(I32).reshape(n_tiles, 1, N_GRPS)
    ngrp = (loc_end[:, -1] // SLOT_ALIGN).astype(I32)
    unused = nused[0] + blk_ids
    fill = jnp.concatenate([
        (N_EXP + n_blk - nused[0])[None],
        jnp.where(nblk_e > 0, blk_end - 1, -1),
        jnp.where(unused < n_blk, unused, -1)]).astype(I32)
    xs = _dispatch(ngrp, fill, grp_row, hx, lpos, n_tok, n_blk * MOE_BLK)
    ys = _experts(bexp, xblk, nused, xs, wg, wu, wd, n_blk, layer)
    return _combine(ngrp, grp_row, ys, lpos.T, w.T, x_new, hx, mods, sg_bf, su_bf, sd_bf, fg,
                    n_tok, final_norm)


def _rope_tables():
    t = np.arange(S)
    row = (t // GRID_W).astype(np.float32)
    col = (t % GRID_W).astype(np.float32)
    half = HD // 2
    inv = jnp.asarray(ROPE_BASE, F32) ** (-jnp.arange(0, half, 2, dtype=F32) / half)
    ang_r = jnp.asarray(row)[:, None] * inv
    ang_c = jnp.asarray(col)[:, None] * inv
    ang = jnp.concatenate([ang_r, ang_r, ang_c, ang_c], axis=-1)
    n_rep = ROT_W // HD
    cos = jnp.tile(jnp.cos(ang), (1, n_rep))
    sin = jnp.tile(jnp.sin(ang), (1, n_rep))
    cos = jnp.concatenate([cos, jnp.ones((IN_TM, ROT_W), F32)], axis=0)
    sin = jnp.concatenate([sin, jnp.zeros((IN_TM, ROT_W), F32)], axis=0)
    return cos, sin


def kernel(x, c, ctx, c_ctx, mod_w, mod_b, norm1_g, w_in, attn_sink, sgu_norm_g, sgu_w, sgu_b, na_rpb,
           group_norm_g, w_out, norm2_g, router_w, router_bias, exp_w_gate, exp_w_up, exp_w_down,
           shared_w_gate, shared_w_up, shared_w_down, final_g):
    tokens = (x.reshape(N_LAT, D), ctx.reshape(N_CTX, D), 0)
    cc = jnp.concatenate([c, c_ctx[None, :], jnp.zeros((16 - B - 1, D), F32)], axis=0)
    mods_all = _modulation(cc, mod_w, mod_b)
    cos_t, sin_t = _rope_tables()
    seg = np.arange(B_W) // HD
    bd = jnp.asarray((seg[:, None] == seg[None, :]).astype(np.float32) / HD, BF16)
    fg = final_g.reshape(1, D)

    out = None
    for l in range(DEPTH):
        last = l == DEPTH - 1
        mods = mods_all[l].reshape(16, 1, 6 * D)
        za, zc, uv = _in_proj(*tokens, mods, norm1_g[l].reshape(1, D), w_in[l].astype(BF16), cos_t, sin_t)
        ya = _attn_a(za, attn_sink[l], with_ctx=not last)
        yc = _na(zc, _na_bias_table(na_rpb[l]), with_ctx=not last)
        n_tok = N_LAT if last else N_ALL
        sb_tab = jnp.repeat(sgu_b[l].T, HD, axis=1)
        x_new, hx, logits_t = _merge(
            *tokens, ya, yc, uv, mods, group_norm_g[l].reshape(1, D), sgu_norm_g[l].reshape(1, B_W),
            sgu_w[l].astype(BF16), sb_tab, bd, w_out[l].astype(BF16), norm2_g[l].reshape(1, D),
            _router_split(router_w[l]), n_tok)
        res = _moe(x_new, hx, logits_t, mods, router_bias[l].reshape(N_EXP, 1),
                   exp_w_gate, exp_w_up, exp_w_down,
                   shared_w_gate[l].astype(BF16), shared_w_up[l].astype(BF16),
                   shared_w_down[l].astype(BF16), fg, n_tok, layer=l, final_norm=last)
        if last:
            out = res.reshape(B, S, D)
        else:
            tokens = (res, res, N_LAT)
    return out
```

```python
import functools
import math

import jax
import jax.numpy as jnp
import numpy as np
from jax import lax
from jax.experimental import pallas as pl
from jax.experimental.pallas import tpu as pltpu

F32 = jnp.float32
BF16 = jnp.bfloat16
I32 = jnp.int32

D = 1024
B = 8
S = 2048
C = 256
DEPTH = 2
GRID_W = 64
HD = 64
A_HEADS = 6
A_KV = 2
A_WIN = 128
A_BLK = 256
ROPE_BASE = 10000.0
SGU_GROUPS = 4
SGU_CHUNK = 128
C_HEADS = 6
C_WIN_R = 8
C_WIN_C = 16
A_W = A_HEADS * HD
B_W = SGU_GROUPS * HD
C_W = C_HEADS * HD
KV_W = A_KV * HD
IN_W = A_W + 2 * KV_W + 2 * B_W + 3 * C_W
N_EXP = 64
TOP_K = 8
N_GRP = 8
TOPK_GRP = 4
EXP_FF = 256
ROUTED_SCALE = 2.5
MOE_BLK = 1024
EPS = 1e-6
NEG = -1e30
SCALE = HD ** -0.5
LOG2E = math.log2(math.e)
Q_SCALE = SCALE * LOG2E

N_LAT = B * S
N_CTX = B * C
N_ALL = N_LAT + N_CTX
ROWS = S // GRID_W

HIGHEST = lax.Precision.HIGHEST
ARB = pltpu.ARBITRARY

NT_DIMS = (((1,), (1,)), ((), ()))


def _cparams(n_axes, vmem_mb=48):
    return pltpu.CompilerParams(
        dimension_semantics=(ARB,) * n_axes, vmem_limit_bytes=vmem_mb * 1024 * 1024)


def _mod_row(i, tm):
    return jnp.where(i < N_LAT // tm, i // (S // tm), B)


def _rms(x):
    return x * lax.rsqrt(jnp.mean(x * x, axis=-1, keepdims=True) + EPS)


def _token_specs(tm, ctx_first_tile):
    n_lat = N_LAT // tm
    lat = pl.BlockSpec((tm, D), lambda i: (jnp.minimum(i, n_lat - 1), 0))
    ctx = pl.BlockSpec((tm, D), lambda i: (ctx_first_tile + jnp.maximum(i - n_lat, 0), 0))
    return lat, ctx


def _token_tile(lat_ref, ctx_ref):
    tm = lat_ref.shape[0]
    n_from_lat = jnp.where(pl.program_id(0) < N_LAT // tm, tm, 0)
    from_lat = lax.broadcasted_iota(I32, lat_ref.shape, 0) < n_from_lat
    return jnp.where(from_lat, lat_ref[...], ctx_ref[...])


def _split_bf16(x):
    hi = x.astype(BF16)
    return hi, (x - hi.astype(F32)).astype(BF16)


def _router_split(rw):
    hi, lo = _split_bf16(jnp.pad(rw, ((0, 0), (0, 128 - N_EXP))))
    return jnp.stack([hi, lo])


MOD_TN = 1024


def _mod_body(cc_ref, w_ref, b_ref, o_ref):
    a = cc_ref[...]
    a = a * jax.nn.sigmoid(a)
    o_ref[...] = jnp.dot(a.astype(BF16), w_ref[...].astype(BF16),
                         preferred_element_type=F32) + b_ref[...]


def _modulation(cc, mod_w, mod_b):
    n_col = 6 * D // MOD_TN
    return pl.pallas_call(
        _mod_body,
        grid=(DEPTH, n_col),
        in_specs=[
            pl.BlockSpec((16, D), lambda l, j: (0, 0)),
            pl.BlockSpec((None, D, MOD_TN), lambda l, j: (l, 0, j)),
            pl.BlockSpec((None, 1, MOD_TN), lambda l, j: (l, 0, j)),
        ],
        out_specs=pl.BlockSpec((None, 16, MOD_TN), lambda l, j: (l, 0, j)),
        out_shape=jax.ShapeDtypeStruct((DEPTH, 16, 6 * D), F32),
        compiler_params=_cparams(2),
        name="modulation",
    )(cc, mod_w, mod_b.reshape(DEPTH, 1, 6 * D))


IN_TM = 512
ROT_W = A_W + KV_W


def _in_body(xl_ref, xc_ref, m_ref, g_ref, w_ref, cos_ref, sin_ref, za_ref, zc_ref, uv_ref):
    x = _token_tile(xl_ref, xc_ref)
    h = _rms(x) * g_ref[...]
    h = h * (1.0 + m_ref[:, D:2 * D]) + m_ref[:, 0:D]
    z = jnp.dot(h.astype(BF16), w_ref[...], preferred_element_type=F32)
    qk = z[:, :ROT_W]
    lane = lax.broadcasted_iota(I32, qk.shape, 1)
    rot = jnp.where((lane & 16) == 0,
                    -pltpu.roll(qk, ROT_W - 16, 1), pltpu.roll(qk, 16, 1))
    qk = qk * cos_ref[...] + rot * sin_ref[...]
    za_ref[...] = jnp.concatenate(
        [qk[:, :A_W] * Q_SCALE, qk[:, A_W:], z[:, ROT_W:ROT_W + KV_W]], axis=1).astype(BF16)
    u0 = ROT_W + KV_W
    uv_ref[...] = jax.nn.gelu(z[:, u0:u0 + 2 * B_W]).astype(BF16)
    c0 = u0 + 2 * B_W
    zc_ref[...] = jnp.concatenate(
        [z[:, c0:c0 + C_W] * Q_SCALE, z[:, c0 + C_W:]], axis=1).astype(BF16)


def _in_proj(x_lat, x_ctx, ctx_first_row, mods, g, w_bf, cos_t, sin_t):
    tm = IN_TM
    n_t = N_ALL // tm

    def tab_idx(i):
        return (jnp.where(i < N_LAT // tm, i % (S // tm), S // tm), 0)

    return pl.pallas_call(
        _in_body,
        grid=(n_t,),
        in_specs=[
            *_token_specs(tm, ctx_first_row // tm),
            pl.BlockSpec((None, 1, 6 * D), lambda i: (_mod_row(i, tm), 0, 0)),
            pl.BlockSpec((1, D), lambda i: (0, 0)),
            pl.BlockSpec((D, IN_W), lambda i: (0, 0)),
            pl.BlockSpec((tm, ROT_W), tab_idx),
            pl.BlockSpec((tm, ROT_W), tab_idx),
        ],
        out_specs=[
            pl.BlockSpec((tm, A_W + 2 * KV_W), lambda i: (i, 0)),
            pl.BlockSpec((tm, 3 * C_W), lambda i: (i, 0)),
            pl.BlockSpec((tm, 2 * B_W), lambda i: (i, 0)),
        ],
        out_shape=[
            jax.ShapeDtypeStruct((N_ALL, A_W + 2 * KV_W), BF16),
            jax.ShapeDtypeStruct((N_ALL, 3 * C_W), BF16),
            jax.ShapeDtypeStruct((N_ALL, 2 * B_W), BF16),
        ],
        compiler_params=_cparams(1),
        name="in_proj",
    )(x_lat, x_ctx, mods, g, w_bf, cos_t, sin_t)


A_BAND = A_BLK + 2 * A_WIN
N_QB_LAT = N_LAT // A_BLK
QB_PER_SEQ = S // A_BLK
QB_PER_CTX = C // A_BLK


def _softmax_pv(s_list, v_list, extra_logit=None):
    m = s_list[0].max(axis=-1, keepdims=True)
    for s in s_list[1:]:
        m = jnp.maximum(m, s.max(axis=-1, keepdims=True))
    if extra_logit is not None:
        m = jnp.maximum(m, extra_logit)
    den = None
    out = None
    for s, v in zip(s_list, v_list):
        p = jnp.exp2(s - m)
        d = p.sum(axis=-1, keepdims=True)
        o = jnp.dot(p.astype(BF16), v, preferred_element_type=F32)
        den = d if den is None else den + d
        out = o if out is None else out + o
    if extra_logit is not None:
        den = den + jnp.exp2(extra_logit - m)
    return out / den


def _attn_a_body(sink_ref, q_ref, kb_ref, vb_ref, kc_ref, vc_ref, o_ref):
    i = pl.program_id(0)
    is_lat = i < N_QB_LAT
    n = i % QB_PER_SEQ
    start = pl.multiple_of(jnp.clip(n * A_BLK - A_WIN, 0, S - A_BAND), A_WIN)
    qpos = n * A_BLK + lax.broadcasted_iota(I32, (A_BLK, A_BAND), 0)
    kpos = start + lax.broadcasted_iota(I32, (A_BLK, A_BAND), 1)
    band_penalty = jnp.where(jnp.abs(kpos - qpos) <= jnp.where(is_lat, A_WIN, -1), 0.0, NEG)
    kb = kb_ref[pl.ds(start, A_BAND), :]
    vb = vb_ref[pl.ds(start, A_BAND), :]
    kc = kc_ref[...]
    vc = vc_ref[...]
    outs = []
    for h in range(A_HEADS):
        kv = h // (A_HEADS // A_KV)
        sl = slice(kv * HD, (kv + 1) * HD)
        q = q_ref[:, h * HD:(h + 1) * HD]
        s_b = lax.dot_general(q, kb[:, sl], NT_DIMS, preferred_element_type=F32) + band_penalty
        s_c = lax.dot_general(q, kc[:, sl], NT_DIMS, preferred_element_type=F32)
        outs.append(_softmax_pv([s_c, s_b], [vc[:, sl], vb[:, sl]], sink_ref[h] * LOG2E))
    o_ref[...] = jnp.concatenate(outs, axis=1).astype(BF16)


def _attn_a(za, sink, with_ctx):
    n_qb = N_QB_LAT + (N_CTX // A_BLK if with_ctx else 0)

    def bidx(i):
        return jnp.where(i < N_QB_LAT, i // QB_PER_SEQ, (i - N_QB_LAT) // QB_PER_CTX)

    k_col = A_W // KV_W
    v_col = k_col + 1
    return pl.pallas_call(
        _attn_a_body,
        grid=(n_qb,),
        in_specs=[
            pl.BlockSpec(memory_space=pltpu.SMEM),
            pl.BlockSpec((A_BLK, A_W), lambda i: (i, 0)),
            pl.BlockSpec((S, KV_W), lambda i: (bidx(i), k_col)),
            pl.BlockSpec((S, KV_W), lambda i: (bidx(i), v_col)),
            pl.BlockSpec((C, KV_W), lambda i: (N_LAT // C + bidx(i), k_col)),
            pl.BlockSpec((C, KV_W), lambda i: (N_LAT // C + bidx(i), v_col)),
        ],
        out_specs=pl.BlockSpec((A_BLK, A_W), lambda i: (i, 0)),
        out_shape=jax.ShapeDtypeStruct((n_qb * A_BLK, A_W), BF16),
        compiler_params=_cparams(1),
        name="attn_window",
    )(sink, za, za, za, za, za)


NA_R = 4
NA_TQ = NA_R * GRID_W
NA_KROWS = 12
NA_WIN = NA_KROWS * GRID_W
NA_STEPS = ROWS // NA_R
NA_PAIRS = NA_KROWS // 2
N_DR = 2 * C_WIN_R - 1


def _na_body(q_ref, k_ref, v_ref, kc_ref, vc_ref, tab_ref, o_ref):
    j = pl.program_id(1)
    is_lat = j < NA_STEPS
    r0 = jnp.minimum(j, NA_STEPS - 1) * NA_R
    u0 = jnp.clip(r0 - C_WIN_R // 2, 0, ROWS - NA_KROWS)
    k0 = pl.multiple_of(u0 * GRID_W, GRID_W)
    kw = k_ref[pl.ds(k0, NA_WIN), :]
    vw = v_ref[pl.ds(k0, NA_WIN), :]
    kc = kc_ref[...]
    vc = vc_ref[...]
    q = q_ref[...]
    left = lax.broadcasted_iota(I32, (1, 2 * GRID_W), 1) < GRID_W

    tab_idx = []
    penalty = []
    for rr in range(NA_R):
        r = r0 + rr
        start = jnp.clip(r - C_WIN_R // 2, 0, ROWS - C_WIN_R)
        idx_row = []
        pen_row = []
        for p in range(NA_PAIRS):
            kr = u0 + 2 * p
            idx_row.append(jnp.clip(kr - r + (C_WIN_R - 1), -1, N_DR - 1) + 1)
            pens = []
            for half in range(2):
                ok = jnp.logical_and(is_lat, jnp.logical_and(kr + half >= start,
                                                             kr + half < start + C_WIN_R))
                pens.append(jnp.where(ok, 0.0, NEG))
            pen_row.append(jnp.where(left, pens[0], pens[1]))
        tab_idx.append(idx_row)
        penalty.append(jnp.concatenate(pen_row, axis=1))

    outs = []
    for h in range(C_HEADS):
        sl = slice(h * HD, (h + 1) * HD)
        qh = q[:, sl]
        s_w = lax.dot_general(qh, kw[:, sl], NT_DIMS, preferred_element_type=F32)
        bias = jnp.concatenate([
            jnp.concatenate([tab_ref[h, pl.ds(tab_idx[rr][p], 1)][0] for p in range(NA_PAIRS)],
                            axis=1) + penalty[rr]
            for rr in range(NA_R)], axis=0)
        s_c = lax.dot_general(qh, kc[:, sl], NT_DIMS, preferred_element_type=F32)
        outs.append(_softmax_pv([s_c, s_w + bias], [vc[:, sl], vw[:, sl]]))
    o_ref[...] = jnp.concatenate(outs, axis=1).astype(BF16)


def _na(zc, tab, with_ctx):
    n_j = NA_STEPS + (1 if with_ctx else 0)

    def qidx(b, j):
        return jnp.where(j < NA_STEPS, b * NA_STEPS + j, N_LAT // NA_TQ + b)

    n_out = N_LAT + (N_CTX if with_ctx else 0)
    return pl.pallas_call(
        _na_body,
        grid=(B, n_j),
        in_specs=[
            pl.BlockSpec((NA_TQ, C_W), lambda b, j: (qidx(b, j), 0)),
            pl.BlockSpec((S, C_W), lambda b, j: (b, 1)),
            pl.BlockSpec((S, C_W), lambda b, j: (b, 2)),
            pl.BlockSpec((C, C_W), lambda b, j: (N_LAT // C + b, 1)),
            pl.BlockSpec((C, C_W), lambda b, j: (N_LAT // C + b, 2)),
            pl.BlockSpec((C_HEADS, N_DR + 1, GRID_W, 2 * GRID_W), lambda b, j: (0, 0, 0, 0)),
        ],
        out_specs=pl.BlockSpec((NA_TQ, C_W), lambda b, j: (qidx(b, j), 0)),
        out_shape=jax.ShapeDtypeStruct((n_out, C_W), BF16),
        compiler_params=_cparams(2),
        name="attn_neighbourhood",
    )(zc, zc, zc, zc, zc, tab)


def _na_bias_table(rpb):
    cq = np.arange(GRID_W)
    col_start = np.clip(cq - C_WIN_C // 2, 0, GRID_W - C_WIN_C)
    col_ok = (cq[None, :] >= col_start[:, None]) & (cq[None, :] < col_start[:, None] + C_WIN_C)
    dc = np.clip(cq[None, :] - cq[:, None], -(C_WIN_C - 1), C_WIN_C - 1) + (C_WIN_C - 1)
    n_dc = 2 * C_WIN_C - 1
    pick = (dc.reshape(-1)[None, :] == np.arange(n_dc)[:, None]).astype(np.float32)
    t = jnp.dot(rpb.astype(F32).reshape(-1, n_dc), jnp.asarray(pick), precision=HIGHEST)
    t = t.reshape(C_HEADS, N_DR, GRID_W, GRID_W) * LOG2E
    t = jnp.where(col_ok[None, None], t, NEG)
    zero = jnp.zeros((C_HEADS, 1, GRID_W, GRID_W), F32)
    ext = jnp.concatenate([zero, t, zero], axis=1)
    return jnp.concatenate([ext[:, :-1], ext[:, 1:]], axis=-1)


MG_TM = 256


def _merge_body(xl_ref, xc_ref, ya_ref, yc_ref, uv_ref, m_ref, gg_ref, sg_ref, sw_ref, sb_ref, bd_ref,
                wo_ref, g2_ref, rw_ref, xo_ref, hx_ref, lg_ref):
    u = uv_ref[:, :B_W].astype(F32)
    v = uv_ref[:, B_W:].astype(F32)
    vv_hi, vv_lo = _split_bf16(v * v)
    ms = (jnp.dot(vv_hi, bd_ref[...], preferred_element_type=F32)
          + jnp.dot(vv_lo, bd_ref[...], preferred_element_type=F32))
    vn = (v * lax.rsqrt(ms + EPS) * sg_ref[...]).astype(BF16)
    lane_grp = lax.broadcasted_iota(I32, (SGU_CHUNK, B_W), 1) // HD
    gates = []
    for c in range(MG_TM // SGU_CHUNK):
        vc = vn[c * SGU_CHUNK:(c + 1) * SGU_CHUNK, :]
        gate = sb_ref[...]
        acc = jnp.zeros((SGU_CHUNK, B_W), F32)
        for g in range(SGU_GROUPS):
            r = jnp.dot(sw_ref[g], vc, preferred_element_type=F32)
            acc = jnp.where(lane_grp == g, r, acc)
        gates.append(acc + gate)
    yb = u * jnp.concatenate(gates, axis=0)
    gg = gg_ref[...]
    ycat = jnp.concatenate([
        _rms(ya_ref[...].astype(F32)) * gg[:, :A_W],
        _rms(yb) * gg[:, A_W:A_W + B_W],
        _rms(yc_ref[...].astype(F32)) * gg[:, A_W + B_W:],
    ], axis=1)
    proj = jnp.dot(ycat.astype(BF16), wo_ref[...], preferred_element_type=F32)
    xn = _token_tile(xl_ref, xc_ref) + m_ref[:, 2 * D:3 * D] * proj
    xo_ref[...] = xn
    hx = _rms(xn) * g2_ref[...]
    hx = hx * (1.0 + m_ref[:, 4 * D:5 * D]) + m_ref[:, 3 * D:4 * D]
    hx_ref[...] = hx.astype(BF16)
    hx_hi, hx_lo = _split_bf16(hx)
    logits = (jnp.dot(hx_hi, rw_ref[0], preferred_element_type=F32)
              + jnp.dot(hx_hi, rw_ref[1], preferred_element_type=F32)
              + jnp.dot(hx_lo, rw_ref[0], preferred_element_type=F32))
    lg_ref[...] = logits.T[:N_EXP, :]


def _merge(x_lat, x_ctx, ctx_first_row, ya, yc, uv, mods, gg, sg, sw_bf, sb_tab, bd, wo_bf, g2, rw_t,
           n_rows):
    tm = MG_TM
    n_t = n_rows // tm
    const2 = lambda i: (0, 0)
    row = lambda i: (i, 0)
    return pl.pallas_call(
        _merge_body,
        grid=(n_t,),
        in_specs=[
            *_token_specs(tm, ctx_first_row // tm),
            pl.BlockSpec((tm, A_W), row),
            pl.BlockSpec((tm, C_W), row),
            pl.BlockSpec((tm, 2 * B_W), row),
            pl.BlockSpec((None, 1, 6 * D), lambda i: (_mod_row(i, tm), 0, 0)),
            pl.BlockSpec((1, D), const2),
            pl.BlockSpec((1, B_W), const2),
            pl.BlockSpec((SGU_GROUPS, SGU_CHUNK, SGU_CHUNK), lambda i: (0, 0, 0)),
            pl.BlockSpec((SGU_CHUNK, B_W), const2),
            pl.BlockSpec((B_W, B_W), const2),
            pl.BlockSpec((D, D), const2),
            pl.BlockSpec((1, D), const2),
            pl.BlockSpec((2, D, 128), lambda i: (0, 0, 0)),
        ],
        out_specs=[
            pl.BlockSpec((tm, D), row),
            pl.BlockSpec((tm, D), row),
            pl.BlockSpec((N_EXP, tm), lambda i: (0, i)),
        ],
        out_shape=[
            jax.ShapeDtypeStruct((n_rows, D), F32),
            jax.ShapeDtypeStruct((n_rows, D), BF16),
            jax.ShapeDtypeStruct((N_EXP, n_rows), F32),
        ],
        compiler_params=_cparams(1),
        name="merge",
    )(x_lat, x_ctx, ya, yc, uv, mods, gg, sg, sw_bf, sb_tab, bd, wo_bf, g2, rw_t)


RT_TM = 256
GRP_SZ = N_EXP // N_GRP
SLOT_ALIGN = 16


def _first_argmax(v, iota):
    m = v.max(axis=0, keepdims=True)
    idx = jnp.where(v == m, iota, float(v.shape[0])).min(axis=0, keepdims=True)
    return m, idx


def _stack_rows(rows, iota):
    out = jnp.zeros(iota.shape, F32)
    for r, v in enumerate(rows):
        out = jnp.where(iota == float(r), v, out)
    return out


def _route_body(lg_ref, rb_ref, w_ref, lp_ref, tc_ref):
    tm = RT_TM
    scores = jax.nn.sigmoid(lg_ref[...])
    sel = scores + rb_ref[...]
    iota_g = lax.broadcasted_iota(I32, (GRP_SZ, tm), 0).astype(F32)
    gs = []
    for g in range(N_GRP):
        v = sel[g * GRP_SZ:(g + 1) * GRP_SZ, :]
        m1, i1 = _first_argmax(v, iota_g)
        m2 = jnp.where(iota_g == i1, -jnp.inf, v).max(axis=0, keepdims=True)
        gs.append(m1 + m2)
    iota_n = lax.broadcasted_iota(I32, (N_GRP, tm), 0).astype(F32)
    gscore = _stack_rows(gs, iota_n)
    gsel = jnp.zeros((N_GRP, tm), F32)
    for _ in range(TOPK_GRP):
        _, gi = _first_argmax(gscore, iota_n)
        hit = iota_n == gi
        gsel = jnp.where(hit, 1.0, gsel)
        gscore = jnp.where(hit, -jnp.inf, gscore)
    emask = jnp.concatenate(
        [jnp.broadcast_to(gsel[g:g + 1, :], (GRP_SZ, tm)) for g in range(N_GRP)], axis=0)
    cand = jnp.where(emask > 0.5, sel, NEG)
    iota_e = lax.broadcasted_iota(I32, (N_EXP, tm), 0).astype(F32)
    hits = []
    ws = []
    member = jnp.zeros((N_EXP, tm), F32)
    for _ in range(TOP_K):
        _, ei = _first_argmax(cand, iota_e)
        hit = iota_e == ei
        hits.append(hit)
        ws.append(jnp.where(hit, scores, 0.0).sum(axis=0, keepdims=True))
        member = jnp.where(hit, 1.0, member)
        cand = jnp.where(hit, -jnp.inf, cand)
    wsum = ws[0]
    for w in ws[1:]:
        wsum = wsum + w
    iota_k = lax.broadcasted_iota(I32, (TOP_K, tm), 0).astype(F32)
    w_ref[...] = _stack_rows(ws, iota_k) / wsum * ROUTED_SCALE
    r_i = lax.broadcasted_iota(I32, (tm, tm), 0)
    c_i = lax.broadcasted_iota(I32, (tm, tm), 1)
    tri = jnp.where(r_i < c_i, 1.0, 0.0).astype(BF16)
    local = jnp.dot(member.astype(BF16), tri, preferred_element_type=F32)
    tile_cnt = member.sum(axis=1, keepdims=True)
    tc_ref[...] = tile_cnt
    aligned = jnp.ceil(tile_cnt / SLOT_ALIGN) * SLOT_ALIGN
    e_r = lax.broadcasted_iota(I32, (N_EXP, N_EXP), 0)
    e_c = lax.broadcasted_iota(I32, (N_EXP, N_EXP), 1)
    below = jnp.where(e_c < e_r, 1.0, 0.0)
    start = jnp.dot(below, jnp.broadcast_to(aligned, (N_EXP, 128)), precision=HIGHEST,
                    preferred_element_type=F32)[:, 0:1]
    pos = local + start
    lp_ref[...] = _stack_rows(
        [jnp.where(hit, pos, 0.0).sum(axis=0, keepdims=True) for hit in hits],
        iota_k).astype(I32)


def _route(logits_t, rbias, n_tok):
    tm = RT_TM
    tok = lambda i: (0, i)
    return pl.pallas_call(
        _route_body,
        grid=(n_tok // tm,),
        in_specs=[
            pl.BlockSpec((N_EXP, tm), tok),
            pl.BlockSpec((N_EXP, 1), lambda i: (0, 0)),
        ],
        out_specs=[
            pl.BlockSpec((TOP_K, tm), tok),
            pl.BlockSpec((TOP_K, tm), tok),
            pl.BlockSpec((None, N_EXP, 1), lambda i: (i, 0, 0)),
        ],
        out_shape=[
            jax.ShapeDtypeStruct((TOP_K, n_tok), F32),
            jax.ShapeDtypeStruct((TOP_K, n_tok), I32),
            jax.ShapeDtypeStruct((n_tok // tm, N_EXP, 1), F32),
        ],
        compiler_params=_cparams(1),
        name="route",
    )(logits_t, rbias)


DP_TM = RT_TM
SORT_ROWS = 3072
N_GRPS = SORT_ROWS // SLOT_ALIGN
MIN_SORT_ROWS = TOP_K * DP_TM
ISSUE_UNROLL = 4
WAIT_GROUPS = 32
N_BURSTS = 6
BURST_ROWS = SORT_ROWS // N_BURSTS
BURST_GRPS = BURST_ROWS // SLOT_ALIGN


def _start_groups(g_lo, g_hi, row_of, make_copy):
    n_full = (g_hi - g_lo) // ISSUE_UNROLL

    def body(q, carry):
        for s in range(ISSUE_UNROLL):
            g = g_lo + q * ISSUE_UNROLL + s
            make_copy(g, row_of(g)).start(priority=s % 2)
        return carry

    def tail(g, carry):
        make_copy(g, row_of(g)).start()
        return carry

    lax.fori_loop(0, n_full, body, 0)
    lax.fori_loop(g_lo + n_full * ISSUE_UNROLL, g_hi, tail, 0)


def _start_burst(q, ngrp, row_of, make_copy):
    _start_groups(jnp.minimum(q * BURST_GRPS, ngrp), jnp.minimum((q + 1) * BURST_GRPS, ngrp),
                  row_of, make_copy)


def _wait_groups(ngrp, make_copy, make_bulk):
    def bulk(q, carry):
        make_bulk().wait()
        return carry

    def single(g, carry):
        make_copy(0, 0).wait()
        return carry

    n_bulk = ngrp // WAIT_GROUPS
    lax.fori_loop(0, n_bulk, bulk, 0)
    lax.fori_loop(n_bulk * WAIT_GROUPS, ngrp, single, 0)


def _dispatch_body(ngrp_ref, fill_ref, grow_ref, hx_ref, lp_ref, xs_ref, zbuf_ref, zero_ref, sems):
    i = pl.program_id(0)
    n_tiles = pl.num_programs(0)
    slot = i % 2

    def zero_copy(b):
        dst = xs_ref.at[pl.ds(pl.multiple_of(b * MOE_BLK, MOE_BLK), MOE_BLK)]
        return pltpu.make_async_copy(zero_ref, dst, sems.at[0])

    @pl.when(i == 0)
    def _():
        zero_ref[...] = jnp.zeros_like(zero_ref)
        n_fill = fill_ref[0]

        def z_start(q, carry):
            @pl.when(fill_ref[1 + q] >= 0)
            def _():
                zero_copy(fill_ref[1 + q]).start()
            return carry

        def z_wait(q, carry):
            @pl.when(fill_ref[1 + q] >= 0)
            def _():
                zero_copy(0).wait()
            return carry

        lax.fori_loop(0, n_fill, z_start, 0)
        lax.fori_loop(0, n_fill, z_wait, 0)

    def group_copy(s, g, row):
        src = zbuf_ref.at[s, pl.ds(pl.multiple_of(g * SLOT_ALIGN, SLOT_ALIGN), SLOT_ALIGN)]
        dst = xs_ref.at[pl.ds(pl.multiple_of(row, SLOT_ALIGN), SLOT_ALIGN)]
        return pltpu.make_async_copy(src, dst, sems.at[s])

    def bulk_copy(s):
        n_rows = WAIT_GROUPS * SLOT_ALIGN
        return pltpu.make_async_copy(
            zbuf_ref.at[s, pl.ds(0, n_rows)], xs_ref.at[pl.ds(0, n_rows)], sems.at[s])

    x = hx_ref[...]
    lpos = lp_ref[...]
    n_cur = ngrp_ref[i]
    pos16 = [jnp.broadcast_to(lpos[k:k + 1, :], (16, DP_TM)).astype(jnp.int16)
             for k in range(TOP_K)]

    def sort_rows(row0, n_rows):
        row_iota = (lax.broadcasted_iota(I32, (n_rows, DP_TM), 0) + row0).astype(jnp.int16)
        onehot = jnp.zeros((n_rows, DP_TM), BF16)
        for k in range(TOP_K):
            onehot = jnp.where(row_iota == jnp.tile(pos16[k], (n_rows // 16, 1)),
                               jnp.ones((), BF16), onehot)
        z = jnp.dot(onehot, x, preferred_element_type=F32)
        zbuf_ref[slot, pl.ds(row0, n_rows), :] = z.astype(BF16)
        _start_groups(jnp.minimum(row0 // SLOT_ALIGN, n_cur),
                      jnp.minimum((row0 + n_rows) // SLOT_ALIGN, n_cur),
                      lambda g: grow_ref[0, g], lambda g, r: group_copy(slot, g, r))

    sort_rows(0, MIN_SORT_ROWS)
    for row0 in range(MIN_SORT_ROWS, SORT_ROWS, BURST_ROWS):
        pl.when(row0 // SLOT_ALIGN < n_cur)(functools.partial(sort_rows, row0, BURST_ROWS))

    @pl.when(i > 0)
    def _():
        _wait_groups(ngrp_ref[jnp.maximum(i - 1, 0)], lambda g, r: group_copy(1 - slot, 0, 0),
                     lambda: bulk_copy(1 - slot))

    @pl.when(i == n_tiles - 1)
    def _():
        _wait_groups(n_cur, lambda g, r: group_copy(slot, 0, 0), lambda: bulk_copy(slot))


def _dispatch(ngrp, fill, grp_row, hx, lpos, n_tok, n_slots):
    return pl.pallas_call(
        _dispatch_body,
        grid_spec=pltpu.PrefetchScalarGridSpec(
            num_scalar_prefetch=2,
            grid=(n_tok // DP_TM,),
            in_specs=[
                pl.BlockSpec((None, 1, N_GRPS), lambda i, *_: (i, 0, 0), memory_space=pltpu.SMEM),
                pl.BlockSpec((DP_TM, D), lambda i, *_: (i, 0)),
                pl.BlockSpec((TOP_K, DP_TM), lambda i, *_: (0, i)),
            ],
            out_specs=pl.BlockSpec(memory_space=pl.ANY),
            scratch_shapes=[
                pltpu.VMEM((2, SORT_ROWS, D), BF16),
                pltpu.VMEM((MOE_BLK, D), BF16),
                pltpu.SemaphoreType.DMA((2,)),
            ],
        ),
        out_shape=jax.ShapeDtypeStruct((n_slots, D), BF16),
        compiler_params=_cparams(1, vmem_mb=56),
        name="moe_dispatch",
    )(ngrp, fill, grp_row, hx, lpos)


def _experts_body(bexp_ref, xblk_ref, nused_ref, xs_ref, wg_ref, wu_ref, wd_ref, ys_ref):
    j = pl.program_id(0)

    @pl.when(j < nused_ref[0])
    def _():
        x = xs_ref[...]
        g = jnp.dot(x, wg_ref[...].astype(BF16), preferred_element_type=F32)
        u = jnp.dot(x, wu_ref[...].astype(BF16), preferred_element_type=F32)
        h = (g * jax.nn.sigmoid(g) * u).astype(BF16)
        y = jnp.dot(h, wd_ref[...].astype(BF16), preferred_element_type=F32)
        ys_ref[...] = y.astype(BF16)


def _experts(bexp, xblk, nused, xs, wg, wu, wd, n_blk, layer):
    w_idx = lambda j, be, xb, nu: (layer, be[j], 0, 0)
    return pl.pallas_call(
        _experts_body,
        grid_spec=pltpu.PrefetchScalarGridSpec(
            num_scalar_prefetch=3,
            grid=(n_blk,),
            in_specs=[
                pl.BlockSpec((MOE_BLK, D), lambda j, be, xb, nu: (xb[j], 0)),
                pl.BlockSpec((None, None, D, EXP_FF), w_idx),
                pl.BlockSpec((None, None, D, EXP_FF), w_idx),
                pl.BlockSpec((None, None, EXP_FF, D), w_idx),
            ],
            out_specs=pl.BlockSpec((MOE_BLK, D), lambda j, be, xb, nu: (xb[j], 0)),
        ),
        out_shape=jax.ShapeDtypeStruct((n_blk * MOE_BLK, D), BF16),
        input_output_aliases={3: 0},
        compiler_params=_cparams(1),
        name="moe_experts",
    )(bexp, xblk, nused, xs, wg, wu, wd)


def _combine_body(ngrp_ref, grow_ref, grow_next_ref, ys_ref, lp_ref, w_ref, x_ref, hx_ref, m_ref,
                  sg_ref, su_ref, sd_ref, fg_ref, o_ref, ybuf_ref, acc_ref, sems, *, final_norm):
    i = pl.program_id(0)
    n_tiles = pl.num_programs(0)
    slot = i % 2

    def group_copy(s, g, row):
        src = ys_ref.at[pl.ds(pl.multiple_of(row, SLOT_ALIGN), SLOT_ALIGN)]
        dst = ybuf_ref.at[s, pl.ds(pl.multiple_of(g * SLOT_ALIGN, SLOT_ALIGN), SLOT_ALIGN)]
        return pltpu.make_async_copy(src, dst, sems.at[s])

    def bulk_copy(s):
        n_rows = WAIT_GROUPS * SLOT_ALIGN
        return pltpu.make_async_copy(
            ys_ref.at[pl.ds(0, n_rows)], ybuf_ref.at[s, pl.ds(0, n_rows)], sems.at[s])

    @pl.when(i == 0)
    def _():
        ybuf_ref[...] = jnp.zeros_like(ybuf_ref)
        _start_groups(0, ngrp_ref[0], lambda g: grow_ref[0, g], lambda g, r: group_copy(0, g, r))

    n_next = jnp.where(i + 1 < n_tiles, ngrp_ref[jnp.minimum(i + 1, n_tiles - 1)], 0)

    def prefetch(q):
        _start_burst(q, n_next, lambda g: grow_next_ref[0, g],
                     lambda g, r: group_copy(1 - slot, g, r))

    hx = hx_ref[...]
    g = jnp.dot(hx, sg_ref[...], preferred_element_type=F32)
    u = jnp.dot(hx, su_ref[...], preferred_element_type=F32)
    h = (g * jax.nn.sigmoid(g) * u).astype(BF16)
    y = jnp.dot(h, sd_ref[...], preferred_element_type=F32)

    lpos = lp_ref[...]
    w = w_ref[...]
    n_own = ngrp_ref[i]
    _wait_groups(n_own, lambda g, r: group_copy(slot, 0, 0), lambda: bulk_copy(slot))
    pos128 = [jnp.broadcast_to(lpos[:, k:k + 1], (DP_TM, 128)).astype(jnp.int16)
              for k in range(TOP_K)]
    wts128 = [jnp.broadcast_to(w[:, k:k + 1], (DP_TM, 128)).astype(BF16) for k in range(TOP_K)]

    def unsorted(col0, n_cols):
        col_iota = (lax.broadcasted_iota(I32, (DP_TM, n_cols), 1) + col0).astype(jnp.int16)
        unsort = jnp.zeros((DP_TM, n_cols), BF16)
        for k in range(TOP_K):
            unsort = jnp.where(col_iota == jnp.tile(pos128[k], (1, n_cols // 128)),
                               jnp.tile(wts128[k], (1, n_cols // 128)), unsort)
        return jnp.dot(unsort, ybuf_ref[slot, pl.ds(col0, n_cols), :],
                       preferred_element_type=F32)

    acc_ref[...] = y
    for q in range(N_BURSTS):
        def add_chunk(q=q):
            acc_ref[...] += unsorted(q * BURST_ROWS, BURST_ROWS)

        if (q + 1) * BURST_ROWS <= MIN_SORT_ROWS:
            add_chunk()
        else:
            pl.when(q * BURST_GRPS < n_own)(add_chunk)
        prefetch(q)
    out = x_ref[...] + m_ref[:, 5 * D:6 * D] * acc_ref[...]
    if final_norm:
        out = _rms(out) * fg_ref[...]
    o_ref[...] = out


def _combine(ngrp, grp_row, ys, lpos_t, w_t, x, hx, mods, sg_bf, su_bf, sd_bf, fg, n_tok, final_norm):
    tm = DP_TM
    row = lambda i, *_: (i, 0)
    const2 = lambda i, *_: (0, 0)
    return pl.pallas_call(
        functools.partial(_combine_body, final_norm=final_norm),
        grid_spec=pltpu.PrefetchScalarGridSpec(
            num_scalar_prefetch=1,
            grid=(n_tok // tm,),
            in_specs=[
                pl.BlockSpec((None, 1, N_GRPS), lambda i, *_: (i, 0, 0), memory_space=pltpu.SMEM),
                pl.BlockSpec((None, 1, N_GRPS), lambda i, *_: (jnp.minimum(i + 1, n_tok // tm - 1), 0, 0),
                             memory_space=pltpu.SMEM),
                pl.BlockSpec(memory_space=pl.ANY),
                pl.BlockSpec((tm, TOP_K), row),
                pl.BlockSpec((tm, TOP_K), row),
                pl.BlockSpec((tm, D), row),
                pl.BlockSpec((tm, D), row),
                pl.BlockSpec((None, 1, 6 * D), lambda i, *_: (_mod_row(i, tm), 0, 0)),
                pl.BlockSpec((D, EXP_FF), const2),
                pl.BlockSpec((D, EXP_FF), const2),
                pl.BlockSpec((EXP_FF, D), const2),
                pl.BlockSpec((1, D), const2),
            ],
            out_specs=pl.BlockSpec((tm, D), row),
            scratch_shapes=[pltpu.VMEM((2, SORT_ROWS, D), BF16), pltpu.VMEM((tm, D), F32),
                            pltpu.SemaphoreType.DMA((2,))],
        ),
        out_shape=jax.ShapeDtypeStruct((n_tok, D), F32),
        compiler_params=_cparams(1, vmem_mb=56),
        name="moe_combine",
    )(ngrp, grp_row, grp_row, ys, lpos_t, w_t, x, hx, mods, sg_bf, su_bf, sd_bf, fg)


def _moe(x_new, hx, logits_t, mods, rbias, wg, wu, wd, sg_bf, su_bf, sd_bf, fg, n_tok, layer,
         final_norm):
    w, lpos, tile_cnt = _route(logits_t, rbias, n_tok)
    n_tiles = n_tok // DP_TM
    n_blk = (n_tok * TOP_K + (SLOT_ALIGN - 1) * N_EXP * n_tiles) // MOE_BLK + N_EXP
    tcnt = tile_cnt[:, :, 0].astype(I32)
    cnt_al = (tcnt + SLOT_ALIGN - 1) // SLOT_ALIGN * SLOT_ALIGN
    loc_end = jnp.cumsum(cnt_al, axis=1)
    loc = loc_end - cnt_al
    slots_e = jnp.sum(cnt_al, axis=0)
    nblk_e = (slots_e + MOE_BLK - 1) // MOE_BLK
    blk_end = jnp.cumsum(nblk_e)
    pstart = (blk_end - nblk_e) * MOE_BLK
    nused = blk_end[-1:].astype(I32)
    blk_ids = jnp.arange(n_blk, dtype=I32)
    xblk = jnp.minimum(blk_ids, nused[0] - 1)
    bexp = jnp.minimum(
        jnp.sum(blk_end[None, :] <= xblk[:, None], axis=1), N_EXP - 1).astype(I32)
    off = pstart[None, :] + jnp.cumsum(cnt_al, axis=0) - cnt_al
    g_row = jnp.arange(N_GRPS, dtype=I32) * SLOT_ALIGN
    e_of_g = jnp.minimum(
        jnp.sum(loc_end[:, None, :] <= g_row[None, :, None], axis=2), N_EXP - 1)
    pick = e_of_g[..., None] == jnp.arange(N_EXP, dtype=I32)
    grp_row = (jnp.sum(jnp.where(pick, (off - loc)[:, None, :], 0), axis=2)
               + g_row[None, :]).astype(I32).reshape(n_tiles, 1, N_GRPS)
    ngrp = (loc_end[:, -1] // SLOT_ALIGN).astype(I32)
    unused = nused[0] + blk_ids
    fill = jnp.concatenate([
        (N_EXP + n_blk - nused[0])[None],
        jnp.where(nblk_e > 0, blk_end - 1, -1),
        jnp.where(unused < n_blk, unused, -1)]).astype(I32)
    xs = _dispatch(ngrp, fill, grp_row, hx, lpos, n_tok, n_blk * MOE_BLK)
    ys = _experts(bexp, xblk, nused, xs, wg, wu, wd, n_blk, layer)
    return _combine(ngrp, grp_row, ys, lpos.T, w.T, x_new, hx, mods, sg_bf, su_bf, sd_bf, fg,
                    n_tok, final_norm)


def _rope_tables():
    t = np.arange(S)
    row = (t // GRID_W).astype(np.float32)
    col = (t % GRID_W).astype(np.float32)
    half = HD // 2
    inv = jnp.asarray(ROPE_BASE, F32) ** (-jnp.arange(0, half, 2, dtype=F32) / half)
    ang_r = jnp.asarray(row)[:, None] * inv
    ang_c = jnp.asarray(col)[:, None] * inv
    ang = jnp.concatenate([ang_r, ang_r, ang_c, ang_c], axis=-1)
    n_rep = ROT_W // HD
    cos = jnp.tile(jnp.cos(ang), (1, n_rep))
    sin = jnp.tile(jnp.sin(ang), (1, n_rep))
    cos = jnp.concatenate([cos, jnp.ones((IN_TM, ROT_W), F32)], axis=0)
    sin = jnp.concatenate([sin, jnp.zeros((IN_TM, ROT_W), F32)], axis=0)
    return cos, sin


def kernel(x, c, ctx, c_ctx, mod_w, mod_b, norm1_g, w_in, attn_sink, sgu_norm_g, sgu_w, sgu_b, na_rpb,
           group_norm_g, w_out, norm2_g, router_w, router_bias, exp_w_gate, exp_w_up, exp_w_down,
           shared_w_gate, shared_w_up, shared_w_down, final_g):
    tokens = (x.reshape(N_LAT, D), ctx.reshape(N_CTX, D), 0)
    cc = jnp.concatenate([c, c_ctx[None, :], jnp.zeros((16 - B - 1, D), F32)], axis=0)
    mods_all = _modulation(cc, mod_w, mod_b)
    cos_t, sin_t = _rope_tables()
    seg = np.arange(B_W) // HD
    bd = jnp.asarray((seg[:, None] == seg[None, :]).astype(np.float32) / HD, BF16)
    fg = final_g.reshape(1, D)

    out = None
    for l in range(DEPTH):
        last = l == DEPTH - 1
        mods = mods_all[l].reshape(16, 1, 6 * D)
        za, zc, uv = _in_proj(*tokens, mods, norm1_g[l].reshape(1, D), w_in[l].astype(BF16), cos_t, sin_t)
        ya = _attn_a(za, attn_sink[l], with_ctx=not last)
        yc = _na(zc, _na_bias_table(na_rpb[l]), with_ctx=not last)
        n_tok = N_LAT if last else N_ALL
        sb_tab = jnp.repeat(sgu_b[l].T, HD, axis=1)
        x_new, hx, logits_t = _merge(
            *tokens, ya, yc, uv, mods, group_norm_g[l].reshape(1, D), sgu_norm_g[l].reshape(1, B_W),
            sgu_w[l].astype(BF16), sb_tab, bd, w_out[l].astype(BF16), norm2_g[l].reshape(1, D),
            _router_split(router_w[l]), n_tok)
        res = _moe(x_new, hx, logits_t, mods, router_bias[l].reshape(N_EXP, 1),
                   exp_w_gate, exp_w_up, exp_w_down,
                   shared_w_gate[l].astype(BF16), shared_w_up[l].astype(BF16),
                   shared_w_down[l].astype(BF16), fg, n_tok, layer=l, final_norm=last)
        if last:
            out = res.reshape(B, S, D)
        else:
            tokens = (res, res, N_LAT)
    return out
```

```python
import functools
import math

import jax
import jax.numpy as jnp
import numpy as np
from jax import lax
from jax.experimental import pallas as pl
from jax.experimental.pallas import tpu as pltpu

F32 = jnp.float32
BF16 = jnp.bfloat16
I32 = jnp.int32

D = 1024
B = 8
S = 2048
C = 256
DEPTH = 2
GRID_W = 64
HD = 64
A_HEADS = 6
A_KV = 2
A_WIN = 128
A_BLK = 256
ROPE_BASE = 10000.0
SGU_GROUPS = 4
SGU_CHUNK = 128
C_HEADS = 6
C_WIN_R = 8
C_WIN_C = 16
A_W = A_HEADS * HD
B_W = SGU_GROUPS * HD
C_W = C_HEADS * HD
KV_W = A_KV * HD
IN_W = A_W + 2 * KV_W + 2 * B_W + 3 * C_W
N_EXP = 64
TOP_K = 8
N_GRP = 8
TOPK_GRP = 4
EXP_FF = 256
ROUTED_SCALE = 2.5
MOE_BLK = 1024
EPS = 1e-6
NEG = -1e30
SCALE = HD ** -0.5
LOG2E = math.log2(math.e)
Q_SCALE = SCALE * LOG2E

N_LAT = B * S
N_CTX = B * C
N_ALL = N_LAT + N_CTX
ROWS = S // GRID_W

HIGHEST = lax.Precision.HIGHEST
ARB = pltpu.ARBITRARY

NT_DIMS = (((1,), (1,)), ((), ()))


def _cparams(n_axes, vmem_mb=48):
    return pltpu.CompilerParams(
        dimension_semantics=(ARB,) * n_axes, vmem_limit_bytes=vmem_mb * 1024 * 1024)


def _mod_row(i, tm):
    return jnp.where(i < N_LAT // tm, i // (S // tm), B)


def _rms(x):
    return x * lax.rsqrt(jnp.mean(x * x, axis=-1, keepdims=True) + EPS)


def _token_specs(tm, ctx_first_tile):
    n_lat = N_LAT // tm
    lat = pl.BlockSpec((tm, D), lambda i: (jnp.minimum(i, n_lat - 1), 0))
    ctx = pl.BlockSpec((tm, D), lambda i: (ctx_first_tile + jnp.maximum(i - n_lat, 0), 0))
    return lat, ctx


def _token_tile(lat_ref, ctx_ref):
    tm = lat_ref.shape[0]
    n_from_lat = jnp.where(pl.program_id(0) < N_LAT // tm, tm, 0)
    from_lat = lax.broadcasted_iota(I32, lat_ref.shape, 0) < n_from_lat
    return jnp.where(from_lat, lat_ref[...], ctx_ref[...])


def _split_bf16(x):
    hi = x.astype(BF16)
    return hi, (x - hi.astype(F32)).astype(BF16)


def _router_split(rw):
    hi, lo = _split_bf16(jnp.pad(rw, ((0, 0), (0, 128 - N_EXP))))
    return jnp.stack([hi, lo])


MOD_TN = 1024


def _mod_body(cc_ref, w_ref, b_ref, o_ref):
    a = cc_ref[...]
    a = a * jax.nn.sigmoid(a)
    o_ref[...] = jnp.dot(a.astype(BF16), w_ref[...].astype(BF16),
                         preferred_element_type=F32) + b_ref[...]


def _modulation(cc, mod_w, mod_b):
    n_col = 6 * D // MOD_TN
    return pl.pallas_call(
        _mod_body,
        grid=(DEPTH, n_col),
        in_specs=[
            pl.BlockSpec((16, D), lambda l, j: (0, 0)),
            pl.BlockSpec((None, D, MOD_TN), lambda l, j: (l, 0, j)),
            pl.BlockSpec((None, 1, MOD_TN), lambda l, j: (l, 0, j)),
        ],
        out_specs=pl.BlockSpec((None, 16, MOD_TN), lambda l, j: (l, 0, j)),
        out_shape=jax.ShapeDtypeStruct((DEPTH, 16, 6 * D), F32),
        compiler_params=_cparams(2),
        name="modulation",
    )(cc, mod_w, mod_b.reshape(DEPTH, 1, 6 * D))


IN_TM = 512
ROT_W = A_W + KV_W


def _in_body(xl_ref, xc_ref, m_ref, g_ref, w_ref, cos_ref, sin_ref, za_ref, zc_ref, uv_ref):
    x = _token_tile(xl_ref, xc_ref)
    h = _rms(x) * g_ref[...]
    h = h * (1.0 + m_ref[:, D:2 * D]) + m_ref[:, 0:D]
    z = jnp.dot(h.astype(BF16), w_ref[...], preferred_element_type=F32)
    qk = z[:, :ROT_W]
    lane = lax.broadcasted_iota(I32, qk.shape, 1)
    rot = jnp.where((lane & 16) == 0,
                    -pltpu.roll(qk, ROT_W - 16, 1), pltpu.roll(qk, 16, 1))
    qk = qk * cos_ref[...] + rot * sin_ref[...]
    za_ref[...] = jnp.concatenate(
        [qk[:, :A_W] * Q_SCALE, qk[:, A_W:], z[:, ROT_W:ROT_W + KV_W]], axis=1).astype(BF16)
    u0 = ROT_W + KV_W
    uv_ref[...] = jax.nn.gelu(z[:, u0:u0 + 2 * B_W]).astype(BF16)
    c0 = u0 + 2 * B_W
    zc_ref[...] = jnp.concatenate(
        [z[:, c0:c0 + C_W] * Q_SCALE, z[:, c0 + C_W:]], axis=1).astype(BF16)


def _in_proj(x_lat, x_ctx, ctx_first_row, mods, g, w_bf, cos_t, sin_t):
    tm = IN_TM
    n_t = N_ALL // tm

    def tab_idx(i):
        return (jnp.where(i < N_LAT // tm, i % (S // tm), S // tm), 0)

    return pl.pallas_call(
        _in_body,
        grid=(n_t,),
        in_specs=[
            *_token_specs(tm, ctx_first_row // tm),
            pl.BlockSpec((None, 1, 6 * D), lambda i: (_mod_row(i, tm), 0, 0)),
            pl.BlockSpec((1, D), lambda i: (0, 0)),
            pl.BlockSpec((D, IN_W), lambda i: (0, 0)),
            pl.BlockSpec((tm, ROT_W), tab_idx),
            pl.BlockSpec((tm, ROT_W), tab_idx),
        ],
        out_specs=[
            pl.BlockSpec((tm, A_W + 2 * KV_W), lambda i: (i, 0)),
            pl.BlockSpec((tm, 3 * C_W), lambda i: (i, 0)),
            pl.BlockSpec((tm, 2 * B_W), lambda i: (i, 0)),
        ],
        out_shape=[
            jax.ShapeDtypeStruct((N_ALL, A_W + 2 * KV_W), BF16),
            jax.ShapeDtypeStruct((N_ALL, 3 * C_W), BF16),
            jax.ShapeDtypeStruct((N_ALL, 2 * B_W), BF16),
        ],
        compiler_params=_cparams(1),
        name="in_proj",
    )(x_lat, x_ctx, mods, g, w_bf, cos_t, sin_t)


A_BAND = A_BLK + 2 * A_WIN
N_QB_LAT = N_LAT // A_BLK
QB_PER_SEQ = S // A_BLK
QB_PER_CTX = C // A_BLK


def _softmax_pv(s_list, v_list, extra_logit=None):
    m = s_list[0].max(axis=-1, keepdims=True)
    for s in s_list[1:]:
        m = jnp.maximum(m, s.max(axis=-1, keepdims=True))
    if extra_logit is not None:
        m = jnp.maximum(m, extra_logit)
    den = None
    out = None
    for s, v in zip(s_list, v_list):
        p = jnp.exp2(s - m)
        d = p.sum(axis=-1, keepdims=True)
        o = jnp.dot(p.astype(BF16), v, preferred_element_type=F32)
        den = d if den is None else den + d
        out = o if out is None else out + o
    if extra_logit is not None:
        den = den + jnp.exp2(extra_logit - m)
    return out / den


def _attn_a_body(sink_ref, q_ref, kb_ref, vb_ref, kc_ref, vc_ref, o_ref):
    j = pl.program_id(1)
    is_lat = j < QB_PER_SEQ
    n = jnp.minimum(j, QB_PER_SEQ - 1)
    start = pl.multiple_of(jnp.clip(n * A_BLK - A_WIN, 0, S - A_BAND), A_WIN)
    qpos = n * A_BLK + lax.broadcasted_iota(I32, (A_BLK, A_BAND), 0)
    kpos = start + lax.broadcasted_iota(I32, (A_BLK, A_BAND), 1)
    band_penalty = jnp.where(jnp.abs(kpos - qpos) <= jnp.where(is_lat, A_WIN, -1), 0.0, NEG)
    kb = kb_ref[pl.ds(start, A_BAND), :]
    vb = vb_ref[pl.ds(start, A_BAND), :]
    kc = kc_ref[...]
    vc = vc_ref[...]
    outs = []
    for h in range(A_HEADS):
        kv = h // (A_HEADS // A_KV)
        sl = slice(kv * HD, (kv + 1) * HD)
        q = q_ref[:, h * HD:(h + 1) * HD]
        s_b = lax.dot_general(q, kb[:, sl], NT_DIMS, preferred_element_type=F32) + band_penalty
        s_c = lax.dot_general(q, kc[:, sl], NT_DIMS, preferred_element_type=F32)
        outs.append(_softmax_pv([s_c, s_b], [vc[:, sl], vb[:, sl]], sink_ref[h] * LOG2E))
    o_ref[...] = jnp.concatenate(outs, axis=1).astype(BF16)


NA_R = 4
NA_TQ = NA_R * GRID_W
NA_KROWS = 12
NA_WIN = NA_KROWS * GRID_W
NA_STEPS = ROWS // NA_R
NA_PAIRS = NA_KROWS // 2
N_DR = 2 * C_WIN_R - 1


def _na_body(q_ref, k_ref, v_ref, kc_ref, vc_ref, tab_ref, o_ref):
    j = pl.program_id(1)
    is_lat = j < NA_STEPS
    r0 = jnp.minimum(j, NA_STEPS - 1) * NA_R
    u0 = jnp.clip(r0 - C_WIN_R // 2, 0, ROWS - NA_KROWS)
    k0 = pl.multiple_of(u0 * GRID_W, GRID_W)
    kw = k_ref[pl.ds(k0, NA_WIN), :]
    vw = v_ref[pl.ds(k0, NA_WIN), :]
    kc = kc_ref[...]
    vc = vc_ref[...]
    q = q_ref[...]
    left = lax.broadcasted_iota(I32, (1, 2 * GRID_W), 1) < GRID_W

    tab_idx = []
    penalty = []
    for rr in range(NA_R):
        r = r0 + rr
        start = jnp.clip(r - C_WIN_R // 2, 0, ROWS - C_WIN_R)
        idx_row = []
        pen_row = []
        for p in range(NA_PAIRS):
            kr = u0 + 2 * p
            idx_row.append(jnp.clip(kr - r + (C_WIN_R - 1), -1, N_DR - 1) + 1)
            pens = []
            for half in range(2):
                ok = jnp.logical_and(is_lat, jnp.logical_and(kr + half >= start,
                                                             kr + half < start + C_WIN_R))
                pens.append(jnp.where(ok, 0.0, NEG))
            pen_row.append(jnp.where(left, pens[0], pens[1]))
        tab_idx.append(idx_row)
        penalty.append(jnp.concatenate(pen_row, axis=1))

    low_q = lax.broadcasted_iota(I32, (NA_TQ, 2 * HD), 1) < HD
    outs = []
    for hp in range(C_HEADS // 2):
        sl = slice(2 * hp * HD, 2 * (hp + 1) * HD)
        q2, kw2, vw2, kc2, vc2 = q[:, sl], kw[:, sl], vw[:, sl], kc[:, sl], vc[:, sl]
        halves = []
        for half in range(2):
            h = 2 * hp + half
            qh = jnp.where(low_q if half == 0 else jnp.logical_not(low_q), q2, jnp.zeros_like(q2))
            s_w = lax.dot_general(qh, kw2, NT_DIMS, preferred_element_type=F32)
            bias = jnp.concatenate([
                jnp.concatenate([tab_ref[h, pl.ds(tab_idx[rr][p], 1)][0]
                                 for p in range(NA_PAIRS)], axis=1) + penalty[rr]
                for rr in range(NA_R)], axis=0)
            s_c = lax.dot_general(qh, kc2, NT_DIMS, preferred_element_type=F32)
            halves.append(_softmax_pv([s_c, s_w + bias], [vc2, vw2]))
        outs.append(jnp.where(low_q, halves[0], halves[1]))
    o_ref[...] = jnp.concatenate(outs, axis=1).astype(BF16)


def _mixers_body(sink_ref, qa_ref, kb_ref, vb_ref, kca_ref, vca_ref, qn_ref, kn_ref, vn_ref,
                 kcn_ref, vcn_ref, tab_ref, oa_ref, on_ref):
    _attn_a_body(sink_ref, qa_ref, kb_ref, vb_ref, kca_ref, vca_ref, oa_ref)
    _na_body(qn_ref, kn_ref, vn_ref, kcn_ref, vcn_ref, tab_ref, on_ref)


def _mixers(za, zc, sink, tab, with_ctx):
    assert A_BLK == NA_TQ and QB_PER_SEQ == NA_STEPS and QB_PER_CTX == 1
    n_j = NA_STEPS + (1 if with_ctx else 0)

    def qidx(b, j):
        return jnp.where(j < NA_STEPS, b * NA_STEPS + j, N_LAT // NA_TQ + b)

    q_row = lambda b, j: (qidx(b, j), 0)
    ctx_blk = lambda col: (lambda b, j: (N_LAT // C + b, col))
    seq_blk = lambda col: (lambda b, j: (b, col))
    ka_col = A_W // KV_W
    n_out = N_LAT + (N_CTX if with_ctx else 0)
    return pl.pallas_call(
        _mixers_body,
        grid=(B, n_j),
        in_specs=[
            pl.BlockSpec(memory_space=pltpu.SMEM),
            pl.BlockSpec((A_BLK, A_W), q_row),
            pl.BlockSpec((S, KV_W), seq_blk(ka_col)),
            pl.BlockSpec((S, KV_W), seq_blk(ka_col + 1)),
            pl.BlockSpec((C, KV_W), ctx_blk(ka_col)),
            pl.BlockSpec((C, KV_W), ctx_blk(ka_col + 1)),
            pl.BlockSpec((NA_TQ, C_W), q_row),
            pl.BlockSpec((S, C_W), seq_blk(1)),
            pl.BlockSpec((S, C_W), seq_blk(2)),
            pl.BlockSpec((C, C_W), ctx_blk(1)),
            pl.BlockSpec((C, C_W), ctx_blk(2)),
            pl.BlockSpec((C_HEADS, N_DR + 1, GRID_W, 2 * GRID_W), lambda b, j: (0, 0, 0, 0)),
        ],
        out_specs=[pl.BlockSpec((A_BLK, A_W), q_row), pl.BlockSpec((NA_TQ, C_W), q_row)],
        out_shape=[jax.ShapeDtypeStruct((n_out, A_W), BF16),
                   jax.ShapeDtypeStruct((n_out, C_W), BF16)],
        compiler_params=_cparams(2),
        name="attn_mixers",
    )(sink, za, za, za, za, za, zc, zc, zc, zc, zc, tab)


def _na_bias_table(rpb):
    cq = np.arange(GRID_W)
    col_start = np.clip(cq - C_WIN_C // 2, 0, GRID_W - C_WIN_C)
    col_ok = (cq[None, :] >= col_start[:, None]) & (cq[None, :] < col_start[:, None] + C_WIN_C)
    dc = np.clip(cq[None, :] - cq[:, None], -(C_WIN_C - 1), C_WIN_C - 1) + (C_WIN_C - 1)
    n_dc = 2 * C_WIN_C - 1
    pick = (dc.reshape(-1)[None, :] == np.arange(n_dc)[:, None]).astype(np.float32)
    t = jnp.dot(rpb.astype(F32).reshape(-1, n_dc), jnp.asarray(pick), precision=HIGHEST)
    t = t.reshape(C_HEADS, N_DR, GRID_W, GRID_W) * LOG2E
    t = jnp.where(col_ok[None, None], t, NEG)
    zero = jnp.zeros((C_HEADS, 1, GRID_W, GRID_W), F32)
    ext = jnp.concatenate([zero, t, zero], axis=1)
    return jnp.concatenate([ext[:, :-1], ext[:, 1:]], axis=-1)


MG_TM = 256


def _merge_body(xl_ref, xc_ref, ya_ref, yc_ref, uv_ref, m_ref, gg_ref, sg_ref, sw_ref, sb_ref, bd_ref,
                wo_ref, g2_ref, rw_ref, xo_ref, hx_ref, lg_ref):
    u = uv_ref[:, :B_W].astype(F32)
    v = uv_ref[:, B_W:].astype(F32)
    vv_hi, vv_lo = _split_bf16(v * v)
    ms = (jnp.dot(vv_hi, bd_ref[...], preferred_element_type=F32)
          + jnp.dot(vv_lo, bd_ref[...], preferred_element_type=F32))
    vn = (v * lax.rsqrt(ms + EPS) * sg_ref[...]).astype(BF16)
    lane_grp = lax.broadcasted_iota(I32, (SGU_CHUNK, B_W), 1) // HD
    gates = []
    for c in range(MG_TM // SGU_CHUNK):
        vc = vn[c * SGU_CHUNK:(c + 1) * SGU_CHUNK, :]
        gate = sb_ref[...]
        acc = jnp.zeros((SGU_CHUNK, B_W), F32)
        for g in range(SGU_GROUPS):
            r = jnp.dot(sw_ref[g], vc, preferred_element_type=F32)
            acc = jnp.where(lane_grp == g, r, acc)
        gates.append(acc + gate)
    yb = u * jnp.concatenate(gates, axis=0)
    gg = gg_ref[...]
    ycat = jnp.concatenate([
        _rms(ya_ref[...].astype(F32)) * gg[:, :A_W],
        _rms(yb) * gg[:, A_W:A_W + B_W],
        _rms(yc_ref[...].astype(F32)) * gg[:, A_W + B_W:],
    ], axis=1)
    proj = jnp.dot(ycat.astype(BF16), wo_ref[...], preferred_element_type=F32)
    xn = _token_tile(xl_ref, xc_ref) + m_ref[:, 2 * D:3 * D] * proj
    xo_ref[...] = xn
    hx = _rms(xn) * g2_ref[...]
    hx = hx * (1.0 + m_ref[:, 4 * D:5 * D]) + m_ref[:, 3 * D:4 * D]
    hx_ref[...] = hx.astype(BF16)
    hx_hi, hx_lo = _split_bf16(hx)
    logits = (jnp.dot(hx_hi, rw_ref[0], preferred_element_type=F32)
              + jnp.dot(hx_hi, rw_ref[1], preferred_element_type=F32)
              + jnp.dot(hx_lo, rw_ref[0], preferred_element_type=F32))
    lg_ref[...] = logits.T[:N_EXP, :]


def _merge(x_lat, x_ctx, ctx_first_row, ya, yc, uv, mods, gg, sg, sw_bf, sb_tab, bd, wo_bf, g2, rw_t,
           n_rows):
    tm = MG_TM
    n_t = n_rows // tm
    const2 = lambda i: (0, 0)
    row = lambda i: (i, 0)
    return pl.pallas_call(
        _merge_body,
        grid=(n_t,),
        in_specs=[
            *_token_specs(tm, ctx_first_row // tm),
            pl.BlockSpec((tm, A_W), row),
            pl.BlockSpec((tm, C_W), row),
            pl.BlockSpec((tm, 2 * B_W), row),
            pl.BlockSpec((None, 1, 6 * D), lambda i: (_mod_row(i, tm), 0, 0)),
            pl.BlockSpec((1, D), const2),
            pl.BlockSpec((1, B_W), const2),
            pl.BlockSpec((SGU_GROUPS, SGU_CHUNK, SGU_CHUNK), lambda i: (0, 0, 0)),
            pl.BlockSpec((SGU_CHUNK, B_W), const2),
            pl.BlockSpec((B_W, B_W), const2),
            pl.BlockSpec((D, D), const2),
            pl.BlockSpec((1, D), const2),
            pl.BlockSpec((2, D, 128), lambda i: (0, 0, 0)),
        ],
        out_specs=[
            pl.BlockSpec((tm, D), row),
            pl.BlockSpec((tm, D), row),
            pl.BlockSpec((N_EXP, tm), lambda i: (0, i)),
        ],
        out_shape=[
            jax.ShapeDtypeStruct((n_rows, D), F32),
            jax.ShapeDtypeStruct((n_rows, D), BF16),
            jax.ShapeDtypeStruct((N_EXP, n_rows), F32),
        ],
        compiler_params=_cparams(1),
        name="merge",
    )(x_lat, x_ctx, ya, yc, uv, mods, gg, sg, sw_bf, sb_tab, bd, wo_bf, g2, rw_t)


RT_TM = 256
GRP_SZ = N_EXP // N_GRP
SLOT_ALIGN = 16


def _first_argmax(v, iota):
    m = v.max(axis=0, keepdims=True)
    idx = jnp.where(v == m, iota, float(v.shape[0])).min(axis=0, keepdims=True)
    return m, idx


def _stack_rows(rows, iota):
    out = jnp.zeros(iota.shape, F32)
    for r, v in enumerate(rows):
        out = jnp.where(iota == float(r), v, out)
    return out


def _route_body(lg_ref, rb_ref, w_ref, lp_ref, tc_ref):
    tm = RT_TM
    scores = jax.nn.sigmoid(lg_ref[...])
    sel = scores + rb_ref[...]
    iota_g = lax.broadcasted_iota(I32, (GRP_SZ, tm), 0).astype(F32)
    gs = []
    for g in range(N_GRP):
        v = sel[g * GRP_SZ:(g + 1) * GRP_SZ, :]
        m1, i1 = _first_argmax(v, iota_g)
        m2 = jnp.where(iota_g == i1, -jnp.inf, v).max(axis=0, keepdims=True)
        gs.append(m1 + m2)
    iota_n = lax.broadcasted_iota(I32, (N_GRP, tm), 0).astype(F32)
    gscore = _stack_rows(gs, iota_n)
    gsel = jnp.zeros((N_GRP, tm), F32)
    for _ in range(TOPK_GRP):
        _, gi = _first_argmax(gscore, iota_n)
        hit = iota_n == gi
        gsel = jnp.where(hit, 1.0, gsel)
        gscore = jnp.where(hit, -jnp.inf, gscore)
    emask = jnp.concatenate(
        [jnp.broadcast_to(gsel[g:g + 1, :], (GRP_SZ, tm)) for g in range(N_GRP)], axis=0)
    cand = jnp.where(emask > 0.5, sel, NEG)
    iota_e = lax.broadcasted_iota(I32, (N_EXP, tm), 0).astype(F32)
    hits = []
    ws = []
    member = jnp.zeros((N_EXP, tm), F32)
    for _ in range(TOP_K):
        _, ei = _first_argmax(cand, iota_e)
        hit = iota_e == ei
        hits.append(hit)
        ws.append(jnp.where(hit, scores, 0.0).sum(axis=0, keepdims=True))
        member = jnp.where(hit, 1.0, member)
        cand = jnp.where(hit, -jnp.inf, cand)
    wsum = ws[0]
    for w in ws[1:]:
        wsum = wsum + w
    iota_k = lax.broadcasted_iota(I32, (TOP_K, tm), 0).astype(F32)
    w_ref[...] = _stack_rows(ws, iota_k) / wsum * ROUTED_SCALE
    r_i = lax.broadcasted_iota(I32, (tm, tm), 0)
    c_i = lax.broadcasted_iota(I32, (tm, tm), 1)
    tri = jnp.where(r_i < c_i, 1.0, 0.0).astype(BF16)
    local = jnp.dot(member.astype(BF16), tri, preferred_element_type=F32)
    tile_cnt = member.sum(axis=1, keepdims=True)
    tc_ref[...] = tile_cnt
    aligned = jnp.ceil(tile_cnt / SLOT_ALIGN) * SLOT_ALIGN
    e_r = lax.broadcasted_iota(I32, (N_EXP, N_EXP), 0)
    e_c = lax.broadcasted_iota(I32, (N_EXP, N_EXP), 1)
    below = jnp.where(e_c < e_r, 1.0, 0.0)
    start = jnp.dot(below, jnp.broadcast_to(aligned, (N_EXP, 128)), precision=HIGHEST,
                    preferred_element_type=F32)[:, 0:1]
    pos = local + start
    lp_ref[...] = _stack_rows(
        [jnp.where(hit, pos, 0.0).sum(axis=0, keepdims=True) for hit in hits],
        iota_k).astype(I32)


def _route(logits_t, rbias, n_tok):
    tm = RT_TM
    tok = lambda i: (0, i)
    return pl.pallas_call(
        _route_body,
        grid=(n_tok // tm,),
        in_specs=[
            pl.BlockSpec((N_EXP, tm), tok),
            pl.BlockSpec((N_EXP, 1), lambda i: (0, 0)),
        ],
        out_specs=[
            pl.BlockSpec((TOP_K, tm), tok),
            pl.BlockSpec((TOP_K, tm), tok),
            pl.BlockSpec((None, N_EXP, 1), lambda i: (i, 0, 0)),
        ],
        out_shape=[
            jax.ShapeDtypeStruct((TOP_K, n_tok), F32),
            jax.ShapeDtypeStruct((TOP_K, n_tok), I32),
            jax.ShapeDtypeStruct((n_tok // tm, N_EXP, 1), F32),
        ],
        compiler_params=_cparams(1),
        name="route",
    )(logits_t, rbias)


DP_TM = RT_TM
SORT_ROWS = 3072
N_GRPS = SORT_ROWS // SLOT_ALIGN
MIN_SORT_ROWS = TOP_K * DP_TM
ISSUE_UNROLL = 4
WAIT_GROUPS = 32
N_BURSTS = 6
BURST_ROWS = SORT_ROWS // N_BURSTS
BURST_GRPS = BURST_ROWS // SLOT_ALIGN


def _start_groups(g_lo, g_hi, row_of, make_copy):
    n_full = (g_hi - g_lo) // ISSUE_UNROLL

    def body(q, carry):
        for s in range(ISSUE_UNROLL):
            g = g_lo + q * ISSUE_UNROLL + s
            make_copy(g, row_of(g)).start(priority=s % 2)
        return carry

    def tail(g, carry):
        make_copy(g, row_of(g)).start()
        return carry

    lax.fori_loop(0, n_full, body, 0)
    lax.fori_loop(g_lo + n_full * ISSUE_UNROLL, g_hi, tail, 0)


def _start_burst(q, ngrp, row_of, make_copy):
    _start_groups(jnp.minimum(q * BURST_GRPS, ngrp), jnp.minimum((q + 1) * BURST_GRPS, ngrp),
                  row_of, make_copy)


def _wait_groups(ngrp, make_copy, make_bulk):
    def bulk(q, carry):
        make_bulk().wait()
        return carry

    def single(g, carry):
        make_copy(0, 0).wait()
        return carry

    n_bulk = ngrp // WAIT_GROUPS
    lax.fori_loop(0, n_bulk, bulk, 0)
    lax.fori_loop(n_bulk * WAIT_GROUPS, ngrp, single, 0)


def _dispatch_body(ngrp_ref, fill_ref, grow_ref, hx_ref, lp_ref, xs_ref, zbuf_ref, zero_ref, sems):
    i = pl.program_id(0)
    n_tiles = pl.num_programs(0)
    slot = i % 2

    def zero_copy(b):
        dst = xs_ref.at[pl.ds(pl.multiple_of(b * MOE_BLK, MOE_BLK), MOE_BLK)]
        return pltpu.make_async_copy(zero_ref, dst, sems.at[0])

    @pl.when(i == 0)
    def _():
        zero_ref[...] = jnp.zeros_like(zero_ref)
        n_fill = fill_ref[0]

        def z_start(q, carry):
            @pl.when(fill_ref[1 + q] >= 0)
            def _():
                zero_copy(fill_ref[1 + q]).start()
            return carry

        def z_wait(q, carry):
            @pl.when(fill_ref[1 + q] >= 0)
            def _():
                zero_copy(0).wait()
            return carry

        lax.fori_loop(0, n_fill, z_start, 0)
        lax.fori_loop(0, n_fill, z_wait, 0)

    def group_copy(s, g, row):
        src = zbuf_ref.at[s, pl.ds(pl.multiple_of(g * SLOT_ALIGN, SLOT_ALIGN), SLOT_ALIGN)]
        dst = xs_ref.at[pl.ds(pl.multiple_of(row, SLOT_ALIGN), SLOT_ALIGN)]
        return pltpu.make_async_copy(src, dst, sems.at[s])

    def bulk_copy(s):
        n_rows = WAIT_GROUPS * SLOT_ALIGN
        return pltpu.make_async_copy(
            zbuf_ref.at[s, pl.ds(0, n_rows)], xs_ref.at[pl.ds(0, n_rows)], sems.at[s])

    x = hx_ref[...]
    lpos = lp_ref[...]
    n_cur = ngrp_ref[i]
    pos16 = [jnp.broadcast_to(lpos[k:k + 1, :], (16, DP_TM)).astype(jnp.int16)
             for k in range(TOP_K)]

    def sort_rows(row0, n_rows):
        row_iota = (lax.broadcasted_iota(I32, (n_rows, DP_TM), 0) + row0).astype(jnp.int16)
        onehot = jnp.zeros((n_rows, DP_TM), BF16)
        for k in range(TOP_K):
            onehot = jnp.where(row_iota == jnp.tile(pos16[k], (n_rows // 16, 1)),
                               jnp.ones((), BF16), onehot)
        z = jnp.dot(onehot, x, preferred_element_type=F32)
        zbuf_ref[slot, pl.ds(row0, n_rows), :] = z.astype(BF16)
        _start_groups(jnp.minimum(row0 // SLOT_ALIGN, n_cur),
                      jnp.minimum((row0 + n_rows) // SLOT_ALIGN, n_cur),
                      lambda g: grow_ref[0, g], lambda g, r: group_copy(slot, g, r))

    sort_rows(0, MIN_SORT_ROWS)
    for row0 in range(MIN_SORT_ROWS, SORT_ROWS, BURST_ROWS):
        pl.when(row0 // SLOT_ALIGN < n_cur)(functools.partial(sort_rows, row0, BURST_ROWS))

    @pl.when(i > 0)
    def _():
        _wait_groups(ngrp_ref[jnp.maximum(i - 1, 0)], lambda g, r: group_copy(1 - slot, 0, 0),
                     lambda: bulk_copy(1 - slot))

    @pl.when(i == n_tiles - 1)
    def _():
        _wait_groups(n_cur, lambda g, r: group_copy(slot, 0, 0), lambda: bulk_copy(slot))


def _dispatch(ngrp, fill, grp_row, hx, lpos, n_tok, n_slots):
    return pl.pallas_call(
        _dispatch_body,
        grid_spec=pltpu.PrefetchScalarGridSpec(
            num_scalar_prefetch=2,
            grid=(n_tok // DP_TM,),
            in_specs=[
                pl.BlockSpec((None, 1, N_GRPS), lambda i, *_: (i, 0, 0), memory_space=pltpu.SMEM),
                pl.BlockSpec((DP_TM, D), lambda i, *_: (i, 0)),
                pl.BlockSpec((TOP_K, DP_TM), lambda i, *_: (0, i)),
            ],
            out_specs=pl.BlockSpec(memory_space=pl.ANY),
            scratch_shapes=[
                pltpu.VMEM((2, SORT_ROWS, D), BF16),
                pltpu.VMEM((MOE_BLK, D), BF16),
                pltpu.SemaphoreType.DMA((2,)),
            ],
        ),
        out_shape=jax.ShapeDtypeStruct((n_slots, D), BF16),
        compiler_params=_cparams(1, vmem_mb=56),
        name="moe_dispatch",
    )(ngrp, fill, grp_row, hx, lpos)


def _experts_body(bexp_ref, xblk_ref, nused_ref, xs_ref, wg_ref, wu_ref, wd_ref, ys_ref):
    j = pl.program_id(0)

    @pl.when(j < nused_ref[0])
    def _():
        x = xs_ref[...]
        g = jnp.dot(x, wg_ref[...].astype(BF16), preferred_element_type=F32)
        u = jnp.dot(x, wu_ref[...].astype(BF16), preferred_element_type=F32)
        h = (g * jax.nn.sigmoid(g) * u).astype(BF16)
        y = jnp.dot(h, wd_ref[...].astype(BF16), preferred_element_type=F32)
        ys_ref[...] = y.astype(BF16)


def _experts(bexp, xblk, nused, xs, wg, wu, wd, n_blk, layer):
    w_idx = lambda j, be, xb, nu: (layer, be[j], 0, 0)
    return pl.pallas_call(
        _experts_body,
        grid_spec=pltpu.PrefetchScalarGridSpec(
            num_scalar_prefetch=3,
            grid=(n_blk,),
            in_specs=[
                pl.BlockSpec((MOE_BLK, D), lambda j, be, xb, nu: (xb[j], 0)),
                pl.BlockSpec((None, None, D, EXP_FF), w_idx),
                pl.BlockSpec((None, None, D, EXP_FF), w_idx),
                pl.BlockSpec((None, None, EXP_FF, D), w_idx),
            ],
            out_specs=pl.BlockSpec((MOE_BLK, D), lambda j, be, xb, nu: (xb[j], 0)),
        ),
        out_shape=jax.ShapeDtypeStruct((n_blk * MOE_BLK, D), BF16),
        input_output_aliases={3: 0},
        compiler_params=_cparams(1),
        name="moe_experts",
    )(bexp, xblk, nused, xs, wg, wu, wd)


def _combine_body(ngrp_ref, grow_ref, grow_next_ref, ys_ref, lp_ref, w_ref, x_ref, hx_ref, m_ref,
                  sg_ref, su_ref, sd_ref, fg_ref, o_ref, ybuf_ref, acc_ref, sems, *, final_norm):
    i = pl.program_id(0)
    n_tiles = pl.num_programs(0)
    slot = i % 2

    def group_copy(s, g, row):
        src = ys_ref.at[pl.ds(pl.multiple_of(row, SLOT_ALIGN), SLOT_ALIGN)]
        dst = ybuf_ref.at[s, pl.ds(pl.multiple_of(g * SLOT_ALIGN, SLOT_ALIGN), SLOT_ALIGN)]
        return pltpu.make_async_copy(src, dst, sems.at[s])

    def bulk_copy(s):
        n_rows = WAIT_GROUPS * SLOT_ALIGN
        return pltpu.make_async_copy(
            ys_ref.at[pl.ds(0, n_rows)], ybuf_ref.at[s, pl.ds(0, n_rows)], sems.at[s])

    @pl.when(i == 0)
    def _():
        ybuf_ref[...] = jnp.zeros_like(ybuf_ref)
        _start_groups(0, ngrp_ref[0], lambda g: grow_ref[0, g], lambda g, r: group_copy(0, g, r))

    n_next = jnp.where(i + 1 < n_tiles, ngrp_ref[jnp.minimum(i + 1, n_tiles - 1)], 0)

    def prefetch(q):
        _start_burst(q, n_next, lambda g: grow_next_ref[0, g],
                     lambda g, r: group_copy(1 - slot, g, r))

    hx = hx_ref[...]
    g = jnp.dot(hx, sg_ref[...], preferred_element_type=F32)
    u = jnp.dot(hx, su_ref[...], preferred_element_type=F32)
    h = (g * jax.nn.sigmoid(g) * u).astype(BF16)
    y = jnp.dot(h, sd_ref[...], preferred_element_type=F32)

    lpos = lp_ref[...]
    w = w_ref[...]
    n_own = ngrp_ref[i]
    _wait_groups(n_own, lambda g, r: group_copy(slot, 0, 0), lambda: bulk_copy(slot))
    pos128 = [jnp.broadcast_to(lpos[:, k:k + 1], (DP_TM, 128)).astype(jnp.int16)
              for k in range(TOP_K)]
    wts128 = [jnp.broadcast_to(w[:, k:k + 1], (DP_TM, 128)).astype(BF16) for k in range(TOP_K)]

    def unsorted(col0, n_cols):
        col_iota = (lax.broadcasted_iota(I32, (DP_TM, n_cols), 1) + col0).astype(jnp.int16)
        unsort = jnp.zeros((DP_TM, n_cols), BF16)
        for k in range(TOP_K):
            unsort = jnp.where(col_iota == jnp.tile(pos128[k], (1, n_cols // 128)),
                               jnp.tile(wts128[k], (1, n_cols // 128)), unsort)
        return jnp.dot(unsort, ybuf_ref[slot, pl.ds(col0, n_cols), :],
                       preferred_element_type=F32)

    acc_ref[...] = y
    for q in range(N_BURSTS):
        def add_chunk(q=q):
            acc_ref[...] += unsorted(q * BURST_ROWS, BURST_ROWS)

        if (q + 1) * BURST_ROWS <= MIN_SORT_ROWS:
            add_chunk()
        else:
            pl.when(q * BURST_GRPS < n_own)(add_chunk)
        prefetch(q)
    out = x_ref[...] + m_ref[:, 5 * D:6 * D] * acc_ref[...]
    if final_norm:
        out = _rms(out) * fg_ref[...]
    o_ref[...] = out


def _combine(ngrp, grp_row, ys, lpos_t, w_t, x, hx, mods, sg_bf, su_bf, sd_bf, fg, n_tok, final_norm):
    tm = DP_TM
    row = lambda i, *_: (i, 0)
    const2 = lambda i, *_: (0, 0)
    return pl.pallas_call(
        functools.partial(_combine_body, final_norm=final_norm),
        grid_spec=pltpu.PrefetchScalarGridSpec(
            num_scalar_prefetch=1,
            grid=(n_tok // tm,),
            in_specs=[
                pl.BlockSpec((None, 1, N_GRPS), lambda i, *_: (i, 0, 0), memory_space=pltpu.SMEM),
                pl.BlockSpec((None, 1, N_GRPS), lambda i, *_: (jnp.minimum(i + 1, n_tok // tm - 1), 0, 0),
                             memory_space=pltpu.SMEM),
                pl.BlockSpec(memory_space=pl.ANY),
                pl.BlockSpec((tm, TOP_K), row),
                pl.BlockSpec((tm, TOP_K), row),
                pl.BlockSpec((tm, D), row),
                pl.BlockSpec((tm, D), row),
                pl.BlockSpec((None, 1, 6 * D), lambda i, *_: (_mod_row(i, tm), 0, 0)),
                pl.BlockSpec((D, EXP_FF), const2),
                pl.BlockSpec((D, EXP_FF), const2),
                pl.BlockSpec((EXP_FF, D), const2),
                pl.BlockSpec((1, D), const2),
            ],
            out_specs=pl.BlockSpec((tm, D), row),
            scratch_shapes=[pltpu.VMEM((2, SORT_ROWS, D), BF16), pltpu.VMEM((tm, D), F32),
                            pltpu.SemaphoreType.DMA((2,))],
        ),
        out_shape=jax.ShapeDtypeStruct((n_tok, D), F32),
        compiler_params=_cparams(1, vmem_mb=56),
        name="moe_combine",
    )(ngrp, grp_row, grp_row, ys, lpos_t, w_t, x, hx, mods, sg_bf, su_bf, sd_bf, fg)


def _moe(x_new, hx, logits_t, mods, rbias, wg, wu, wd, sg_bf, su_bf, sd_bf, fg, n_tok, layer,
         final_norm):
    w, lpos, tile_cnt = _route(logits_t, rbias, n_tok)
    n_tiles = n_tok // DP_TM
    n_blk = (n_tok * TOP_K + (SLOT_ALIGN - 1) * N_EXP * n_tiles) // MOE_BLK + N_EXP
    tcnt = tile_cnt[:, :, 0].astype(I32)
    cnt_al = (tcnt + SLOT_ALIGN - 1) // SLOT_ALIGN * SLOT_ALIGN
    loc_end = jnp.cumsum(cnt_al, axis=1)
    loc = loc_end - cnt_al
    slots_e = jnp.sum(cnt_al, axis=0)
    nblk_e = (slots_e + MOE_BLK - 1) // MOE_BLK
    blk_end = jnp.cumsum(nblk_e)
    pstart = (blk_end - nblk_e) * MOE_BLK
    nused = blk_end[-1:].astype(I32)
    blk_ids = jnp.arange(n_blk, dtype=I32)
    xblk = jnp.minimum(blk_ids, nused[0] - 1)
    bexp = jnp.minimum(
        jnp.sum(blk_end[None, :] <= xblk[:, None], axis=1), N_EXP - 1).astype(I32)
    off = pstart[None, :] + jnp.cumsum(cnt_al, axis=0) - cnt_al
    g_row = jnp.arange(N_GRPS, dtype=I32) * SLOT_ALIGN
    e_of_g = jnp.minimum(
        jnp.sum(loc_end[:, None, :] <= g_row[None, :, None], axis=2), N_EXP - 1)
    pick = e_of_g[..., None] == jnp.arange(N_EXP, dtype=I32)
    grp_row = (jnp.sum(jnp.where(pick, (off - loc)[:, None, :], 0), axis=2)
               + g_row[None, :]).astype(I32).reshape(n_tiles, 1, N_GRPS)
    ngrp = (loc_end[:, -1] // SLOT_ALIGN).astype(I32)
    unused = nused[0] + blk_ids
    fill = jnp.concatenate([
        (N_EXP + n_blk - nused[0])[None],
        jnp.where(nblk_e > 0, blk_end - 1, -1),
        jnp.where(unused < n_blk, unused, -1)]).astype(I32)
    xs = _dispatch(ngrp, fill, grp_row, hx, lpos, n_tok, n_blk * MOE_BLK)
    ys = _experts(bexp, xblk, nused, xs, wg, wu, wd, n_blk, layer)
    return _combine(ngrp, grp_row, ys, lpos.T, w.T, x_new, hx, mods, sg_bf, su_bf, sd_bf, fg,
                    n_tok, final_norm)


def _rope_tables():
    t = np.arange(S)
    row = (t // GRID_W).astype(np.float32)
    col = (t % GRID_W).astype(np.float32)
    half = HD // 2
    inv = jnp.asarray(ROPE_BASE, F32) ** (-jnp.arange(0, half, 2, dtype=F32) / half)
    ang_r = jnp.asarray(row)[:, None] * inv
    ang_c = jnp.asarray(col)[:, None] * inv
    ang = jnp.concatenate([ang_r, ang_r, ang_c, ang_c], axis=-1)
    n_rep = ROT_W // HD
    cos = jnp.tile(jnp.cos(ang), (1, n_rep))
    sin = jnp.tile(jnp.sin(ang), (1, n_rep))
    cos = jnp.concatenate([cos, jnp.ones((IN_TM, ROT_W), F32)], axis=0)
    sin = jnp.concatenate([sin, jnp.zeros((IN_TM, ROT_W), F32)], axis=0)
    return cos, sin


def kernel(x, c, ctx, c_ctx, mod_w, mod_b, norm1_g, w_in, attn_sink, sgu_norm_g, sgu_w, sgu_b, na_rpb,
           group_norm_g, w_out, norm2_g, router_w, router_bias, exp_w_gate, exp_w_up, exp_w_down,
           shared_w_gate, shared_w_up, shared_w_down, final_g):
    tokens = (x.reshape(N_LAT, D), ctx.reshape(N_CTX, D), 0)
    cc = jnp.concatenate([c, c_ctx[None, :], jnp.zeros((16 - B - 1, D), F32)], axis=0)
    mods_all = _modulation(cc, mod_w, mod_b)
    cos_t, sin_t = _rope_tables()
    seg = np.arange(B_W) // HD
    bd = jnp.asarray((seg[:, None] == seg[None, :]).astype(np.float32) / HD, BF16)
    fg = final_g.reshape(1, D)

    out = None
    for l in range(DEPTH):
        last = l == DEPTH - 1
        mods = mods_all[l].reshape(16, 1, 6 * D)
        za, zc, uv = _in_proj(*tokens, mods, norm1_g[l].reshape(1, D), w_in[l].astype(BF16), cos_t, sin_t)
        ya, yc = _mixers(za, zc, attn_sink[l], _na_bias_table(na_rpb[l]), with_ctx=not last)
        n_tok = N_LAT if last else N_ALL
        sb_tab = jnp.repeat(sgu_b[l].T, HD, axis=1)
        x_new, hx, logits_t = _merge(
            *tokens, ya, yc, uv, mods, group_norm_g[l].reshape(1, D), sgu_norm_g[l].reshape(1, B_W),
            sgu_w[l].astype(BF16), sb_tab, bd, w_out[l].astype(BF16), norm2_g[l].reshape(1, D),
            _router_split(router_w[l]), n_tok)
        res = _moe(x_new, hx, logits_t, mods, router_bias[l].reshape(N_EXP, 1),
                   exp_w_gate, exp_w_up, exp_w_down,
                   shared_w_gate[l].astype(BF16), shared_w_up[l].astype(BF16),
                   shared_w_down[l].astype(BF16), fg, n_tok, layer=l, final_norm=last)
        if last:
            out = res.reshape(B, S, D)
        else:
            tokens = (res, res, N_LAT)
    return out
```

```python
import functools
import math

import jax
import jax.numpy as jnp
import numpy as np
from jax import lax
from jax.experimental import pallas as pl
from jax.experimental.pallas import tpu as pltpu

F32 = jnp.float32
BF16 = jnp.bfloat16
I32 = jnp.int32

D = 1024
B = 8
S = 2048
C = 256
DEPTH = 2
GRID_W = 64
HD = 64
A_HEADS = 6
A_KV = 2
A_WIN = 128
A_BLK = 256
ROPE_BASE = 10000.0
SGU_GROUPS = 4
SGU_CHUNK = 128
C_HEADS = 6
C_WIN_R = 8
C_WIN_C = 16
A_W = A_HEADS * HD
B_W = SGU_GROUPS * HD
C_W = C_HEADS * HD
KV_W = A_KV * HD
IN_W = A_W + 2 * KV_W + 2 * B_W + 3 * C_W
N_EXP = 64
TOP_K = 8
N_GRP = 8
TOPK_GRP = 4
EXP_FF = 256
ROUTED_SCALE = 2.5
MOE_BLK = 1024
EPS = 1e-6
NEG = -1e30
SCALE = HD ** -0.5
LOG2E = math.log2(math.e)
Q_SCALE = SCALE * LOG2E

N_LAT = B * S
N_CTX = B * C
N_ALL = N_LAT + N_CTX
ROWS = S // GRID_W

HIGHEST = lax.Precision.HIGHEST
ARB = pltpu.ARBITRARY

NT_DIMS = (((1,), (1,)), ((), ()))


def _cparams(n_axes, vmem_mb=48):
    return pltpu.CompilerParams(
        dimension_semantics=(ARB,) * n_axes, vmem_limit_bytes=vmem_mb * 1024 * 1024)


def _mod_row(i, tm):
    return jnp.where(i < N_LAT // tm, i // (S // tm), B)


def _rms(x):
    return x * lax.rsqrt(jnp.mean(x * x, axis=-1, keepdims=True) + EPS)


def _token_specs(tm, ctx_first_tile):
    n_lat = N_LAT // tm
    lat = pl.BlockSpec((tm, D), lambda i: (jnp.minimum(i, n_lat - 1), 0))
    ctx = pl.BlockSpec((tm, D), lambda i: (ctx_first_tile + jnp.maximum(i - n_lat, 0), 0))
    return lat, ctx


def _token_tile(lat_ref, ctx_ref):
    tm = lat_ref.shape[0]
    n_from_lat = jnp.where(pl.program_id(0) < N_LAT // tm, tm, 0)
    from_lat = lax.broadcasted_iota(I32, lat_ref.shape, 0) < n_from_lat
    return jnp.where(from_lat, lat_ref[...], ctx_ref[...])


def _split_bf16(x):
    hi = x.astype(BF16)
    return hi, (x - hi.astype(F32)).astype(BF16)


def _router_split(rw):
    hi, lo = _split_bf16(jnp.pad(rw, ((0, 0), (0, 128 - N_EXP))))
    return jnp.stack([hi, lo])


MOD_TN = 1024


def _mod_body(cc_ref, w_ref, b_ref, o_ref):
    a = cc_ref[...]
    a = a * jax.nn.sigmoid(a)
    o_ref[...] = jnp.dot(a.astype(BF16), w_ref[...].astype(BF16),
                         preferred_element_type=F32) + b_ref[...]


def _modulation(cc, mod_w, mod_b):
    n_col = 6 * D // MOD_TN
    return pl.pallas_call(
        _mod_body,
        grid=(DEPTH, n_col),
        in_specs=[
            pl.BlockSpec((16, D), lambda l, j: (0, 0)),
            pl.BlockSpec((None, D, MOD_TN), lambda l, j: (l, 0, j)),
            pl.BlockSpec((None, 1, MOD_TN), lambda l, j: (l, 0, j)),
        ],
        out_specs=pl.BlockSpec((None, 16, MOD_TN), lambda l, j: (l, 0, j)),
        out_shape=jax.ShapeDtypeStruct((DEPTH, 16, 6 * D), F32),
        compiler_params=_cparams(2),
        name="modulation",
    )(cc, mod_w, mod_b.reshape(DEPTH, 1, 6 * D))


IN_TM = 512
ROT_W = A_W + KV_W


def _in_body(xl_ref, xc_ref, m_ref, g_ref, w_ref, cos_ref, sin_ref, za_ref, zc_ref, uv_ref):
    x = _token_tile(xl_ref, xc_ref)
    h = _rms(x) * g_ref[...]
    h = h * (1.0 + m_ref[:, D:2 * D]) + m_ref[:, 0:D]
    z = jnp.dot(h.astype(BF16), w_ref[...], preferred_element_type=F32)
    qk = z[:, :ROT_W]
    lane = lax.broadcasted_iota(I32, qk.shape, 1)
    rot = jnp.where((lane & 16) == 0,
                    -pltpu.roll(qk, ROT_W - 16, 1), pltpu.roll(qk, 16, 1))
    qk = qk * cos_ref[...] + rot * sin_ref[...]
    za_ref[...] = jnp.concatenate(
        [qk[:, :A_W] * Q_SCALE, qk[:, A_W:], z[:, ROT_W:ROT_W + KV_W]], axis=1).astype(BF16)
    u0 = ROT_W + KV_W
    uv_ref[...] = jax.nn.gelu(z[:, u0:u0 + 2 * B_W]).astype(BF16)
    c0 = u0 + 2 * B_W
    zc_ref[...] = jnp.concatenate(
        [z[:, c0:c0 + C_W] * Q_SCALE, z[:, c0 + C_W:]], axis=1).astype(BF16)


def _in_proj(x_lat, x_ctx, ctx_first_row, mods, g, w_bf, cos_t, sin_t):
    tm = IN_TM
    n_t = N_ALL // tm

    def tab_idx(i):
        return (jnp.where(i < N_LAT // tm, i % (S // tm), S // tm), 0)

    return pl.pallas_call(
        _in_body,
        grid=(n_t,),
        in_specs=[
            *_token_specs(tm, ctx_first_row // tm),
            pl.BlockSpec((None, 1, 6 * D), lambda i: (_mod_row(i, tm), 0, 0)),
            pl.BlockSpec((1, D), lambda i: (0, 0)),
            pl.BlockSpec((D, IN_W), lambda i: (0, 0)),
            pl.BlockSpec((tm, ROT_W), tab_idx),
            pl.BlockSpec((tm, ROT_W), tab_idx),
        ],
        out_specs=[
            pl.BlockSpec((tm, A_W + 2 * KV_W), lambda i: (i, 0)),
            pl.BlockSpec((tm, 3 * C_W), lambda i: (i, 0)),
            pl.BlockSpec((tm, 2 * B_W), lambda i: (i, 0)),
        ],
        out_shape=[
            jax.ShapeDtypeStruct((N_ALL, A_W + 2 * KV_W), BF16),
            jax.ShapeDtypeStruct((N_ALL, 3 * C_W), BF16),
            jax.ShapeDtypeStruct((N_ALL, 2 * B_W), BF16),
        ],
        compiler_params=_cparams(1),
        name="in_proj",
    )(x_lat, x_ctx, mods, g, w_bf, cos_t, sin_t)


A_BAND = A_BLK + 2 * A_WIN
N_QB_LAT = N_LAT // A_BLK
QB_PER_SEQ = S // A_BLK
QB_PER_CTX = C // A_BLK


def _softmax_pv(s_list, v_list, extra_logit=None):
    m = s_list[0].max(axis=-1, keepdims=True)
    for s in s_list[1:]:
        m = jnp.maximum(m, s.max(axis=-1, keepdims=True))
    if extra_logit is not None:
        m = jnp.maximum(m, extra_logit)
    den = None
    out = None
    for s, v in zip(s_list, v_list):
        p = jnp.exp2(s - m)
        d = p.sum(axis=-1, keepdims=True)
        o = jnp.dot(p.astype(BF16), v, preferred_element_type=F32)
        den = d if den is None else den + d
        out = o if out is None else out + o
    if extra_logit is not None:
        den = den + jnp.exp2(extra_logit - m)
    return out / den


def _attn_a_body(sink_ref, q_ref, kb_ref, vb_ref, kc_ref, vc_ref, o_ref):
    j = pl.program_id(1)
    is_lat = j < QB_PER_SEQ
    n = jnp.minimum(j, QB_PER_SEQ - 1)
    start = pl.multiple_of(jnp.clip(n * A_BLK - A_WIN, 0, S - A_BAND), A_WIN)
    qpos = n * A_BLK + lax.broadcasted_iota(I32, (A_BLK, A_BAND), 0)
    kpos = start + lax.broadcasted_iota(I32, (A_BLK, A_BAND), 1)
    band_penalty = jnp.where(jnp.abs(kpos - qpos) <= jnp.where(is_lat, A_WIN, -1), 0.0, NEG)
    kb = kb_ref[pl.ds(start, A_BAND), :]
    vb = vb_ref[pl.ds(start, A_BAND), :]
    kc = kc_ref[...]
    vc = vc_ref[...]
    outs = []
    for h in range(A_HEADS):
        kv = h // (A_HEADS // A_KV)
        sl = slice(kv * HD, (kv + 1) * HD)
        q = q_ref[:, h * HD:(h + 1) * HD]
        s_b = lax.dot_general(q, kb[:, sl], NT_DIMS, preferred_element_type=F32) + band_penalty
        s_c = lax.dot_general(q, kc[:, sl], NT_DIMS, preferred_element_type=F32)
        outs.append(_softmax_pv([s_c, s_b], [vc[:, sl], vb[:, sl]], sink_ref[h] * LOG2E))
    o_ref[...] = jnp.concatenate(outs, axis=1).astype(BF16)


NA_R = 4
NA_TQ = NA_R * GRID_W
NA_KROWS = 12
NA_WIN = NA_KROWS * GRID_W
NA_STEPS = ROWS // NA_R
NA_PAIRS = NA_KROWS // 2
N_DR = 2 * C_WIN_R - 1


def _na_body(q_ref, k_ref, v_ref, kc_ref, vc_ref, tab_ref, o_ref):
    j = pl.program_id(1)
    is_lat = j < NA_STEPS
    r0 = jnp.minimum(j, NA_STEPS - 1) * NA_R
    u0 = jnp.clip(r0 - C_WIN_R // 2, 0, ROWS - NA_KROWS)
    k0 = pl.multiple_of(u0 * GRID_W, GRID_W)
    kw = k_ref[pl.ds(k0, NA_WIN), :]
    vw = v_ref[pl.ds(k0, NA_WIN), :]
    kc = kc_ref[...]
    vc = vc_ref[...]
    q = q_ref[...]
    left = lax.broadcasted_iota(I32, (1, 2 * GRID_W), 1) < GRID_W

    tab_idx = []
    penalty = []
    for rr in range(NA_R):
        r = r0 + rr
        start = jnp.clip(r - C_WIN_R // 2, 0, ROWS - C_WIN_R)
        idx_row = []
        pen_row = []
        for p in range(NA_PAIRS):
            kr = u0 + 2 * p
            idx_row.append(jnp.clip(kr - r + (C_WIN_R - 1), -1, N_DR - 1) + 1)
            pens = []
            for half in range(2):
                ok = jnp.logical_and(is_lat, jnp.logical_and(kr + half >= start,
                                                             kr + half < start + C_WIN_R))
                pens.append(jnp.where(ok, 0.0, NEG))
            pen_row.append(jnp.where(left, pens[0], pens[1]))
        tab_idx.append(idx_row)
        penalty.append(jnp.concatenate(pen_row, axis=1))

    low_q = lax.broadcasted_iota(I32, (NA_TQ, 2 * HD), 1) < HD
    outs = []
    for hp in range(C_HEADS // 2):
        sl = slice(2 * hp * HD, 2 * (hp + 1) * HD)
        q2, kw2, vw2, kc2, vc2 = q[:, sl], kw[:, sl], vw[:, sl], kc[:, sl], vc[:, sl]
        halves = []
        for half in range(2):
            h = 2 * hp + half
            qh = jnp.where(low_q if half == 0 else jnp.logical_not(low_q), q2, jnp.zeros_like(q2))
            s_w = lax.dot_general(qh, kw2, NT_DIMS, preferred_element_type=F32)
            bias = jnp.concatenate([
                jnp.concatenate([tab_ref[h, pl.ds(tab_idx[rr][p], 1)][0]
                                 for p in range(NA_PAIRS)], axis=1) + penalty[rr]
                for rr in range(NA_R)], axis=0)
            s_c = lax.dot_general(qh, kc2, NT_DIMS, preferred_element_type=F32)
            halves.append(_softmax_pv([s_c, s_w + bias], [vc2, vw2]))
        outs.append(jnp.where(low_q, halves[0], halves[1]))
    o_ref[...] = jnp.concatenate(outs, axis=1).astype(BF16)


def _mixers_body(sink_ref, qa_ref, kb_ref, vb_ref, kca_ref, vca_ref, qn_ref, kn_ref, vn_ref,
                 kcn_ref, vcn_ref, tab_ref, oa_ref, on_ref):
    _attn_a_body(sink_ref, qa_ref, kb_ref, vb_ref, kca_ref, vca_ref, oa_ref)
    _na_body(qn_ref, kn_ref, vn_ref, kcn_ref, vcn_ref, tab_ref, on_ref)


def _mixers(za, zc, sink, tab, with_ctx):
    assert A_BLK == NA_TQ and QB_PER_SEQ == NA_STEPS and QB_PER_CTX == 1
    n_j = NA_STEPS + (1 if with_ctx else 0)

    def qidx(b, j):
        return jnp.where(j < NA_STEPS, b * NA_STEPS + j, N_LAT // NA_TQ + b)

    q_row = lambda b, j: (qidx(b, j), 0)
    ctx_blk = lambda col: (lambda b, j: (N_LAT // C + b, col))
    seq_blk = lambda col: (lambda b, j: (b, col))
    ka_col = A_W // KV_W
    n_out = N_LAT + (N_CTX if with_ctx else 0)
    return pl.pallas_call(
        _mixers_body,
        grid=(B, n_j),
        in_specs=[
            pl.BlockSpec(memory_space=pltpu.SMEM),
            pl.BlockSpec((A_BLK, A_W), q_row),
            pl.BlockSpec((S, KV_W), seq_blk(ka_col)),
            pl.BlockSpec((S, KV_W), seq_blk(ka_col + 1)),
            pl.BlockSpec((C, KV_W), ctx_blk(ka_col)),
            pl.BlockSpec((C, KV_W), ctx_blk(ka_col + 1)),
            pl.BlockSpec((NA_TQ, C_W), q_row),
            pl.BlockSpec((S, C_W), seq_blk(1)),
            pl.BlockSpec((S, C_W), seq_blk(2)),
            pl.BlockSpec((C, C_W), ctx_blk(1)),
            pl.BlockSpec((C, C_W), ctx_blk(2)),
            pl.BlockSpec((C_HEADS, N_DR + 1, GRID_W, 2 * GRID_W), lambda b, j: (0, 0, 0, 0)),
        ],
        out_specs=[pl.BlockSpec((A_BLK, A_W), q_row), pl.BlockSpec((NA_TQ, C_W), q_row)],
        out_shape=[jax.ShapeDtypeStruct((n_out, A_W), BF16),
                   jax.ShapeDtypeStruct((n_out, C_W), BF16)],
        compiler_params=_cparams(2),
        name="attn_mixers",
    )(sink, za, za, za, za, za, zc, zc, zc, zc, zc, tab)


def _na_bias_table(rpb):
    cq = np.arange(GRID_W)
    col_start = np.clip(cq - C_WIN_C // 2, 0, GRID_W - C_WIN_C)
    col_ok = (cq[None, :] >= col_start[:, None]) & (cq[None, :] < col_start[:, None] + C_WIN_C)
    dc = np.clip(cq[None, :] - cq[:, None], -(C_WIN_C - 1), C_WIN_C - 1) + (C_WIN_C - 1)
    n_dc = 2 * C_WIN_C - 1
    pick = (dc.reshape(-1)[None, :] == np.arange(n_dc)[:, None]).astype(np.float32)
    t = jnp.dot(rpb.astype(F32).reshape(-1, n_dc), jnp.asarray(pick), precision=HIGHEST)
    t = t.reshape(C_HEADS, N_DR, GRID_W, GRID_W) * LOG2E
    t = jnp.where(col_ok[None, None], t, NEG)
    zero = jnp.zeros((C_HEADS, 1, GRID_W, GRID_W), F32)
    ext = jnp.concatenate([zero, t, zero], axis=1)
    return jnp.concatenate([ext[:, :-1], ext[:, 1:]], axis=-1)


MG_TM = 256


def _merge_body(xl_ref, xc_ref, ya_ref, yc_ref, uv_ref, m_ref, gg_ref, sg_ref, sw_ref, sb_ref, bd_ref,
                wo_ref, g2_ref, rw_ref, xo_ref, hx_ref, lg_ref):
    u = uv_ref[:, :B_W].astype(F32)
    v = uv_ref[:, B_W:].astype(F32)
    vv_hi, vv_lo = _split_bf16(v * v)
    ms = (jnp.dot(vv_hi, bd_ref[...], preferred_element_type=F32)
          + jnp.dot(vv_lo, bd_ref[...], preferred_element_type=F32))
    vn = (v * lax.rsqrt(ms + EPS) * sg_ref[...]).astype(BF16)
    lane_grp = lax.broadcasted_iota(I32, (SGU_CHUNK, B_W), 1) // HD
    gates = []
    for c in range(MG_TM // SGU_CHUNK):
        vc = vn[c * SGU_CHUNK:(c + 1) * SGU_CHUNK, :]
        gate = sb_ref[...]
        acc = jnp.zeros((SGU_CHUNK, B_W), F32)
        for g in range(SGU_GROUPS):
            r = jnp.dot(sw_ref[g], vc, preferred_element_type=F32)
            acc = jnp.where(lane_grp == g, r, acc)
        gates.append(acc + gate)
    yb = u * jnp.concatenate(gates, axis=0)
    gg = gg_ref[...]
    ycat = jnp.concatenate([
        _rms(ya_ref[...].astype(F32)) * gg[:, :A_W],
        _rms(yb) * gg[:, A_W:A_W + B_W],
        _rms(yc_ref[...].astype(F32)) * gg[:, A_W + B_W:],
    ], axis=1)
    proj = jnp.dot(ycat.astype(BF16), wo_ref[...], preferred_element_type=F32)
    xn = _token_tile(xl_ref, xc_ref) + m_ref[:, 2 * D:3 * D] * proj
    xo_ref[...] = xn
    hx = _rms(xn) * g2_ref[...]
    hx = hx * (1.0 + m_ref[:, 4 * D:5 * D]) + m_ref[:, 3 * D:4 * D]
    hx_ref[...] = hx.astype(BF16)
    hx_hi, hx_lo = _split_bf16(hx)
    logits = (jnp.dot(hx_hi, rw_ref[0], preferred_element_type=F32)
              + jnp.dot(hx_hi, rw_ref[1], preferred_element_type=F32)
              + jnp.dot(hx_lo, rw_ref[0], preferred_element_type=F32))
    lg_ref[...] = logits.T[:N_EXP, :]


def _merge(x_lat, x_ctx, ctx_first_row, ya, yc, uv, mods, gg, sg, sw_bf, sb_tab, bd, wo_bf, g2, rw_t,
           n_rows):
    tm = MG_TM
    n_t = n_rows // tm
    const2 = lambda i: (0, 0)
    row = lambda i: (i, 0)
    return pl.pallas_call(
        _merge_body,
        grid=(n_t,),
        in_specs=[
            *_token_specs(tm, ctx_first_row // tm),
            pl.BlockSpec((tm, A_W), row),
            pl.BlockSpec((tm, C_W), row),
            pl.BlockSpec((tm, 2 * B_W), row),
            pl.BlockSpec((None, 1, 6 * D), lambda i: (_mod_row(i, tm), 0, 0)),
            pl.BlockSpec((1, D), const2),
            pl.BlockSpec((1, B_W), const2),
            pl.BlockSpec((SGU_GROUPS, SGU_CHUNK, SGU_CHUNK), lambda i: (0, 0, 0)),
            pl.BlockSpec((SGU_CHUNK, B_W), const2),
            pl.BlockSpec((B_W, B_W), const2),
            pl.BlockSpec((D, D), const2),
            pl.BlockSpec((1, D), const2),
            pl.BlockSpec((2, D, 128), lambda i: (0, 0, 0)),
        ],
        out_specs=[
            pl.BlockSpec((tm, D), row),
            pl.BlockSpec((tm, D), row),
            pl.BlockSpec((N_EXP, tm), lambda i: (0, i)),
        ],
        out_shape=[
            jax.ShapeDtypeStruct((n_rows, D), F32),
            jax.ShapeDtypeStruct((n_rows, D), BF16),
            jax.ShapeDtypeStruct((N_EXP, n_rows), F32),
        ],
        compiler_params=_cparams(1),
        name="merge",
    )(x_lat, x_ctx, ya, yc, uv, mods, gg, sg, sw_bf, sb_tab, bd, wo_bf, g2, rw_t)


RT_TM = 256
GRP_SZ = N_EXP // N_GRP
SLOT_ALIGN = 16


def _first_argmax(v, iota):
    m = v.max(axis=0, keepdims=True)
    idx = jnp.where(v == m, iota, float(v.shape[0])).min(axis=0, keepdims=True)
    return m, idx


def _stack_rows(rows, iota):
    out = jnp.zeros(iota.shape, F32)
    for r, v in enumerate(rows):
        out = jnp.where(iota == float(r), v, out)
    return out


def _route_body(lg_ref, rb_ref, w_ref, lp_ref, tc_ref):
    tm = RT_TM
    scores = jax.nn.sigmoid(lg_ref[...])
    sel = scores + rb_ref[...]
    iota_g = lax.broadcasted_iota(I32, (GRP_SZ, tm), 0).astype(F32)
    gs = []
    for g in range(N_GRP):
        v = sel[g * GRP_SZ:(g + 1) * GRP_SZ, :]
        m1, i1 = _first_argmax(v, iota_g)
        m2 = jnp.where(iota_g == i1, -jnp.inf, v).max(axis=0, keepdims=True)
        gs.append(m1 + m2)
    iota_n = lax.broadcasted_iota(I32, (N_GRP, tm), 0).astype(F32)
    gscore = _stack_rows(gs, iota_n)
    gsel = jnp.zeros((N_GRP, tm), F32)
    for _ in range(TOPK_GRP):
        _, gi = _first_argmax(gscore, iota_n)
        hit = iota_n == gi
        gsel = jnp.where(hit, 1.0, gsel)
        gscore = jnp.where(hit, -jnp.inf, gscore)
    emask = jnp.concatenate(
        [jnp.broadcast_to(gsel[g:g + 1, :], (GRP_SZ, tm)) for g in range(N_GRP)], axis=0)
    cand = jnp.where(emask > 0.5, sel, NEG)
    iota_e = lax.broadcasted_iota(I32, (N_EXP, tm), 0).astype(F32)
    hits = []
    ws = []
    member = jnp.zeros((N_EXP, tm), F32)
    for _ in range(TOP_K):
        _, ei = _first_argmax(cand, iota_e)
        hit = iota_e == ei
        hits.append(hit)
        ws.append(jnp.where(hit, scores, 0.0).sum(axis=0, keepdims=True))
        member = jnp.where(hit, 1.0, member)
        cand = jnp.where(hit, -jnp.inf, cand)
    wsum = ws[0]
    for w in ws[1:]:
        wsum = wsum + w
    iota_k = lax.broadcasted_iota(I32, (TOP_K, tm), 0).astype(F32)
    w_ref[...] = _stack_rows(ws, iota_k) / wsum * ROUTED_SCALE
    r_i = lax.broadcasted_iota(I32, (tm, tm), 0)
    c_i = lax.broadcasted_iota(I32, (tm, tm), 1)
    tri = jnp.where(r_i < c_i, 1.0, 0.0).astype(BF16)
    local = jnp.dot(member.astype(BF16), tri, preferred_element_type=F32)
    tile_cnt = member.sum(axis=1, keepdims=True)
    tc_ref[...] = tile_cnt
    aligned = jnp.ceil(tile_cnt / SLOT_ALIGN) * SLOT_ALIGN
    e_r = lax.broadcasted_iota(I32, (N_EXP, N_EXP), 0)
    e_c = lax.broadcasted_iota(I32, (N_EXP, N_EXP), 1)
    below = jnp.where(e_c < e_r, 1.0, 0.0)
    start = jnp.dot(below, jnp.broadcast_to(aligned, (N_EXP, 128)), precision=HIGHEST,
                    preferred_element_type=F32)[:, 0:1]
    pos = local + start
    lp_ref[...] = _stack_rows(
        [jnp.where(hit, pos, 0.0).sum(axis=0, keepdims=True) for hit in hits],
        iota_k).astype(I32)


def _route(logits_t, rbias, n_tok):
    tm = RT_TM
    tok = lambda i: (0, i)
    return pl.pallas_call(
        _route_body,
        grid=(n_tok // tm,),
        in_specs=[
            pl.BlockSpec((N_EXP, tm), tok),
            pl.BlockSpec((N_EXP, 1), lambda i: (0, 0)),
        ],
        out_specs=[
            pl.BlockSpec((TOP_K, tm), tok),
            pl.BlockSpec((TOP_K, tm), tok),
            pl.BlockSpec((None, N_EXP, 1), lambda i: (i, 0, 0)),
        ],
        out_shape=[
            jax.ShapeDtypeStruct((TOP_K, n_tok), F32),
            jax.ShapeDtypeStruct((TOP_K, n_tok), I32),
            jax.ShapeDtypeStruct((n_tok // tm, N_EXP, 1), F32),
        ],
        compiler_params=_cparams(1),
        name="route",
    )(logits_t, rbias)


DP_TM = RT_TM
SORT_ROWS = 3072
N_GRPS = SORT_ROWS // SLOT_ALIGN
MIN_SORT_ROWS = TOP_K * DP_TM
FILL_SEM = 2
ISSUE_UNROLL = 4
WAIT_GROUPS = 32
N_BURSTS = 6
BURST_ROWS = SORT_ROWS // N_BURSTS
BURST_GRPS = BURST_ROWS // SLOT_ALIGN


def _start_groups(g_lo, g_hi, row_of, make_copy):
    n_full = (g_hi - g_lo) // ISSUE_UNROLL

    def body(q, carry):
        for s in range(ISSUE_UNROLL):
            g = g_lo + q * ISSUE_UNROLL + s
            make_copy(g, row_of(g)).start(priority=s % 2)
        return carry

    def tail(g, carry):
        make_copy(g, row_of(g)).start()
        return carry

    lax.fori_loop(0, n_full, body, 0)
    lax.fori_loop(g_lo + n_full * ISSUE_UNROLL, g_hi, tail, 0)


def _start_burst(q, ngrp, row_of, make_copy):
    _start_groups(jnp.minimum(q * BURST_GRPS, ngrp), jnp.minimum((q + 1) * BURST_GRPS, ngrp),
                  row_of, make_copy)


def _wait_groups(ngrp, make_copy, make_bulk):
    def bulk(q, carry):
        make_bulk().wait()
        return carry

    def single(g, carry):
        make_copy(0, 0).wait()
        return carry

    n_bulk = ngrp // WAIT_GROUPS
    lax.fori_loop(0, n_bulk, bulk, 0)
    lax.fori_loop(n_bulk * WAIT_GROUPS, ngrp, single, 0)


def _dispatch_body(ngrp_ref, fill_ref, grow_ref, hx_ref, lp_ref, xs_ref, zbuf_ref, zero_ref, sems):
    i = pl.program_id(0)
    n_tiles = pl.num_programs(0)
    slot = i % 2

    def zero_copy(b, sem_idx):
        dst = xs_ref.at[pl.ds(pl.multiple_of(b * MOE_BLK, MOE_BLK), MOE_BLK)]
        return pltpu.make_async_copy(zero_ref, dst, sems.at[sem_idx])

    def zero_blocks(q_lo, q_hi, sem_idx, start):
        def body(q, carry):
            @pl.when(fill_ref[1 + q] >= 0)
            def _():
                if start:
                    zero_copy(fill_ref[1 + q], sem_idx).start()
                else:
                    zero_copy(0, sem_idx).wait()
            return carry

        lax.fori_loop(q_lo, q_hi, body, 0)

    @pl.when(i == 0)
    def _():
        zero_ref[...] = jnp.zeros_like(zero_ref)
        zero_blocks(0, N_EXP, 0, start=True)
        zero_blocks(N_EXP, fill_ref[0], FILL_SEM, start=True)
        zero_blocks(0, N_EXP, 0, start=False)

    @pl.when(i == n_tiles - 1)
    def _():
        zero_blocks(N_EXP, fill_ref[0], FILL_SEM, start=False)

    def group_copy(s, g, row):
        src = zbuf_ref.at[s, pl.ds(pl.multiple_of(g * SLOT_ALIGN, SLOT_ALIGN), SLOT_ALIGN)]
        dst = xs_ref.at[pl.ds(pl.multiple_of(row, SLOT_ALIGN), SLOT_ALIGN)]
        return pltpu.make_async_copy(src, dst, sems.at[s])

    def bulk_copy(s):
        n_rows = WAIT_GROUPS * SLOT_ALIGN
        return pltpu.make_async_copy(
            zbuf_ref.at[s, pl.ds(0, n_rows)], xs_ref.at[pl.ds(0, n_rows)], sems.at[s])

    x = hx_ref[...]
    lpos = lp_ref[...]
    n_cur = ngrp_ref[i]
    pos16 = [jnp.broadcast_to(lpos[k:k + 1, :], (16, DP_TM)).astype(jnp.int16)
             for k in range(TOP_K)]

    def sort_rows(row0, n_rows):
        row_iota = (lax.broadcasted_iota(I32, (n_rows, DP_TM), 0) + row0).astype(jnp.int16)
        onehot = jnp.zeros((n_rows, DP_TM), BF16)
        for k in range(TOP_K):
            onehot = jnp.where(row_iota == jnp.tile(pos16[k], (n_rows // 16, 1)),
                               jnp.ones((), BF16), onehot)
        z = jnp.dot(onehot, x, preferred_element_type=F32)
        zbuf_ref[slot, pl.ds(row0, n_rows), :] = z.astype(BF16)
        _start_groups(jnp.minimum(row0 // SLOT_ALIGN, n_cur),
                      jnp.minimum((row0 + n_rows) // SLOT_ALIGN, n_cur),
                      lambda g: grow_ref[0, g], lambda g, r: group_copy(slot, g, r))

    sort_rows(0, MIN_SORT_ROWS)
    for row0 in range(MIN_SORT_ROWS, SORT_ROWS, BURST_ROWS):
        pl.when(row0 // SLOT_ALIGN < n_cur)(functools.partial(sort_rows, row0, BURST_ROWS))

    @pl.when(i > 0)
    def _():
        _wait_groups(ngrp_ref[jnp.maximum(i - 1, 0)], lambda g, r: group_copy(1 - slot, 0, 0),
                     lambda: bulk_copy(1 - slot))

    @pl.when(i == n_tiles - 1)
    def _():
        _wait_groups(n_cur, lambda g, r: group_copy(slot, 0, 0), lambda: bulk_copy(slot))


def _dispatch(ngrp, fill, grp_row, hx, lpos, n_tok, n_slots):
    return pl.pallas_call(
        _dispatch_body,
        grid_spec=pltpu.PrefetchScalarGridSpec(
            num_scalar_prefetch=2,
            grid=(n_tok // DP_TM,),
            in_specs=[
                pl.BlockSpec((None, 1, N_GRPS), lambda i, *_: (i, 0, 0), memory_space=pltpu.SMEM),
                pl.BlockSpec((DP_TM, D), lambda i, *_: (i, 0)),
                pl.BlockSpec((TOP_K, DP_TM), lambda i, *_: (0, i)),
            ],
            out_specs=pl.BlockSpec(memory_space=pl.ANY),
            scratch_shapes=[
                pltpu.VMEM((2, SORT_ROWS, D), BF16),
                pltpu.VMEM((MOE_BLK, D), BF16),
                pltpu.SemaphoreType.DMA((3,)),
            ],
        ),
        out_shape=jax.ShapeDtypeStruct((n_slots, D), BF16),
        compiler_params=_cparams(1, vmem_mb=56),
        name="moe_dispatch",
    )(ngrp, fill, grp_row, hx, lpos)


def _experts_body(bexp_ref, xblk_ref, nused_ref, xs_ref, wg_ref, wu_ref, wd_ref, ys_ref):
    j = pl.program_id(0)

    @pl.when(j < nused_ref[0])
    def _():
        x = xs_ref[...]
        g = jnp.dot(x, wg_ref[...].astype(BF16), preferred_element_type=F32)
        u = jnp.dot(x, wu_ref[...].astype(BF16), preferred_element_type=F32)
        h = (g * jax.nn.sigmoid(g) * u).astype(BF16)
        y = jnp.dot(h, wd_ref[...].astype(BF16), preferred_element_type=F32)
        ys_ref[...] = y.astype(BF16)


def _experts(bexp, xblk, nused, xs, wg, wu, wd, n_blk, layer):
    w_idx = lambda j, be, xb, nu: (layer, be[j], 0, 0)
    return pl.pallas_call(
        _experts_body,
        grid_spec=pltpu.PrefetchScalarGridSpec(
            num_scalar_prefetch=3,
            grid=(n_blk,),
            in_specs=[
                pl.BlockSpec((MOE_BLK, D), lambda j, be, xb, nu: (xb[j], 0)),
                pl.BlockSpec((None, None, D, EXP_FF), w_idx),
                pl.BlockSpec((None, None, D, EXP_FF), w_idx),
                pl.BlockSpec((None, None, EXP_FF, D), w_idx),
            ],
            out_specs=pl.BlockSpec((MOE_BLK, D), lambda j, be, xb, nu: (xb[j], 0)),
        ),
        out_shape=jax.ShapeDtypeStruct((n_blk * MOE_BLK, D), BF16),
        input_output_aliases={3: 0},
        compiler_params=_cparams(1),
        name="moe_experts",
    )(bexp, xblk, nused, xs, wg, wu, wd)


def _combine_body(ngrp_ref, grow_ref, grow_next_ref, ys_ref, lp_ref, w_ref, x_ref, hx_ref, m_ref,
                  sg_ref, su_ref, sd_ref, fg_ref, o_ref, ybuf_ref, acc_ref, sems, *, final_norm):
    i = pl.program_id(0)
    n_tiles = pl.num_programs(0)
    slot = i % 2

    def group_copy(s, g, row):
        src = ys_ref.at[pl.ds(pl.multiple_of(row, SLOT_ALIGN), SLOT_ALIGN)]
        dst = ybuf_ref.at[s, pl.ds(pl.multiple_of(g * SLOT_ALIGN, SLOT_ALIGN), SLOT_ALIGN)]
        return pltpu.make_async_copy(src, dst, sems.at[s])

    def bulk_copy(s):
        n_rows = WAIT_GROUPS * SLOT_ALIGN
        return pltpu.make_async_copy(
            ys_ref.at[pl.ds(0, n_rows)], ybuf_ref.at[s, pl.ds(0, n_rows)], sems.at[s])

    @pl.when(i == 0)
    def _():
        ybuf_ref[...] = jnp.zeros_like(ybuf_ref)
        _start_groups(0, ngrp_ref[0], lambda g: grow_ref[0, g], lambda g, r: group_copy(0, g, r))

    n_next = jnp.where(i + 1 < n_tiles, ngrp_ref[jnp.minimum(i + 1, n_tiles - 1)], 0)

    def prefetch(q):
        _start_burst(q, n_next, lambda g: grow_next_ref[0, g],
                     lambda g, r: group_copy(1 - slot, g, r))

    hx = hx_ref[...]
    g = jnp.dot(hx, sg_ref[...], preferred_element_type=F32)
    u = jnp.dot(hx, su_ref[...], preferred_element_type=F32)
    h = (g * jax.nn.sigmoid(g) * u).astype(BF16)
    y = jnp.dot(h, sd_ref[...], preferred_element_type=F32)

    lpos = lp_ref[...]
    w = w_ref[...]
    n_own = ngrp_ref[i]
    _wait_groups(n_own, lambda g, r: group_copy(slot, 0, 0), lambda: bulk_copy(slot))
    pos128 = [jnp.broadcast_to(lpos[:, k:k + 1], (DP_TM, 128)).astype(jnp.int16)
              for k in range(TOP_K)]
    wts128 = [jnp.broadcast_to(w[:, k:k + 1], (DP_TM, 128)).astype(BF16) for k in range(TOP_K)]

    def unsorted(col0, n_cols):
        col_iota = (lax.broadcasted_iota(I32, (DP_TM, n_cols), 1) + col0).astype(jnp.int16)
        unsort = jnp.zeros((DP_TM, n_cols), BF16)
        for k in range(TOP_K):
            unsort = jnp.where(col_iota == jnp.tile(pos128[k], (1, n_cols // 128)),
                               jnp.tile(wts128[k], (1, n_cols // 128)), unsort)
        return jnp.dot(unsort, ybuf_ref[slot, pl.ds(col0, n_cols), :],
                       preferred_element_type=F32)

    acc_ref[...] = y
    for q in range(N_BURSTS):
        def add_chunk(q=q):
            acc_ref[...] += unsorted(q * BURST_ROWS, BURST_ROWS)

        if (q + 1) * BURST_ROWS <= MIN_SORT_ROWS:
            add_chunk()
        else:
            pl.when(q * BURST_GRPS < n_own)(add_chunk)
        prefetch(q)
    out = x_ref[...] + m_ref[:, 5 * D:6 * D] * acc_ref[...]
    if final_norm:
        out = _rms(out) * fg_ref[...]
    o_ref[...] = out


def _combine(ngrp, grp_row, ys, lpos_t, w_t, x, hx, mods, sg_bf, su_bf, sd_bf, fg, n_tok, final_norm):
    tm = DP_TM
    row = lambda i, *_: (i, 0)
    const2 = lambda i, *_: (0, 0)
    return pl.pallas_call(
        functools.partial(_combine_body, final_norm=final_norm),
        grid_spec=pltpu.PrefetchScalarGridSpec(
            num_scalar_prefetch=1,
            grid=(n_tok // tm,),
            in_specs=[
                pl.BlockSpec((None, 1, N_GRPS), lambda i, *_: (i, 0, 0), memory_space=pltpu.SMEM),
                pl.BlockSpec((None, 1, N_GRPS), lambda i, *_: (jnp.minimum(i + 1, n_tok // tm - 1), 0, 0),
                             memory_space=pltpu.SMEM),
                pl.BlockSpec(memory_space=pl.ANY),
                pl.BlockSpec((tm, TOP_K), row),
                pl.BlockSpec((tm, TOP_K), row),
                pl.BlockSpec((tm, D), row),
                pl.BlockSpec((tm, D), row),
                pl.BlockSpec((None, 1, 6 * D), lambda i, *_: (_mod_row(i, tm), 0, 0)),
                pl.BlockSpec((D, EXP_FF), const2),
                pl.BlockSpec((D, EXP_FF), const2),
                pl.BlockSpec((EXP_FF, D), const2),
                pl.BlockSpec((1, D), const2),
            ],
            out_specs=pl.BlockSpec((tm, D), row),
            scratch_shapes=[pltpu.VMEM((2, SORT_ROWS, D), BF16), pltpu.VMEM((tm, D), F32),
                            pltpu.SemaphoreType.DMA((2,))],
        ),
        out_shape=jax.ShapeDtypeStruct((n_tok, D), F32),
        compiler_params=_cparams(1, vmem_mb=56),
        name="moe_combine",
    )(ngrp, grp_row, grp_row, ys, lpos_t, w_t, x, hx, mods, sg_bf, su_bf, sd_bf, fg)


def _moe(x_new, hx, logits_t, mods, rbias, wg, wu, wd, sg_bf, su_bf, sd_bf, fg, n_tok, layer,
         final_norm):
    w, lpos, tile_cnt = _route(logits_t, rbias, n_tok)
    n_tiles = n_tok // DP_TM
    n_blk = (n_tok * TOP_K + (SLOT_ALIGN - 1) * N_EXP * n_tiles) // MOE_BLK + N_EXP
    tcnt = tile_cnt[:, :, 0].astype(I32)
    cnt_al = (tcnt + SLOT_ALIGN - 1) // SLOT_ALIGN * SLOT_ALIGN
    loc_end = jnp.cumsum(cnt_al, axis=1)
    loc = loc_end - cnt_al
    slots_e = jnp.sum(cnt_al, axis=0)
    nblk_e = (slots_e + MOE_BLK - 1) // MOE_BLK
    blk_end = jnp.cumsum(nblk_e)
    pstart = (blk_end - nblk_e) * MOE_BLK
    nused = blk_end[-1:].astype(I32)
    blk_ids = jnp.arange(n_blk, dtype=I32)
    xblk = jnp.minimum(blk_ids, nused[0] - 1)
    bexp = jnp.minimum(
        jnp.sum(blk_end[None, :] <= xblk[:, None], axis=1), N_EXP - 1).astype(I32)
    off = pstart[None, :] + jnp.cumsum(cnt_al, axis=0) - cnt_al
    g_row = jnp.arange(N_GRPS, dtype=I32) * SLOT_ALIGN
    e_of_g = jnp.minimum(
        jnp.sum(loc_end[:, None, :] <= g_row[None, :, None], axis=2), N_EXP - 1)
    pick = e_of_g[..., None] == jnp.arange(N_EXP, dtype=I32)
    grp_row = (jnp.sum(jnp.where(pick, (off - loc)[:, None, :], 0), axis=2)
               + g_row[None, :]).astype(I32).reshape(n_tiles, 1, N_GRPS)
    ngrp = (loc_end[:, -1] // SLOT_ALIGN).astype(I32)
    unused = nused[0] + blk_ids
    fill = jnp.concatenate([
        (N_EXP + n_blk - nused[0])[None],
        jnp.where(nblk_e > 0, blk_end - 1, -1),
        jnp.where(unused < n_blk, unused, -1)]).astype(I32)
    xs = _dispatch(ngrp, fill, grp_row, hx, lpos, n_tok, n_blk * MOE_BLK)
    ys = _experts(bexp, xblk, nused, xs, wg, wu, wd, n_blk, layer)
    return _combine(ngrp, grp_row, ys, lpos.T, w.T, x_new, hx, mods, sg_bf, su_bf, sd_bf, fg,
                    n_tok, final_norm)


def _rope_tables():
    t = np.arange(S)
    row = (t // GRID_W).astype(np.float32)
    col = (t % GRID_W).astype(np.float32)
    half = HD // 2
    inv = jnp.asarray(ROPE_BASE, F32) ** (-jnp.arange(0, half, 2, dtype=F32) / half)
    ang_r = jnp.asarray(row)[:, None] * inv
    ang_c = jnp.asarray(col)[:, None] * inv
    ang = jnp.concatenate([ang_r, ang_r, ang_c, ang_c], axis=-1)
    n_rep = ROT_W // HD
    cos = jnp.tile(jnp.cos(ang), (1, n_rep))
    sin = jnp.tile(jnp.sin(ang), (1, n_rep))
    cos = jnp.concatenate([cos, jnp.ones((IN_TM, ROT_W), F32)], axis=0)
    sin = jnp.concatenate([sin, jnp.zeros((IN_TM, ROT_W), F32)], axis=0)
    return cos, sin


def kernel(x, c, ctx, c_ctx, mod_w, mod_b, norm1_g, w_in, attn_sink, sgu_norm_g, sgu_w, sgu_b, na_rpb,
           group_norm_g, w_out, norm2_g, router_w, router_bias, exp_w_gate, exp_w_up, exp_w_down,
           shared_w_gate, shared_w_up, shared_w_down, final_g):
    tokens = (x.reshape(N_LAT, D), ctx.reshape(N_CTX, D), 0)
    cc = jnp.concatenate([c, c_ctx[None, :], jnp.zeros((16 - B - 1, D), F32)], axis=0)
    mods_all = _modulation(cc, mod_w, mod_b)
    cos_t, sin_t = _rope_tables()
    seg = np.arange(B_W) // HD
    bd = jnp.asarray((seg[:, None] == seg[None, :]).astype(np.float32) / HD, BF16)
    fg = final_g.reshape(1, D)

    out = None
    for l in range(DEPTH):
        last = l == DEPTH - 1
        mods = mods_all[l].reshape(16, 1, 6 * D)
        za, zc, uv = _in_proj(*tokens, mods, norm1_g[l].reshape(1, D), w_in[l].astype(BF16), cos_t, sin_t)
        ya, yc = _mixers(za, zc, attn_sink[l], _na_bias_table(na_rpb[l]), with_ctx=not last)
        n_tok = N_LAT if last else N_ALL
        sb_tab = jnp.repeat(sgu_b[l].T, HD, axis=1)
        x_new, hx, logits_t = _merge(
            *tokens, ya, yc, uv, mods, group_norm_g[l].reshape(1, D), sgu_norm_g[l].reshape(1, B_W),
            sgu_w[l].astype(BF16), sb_tab, bd, w_out[l].astype(BF16), norm2_g[l].reshape(1, D),
            _router_split(router_w[l]), n_tok)
        res = _moe(x_new, hx, logits_t, mods, router_bias[l].reshape(N_EXP, 1),
                   exp_w_gate, exp_w_up, exp_w_down,
                   shared_w_gate[l].astype(BF16), shared_w_up[l].astype(BF16),
                   shared_w_down[l].astype(BF16), fg, n_tok, layer=l, final_norm=last)
        if last:
            out = res.reshape(B, S, D)
        else:
            tokens = (res, res, N_LAT)
    return out
```
